```python
import jax, jax.numpy as jnp
from jax import lax
import numpy as np

D_MODEL = 1024
BATCH = 8
SEQ = 2048
DEPTH = 2

GRID_W = 64
CTX_LEN = 256
EPS = 1e-6
ATTN_Q_BLOCK = 128

NA_HEADS = 4
NA_HEAD_DIM = 64
NA_WIN_H = 8
NA_WIN_W = 16
NA_WIDTH = NA_HEADS * NA_HEAD_DIM

MLA_HEADS = 4
MLA_Q_RANK = 256
MLA_KV_RANK = 128
MLA_NOPE_DIM = 64
MLA_ROPE_DIM = 32
MLA_V_DIM = 64
MLA_WIDTH = MLA_HEADS * MLA_V_DIM
ROPE_THETA = 10000.0

GLA_HEADS = 4
GLA_DK = 64
GLA_DV = 128
GLA_KW = GLA_HEADS * GLA_DK
GLA_VW = GLA_HEADS * GLA_DV
GLA_GATE_RANK = 16
GLA_TAU = 16.0
GLA_CHUNK = 64

N_BRANCH = 3
N_EXPERTS = 16
EXPERT_FF = 1024
EC_CAPACITY = 2

IN_SIZES = (NA_WIDTH, NA_WIDTH, NA_WIDTH,
            MLA_Q_RANK, MLA_KV_RANK, MLA_ROPE_DIM,
            GLA_KW, GLA_KW, GLA_VW, GLA_GATE_RANK, GLA_GATE_RANK, GLA_VW,
            D_MODEL, D_MODEL, D_MODEL)
D_IN = sum(IN_SIZES)

kernel_name = 'hybrid_na_mla_gla_ec_diffusion_block'


def rms_norm(x, g):
    xf = x.astype(jnp.float32)
    y = xf * lax.rsqrt(jnp.mean(xf * xf, axis=-1, keepdims=True) + EPS)
    return (y * g.astype(jnp.float32)).astype(x.dtype)


def modulate(h, shift, scale):
    return h * (1 + scale) + shift


def split_cols(z, sizes):
    parts, off = [], 0
    for s in sizes:
        parts.append(z[..., off:off + s])
        off += s
    return parts


def to_heads(z, n_heads):
    B, T, W = z.shape
    return z.reshape(B, T, n_heads, W // n_heads).transpose(0, 2, 1, 3)


def from_heads(z):
    B, H, T, d = z.shape
    return z.transpose(0, 2, 1, 3).reshape(B, T, H * d)


def axial_rope(x, rows, cols):
    half = x.shape[-1] // 2
    quarter = half // 2
    inv_freq = ROPE_THETA ** (-jnp.arange(quarter, dtype=jnp.float32) / quarter)

    def rot(xa, p):
        ang = p.astype(jnp.float32)[:, None] * inv_freq
        cos = jnp.cos(ang).astype(x.dtype)
        sin = jnp.sin(ang).astype(x.dtype)
        x1, x2 = xa[..., :quarter], xa[..., quarter:]
        return jnp.concatenate([x1 * cos - x2 * sin, x1 * sin + x2 * cos], axis=-1)

    return jnp.concatenate([rot(x[..., :half], rows), rot(x[..., half:], cols)], axis=-1)


def block_attention(q, k, v):
    B, H, T, dq = q.shape
    nb = T // ATTN_Q_BLOCK
    scale = dq ** -0.5
    qb = q.reshape(B, H, nb, ATTN_Q_BLOCK, dq).transpose(2, 0, 1, 3, 4)

    def one_block(qi):
        s = jnp.einsum('bhqd,bhkd->bhqk', qi, k).astype(jnp.float32) * scale
        p = jax.nn.softmax(s, axis=-1).astype(v.dtype)
        return jnp.einsum('bhqk,bhkd->bhqd', p, v)

    o = lax.map(one_block, qb)
    return o.transpose(1, 2, 0, 3, 4).reshape(B, H, T, v.shape[-1])


def neighbourhood_attention(q, k, v, k_ctx, v_ctx, rpb):
    B, H, T, dh = q.shape
    rows = T // GRID_W
    kh = min(NA_WIN_H, rows)
    kw = NA_WIN_W
    scale = dh ** -0.5
    qg = q.reshape(B, H, rows, GRID_W, dh)
    kg = k.reshape(B, H, rows, GRID_W, dh)
    vg = v.reshape(B, H, rows, GRID_W, dh)
    col = jnp.arange(GRID_W)
    col_start = jnp.clip(col - kw // 2, 0, GRID_W - kw)
    col_idx = col_start[:, None] + jnp.arange(kw)[None, :]
    col_off = col_idx - col[:, None] + (NA_WIN_W - 1)
    rpb_c = rpb[:, :, col_off]

    def one_row(r):
        rs = jnp.clip(r - kh // 2, 0, rows - kh)
        kb = lax.dynamic_slice_in_dim(kg, rs, kh, axis=2)[:, :, :, col_idx, :]
        vb = lax.dynamic_slice_in_dim(vg, rs, kh, axis=2)[:, :, :, col_idx, :]
        qr = lax.dynamic_index_in_dim(qg, r, axis=2, keepdims=False)
        s_loc = jnp.einsum('bhcd,bhicjd->bhcij', qr, kb).astype(jnp.float32) * scale
        row_off = rs + jnp.arange(kh) - r + (NA_WIN_H - 1)
        bias = jnp.take(rpb_c, row_off, axis=1).transpose(0, 2, 1, 3)
        s_loc = s_loc + bias[None].astype(jnp.float32)
        s_ctx = jnp.einsum('bhcd,bhnd->bhcn', qr, k_ctx).astype(jnp.float32) * scale
        s = jnp.concatenate([s_loc.reshape(B, H, GRID_W, kh * kw), s_ctx], axis=-1)
        p = jax.nn.softmax(s, axis=-1).astype(v.dtype)
        p_loc = p[..., :kh * kw].reshape(B, H, GRID_W, kh, kw)
        p_ctx = p[..., kh * kw:]
        return (jnp.einsum('bhcij,bhicjd->bhcd', p_loc, vb)
                + jnp.einsum('bhcn,bhnd->bhcd', p_ctx, v_ctx))

    o = lax.map(one_row, jnp.arange(rows))
    return o.transpose(1, 2, 0, 3, 4).reshape(B, H, T, dh)


def mla_heads(dq, dkv, kr, q_norm, w_uq, kv_norm, w_ukv, pos):
    B, T, _ = dq.shape
    qh = (rms_norm(dq, q_norm) @ w_uq).reshape(B, T, MLA_HEADS, MLA_NOPE_DIM + MLA_ROPE_DIM).transpose(0, 2, 1, 3)
    q_nope, q_rope = qh[..., :MLA_NOPE_DIM], qh[..., MLA_NOPE_DIM:]
    kvh = (rms_norm(dkv, kv_norm) @ w_ukv).reshape(B, T, MLA_HEADS, MLA_NOPE_DIM + MLA_V_DIM).transpose(0, 2, 1, 3)
    k_nope, v = kvh[..., :MLA_NOPE_DIM], kvh[..., MLA_NOPE_DIM:]
    k_rope = kr[:, None]
    if pos is not None:
        q_rope = axial_rope(q_rope, pos[0], pos[1])
        k_rope = axial_rope(k_rope, pos[0], pos[1])
    k_rope = jnp.broadcast_to(k_rope, (B, MLA_HEADS, T, MLA_ROPE_DIM))
    q = jnp.concatenate([q_nope, q_rope], axis=-1)
    k = jnp.concatenate([k_nope, k_rope], axis=-1)
    return q, k, v


def gla_log_decay(a_low, w2, b2):
    z = (a_low @ w2 + b2).astype(jnp.float32)
    return to_heads(jax.nn.log_sigmoid(z) / GLA_TAU, GLA_HEADS)


def gla_heads(q, k, v, a_f, a_b, w_af, b_af, w_ab, b_ab):
    return (to_heads(q, GLA_HEADS), to_heads(k, GLA_HEADS), to_heads(v, GLA_HEADS),
            gla_log_decay(a_f, w_af, b_af), gla_log_decay(a_b, w_ab, b_ab))


def gla_chunked(q, k, v, g, s0):
    B, H, T, dk = q.shape
    dv = v.shape[-1]
    C = GLA_CHUNK
    N = T // C
    qf = q.astype(jnp.float32).reshape(B, H, N, C, dk) * dk ** -0.5
    kf = k.astype(jnp.float32).reshape(B, H, N, C, dk)
    vf = v.astype(jnp.float32).reshape(B, H, N, C, dv)
    b = jnp.cumsum(g.reshape(B, H, N, C, dk), axis=3)
    b_last = b[:, :, :, -1:, :]
    q_dec = qf * jnp.exp(b)
    k_dec = kf * jnp.exp(-b)
    k_to_end = kf * jnp.exp(b_last - b)
    mask = jnp.tril(jnp.ones((C, C), dtype=bool))
    a = jnp.where(mask, jnp.einsum('bhntd,bhnsd->bhnts', q_dec, k_dec), 0.0)
    o_intra = jnp.einsum('bhnts,bhnsv->bhntv', a, vf)
    kv_chunk = jnp.einsum('bhnsd,bhnsv->bhndv', k_to_end, vf)
    decay = jnp.exp(b_last[:, :, :, 0, :])

    def step(s, inp):
        dec, kv = inp
        return dec[..., None] * s + kv, s

    s_fin, s_prev = lax.scan(step, s0, (jnp.moveaxis(decay, 2, 0), jnp.moveaxis(kv_chunk, 2, 0)))
    o_inter = jnp.einsum('bhntd,bhndv->bhntv', q_dec, jnp.moveaxis(s_prev, 0, 2))
    o = (o_intra + o_inter).reshape(B, H, T, dv)
    return o.astype(v.dtype), s_fin


def gla_bidirectional(qc, kc, vc, gfc, gbc, ql, kl, vl, gfl, gbl):
    B, H, _, dk = qc.shape
    s0 = jnp.zeros((B, H, dk, vc.shape[-1]), jnp.float32)
    rev = lambda u: jnp.flip(u, axis=2)
    oc_f, sc_f = gla_chunked(qc, kc, vc, gfc, s0)
    oc_b, sc_b = gla_chunked(rev(qc), rev(kc), rev(vc), rev(gbc), s0)
    ol_f, _ = gla_chunked(ql, kl, vl, gfl, sc_f)
    ol_b, _ = gla_chunked(rev(ql), rev(kl), rev(vl), rev(gbl), sc_b)
    return ol_f + rev(ol_b), oc_f + rev(oc_b)


def gla_output(o, og, o_norm):
    return from_heads(rms_norm(o, o_norm)) * jax.nn.silu(og)


def gated_merge(o_a, o_b, o_c, m_a, m_b, m_c, w_br_a, w_br_b, w_br_c, w_out):
    y = (jax.nn.sigmoid(m_a) * (o_a @ w_br_a)
         + jax.nn.sigmoid(m_b) * (o_b @ w_br_b)
         + jax.nn.sigmoid(m_c) * (o_c @ w_br_c))
    return y @ w_out


def expert_choice_moe(h, w_router, w_gate, w_up, w_down):
    B, T, D = h.shape
    cap = EC_CAPACITY * T // N_EXPERTS
    aff = jax.nn.softmax((h @ w_router).astype(jnp.float32), axis=-1)
    top_w, top_idx = lax.top_k(aff.transpose(0, 2, 1), cap)
    xg = jax.vmap(lambda hb, ib: hb[ib])(h, top_idx)
    hid = jax.nn.silu(jnp.einsum('becd,edf->becf', xg, w_gate)) * jnp.einsum('becd,edf->becf', xg, w_up)
    ye = jnp.einsum('becf,efd->becd', hid, w_down) * top_w[..., None].astype(h.dtype)
    return jax.vmap(lambda ib, yb: jnp.zeros((T, D), h.dtype).at[ib.reshape(-1)].add(yb.reshape(-1, D)))(top_idx, ye)


def hybrid_layer(x, ctx, c, c_ctx, ada_w, ada_b, norm1, norm2, w_in, na_rpb,
                 mla_q_norm, mla_w_uq, mla_kv_norm, mla_w_ukv,
                 gla_w_af, gla_b_af, gla_w_ab, gla_b_ab, gla_o_norm,
                 w_br_a, w_br_b, w_br_c, w_out,
                 moe_router, moe_w_gate, moe_w_up, moe_w_down, ctx_out):
    B, T, D = x.shape
    L = ctx.shape[1]
    t = jnp.arange(T)
    pos = (t // GRID_W, t % GRID_W)
    mod_l = [m[:, None, :] for m in jnp.split(jax.nn.silu(c) @ ada_w + ada_b, 6, axis=-1)]
    mod_c = jnp.split(jax.nn.silu(c_ctx) @ ada_w + ada_b, 6, axis=-1)

    h_l = modulate(rms_norm(x, norm1), mod_l[0], mod_l[1])
    h_c = modulate(rms_norm(ctx, norm1), mod_c[0], mod_c[1])
    z = jnp.concatenate([h_c, h_l], axis=1) @ w_in
    zc = split_cols(z[:, :L], IN_SIZES)
    zl = split_cols(z[:, L:], IN_SIZES)

    qa_l, ka_l, va_l = (to_heads(u, NA_HEADS) for u in zl[0:3])
    qa_c, ka_c, va_c = (to_heads(u, NA_HEADS) for u in zc[0:3])
    oa_l = from_heads(neighbourhood_attention(qa_l, ka_l, va_l, ka_c, va_c, na_rpb))

    qb_l, kb_l, vb_l = mla_heads(zl[3], zl[4], zl[5], mla_q_norm, mla_w_uq, mla_kv_norm, mla_w_ukv, pos)
    qb_c, kb_c, vb_c = mla_heads(zc[3], zc[4], zc[5], mla_q_norm, mla_w_uq, mla_kv_norm, mla_w_ukv, None)
    ob_l = from_heads(block_attention(qb_l, jnp.concatenate([kb_c, kb_l], axis=2),
                                      jnp.concatenate([vb_c, vb_l], axis=2)))

    gc = gla_heads(zc[6], zc[7], zc[8], zc[9], zc[10], gla_w_af, gla_b_af, gla_w_ab, gla_b_ab)
    gl = gla_heads(zl[6], zl[7], zl[8], zl[9], zl[10], gla_w_af, gla_b_af, gla_w_ab, gla_b_ab)
    oc_l_raw, oc_c_raw = gla_bidirectional(*gc, *gl)
    oc_l = gla_output(oc_l_raw, zl[11], gla_o_norm)

    y_l = gated_merge(oa_l, ob_l, oc_l, zl[12], zl[13], zl[14], w_br_a, w_br_b, w_br_c, w_out)
    x = x + mod_l[2] * y_l
    x = x + mod_l[5] * expert_choice_moe(modulate(rms_norm(x, norm2), mod_l[3], mod_l[4]),
                                         moe_router, moe_w_gate, moe_w_up, moe_w_down)

    if ctx_out:
        oa_c = from_heads(block_attention(qa_c, ka_c, va_c))
        ob_c = from_heads(block_attention(qb_c, kb_c, vb_c))
        oc_c = gla_output(oc_c_raw, zc[11], gla_o_norm)
        y_c = gated_merge(oa_c, ob_c, oc_c, zc[12], zc[13], zc[14], w_br_a, w_br_b, w_br_c, w_out)
        ctx = ctx + mod_c[2] * y_c
        ctx = ctx + mod_c[5] * expert_choice_moe(modulate(rms_norm(ctx, norm2), mod_c[3], mod_c[4]),
                                                 moe_router, moe_w_gate, moe_w_up, moe_w_down)
    return x, ctx


def setup_inputs(seed: int = 0) -> dict:
    key = jax.random.key(seed)
    ks = iter(jax.random.split(key, 32))

    def nrm(shape, std):
        return jax.random.normal(next(ks), shape, jnp.float32) * std

    def gain(shape):
        return 1.0 + nrm(shape, 0.02)

    D = D_MODEL
    return {
        'x': nrm((BATCH, SEQ, D), 1.0),
        'c': nrm((BATCH, D), 1.0),
        'ctx': nrm((BATCH, CTX_LEN, D), 1.0),
        'c_ctx': nrm((D,), 1.0),
        'ada_w': nrm((DEPTH, D, 6 * D), 0.5 * D ** -0.5),
        'ada_b': nrm((DEPTH, 6 * D), 0.01),
        'norm1': gain((DEPTH, D)),
        'norm2': gain((DEPTH, D)),
        'w_in': nrm((DEPTH, D, D_IN), D ** -0.5),
        'na_rpb': nrm((DEPTH, NA_HEADS, 2 * NA_WIN_H - 1, 2 * NA_WIN_W - 1), 0.1),
        'mla_q_norm': gain((DEPTH, MLA_Q_RANK)),
        'mla_w_uq': nrm((DEPTH, MLA_Q_RANK, MLA_HEADS * (MLA_NOPE_DIM + MLA_ROPE_DIM)), MLA_Q_RANK ** -0.5),
        'mla_kv_norm': gain((DEPTH, MLA_KV_RANK)),
        'mla_w_ukv': nrm((DEPTH, MLA_KV_RANK, MLA_HEADS * (MLA_NOPE_DIM + MLA_V_DIM)), MLA_KV_RANK ** -0.5),
        'gla_w_af': nrm((DEPTH, GLA_GATE_RANK, GLA_KW), GLA_GATE_RANK ** -0.5),
        'gla_b_af': 1.0 + nrm((DEPTH, GLA_KW), 0.5),
        'gla_w_ab': nrm((DEPTH, GLA_GATE_RANK, GLA_KW), GLA_GATE_RANK ** -0.5),
        'gla_b_ab': 1.0 + nrm((DEPTH, GLA_KW), 0.5),
        'gla_o_norm': gain((DEPTH, GLA_DV)),
        'w_br_a': nrm((DEPTH, NA_WIDTH, D), NA_WIDTH ** -0.5),
        'w_br_b': nrm((DEPTH, MLA_WIDTH, D), MLA_WIDTH ** -0.5),
        'w_br_c': nrm((DEPTH, GLA_VW, D), GLA_VW ** -0.5),
        'w_out': nrm((DEPTH, D, D), D ** -0.5),
        'moe_router': nrm((DEPTH, D, N_EXPERTS), D ** -0.5),
        'moe_w_gate': nrm((DEPTH, N_EXPERTS, D, EXPERT_FF), D ** -0.5),
        'moe_w_up': nrm((DEPTH, N_EXPERTS, D, EXPERT_FF), D ** -0.5),
        'moe_w_down': nrm((DEPTH, N_EXPERTS, EXPERT_FF, D), EXPERT_FF ** -0.5),
        'final_norm': gain((D,)),
    }


def reference(x, c, ctx, c_ctx, ada_w, ada_b, norm1, norm2, w_in, na_rpb,
              mla_q_norm, mla_w_uq, mla_kv_norm, mla_w_ukv,
              gla_w_af, gla_b_af, gla_w_ab, gla_b_ab, gla_o_norm,
              w_br_a, w_br_b, w_br_c, w_out,
              moe_router, moe_w_gate, moe_w_up, moe_w_down, final_norm):
    for l in range(DEPTH):
        x, ctx = hybrid_layer(x, ctx, c, c_ctx, ada_w[l], ada_b[l], norm1[l], norm2[l], w_in[l], na_rpb[l],
                              mla_q_norm[l], mla_w_uq[l], mla_kv_norm[l], mla_w_ukv[l],
                              gla_w_af[l], gla_b_af[l], gla_w_ab[l], gla_b_ab[l], gla_o_norm[l],
                              w_br_a[l], w_br_b[l], w_br_c[l], w_out[l],
                              moe_router[l], moe_w_gate[l], moe_w_up[l], moe_w_down[l],
                              ctx_out=(l < DEPTH - 1))
    return rms_norm(x, final_norm)
```

```python
import functools

import numpy as np
import jax
import jax.numpy as jnp
from jax import lax
from jax.experimental import pallas as pl
from jax.experimental.pallas import tpu as pltpu

BF = jnp.bfloat16
F32 = jnp.float32

D_MODEL = 1024
GRID_W = 64
EPS = 1e-6
NA_HEADS = 4
NA_HEAD_DIM = 64
NA_WIN_H = 8
NA_WIN_W = 16
MLA_HEADS = 4
MLA_Q_RANK = 256
MLA_KV_RANK = 128
MLA_NOPE_DIM = 64
MLA_ROPE_DIM = 32
MLA_V_DIM = 64
ROPE_THETA = 10000.0
GLA_HEADS = 4
GLA_DK = 64
GLA_DV = 128
GLA_GATE_RANK = 16
GLA_TAU = 16.0
GLA_CHUNK = 64
N_EXPERTS = 16
EXPERT_FF = 1024
EC_CAPACITY = 2
IN_SIZES = (256, 256, 256, 256, 128, 32, 256, 256, 512, 16, 16, 512, 1024, 1024, 1024)

LANES = 128
VMEM_LIMIT = 56 * 1024 * 1024

Z_MA, Z_MB, Z_MC = 0, 1024, 2048
Z_GV, Z_OG = 3072, 3584
Z_QA, Z_KA, Z_VA = 4096, 4352, 4608
Z_GQ, Z_GK = 4864, 5120
Z_MLA = 5376
Z_SM = 5760
Z_COLS = 5888
SM_KR, SM_AF, SM_AB = 0, 32, 48

SUPER = 256
NEG = -1e30


def _cparams(*sem):
    return pltpu.CompilerParams(dimension_semantics=sem, vmem_limit_bytes=VMEM_LIMIT)


def _dot(a, b):
    return jnp.dot(a, b, preferred_element_type=F32)


def _dot_nt(a, b):
    return lax.dot_general(a, b, (((1,), (1,)), ((), ())), preferred_element_type=F32)


def _dot_tn(a, b):
    return lax.dot_general(a, b, (((0,), (0,)), ((), ())), preferred_element_type=F32)


def _rms(x, g):
    return x * lax.rsqrt(jnp.mean(x * x, axis=-1, keepdims=True) + EPS) * g


def _split3(x):
    hi = x.astype(BF)
    r = x - hi.astype(F32)
    mid = r.astype(BF)
    lo = (r - mid.astype(F32)).astype(BF)
    return hi, mid, lo


def _ada_kernel(c_ref, w_ref, b_ref, o_ref):
    c = c_ref[...]
    a = (c * jax.nn.sigmoid(c)).astype(BF)
    o_ref[...] = _dot(a, w_ref[...].astype(BF)) + b_ref[...]


def _ada(cc, ada_w, ada_b):
    n = ada_w.shape[1]
    tn = 1536
    return pl.pallas_call(
        _ada_kernel,
        grid=(n // tn,),
        in_specs=[pl.BlockSpec((16, D_MODEL), lambda j: (0, 0)),
                  pl.BlockSpec((D_MODEL, tn), lambda j: (0, j)),
                  pl.BlockSpec((1, tn), lambda j: (0, j))],
        out_specs=pl.BlockSpec((16, tn), lambda j: (0, j)),
        out_shape=jax.ShapeDtypeStruct((16, n), F32),
        compiler_params=_cparams("arbitrary"),
        name="ada_mod",
    )(cc, ada_w, ada_b.reshape(1, n))


def _in_proj_kernel(*refs, with_moe):
    if with_moe:
        x_ref, moe_ref, gate_ref, g_ref, shift_ref, scale_ref, w_ref, o_ref, xo_ref, h_ref = refs
        x = x_ref[0] + gate_ref[0] * moe_ref[0].T
        xo_ref[0] = x
    else:
        x_ref, g_ref, shift_ref, scale_ref, w_ref, o_ref, h_ref = refs
        x = x_ref[0]
    h = _rms(x, g_ref[...]) * (1.0 + scale_ref[0]) + shift_ref[0]
    h_ref[...] = h.astype(BF)
    for n0 in range(0, Z_COLS, 256):
        o_ref[0, :, n0:n0 + 256] = _dot(h_ref[...], w_ref[:, n0:n0 + 256]).astype(BF)


def _in_proj(x, moe_t, prev_mods3, mods3, mod_row, g, w, tm=256):
    B, T, D = x.shape
    with_moe = moe_t is not None

    def mod_spec(k):
        return pl.BlockSpec((1, 1, D), lambda b, i: (mod_row(b) * 6 + k, 0, 0))

    x_spec = pl.BlockSpec((1, tm, D), lambda b, i: (b, i, 0))
    in_specs = [x_spec]
    args = [x]
    if with_moe:
        in_specs += [pl.BlockSpec((1, D, tm), lambda b, i: (b, 0, i)), mod_spec(5)]
        args += [moe_t, prev_mods3]
    in_specs += [pl.BlockSpec((1, D), lambda b, i: (0, 0)), mod_spec(0), mod_spec(1),
                 pl.BlockSpec((D, Z_COLS), lambda b, i: (0, 0))]
    args += [g.reshape(1, D), mods3, mods3, w]
    z_spec = pl.BlockSpec((1, tm, Z_COLS), lambda b, i: (b, i, 0))
    z_shape = jax.ShapeDtypeStruct((B, T, Z_COLS), BF)
    if with_moe:
        out_specs, out_shape = [z_spec, x_spec], [z_shape, jax.ShapeDtypeStruct((B, T, D), F32)]
    else:
        out_specs, out_shape = z_spec, z_shape
    res = pl.pallas_call(
        functools.partial(_in_proj_kernel, with_moe=with_moe),
        grid=(B, T // tm),
        in_specs=in_specs, out_specs=out_specs, out_shape=out_shape,
        scratch_shapes=[pltpu.VMEM((tm, D), BF)],
        compiler_params=_cparams("arbitrary", "arbitrary"),
        name="in_proj",
    )(*args)
    return (res[0], res[1]) if with_moe else (res, x)


def _softmax_pv(s_list, v_list):
    m = s_list[0].max(axis=-1, keepdims=True)
    for s in s_list[1:]:
        m = jnp.maximum(m, s.max(axis=-1, keepdims=True))
    l = None
    o = None
    for s, v in zip(s_list, v_list):
        p = jnp.exp(s - m)
        ps = p.sum(axis=-1, keepdims=True)
        pv = _dot(p.astype(BF), v)
        l = ps if l is None else l + ps
        o = pv if o is None else o + pv
    return o / l


def _lane_masks():
    lane = lax.broadcasted_iota(jnp.int32, (1, LANES), 1)
    return lane < 64, lane >= 64


NA_ROWS_PER_STEP = 2
NA_BAND = 10
NA_CASE_ROWS = (0, 2, 4, 28, 30)


def _na_bias(rpb, rows):
    qc = np.arange(GRID_W)
    kj = np.arange(GRID_W)
    cs = np.clip(qc - NA_WIN_W // 2, 0, GRID_W - NA_WIN_W)
    col_ok = (kj[None, :] >= cs[:, None]) & (kj[None, :] < cs[:, None] + NA_WIN_W)
    co = kj[None, :] - qc[:, None] + (NA_WIN_W - 1)
    onehot = np.zeros((2 * NA_WIN_W - 1, GRID_W, GRID_W), np.float32)
    qi, ki = np.nonzero(col_ok)
    onehot[co[qi, ki], qi, ki] = 1.0
    toep = jnp.einsum("hrc,cp->hrp", rpb.astype(F32), jnp.asarray(onehot.reshape(2 * NA_WIN_W - 1, -1)),
                      precision=lax.Precision.HIGHEST)
    toep = toep.reshape(NA_HEADS, 2 * NA_WIN_H - 1, GRID_W, GRID_W) + jnp.asarray(np.where(col_ok, 0.0, NEG), F32)
    neg = jnp.full((NA_HEADS, GRID_W, GRID_W), NEG, F32)
    cases = []
    for r0 in NA_CASE_ROWS:
        bs = int(np.clip(r0 - NA_WIN_H // 2, 0, rows - NA_BAND))
        q_rows = []
        for qr in range(NA_ROWS_PER_STEP):
            r = r0 + qr
            rs = int(np.clip(r - NA_WIN_H // 2, 0, rows - NA_WIN_H))
            blocks = []
            for i in range(NA_BAND):
                krow = bs + i
                blocks.append(toep[:, krow - r + NA_WIN_H - 1] if rs <= krow < rs + NA_WIN_H else neg)
            q_rows.append(jnp.concatenate(blocks, axis=-1))
        cases.append(jnp.concatenate(q_rows, axis=1))
    return jnp.stack(cases, axis=1)


def _na_kernel(*refs, rows, ctx_out):
    if ctx_out:
        ql_ref, kl_ref, vl_ref, qc_ref, kc_ref, vc_ref, bias_ref, ol_ref, oc_ref = refs
    else:
        ql_ref, kl_ref, vl_ref, kc_ref, vc_ref, bias_ref, ol_ref = refs
    scale = NA_HEAD_DIM ** -0.5
    masks = _lane_masks()
    nq = NA_ROWS_PER_STEP * GRID_W
    nk = NA_BAND * GRID_W

    def body(rp, carry):
        r0 = rp * NA_ROWS_PER_STEP
        bs = jnp.clip(r0 - NA_WIN_H // 2, 0, rows - NA_BAND)
        case = jnp.where(r0 < 4, r0 // 2, jnp.where(r0 > rows - 6, (r0 - (rows - 4)) // 2 + 3, 2))
        q0 = pl.multiple_of(r0 * GRID_W, nq)
        k0 = pl.multiple_of(bs * GRID_W, 2 * GRID_W)
        for hp in range(NA_HEADS // 2):
            sl = slice(hp * LANES, (hp + 1) * LANES)
            q = ql_ref[0, pl.ds(q0, nq), sl]
            kb = kl_ref[0, pl.ds(k0, nk), sl]
            vb = vl_ref[0, pl.ds(k0, nk), sl]
            kc = kc_ref[0, :, sl]
            vc = vc_ref[0, :, sl]
            outs = []
            for hh in range(2):
                qm = jnp.where(masks[hh], q, jnp.zeros_like(q))
                s_loc = _dot_nt(qm, kb) * scale + bias_ref[2 * hp + hh, case]
                s_ctx = _dot_nt(qm, kc) * scale
                outs.append(_softmax_pv([s_loc, s_ctx], [vb, vc]))
            ol_ref[0, pl.ds(q0, nq), sl] = jnp.where(masks[0], outs[0], outs[1]).astype(BF)
        return carry

    lax.fori_loop(0, rows // NA_ROWS_PER_STEP, body, 0)

    if ctx_out:
        for hp in range(NA_HEADS // 2):
            sl = slice(hp * LANES, (hp + 1) * LANES)
            q = qc_ref[0, :, sl]
            kc = kc_ref[0, :, sl]
            vc = vc_ref[0, :, sl]
            outs = []
            for hh in range(2):
                qm = jnp.where(masks[hh], q, jnp.zeros_like(q))
                outs.append(_softmax_pv([_dot_nt(qm, kc) * scale], [vc]))
            oc_ref[0, :, sl] = jnp.where(masks[0], outs[0], outs[1]).astype(BF)


def _na_attention(z_l, z_c, rpb, ctx_out):
    B, T, _ = z_l.shape
    L = z_c.shape[1]
    rows = T // GRID_W
    assert rows >= NA_BAND + 22 and rows % 2 == 0
    bias = _na_bias(rpb, rows)

    def zl(col):
        return pl.BlockSpec((1, T, 256), lambda b: (b, 0, col // 256))

    def zc(col):
        return pl.BlockSpec((1, L, 256), lambda b: (b, 0, col // 256))

    bias_spec = pl.BlockSpec(bias.shape, lambda b: (0, 0, 0, 0))
    ol_spec = pl.BlockSpec((1, T, 256), lambda b: (b, 0, 0))
    ol_shape = jax.ShapeDtypeStruct((B, T, 256), BF)
    if ctx_out:
        in_specs = [zl(Z_QA), zl(Z_KA), zl(Z_VA), zc(Z_QA), zc(Z_KA), zc(Z_VA), bias_spec]
        args = [z_l, z_l, z_l, z_c, z_c, z_c, bias]
        out_specs = [ol_spec, pl.BlockSpec((1, L, 256), lambda b: (b, 0, 0))]
        out_shape = [ol_shape, jax.ShapeDtypeStruct((B, L, 256), BF)]
    else:
        in_specs = [zl(Z_QA), zl(Z_KA), zl(Z_VA), zc(Z_KA), zc(Z_VA), bias_spec]
        args = [z_l, z_l, z_l, z_c, z_c, bias]
        out_specs, out_shape = ol_spec, ol_shape
    res = pl.pallas_call(
        functools.partial(_na_kernel, rows=rows, ctx_out=ctx_out),
        grid=(B,), in_specs=in_specs, out_specs=out_specs, out_shape=out_shape,
        compiler_params=_cparams("arbitrary"),
        name="na_attn",
    )(*args)
    return (res[0], res[1]) if ctx_out else (res, None)


MLA_QK = MLA_NOPE_DIM + MLA_ROPE_DIM


def _mla_proj_kernel(*refs, rope):
    if rope:
        (zm_ref, sm_ref, qn_ref, kvn_ref, wq_ref, wqs_ref, wk_ref, wv_ref, e_ref, es_ref, cos_ref, sin_ref,
         q_ref, k_ref, v_ref) = refs
    else:
        zm_ref, sm_ref, qn_ref, kvn_ref, wq_ref, wk_ref, wv_ref, e_ref, q_ref, k_ref, v_ref = refs
    zm = zm_ref[0].astype(F32)
    nq = _rms(zm[:, :MLA_Q_RANK], qn_ref[...]).astype(BF)
    nkv = _rms(zm[:, MLA_Q_RANK:], kvn_ref[...]).astype(BF)
    sm = sm_ref[0]
    scale = MLA_QK ** -0.5
    q = _dot(nq, wq_ref[...])
    k = _dot(nkv, wk_ref[...]) + _dot(sm, e_ref[...])
    if rope:
        cos = cos_ref[...]
        sin = sin_ref[...]
        q = q * cos + _dot(nq, wqs_ref[...]) * sin
        k = k * cos + _dot(sm, es_ref[...]) * sin
    q_ref[0] = (q * scale).astype(BF)
    k_ref[0] = k.astype(BF)
    v_ref[0] = _dot(nkv, wv_ref[...]).astype(BF)


def _rope_swap(e):
    return np.where(e % 16 < 8, e + 8, e - 8)


def _mla_weights(w_uq, w_ukv):
    hq = MLA_NOPE_DIM + MLA_ROPE_DIM
    hkv = MLA_NOPE_DIM + MLA_V_DIM
    e = np.arange(MLA_ROPE_DIM)
    wq = jnp.zeros((MLA_Q_RANK, MLA_HEADS * LANES), F32)
    wqs = jnp.zeros((MLA_Q_RANK, MLA_HEADS * LANES), F32)
    wk = jnp.zeros((MLA_KV_RANK, MLA_HEADS * LANES), F32)
    em = np.zeros((LANES, MLA_HEADS * LANES), np.float32)
    ems = np.zeros((LANES, MLA_HEADS * LANES), np.float32)
    wv = []
    for h in range(MLA_HEADS):
        wq = wq.at[:, h * LANES:h * LANES + hq].set(w_uq[:, h * hq:(h + 1) * hq])
        wqs = wqs.at[:, h * LANES + MLA_NOPE_DIM + e].set(w_uq[:, h * hq + MLA_NOPE_DIM + _rope_swap(e)])
        wk = wk.at[:, h * LANES:h * LANES + MLA_NOPE_DIM].set(w_ukv[:, h * hkv:h * hkv + MLA_NOPE_DIM])
        wv.append(w_ukv[:, h * hkv + MLA_NOPE_DIM:(h + 1) * hkv])
        em[SM_KR + e, h * LANES + MLA_NOPE_DIM + e] = 1.0
        ems[SM_KR + _rope_swap(e), h * LANES + MLA_NOPE_DIM + e] = 1.0
    wv = jnp.concatenate(wv, axis=1)
    return (wq.astype(BF), wqs.astype(BF), wk.astype(BF), wv.astype(BF),
            jnp.asarray(em, BF), jnp.asarray(ems, BF))


def _rope_tables(T):
    t = jnp.arange(T)
    quarter = MLA_ROPE_DIM // 4
    inv_freq = ROPE_THETA ** (-jnp.arange(quarter, dtype=F32) / quarter)
    cos_h = jnp.ones((T, LANES), F32)
    sin_h = jnp.zeros((T, LANES), F32)
    for part, p in enumerate((t // GRID_W, t % GRID_W)):
        ang = p.astype(F32)[:, None] * inv_freq
        c, s = jnp.cos(ang), jnp.sin(ang)
        base = MLA_NOPE_DIM + part * 2 * quarter
        cos_h = cos_h.at[:, base:base + quarter].set(c).at[:, base + quarter:base + 2 * quarter].set(c)
        sin_h = sin_h.at[:, base:base + quarter].set(-s).at[:, base + quarter:base + 2 * quarter].set(s)
    return jnp.tile(cos_h, (1, MLA_HEADS)), jnp.tile(sin_h, (1, MLA_HEADS))


def _mla_proj(z, q_norm, kv_norm, weights, rope_tabs, tm=256):
    B, T, _ = z.shape
    wq, wqs, wk, wv, em, ems = weights
    rope = rope_tabs is not None
    W = MLA_HEADS * LANES

    def const(a):
        return pl.BlockSpec(a.shape, lambda b, i: (0,) * a.ndim)

    zm_spec = pl.BlockSpec((1, tm, 384), lambda b, i: (b, i, Z_MLA // 384))
    sm_spec = pl.BlockSpec((1, tm, LANES), lambda b, i: (b, i, Z_SM // LANES))
    qn = q_norm.reshape(1, -1)
    kvn = kv_norm.reshape(1, -1)
    if rope:
        tab_spec = pl.BlockSpec((tm, W), lambda b, i: (i, 0))
        ins = [z, z, qn, kvn, wq, wqs, wk, wv, em, ems, rope_tabs[0], rope_tabs[1]]
        in_specs = [zm_spec, sm_spec] + [const(a) for a in ins[2:10]] + [tab_spec, tab_spec]
    else:
        ins = [z, z, qn, kvn, wq, wk, wv, em]
        in_specs = [zm_spec, sm_spec] + [const(a) for a in ins[2:]]
    return pl.pallas_call(
        functools.partial(_mla_proj_kernel, rope=rope),
        grid=(B, T // tm), in_specs=in_specs,
        out_specs=[pl.BlockSpec((1, tm, W), lambda b, i: (b, i, 0)),
                   pl.BlockSpec((1, tm, W), lambda b, i: (b, i, 0)),
                   pl.BlockSpec((1, tm, 256), lambda b, i: (b, i, 0))],
        out_shape=[jax.ShapeDtypeStruct((B, T, W), BF), jax.ShapeDtypeStruct((B, T, W), BF),
                   jax.ShapeDtypeStruct((B, T, 256), BF)],
        compiler_params=_cparams("arbitrary", "arbitrary"),
        name="mla_proj",
    )(*ins)


def _mla_attn_kernel(*refs, with_latent):
    if with_latent:
        q_ref, kc_ref, vc_ref, kl_ref, vl_ref, o_ref = refs
    else:
        q_ref, kc_ref, vc_ref, o_ref = refs
    masks = _lane_masks()
    for hp in range(MLA_HEADS // 2):
        vsl = slice(hp * LANES, (hp + 1) * LANES)
        outs = []
        for hh in range(2):
            h = 2 * hp + hh
            hsl = slice(h * LANES, (h + 1) * LANES)
            q = q_ref[0, :, hsl]
            s_list = [_dot_nt(q, kc_ref[0, :, hsl])]
            v_list = [vc_ref[0, :, vsl]]
            if with_latent:
                s_list.append(_dot_nt(q, kl_ref[0, :, hsl]))
                v_list.append(vl_ref[0, :, vsl])
            outs.append(_softmax_pv(s_list, v_list))
        o_ref[0, :, vsl] = jnp.where(masks[0], outs[0], outs[1]).astype(BF)


def _mla_attn(q, k_c, v_c, k_l=None, v_l=None, tq=256):
    B, Tq, W = q.shape
    L = k_c.shape[1]
    with_latent = k_l is not None
    in_specs = [pl.BlockSpec((1, tq, W), lambda b, i: (b, i, 0)),
                pl.BlockSpec((1, L, W), lambda b, i: (b, 0, 0)),
                pl.BlockSpec((1, L, 256), lambda b, i: (b, 0, 0))]
    args = [q, k_c, v_c]
    if with_latent:
        T = k_l.shape[1]
        in_specs += [pl.BlockSpec((1, T, W), lambda b, i: (b, 0, 0)),
                     pl.BlockSpec((1, T, 256), lambda b, i: (b, 0, 0))]
        args += [k_l, v_l]
    return pl.pallas_call(
        functools.partial(_mla_attn_kernel, with_latent=with_latent),
        grid=(B, Tq // tq), in_specs=in_specs,
        out_specs=pl.BlockSpec((1, tq, 256), lambda b, i: (b, i, 0)),
        out_shape=jax.ShapeDtypeStruct((B, Tq, 256), BF),
        compiler_params=_cparams("arbitrary", "arbitrary"),
        name="mla_attn",
    )(*args)


def _gla_superchunk(q, k, v, sm, w_a, b_a, tri, same, st_ref, reverse):
    n_chunk = SUPER // GLA_CHUNK
    g = jax.nn.log_sigmoid(_dot(sm, w_a) + b_a) * (1.0 / GLA_TAU)
    parts = _split3(g)
    b = _dot(tri, parts[0]) + _dot(tri, parts[1]) + _dot(tri, parts[2])
    tot = _dot(same, parts[0]) + _dot(same, parts[1]) + _dot(same, parts[2])
    qd_all = q.astype(F32) * (GLA_DK ** -0.5) * jnp.exp(b)
    kd_all = k.astype(F32) * jnp.exp(-b)
    ke_all = k.astype(F32) * jnp.exp(tot - b)
    dec_all = jnp.exp(tot)
    keep = tri > 0
    masks = _lane_masks()
    rowblk = lax.broadcasted_iota(jnp.int32, (SUPER, LANES), 0) // GLA_CHUNK
    order = range(n_chunk - 1, -1, -1) if reverse else range(n_chunk)
    outs = []
    for hp in range(GLA_HEADS // 2):
        sl = slice(hp * LANES, (hp + 1) * LANES)
        qd = qd_all[:, sl]
        kd = kd_all[:, sl].astype(BF)
        ke = ke_all[:, sl].astype(BF)
        dec = dec_all[:, sl]
        zero = jnp.zeros_like(ke)
        ke_bd = jnp.concatenate([jnp.where(rowblk == i, ke, zero) for i in range(n_chunk)], axis=1)
        for hh in range(2):
            h = 2 * hp + hh
            qm = jnp.where(masks[hh], qd, 0.0).astype(BF)
            vh = v[:, h * GLA_DV:(h + 1) * GLA_DV]
            a = jnp.where(keep, _dot_nt(qm, kd), 0.0).astype(BF)
            o = _dot(a, vh)
            kv_t = _dot_tn(vh, ke_bd)
            s = st_ref[h]
            before = [None] * n_chunk
            for i in order:
                before[i] = s.astype(BF)
                s = s * dec[i * GLA_CHUNK:i * GLA_CHUNK + 1, :] + kv_t[:, i * LANES:(i + 1) * LANES]
            st_ref[h] = s
            qm_bd = jnp.concatenate([jnp.where(rowblk == i, qm, jnp.zeros_like(qm)) for i in range(n_chunk)], axis=1)
            o = o + _dot_nt(qm_bd, jnp.concatenate(before, axis=1))
            outs.append(o)
    return jnp.concatenate(outs, axis=1)


def _gla_finish(o, og, onorm):
    ys = []
    for h in range(GLA_HEADS):
        ys.append(_rms(o[:, h * GLA_DV:(h + 1) * GLA_DV], onorm))
    y = jnp.concatenate(ys, axis=1)
    ogf = og.astype(F32)
    return (y * (ogf * jax.nn.sigmoid(ogf))).astype(BF)


def _gla_kernel(*refs, n_lat, ctx_out):
    (ql_ref, kl_ref, vl_ref, sml_ref, ogl_ref, qc_ref, kc_ref, vc_ref, smc_ref, ogc_ref,
     waf_ref, baf_ref, wab_ref, bab_ref, onorm_ref, trif_ref, trib_ref, same_ref) = refs[:18]
    if ctx_out:
        ol_ref, oc_ref, ofl_ref, ofc_ref, st_ref = refs[18:]
    else:
        ol_ref, ofl_ref, st_ref = refs[18:]
    onorm = onorm_ref[...]

    def run(direction):
        reverse = direction == 1
        w_a = (wab_ref if reverse else waf_ref)[...]
        b_a = (bab_ref if reverse else baf_ref)[...]
        tri = (trib_ref if reverse else trif_ref)[...]
        same = same_ref[...]
        st_ref[...] = jnp.zeros_like(st_ref)
        o_c = _gla_superchunk(qc_ref[0], kc_ref[0], vc_ref[0], smc_ref[0], w_a, b_a, tri, same, st_ref, reverse)
        if ctx_out:
            if reverse:
                oc_ref[0] = _gla_finish(ofc_ref[...] + o_c, ogc_ref[0], onorm)
            else:
                ofc_ref[...] = o_c

        def body(j, carry):
            sc = (n_lat - 1 - j) if reverse else j
            r0 = pl.multiple_of(sc * SUPER, SUPER)
            rows = pl.ds(r0, SUPER)
            o = _gla_superchunk(ql_ref[0, rows, :], kl_ref[0, rows, :], vl_ref[0, rows, :], sml_ref[0, rows, :],
                                w_a, b_a, tri, same, st_ref, reverse)
            if reverse:
                ol_ref[0, rows, :] = _gla_finish(ofl_ref[rows, :] + o, ogl_ref[0, rows, :], onorm)
            else:
                ofl_ref[rows, :] = o
            return carry

        lax.fori_loop(0, n_lat, body, 0)

    run(0)
    run(1)


def _gla(z_l, z_c, w_af, b_af, w_ab, b_ab, o_norm, ctx_out):
    B, T, _ = z_l.shape
    L = z_c.shape[1]
    assert L == SUPER and T % SUPER == 0
    kw = GLA_HEADS * GLA_DK
    vw = GLA_HEADS * GLA_DV

    def gate_w(w, off):
        return jnp.zeros((LANES, kw), F32).at[off:off + GLA_GATE_RANK].set(w).astype(BF)

    r = np.arange(SUPER)
    same_np = (r[:, None] // GLA_CHUNK) == (r[None, :] // GLA_CHUNK)
    trif = jnp.asarray(same_np & (r[None, :] <= r[:, None]), BF)
    trib = jnp.asarray(same_np & (r[None, :] >= r[:, None]), BF)
    same = jnp.asarray(same_np, BF)

    def zspec(n, width, col):
        return pl.BlockSpec((1, n, width), lambda b: (b, 0, col // width))

    def const(a):
        return pl.BlockSpec(a.shape, lambda b: (0,) * a.ndim)

    consts = [gate_w(w_af, SM_AF), b_af.reshape(1, kw), gate_w(w_ab, SM_AB), b_ab.reshape(1, kw),
              o_norm.reshape(1, GLA_DV), trif, trib, same]
    in_specs = ([zspec(T, kw, Z_GQ), zspec(T, kw, Z_GK), zspec(T, vw, Z_GV), zspec(T, LANES, Z_SM), zspec(T, vw, Z_OG),
                 zspec(L, kw, Z_GQ), zspec(L, kw, Z_GK), zspec(L, vw, Z_GV), zspec(L, LANES, Z_SM), zspec(L, vw, Z_OG)]
                + [const(a) for a in consts])
    ol_spec = pl.BlockSpec((1, T, vw), lambda b: (b, 0, 0))
    ol_shape = jax.ShapeDtypeStruct((B, T, vw), BF)
    scratch = [pltpu.VMEM((T, vw), F32)]
    if ctx_out:
        out_specs = [ol_spec, pl.BlockSpec((1, L, vw), lambda b: (b, 0, 0))]
        out_shape = [ol_shape, jax.ShapeDtypeStruct((B, L, vw), BF)]
        scratch.append(pltpu.VMEM((L, vw), F32))
    else:
        out_specs, out_shape = ol_spec, ol_shape
    scratch.append(pltpu.VMEM((GLA_HEADS, GLA_DV, LANES), F32))
    res = pl.pallas_call(
        functools.partial(_gla_kernel, n_lat=T // SUPER, ctx_out=ctx_out),
        grid=(B,), in_specs=in_specs, out_specs=out_specs, out_shape=out_shape,
        scratch_shapes=scratch,
        compiler_params=_cparams("arbitrary"),
        name="gla",
    )(*([z_l] * 5 + [z_c] * 5 + consts))
    return (res[0], res[1]) if ctx_out else (res, None)


def _merge_kernel(oa_ref, ob_ref, oc_ref, ma_ref, mb_ref, mc_ref, x_ref, gate_ref, shift_ref, scale_ref, g_ref,
                  wa_ref, wb_ref, wc_ref, wo_ref, wr_ref, xo_ref, h_ref, aff_ref, y_ref):
    def branch(o_ref, m_ref, w_ref, n0):
        m = m_ref[0, :, n0:n0 + 256].astype(F32)
        return jax.nn.sigmoid(m) * _dot(o_ref[0], w_ref[:, n0:n0 + 256])

    for n0 in range(0, D_MODEL, 256):
        y = branch(oa_ref, ma_ref, wa_ref, n0) + branch(ob_ref, mb_ref, wb_ref, n0) + branch(oc_ref, mc_ref, wc_ref, n0)
        y_ref[:, n0:n0 + 256] = y.astype(BF)
    gate = gate_ref[0]
    for n0 in range(0, D_MODEL, 256):
        xo_ref[0, :, n0:n0 + 256] = (x_ref[0, :, n0:n0 + 256]
                                      + gate[:, n0:n0 + 256] * _dot(y_ref[...], wo_ref[:, n0:n0 + 256]))
    h = _rms(xo_ref[0], g_ref[...]) * (1.0 + scale_ref[0]) + shift_ref[0]
    h_ref[0] = h.astype(BF)
    logits = lax.dot_general(wr_ref[...], h, (((1,), (1,)), ((), ())), precision=lax.Precision.HIGHEST,
                             preferred_element_type=F32)
    e = jnp.exp(logits - logits.max(axis=0, keepdims=True))
    aff_ref[0] = e / e.sum(axis=0, keepdims=True)


def _merge(oa, ob, oc, z, x, mods3, mod_row, norm2, wa, wb, wc, wo, wr_t, tm=256):
    B, T, D = x.shape

    def tok(width, col=0):
        return pl.BlockSpec((1, tm, width), lambda b, i: (b, i, col // width))

    def mod_spec(k):
        return pl.BlockSpec((1, 1, D), lambda b, i: (mod_row(b) * 6 + k, 0, 0))

    def const(a):
        return pl.BlockSpec(a.shape, lambda b, i: (0,) * a.ndim)

    g = norm2.reshape(1, D)
    in_specs = [tok(256), tok(256), tok(512), tok(D, Z_MA), tok(D, Z_MB), tok(D, Z_MC), tok(D),
                mod_spec(2), mod_spec(3), mod_spec(4), const(g), const(wa), const(wb), const(wc), const(wo), const(wr_t)]
    return pl.pallas_call(
        _merge_kernel,
        grid=(B, T // tm), in_specs=in_specs,
        out_specs=[tok(D), tok(D), pl.BlockSpec((1, N_EXPERTS, tm), lambda b, i: (b, 0, i))],
        out_shape=[jax.ShapeDtypeStruct((B, T, D), F32), jax.ShapeDtypeStruct((B, T, D), BF),
                   jax.ShapeDtypeStruct((B, N_EXPERTS, T), F32)],
        scratch_shapes=[pltpu.VMEM((tm, D), BF)],
        compiler_params=_cparams("arbitrary", "arbitrary"),
        name="merge",
    )(oa, ob, oc, z, z, z, x, mods3, mods3, mods3, g, wa, wb, wc, wo, wr_t)


def _route_kernel(aff_ref, tri_ref, pos_ref, *, cap, slot_stride):
    aff = aff_ref[0]
    E, T = aff.shape

    def search(i, thr):
        cand = thr | (jnp.int32(1) << (30 - i))
        cnt = jnp.sum((aff >= pltpu.bitcast(cand, F32)).astype(F32), axis=1, keepdims=True)
        return jnp.where(cnt >= cap, cand, thr)

    thr = lax.fori_loop(0, 31, search, jnp.zeros((E, 1), jnp.int32))
    thr_f = pltpu.bitcast(thr, F32)
    gt = aff > thr_f
    eq = aff == thr_f
    need = cap - jnp.sum(gt.astype(F32), axis=1, keepdims=True)
    tri = tri_ref[...]

    def excl_prefix(flags):
        run = jnp.zeros((E, 1), F32)
        blocks = []
        for j in range(T // LANES):
            blk = flags[:, j * LANES:(j + 1) * LANES].astype(F32)
            blocks.append(_dot(blk.astype(BF), tri) + run)
            run = run + jnp.sum(blk, axis=1, keepdims=True)
        return jnp.concatenate(blocks, axis=1)

    sel = gt | (eq & (excl_prefix(eq) < need))
    slot = excl_prefix(sel).astype(jnp.int32) + slot_stride * pl.program_id(0)
    pos_ref[0] = jnp.where(sel, slot, -1)


def _route(aff_t, cap, slot_stride):
    B, E, T = aff_t.shape
    r = np.arange(LANES)
    tri = jnp.asarray(r[:, None] < r[None, :], BF)
    return pl.pallas_call(
        functools.partial(_route_kernel, cap=cap, slot_stride=slot_stride),
        grid=(B,),
        in_specs=[pl.BlockSpec((1, E, T), lambda b: (b, 0, 0)), pl.BlockSpec((LANES, LANES), lambda b: (0, 0))],
        out_specs=pl.BlockSpec((1, E, T), lambda b: (b, 0, 0)),
        out_shape=jax.ShapeDtypeStruct((B, E, T), jnp.int32),
        compiler_params=_cparams("arbitrary"),
        name="route",
    )(aff_t, tri)


def _moe_kernel(h_ref, pos_ref, aff_ref, wg_ref, wu_ref, wd_ref, o_ref, *, n_slots):
    e = pl.program_id(1)
    pos = pos_ref[0, pl.ds(e, 1), :]
    aff = aff_ref[0, pl.ds(e, 1), :]
    T = pos.shape[1]
    hit = lax.broadcasted_iota(jnp.int32, (n_slots, T), 0) == pos
    p = hit.astype(BF)
    w_slot = jnp.sum(jnp.where(hit, aff, 0.0), axis=1, keepdims=True)
    xg = _dot(p, h_ref[0]).astype(BF)
    gt = _dot(xg, wg_ref[0])
    up = _dot(xg, wu_ref[0])
    hid = (gt * jax.nn.sigmoid(gt) * up * w_slot).astype(BF)
    y_t = _dot_nt(wd_ref[0], hid).astype(BF)

    @pl.when(e == 0)
    def _():
        o_ref[...] = jnp.zeros_like(o_ref)

    for t0 in range(0, T, 256):
        o_ref[0, :, t0:t0 + 256] += _dot(y_t, p[:, t0:t0 + 256])


def _moe(h, pos, aff_t, wg, wu, wd_t, n_slots):
    B, T, D = h.shape
    E = pos.shape[1]
    wspec = pl.BlockSpec((1,) + wg.shape[1:], lambda b, e: (e, 0, 0))
    return pl.pallas_call(
        functools.partial(_moe_kernel, n_slots=n_slots),
        grid=(B, E),
        in_specs=[pl.BlockSpec((1, T, D), lambda b, e: (b, 0, 0)),
                  pl.BlockSpec((1, E, T), lambda b, e: (b, 0, 0)),
                  pl.BlockSpec((1, E, T), lambda b, e: (b, 0, 0)),
                  wspec, wspec, wspec],
        out_specs=pl.BlockSpec((1, D, T), lambda b, e: (b, 0, 0)),
        out_shape=jax.ShapeDtypeStruct((B, D, T), F32),
        compiler_params=_cparams("arbitrary", "arbitrary"),
        name="moe",
    )(h, pos, aff_t, wg, wu, wd_t)


def _final_kernel(x_ref, moe_ref, gate_ref, g_ref, o_ref):
    x = x_ref[0] + gate_ref[0] * moe_ref[0].T
    o_ref[0] = _rms(x, g_ref[...])


def _final(x, moe_t, mods3, g, tm=256):
    B, T, D = x.shape
    return pl.pallas_call(
        _final_kernel,
        grid=(B, T // tm),
        in_specs=[pl.BlockSpec((1, tm, D), lambda b, i: (b, i, 0)),
                  pl.BlockSpec((1, D, tm), lambda b, i: (b, 0, i)),
                  pl.BlockSpec((1, 1, D), lambda b, i: (b * 6 + 5, 0, 0)),
                  pl.BlockSpec((1, D), lambda b, i: (0, 0))],
        out_specs=pl.BlockSpec((1, tm, D), lambda b, i: (b, i, 0)),
        out_shape=jax.ShapeDtypeStruct((B, T, D), F32),
        compiler_params=_cparams("arbitrary", "arbitrary"),
        name="final_norm",
    )(x, moe_t, mods3, g.reshape(1, D))


def _split_cols(w):
    parts, off = [], 0
    for s in IN_SIZES:
        parts.append(w[..., off:off + s])
        off += s
    return parts


def _reorder_w_in(w_in):
    qa, ka, va, dq, dkv, kr, gq, gk, gv, a_f, a_b, og, m_a, m_b, m_c = _split_cols(w_in)
    pad = jnp.zeros((w_in.shape[0], LANES - MLA_ROPE_DIM - 2 * GLA_GATE_RANK), w_in.dtype)
    return jnp.concatenate([m_a, m_b, m_c, gv, og, qa, ka, va, gq, gk, dq, dkv, kr, a_f, a_b, pad], axis=1).astype(BF)


def kernel(x, c, ctx, c_ctx, ada_w, ada_b, norm1, norm2, w_in, na_rpb, mla_q_norm, mla_w_uq, mla_kv_norm, mla_w_ukv,
           gla_w_af, gla_b_af, gla_w_ab, gla_b_ab, gla_o_norm, w_br_a, w_br_b, w_br_c, w_out,
           moe_router, moe_w_gate, moe_w_up, moe_w_down, final_norm):
    B, T, D = x.shape
    L = ctx.shape[1]
    depth = ada_w.shape[0]
    cap_l = EC_CAPACITY * T // N_EXPERTS
    cap_c = EC_CAPACITY * L // N_EXPERTS
    ctx_row = B
    cc = jnp.zeros((16, D), F32).at[:B].set(c).at[ctx_row].set(c_ctx)
    rope_tabs = _rope_tables(T)

    def lat_row(b):
        return b

    def ctx_row_fn(b):
        return ctx_row

    moe_l = moe_c = mods3 = None
    for l in range(depth):
        ctx_out = l < depth - 1
        prev_mods3 = mods3
        mods3 = _ada(cc, ada_w[l], ada_b[l]).reshape(16 * 6, 1, D)
        w_in_r = _reorder_w_in(w_in[l])
        z_l, x = _in_proj(x, moe_l, prev_mods3, mods3, lat_row, norm1[l], w_in_r)
        z_c, ctx = _in_proj(ctx, moe_c, prev_mods3, mods3, ctx_row_fn, norm1[l], w_in_r)

        oa_l, oa_c = _na_attention(z_l, z_c, na_rpb[l], ctx_out)

        mla_w = _mla_weights(mla_w_uq[l], mla_w_ukv[l])
        qb_l, kb_l, vb_l = _mla_proj(z_l, mla_q_norm[l], mla_kv_norm[l], mla_w, rope_tabs)
        qb_c, kb_c, vb_c = _mla_proj(z_c, mla_q_norm[l], mla_kv_norm[l], mla_w, None)
        ob_l = _mla_attn(qb_l, kb_c, vb_c, kb_l, vb_l)

        oc_l, oc_c = _gla(z_l, z_c, gla_w_af[l], gla_b_af[l], gla_w_ab[l], gla_b_ab[l], gla_o_norm[l], ctx_out)

        wa, wb, wc, wo = (w.astype(BF) for w in (w_br_a[l], w_br_b[l], w_br_c[l], w_out[l]))
        wr_t = moe_router[l].T
        wg, wu = moe_w_gate[l].astype(BF), moe_w_up[l].astype(BF)
        wd_t = jnp.swapaxes(moe_w_down[l], 1, 2).astype(BF)

        x, h_l, aff_l = _merge(oa_l, ob_l, oc_l, z_l, x, mods3, lat_row, norm2[l], wa, wb, wc, wo, wr_t)
        pos_l = _route(aff_l, cap_l, 0)
        moe_l = _moe(h_l, pos_l, aff_l, wg, wu, wd_t, cap_l)

        if ctx_out:
            ob_c = _mla_attn(qb_c, kb_c, vb_c)
            ctx, h_c, aff_c = _merge(oa_c, ob_c, oc_c, z_c, ctx, mods3, ctx_row_fn, norm2[l], wa, wb, wc, wo, wr_t)
            pos_c = _route(aff_c, cap_c, cap_c)
            flat = lambda a: jnp.swapaxes(a, 0, 1).reshape(1, N_EXPERTS, B * L)
            moe_c = _moe(h_c.reshape(1, B * L, D), flat(pos_c), flat(aff_c), wg, wu, wd_t, B * cap_c)
            moe_c = jnp.swapaxes(moe_c.reshape(D, B, L), 0, 1)
    return _final(x, moe_l, mods3, final_norm)
```

```python
import functools

import numpy as np
import jax
import jax.numpy as jnp
from jax import lax
from jax.experimental import pallas as pl
from jax.experimental.pallas import tpu as pltpu

BF = jnp.bfloat16
F32 = jnp.float32

D_MODEL = 1024
GRID_W = 64
EPS = 1e-6
NA_HEADS = 4
NA_HEAD_DIM = 64
NA_WIN_H = 8
NA_WIN_W = 16
MLA_HEADS = 4
MLA_Q_RANK = 256
MLA_KV_RANK = 128
MLA_NOPE_DIM = 64
MLA_ROPE_DIM = 32
MLA_V_DIM = 64
ROPE_THETA = 10000.0
GLA_HEADS = 4
GLA_DK = 64
GLA_DV = 128
GLA_GATE_RANK = 16
GLA_TAU = 16.0
GLA_CHUNK = 64
N_EXPERTS = 16
EXPERT_FF = 1024
EC_CAPACITY = 2
IN_SIZES = (256, 256, 256, 256, 128, 32, 256, 256, 512, 16, 16, 512, 1024, 1024, 1024)

LANES = 128
VMEM_LIMIT = 56 * 1024 * 1024
MOD_ROWS = 16

Z_MA, Z_MB, Z_MC = 0, 1024, 2048
Z_GV, Z_OG = 3072, 3584
Z_QA, Z_KA, Z_VA = 4096, 4352, 4608
Z_GQ, Z_GK = 4864, 5120
Z_MLA = 5376
Z_SM = 5760
Z_COLS = 5888
SM_KR, SM_AF, SM_AB = 0, 32, 48

SUPER = 256
NEG = -1e30


def _cparams(*sem):
    return pltpu.CompilerParams(dimension_semantics=sem, vmem_limit_bytes=VMEM_LIMIT)


def _layer(a, l):
    return pl.BlockSpec((None,) + a.shape[1:], lambda *_: (l,) + (0,) * (a.ndim - 1))


def _whole(a):
    return pl.BlockSpec(a.shape, lambda *_: (0,) * a.ndim)


def _dot(a, b):
    return jnp.dot(a, b, preferred_element_type=F32)


def _dot_nt(a, b):
    return lax.dot_general(a, b, (((1,), (1,)), ((), ())), preferred_element_type=F32)


def _dot_tn(a, b):
    return lax.dot_general(a, b, (((0,), (0,)), ((), ())), preferred_element_type=F32)


def _rms(x, g):
    return x * lax.rsqrt(jnp.mean(x * x, axis=-1, keepdims=True) + EPS) * g


def _split3(x):
    hi = x.astype(BF)
    r = x - hi.astype(F32)
    mid = r.astype(BF)
    lo = (r - mid.astype(F32)).astype(BF)
    return hi, mid, lo


def _ada_kernel(c_ref, w_ref, b_ref, o_ref):
    c = c_ref[...]
    a = (c * jax.nn.sigmoid(c)).astype(BF)
    o_ref[...] = _dot(a, w_ref[...].astype(BF)) + b_ref[...]


def _ada(cc, ada_w, ada_b3, l):
    n = ada_w.shape[2]
    tn = 1536
    return pl.pallas_call(
        _ada_kernel,
        grid=(n // tn,),
        in_specs=[pl.BlockSpec((MOD_ROWS, D_MODEL), lambda j: (0, 0)),
                  pl.BlockSpec((None, D_MODEL, tn), lambda j: (l, 0, j)),
                  pl.BlockSpec((None, 1, tn), lambda j: (l, 0, j))],
        out_specs=pl.BlockSpec((MOD_ROWS, tn), lambda j: (0, j)),
        out_shape=jax.ShapeDtypeStruct((MOD_ROWS, n), F32),
        compiler_params=_cparams("arbitrary"),
        name="ada_mod",
    )(cc, ada_w, ada_b3)


def _in_proj_kernel(*refs, with_moe):
    if with_moe:
        x_ref, moe_ref, gate_ref, g_ref, shift_ref, scale_ref, w_ref, o_ref, xo_ref, h_ref = refs
        x = x_ref[0] + gate_ref[0] * moe_ref[0]
        xo_ref[0] = x
    else:
        x_ref, g_ref, shift_ref, scale_ref, w_ref, o_ref, h_ref = refs
        x = x_ref[0]
    h = _rms(x, g_ref[...]) * (1.0 + scale_ref[0]) + shift_ref[0]
    h_ref[...] = h.astype(BF)
    for n0 in range(0, Z_COLS, 256):
        o_ref[0, :, n0:n0 + 256] = _dot(h_ref[...], w_ref[:, n0:n0 + 256]).astype(BF)


def _in_proj(x, moe, prev_mods3, mods3, mod_row, norm1_3, w, l, tm=256):
    B, T, D = x.shape
    with_moe = moe is not None

    def mod_spec(k):
        return pl.BlockSpec((1, 1, D), lambda b, i: (mod_row(b) * 6 + k, 0, 0))

    x_spec = pl.BlockSpec((1, tm, D), lambda b, i: (b, i, 0))
    in_specs = [x_spec]
    args = [x]
    if with_moe:
        in_specs += [x_spec, mod_spec(5)]
        args += [moe, prev_mods3]
    in_specs += [_layer(norm1_3, l), mod_spec(0), mod_spec(1), _layer(w, l)]
    args += [norm1_3, mods3, mods3, w]
    z_spec = pl.BlockSpec((1, tm, Z_COLS), lambda b, i: (b, i, 0))
    z_shape = jax.ShapeDtypeStruct((B, T, Z_COLS), BF)
    if with_moe:
        out_specs, out_shape = [z_spec, x_spec], [z_shape, jax.ShapeDtypeStruct((B, T, D), F32)]
    else:
        out_specs, out_shape = z_spec, z_shape
    res = pl.pallas_call(
        functools.partial(_in_proj_kernel, with_moe=with_moe),
        grid=(B, T // tm),
        in_specs=in_specs, out_specs=out_specs, out_shape=out_shape,
        scratch_shapes=[pltpu.VMEM((tm, D), BF)],
        compiler_params=_cparams("arbitrary", "arbitrary"),
        name="in_proj",
    )(*args)
    return (res[0], res[1]) if with_moe else (res, x)


def _softmax_pv(s_list, v_list):
    m = s_list[0].max(axis=-1, keepdims=True)
    for s in s_list[1:]:
        m = jnp.maximum(m, s.max(axis=-1, keepdims=True))
    l = None
    o = None
    for s, v in zip(s_list, v_list):
        p = jnp.exp(s - m)
        ps = p.sum(axis=-1, keepdims=True)
        pv = _dot(p.astype(BF), v)
        l = ps if l is None else l + ps
        o = pv if o is None else o + pv
    return o / l


def _lane_masks():
    lane = lax.broadcasted_iota(jnp.int32, (1, LANES), 1)
    return lane < 64, lane >= 64


NA_ROWS_PER_STEP = 2
NA_BAND = 10
NA_CASE_ROWS = (0, 2, 4, 28, 30)


def _na_bias(rpb, rows):
    depth = rpb.shape[0]
    qc = np.arange(GRID_W)
    kj = np.arange(GRID_W)
    cs = np.clip(qc - NA_WIN_W // 2, 0, GRID_W - NA_WIN_W)
    col_ok = (kj[None, :] >= cs[:, None]) & (kj[None, :] < cs[:, None] + NA_WIN_W)
    co = kj[None, :] - qc[:, None] + (NA_WIN_W - 1)
    onehot = np.zeros((2 * NA_WIN_W - 1, GRID_W, GRID_W), np.float32)
    qi, ki = np.nonzero(col_ok)
    onehot[co[qi, ki], qi, ki] = 1.0
    toep = jnp.einsum("lhrc,cp->lhrp", rpb.astype(F32), jnp.asarray(onehot.reshape(2 * NA_WIN_W - 1, -1)),
                      precision=lax.Precision.HIGHEST)
    toep = (toep.reshape(depth, NA_HEADS, 2 * NA_WIN_H - 1, GRID_W, GRID_W)
            + jnp.asarray(np.where(col_ok, 0.0, NEG), F32))
    neg = jnp.full((depth, NA_HEADS, GRID_W, GRID_W), NEG, F32)
    cases = []
    for r0 in NA_CASE_ROWS:
        bs = int(np.clip(r0 - NA_WIN_H // 2, 0, rows - NA_BAND))
        q_rows = []
        for qr in range(NA_ROWS_PER_STEP):
            r = r0 + qr
            rs = int(np.clip(r - NA_WIN_H // 2, 0, rows - NA_WIN_H))
            blocks = []
            for i in range(NA_BAND):
                krow = bs + i
                blocks.append(toep[:, :, krow - r + NA_WIN_H - 1] if rs <= krow < rs + NA_WIN_H else neg)
            q_rows.append(jnp.concatenate(blocks, axis=-1))
        cases.append(jnp.concatenate(q_rows, axis=-2))
    return jnp.stack(cases, axis=2)


def _na_kernel(*refs, rows, ctx_out):
    if ctx_out:
        ql_ref, kl_ref, vl_ref, qc_ref, kc_ref, vc_ref, bias_ref, ol_ref, oc_ref = refs
    else:
        ql_ref, kl_ref, vl_ref, kc_ref, vc_ref, bias_ref, ol_ref = refs
    scale = NA_HEAD_DIM ** -0.5
    masks = _lane_masks()
    nq = NA_ROWS_PER_STEP * GRID_W
    nk = NA_BAND * GRID_W

    def body(rp, carry):
        r0 = rp * NA_ROWS_PER_STEP
        bs = jnp.clip(r0 - NA_WIN_H // 2, 0, rows - NA_BAND)
        case = jnp.where(r0 < 4, r0 // 2, jnp.where(r0 > rows - 6, (r0 - (rows - 4)) // 2 + 3, 2))
        q0 = pl.multiple_of(r0 * GRID_W, nq)
        k0 = pl.multiple_of(bs * GRID_W, 2 * GRID_W)
        for hp in range(NA_HEADS // 2):
            sl = slice(hp * LANES, (hp + 1) * LANES)
            q = ql_ref[0, pl.ds(q0, nq), sl]
            kb = kl_ref[0, pl.ds(k0, nk), sl]
            vb = vl_ref[0, pl.ds(k0, nk), sl]
            kc = kc_ref[0, :, sl]
            vc = vc_ref[0, :, sl]
            outs = []
            for hh in range(2):
                qm = jnp.where(masks[hh], q, jnp.zeros_like(q))
                s_loc = _dot_nt(qm, kb) * scale + bias_ref[2 * hp + hh, case]
                s_ctx = _dot_nt(qm, kc) * scale
                outs.append(_softmax_pv([s_loc, s_ctx], [vb, vc]))
            ol_ref[0, pl.ds(q0, nq), sl] = jnp.where(masks[0], outs[0], outs[1]).astype(BF)
        return carry

    lax.fori_loop(0, rows // NA_ROWS_PER_STEP, body, 0)

    if ctx_out:
        for hp in range(NA_HEADS // 2):
            sl = slice(hp * LANES, (hp + 1) * LANES)
            q = qc_ref[0, :, sl]
            kc = kc_ref[0, :, sl]
            vc = vc_ref[0, :, sl]
            outs = []
            for hh in range(2):
                qm = jnp.where(masks[hh], q, jnp.zeros_like(q))
                outs.append(_softmax_pv([_dot_nt(qm, kc) * scale], [vc]))
            oc_ref[0, :, sl] = jnp.where(masks[0], outs[0], outs[1]).astype(BF)


def _na_attention(z_l, z_c, bias, l, ctx_out):
    B, T, _ = z_l.shape
    L = z_c.shape[1]
    rows = T // GRID_W

    def zl(col):
        return pl.BlockSpec((1, T, 256), lambda b: (b, 0, col // 256))

    def zc(col):
        return pl.BlockSpec((1, L, 256), lambda b: (b, 0, col // 256))

    ol_spec = pl.BlockSpec((1, T, 256), lambda b: (b, 0, 0))
    ol_shape = jax.ShapeDtypeStruct((B, T, 256), BF)
    if ctx_out:
        in_specs = [zl(Z_QA), zl(Z_KA), zl(Z_VA), zc(Z_QA), zc(Z_KA), zc(Z_VA), _layer(bias, l)]
        args = [z_l, z_l, z_l, z_c, z_c, z_c, bias]
        out_specs = [ol_spec, pl.BlockSpec((1, L, 256), lambda b: (b, 0, 0))]
        out_shape = [ol_shape, jax.ShapeDtypeStruct((B, L, 256), BF)]
    else:
        in_specs = [zl(Z_QA), zl(Z_KA), zl(Z_VA), zc(Z_KA), zc(Z_VA), _layer(bias, l)]
        args = [z_l, z_l, z_l, z_c, z_c, bias]
        out_specs, out_shape = ol_spec, ol_shape
    res = pl.pallas_call(
        functools.partial(_na_kernel, rows=rows, ctx_out=ctx_out),
        grid=(B,), in_specs=in_specs, out_specs=out_specs, out_shape=out_shape,
        compiler_params=_cparams("arbitrary"),
        name="na_attn",
    )(*args)
    return (res[0], res[1]) if ctx_out else (res, None)


MLA_QK = MLA_NOPE_DIM + MLA_ROPE_DIM
MLA_W = MLA_HEADS * LANES


def _mla_proj_kernel(*refs, rope):
    if rope:
        (zm_ref, sm_ref, qn_ref, kvn_ref, wq_ref, wqs_ref, wk_ref, wv_ref, e_ref, es_ref, cos_ref, sin_ref,
         q_ref, k_ref, v_ref) = refs
    else:
        zm_ref, sm_ref, qn_ref, kvn_ref, wq_ref, wk_ref, wv_ref, e_ref, q_ref, k_ref, v_ref = refs
    zm = zm_ref[0].astype(F32)
    nq = _rms(zm[:, :MLA_Q_RANK], qn_ref[...]).astype(BF)
    nkv = _rms(zm[:, MLA_Q_RANK:], kvn_ref[...]).astype(BF)
    sm = sm_ref[0]
    scale = MLA_QK ** -0.5
    q = _dot(nq, wq_ref[...])
    k = _dot(nkv, wk_ref[...]) + _dot(sm, e_ref[...])
    if rope:
        cos = cos_ref[...]
        sin = sin_ref[...]
        q = q * cos + _dot(nq, wqs_ref[...]) * sin
        k = k * cos + _dot(sm, es_ref[...]) * sin
    q_ref[0] = (q * scale).astype(BF)
    k_ref[0] = k.astype(BF)
    v_ref[0] = _dot(nkv, wv_ref[...]).astype(BF)


def _rope_swap(e):
    return np.where(e % 16 < 8, e + 8, e - 8)


def _mla_weights(w_uq, w_ukv):
    depth = w_uq.shape[0]
    quarter = MLA_ROPE_DIM // 4
    q4 = w_uq.reshape(depth, MLA_Q_RANK, MLA_HEADS, MLA_QK)
    nope, rope = q4[..., :MLA_NOPE_DIM], q4[..., MLA_NOPE_DIM:]
    rope_sw = jnp.flip(rope.reshape(depth, MLA_Q_RANK, MLA_HEADS, 2, 2, quarter), axis=4).reshape(rope.shape)
    zpad = jnp.zeros(rope.shape, w_uq.dtype)
    wq = jnp.concatenate([nope, rope, zpad], axis=-1).reshape(depth, MLA_Q_RANK, MLA_W)
    wqs = jnp.concatenate([jnp.zeros(nope.shape, w_uq.dtype), rope_sw, zpad], axis=-1).reshape(depth, MLA_Q_RANK, MLA_W)
    kv4 = w_ukv.reshape(depth, MLA_KV_RANK, MLA_HEADS, MLA_NOPE_DIM + MLA_V_DIM)
    k_nope, v = kv4[..., :MLA_NOPE_DIM], kv4[..., MLA_NOPE_DIM:]
    wk = jnp.concatenate([k_nope, jnp.zeros(k_nope.shape, w_ukv.dtype)], axis=-1).reshape(depth, MLA_KV_RANK, MLA_W)
    wv = v.reshape(depth, MLA_KV_RANK, MLA_HEADS * MLA_V_DIM)
    e = np.arange(MLA_ROPE_DIM)
    em = np.zeros((LANES, MLA_W), np.float32)
    ems = np.zeros((LANES, MLA_W), np.float32)
    for h in range(MLA_HEADS):
        em[SM_KR + e, h * LANES + MLA_NOPE_DIM + e] = 1.0
        ems[SM_KR + _rope_swap(e), h * LANES + MLA_NOPE_DIM + e] = 1.0
    return (wq.astype(BF), wqs.astype(BF), wk.astype(BF), wv.astype(BF), jnp.asarray(em, BF), jnp.asarray(ems, BF))


def _rope_tables(T):
    t = jnp.arange(T)
    quarter = MLA_ROPE_DIM // 4
    inv_freq = ROPE_THETA ** (-jnp.arange(quarter, dtype=F32) / quarter)
    ang_r = (t // GRID_W).astype(F32)[:, None] * inv_freq
    ang_c = (t % GRID_W).astype(F32)[:, None] * inv_freq
    cr, sr, ccol, scol = jnp.cos(ang_r), jnp.sin(ang_r), jnp.cos(ang_c), jnp.sin(ang_c)
    ones = jnp.ones((T, MLA_NOPE_DIM), F32)
    pad1 = jnp.ones((T, LANES - MLA_QK), F32)
    cos_h = jnp.concatenate([ones, cr, cr, ccol, ccol, pad1], axis=1)
    sin_h = jnp.concatenate([0.0 * ones, -sr, sr, -scol, scol, 0.0 * pad1], axis=1)
    return jnp.tile(cos_h, (1, MLA_HEADS)), jnp.tile(sin_h, (1, MLA_HEADS))


def _mla_proj(z, qn3, kvn3, weights, rope_tabs, l, tm=256):
    B, T, _ = z.shape
    wq, wqs, wk, wv, em, ems = weights
    rope = rope_tabs is not None
    zm_spec = pl.BlockSpec((1, tm, 384), lambda b, i: (b, i, Z_MLA // 384))
    sm_spec = pl.BlockSpec((1, tm, LANES), lambda b, i: (b, i, Z_SM // LANES))
    if rope:
        tab_spec = pl.BlockSpec((tm, MLA_W), lambda b, i: (i, 0))
        ins = [z, z, qn3, kvn3, wq, wqs, wk, wv, em, ems, rope_tabs[0], rope_tabs[1]]
        in_specs = ([zm_spec, sm_spec] + [_layer(a, l) for a in ins[2:8]] + [_whole(em), _whole(ems)]
                    + [tab_spec, tab_spec])
    else:
        ins = [z, z, qn3, kvn3, wq, wk, wv, em]
        in_specs = [zm_spec, sm_spec] + [_layer(a, l) for a in ins[2:7]] + [_whole(em)]
    return pl.pallas_call(
        functools.partial(_mla_proj_kernel, rope=rope),
        grid=(B, T // tm), in_specs=in_specs,
        out_specs=[pl.BlockSpec((1, tm, MLA_W), lambda b, i: (b, i, 0)),
                   pl.BlockSpec((1, tm, MLA_W), lambda b, i: (b, i, 0)),
                   pl.BlockSpec((1, tm, 256), lambda b, i: (b, i, 0))],
        out_shape=[jax.ShapeDtypeStruct((B, T, MLA_W), BF), jax.ShapeDtypeStruct((B, T, MLA_W), BF),
                   jax.ShapeDtypeStruct((B, T, 256), BF)],
        compiler_params=_cparams("arbitrary", "arbitrary"),
        name="mla_proj",
    )(*ins)


def _mla_attn_kernel(*refs, with_latent):
    if with_latent:
        q_ref, kc_ref, vc_ref, kl_ref, vl_ref, o_ref = refs
    else:
        q_ref, kc_ref, vc_ref, o_ref = refs
    masks = _lane_masks()
    for hp in range(MLA_HEADS // 2):
        vsl = slice(hp * LANES, (hp + 1) * LANES)
        outs = []
        for hh in range(2):
            h = 2 * hp + hh
            hsl = slice(h * LANES, (h + 1) * LANES)
            q = q_ref[0, :, hsl]
            s_list = [_dot_nt(q, kc_ref[0, :, hsl])]
            v_list = [vc_ref[0, :, vsl]]
            if with_latent:
                s_list.append(_dot_nt(q, kl_ref[0, :, hsl]))
                v_list.append(vl_ref[0, :, vsl])
            outs.append(_softmax_pv(s_list, v_list))
        o_ref[0, :, vsl] = jnp.where(masks[0], outs[0], outs[1]).astype(BF)


def _mla_attn(q, k_c, v_c, k_l=None, v_l=None, tq=256):
    B, Tq, W = q.shape
    L = k_c.shape[1]
    with_latent = k_l is not None
    in_specs = [pl.BlockSpec((1, tq, W), lambda b, i: (b, i, 0)),
                pl.BlockSpec((1, L, W), lambda b, i: (b, 0, 0)),
                pl.BlockSpec((1, L, 256), lambda b, i: (b, 0, 0))]
    args = [q, k_c, v_c]
    if with_latent:
        T = k_l.shape[1]
        in_specs += [pl.BlockSpec((1, T, W), lambda b, i: (b, 0, 0)),
                     pl.BlockSpec((1, T, 256), lambda b, i: (b, 0, 0))]
        args += [k_l, v_l]
    return pl.pallas_call(
        functools.partial(_mla_attn_kernel, with_latent=with_latent),
        grid=(B, Tq // tq), in_specs=in_specs,
        out_specs=pl.BlockSpec((1, tq, 256), lambda b, i: (b, i, 0)),
        out_shape=jax.ShapeDtypeStruct((B, Tq, 256), BF),
        compiler_params=_cparams("arbitrary", "arbitrary"),
        name="mla_attn",
    )(*args)


def _gla_superchunk(q, k, v, sm, w_a, b_a, tri, same, st_ref, reverse):
    n_chunk = SUPER // GLA_CHUNK
    g = jax.nn.log_sigmoid(_dot(sm, w_a) + b_a) * (1.0 / GLA_TAU)
    parts = _split3(g)
    b = _dot(tri, parts[0]) + _dot(tri, parts[1]) + _dot(tri, parts[2])
    tot = _dot(same, parts[0]) + _dot(same, parts[1]) + _dot(same, parts[2])
    qd_all = q.astype(F32) * (GLA_DK ** -0.5) * jnp.exp(b)
    kd_all = k.astype(F32) * jnp.exp(-b)
    ke_all = k.astype(F32) * jnp.exp(tot - b)
    dec_all = jnp.exp(tot)
    keep = tri > 0
    masks = _lane_masks()
    rowblk = lax.broadcasted_iota(jnp.int32, (SUPER, LANES), 0) // GLA_CHUNK
    order = range(n_chunk - 1, -1, -1) if reverse else range(n_chunk)
    outs = []
    for hp in range(GLA_HEADS // 2):
        sl = slice(hp * LANES, (hp + 1) * LANES)
        qd = qd_all[:, sl]
        kd = kd_all[:, sl].astype(BF)
        ke = ke_all[:, sl].astype(BF)
        dec = dec_all[:, sl]
        zero = jnp.zeros_like(ke)
        ke_bd = jnp.concatenate([jnp.where(rowblk == i, ke, zero) for i in range(n_chunk)], axis=1)
        for hh in range(2):
            h = 2 * hp + hh
            qm = jnp.where(masks[hh], qd, 0.0).astype(BF)
            vh = v[:, h * GLA_DV:(h + 1) * GLA_DV]
            a = jnp.where(keep, _dot_nt(qm, kd), 0.0).astype(BF)
            o = _dot(a, vh)
            kv_t = _dot_tn(vh, ke_bd)
            s = st_ref[h]
            before = [None] * n_chunk
            for i in order:
                before[i] = s.astype(BF)
                s = s * dec[i * GLA_CHUNK:i * GLA_CHUNK + 1, :] + kv_t[:, i * LANES:(i + 1) * LANES]
            st_ref[h] = s
            qm_bd = jnp.concatenate([jnp.where(rowblk == i, qm, jnp.zeros_like(qm)) for i in range(n_chunk)], axis=1)
            o = o + _dot_nt(qm_bd, jnp.concatenate(before, axis=1))
            outs.append(o)
    return jnp.concatenate(outs, axis=1)


def _gla_finish(o, og, onorm):
    ys = []
    for h in range(GLA_HEADS):
        ys.append(_rms(o[:, h * GLA_DV:(h + 1) * GLA_DV], onorm))
    y = jnp.concatenate(ys, axis=1)
    ogf = og.astype(F32)
    return (y * (ogf * jax.nn.sigmoid(ogf))).astype(BF)


def _gla_kernel(*refs, n_lat, ctx_out):
    (ql_ref, kl_ref, vl_ref, sml_ref, ogl_ref, qc_ref, kc_ref, vc_ref, smc_ref, ogc_ref,
     waf_ref, baf_ref, wab_ref, bab_ref, onorm_ref, trif_ref, trib_ref, same_ref) = refs[:18]
    if ctx_out:
        ol_ref, oc_ref, ofl_ref, ofc_ref, st_ref = refs[18:]
    else:
        ol_ref, ofl_ref, st_ref = refs[18:]
    onorm = onorm_ref[...]

    def run(direction):
        reverse = direction == 1
        w_a = (wab_ref if reverse else waf_ref)[...]
        b_a = (bab_ref if reverse else baf_ref)[...]
        tri = (trib_ref if reverse else trif_ref)[...]
        same = same_ref[...]
        st_ref[...] = jnp.zeros_like(st_ref)
        o_c = _gla_superchunk(qc_ref[0], kc_ref[0], vc_ref[0], smc_ref[0], w_a, b_a, tri, same, st_ref, reverse)
        if ctx_out:
            if reverse:
                oc_ref[0] = _gla_finish(ofc_ref[...] + o_c, ogc_ref[0], onorm)
            else:
                ofc_ref[...] = o_c

        def body(j, carry):
            sc = (n_lat - 1 - j) if reverse else j
            r0 = pl.multiple_of(sc * SUPER, SUPER)
            rows = pl.ds(r0, SUPER)
            o = _gla_superchunk(ql_ref[0, rows, :], kl_ref[0, rows, :], vl_ref[0, rows, :], sml_ref[0, rows, :],
                                w_a, b_a, tri, same, st_ref, reverse)
            if reverse:
                ol_ref[0, rows, :] = _gla_finish(ofl_ref[rows, :] + o, ogl_ref[0, rows, :], onorm)
            else:
                ofl_ref[rows, :] = o
            return carry

        lax.fori_loop(0, n_lat, body, 0)

    run(0)
    run(1)


def _gla_consts():
    r = np.arange(SUPER)
    same_np = (r[:, None] // GLA_CHUNK) == (r[None, :] // GLA_CHUNK)
    return (jnp.asarray(same_np & (r[None, :] <= r[:, None]), BF),
            jnp.asarray(same_np & (r[None, :] >= r[:, None]), BF), jnp.asarray(same_np, BF))


def _gla(z_l, z_c, params, l, ctx_out):
    B, T, _ = z_l.shape
    L = z_c.shape[1]
    assert L == SUPER and T % SUPER == 0
    kw = GLA_HEADS * GLA_DK
    vw = GLA_HEADS * GLA_DV
    consts = _gla_consts()

    def zspec(n, width, col):
        return pl.BlockSpec((1, n, width), lambda b: (b, 0, col // width))

    in_specs = ([zspec(T, kw, Z_GQ), zspec(T, kw, Z_GK), zspec(T, vw, Z_GV), zspec(T, LANES, Z_SM), zspec(T, vw, Z_OG),
                 zspec(L, kw, Z_GQ), zspec(L, kw, Z_GK), zspec(L, vw, Z_GV), zspec(L, LANES, Z_SM), zspec(L, vw, Z_OG)]
                + [_layer(a, l) for a in params] + [_whole(a) for a in consts])
    ol_spec = pl.BlockSpec((1, T, vw), lambda b: (b, 0, 0))
    ol_shape = jax.ShapeDtypeStruct((B, T, vw), BF)
    scratch = [pltpu.VMEM((T, vw), F32)]
    if ctx_out:
        out_specs = [ol_spec, pl.BlockSpec((1, L, vw), lambda b: (b, 0, 0))]
        out_shape = [ol_shape, jax.ShapeDtypeStruct((B, L, vw), BF)]
        scratch.append(pltpu.VMEM((L, vw), F32))
    else:
        out_specs, out_shape = ol_spec, ol_shape
    scratch.append(pltpu.VMEM((GLA_HEADS, GLA_DV, LANES), F32))
    res = pl.pallas_call(
        functools.partial(_gla_kernel, n_lat=T // SUPER, ctx_out=ctx_out),
        grid=(B,), in_specs=in_specs, out_specs=out_specs, out_shape=out_shape,
        scratch_shapes=scratch,
        compiler_params=_cparams("arbitrary"),
        name="gla",
    )(*([z_l] * 5 + [z_c] * 5 + list(params) + list(consts)))
    return (res[0], res[1]) if ctx_out else (res, None)


MERGE_SUB = 256


def _merge_kernel(oa_ref, ob_ref, oc_ref, ma_ref, mb_ref, mc_ref, x_ref, gate_ref, shift_ref, scale_ref, g_ref,
                  wa_ref, wb_ref, wc_ref, wo_ref, wr_ref, xo_ref, h_ref, aff_ref, y_ref):
    tm = x_ref.shape[1]
    gate = gate_ref[0]
    for r0 in range(0, tm, MERGE_SUB):
        rows = slice(r0, r0 + MERGE_SUB)

        def branch(o_ref, m_ref, w_ref, n0):
            m = m_ref[0, rows, n0:n0 + 256].astype(F32)
            return jax.nn.sigmoid(m) * _dot(o_ref[0, rows, :], w_ref[:, n0:n0 + 256])

        for n0 in range(0, D_MODEL, 256):
            y = (branch(oa_ref, ma_ref, wa_ref, n0) + branch(ob_ref, mb_ref, wb_ref, n0)
                 + branch(oc_ref, mc_ref, wc_ref, n0))
            y_ref[rows, n0:n0 + 256] = y.astype(BF)
        for n0 in range(0, D_MODEL, 256):
            xo_ref[0, rows, n0:n0 + 256] = (x_ref[0, rows, n0:n0 + 256]
                                            + gate[:, n0:n0 + 256] * _dot(y_ref[rows, :], wo_ref[:, n0:n0 + 256]))
        h = _rms(xo_ref[0, rows, :], g_ref[...]) * (1.0 + scale_ref[0]) + shift_ref[0]
        h_ref[0, rows, :] = h.astype(BF)
        logits = lax.dot_general(wr_ref[...], h, (((1,), (1,)), ((), ())), precision=lax.Precision.HIGHEST,
                                 preferred_element_type=F32)
        e = jnp.exp(logits - logits.max(axis=0, keepdims=True))
        aff_ref[0, :, rows] = e / e.sum(axis=0, keepdims=True)


def _merge(oa, ob, oc, z, x, mods3, mod_row, norm2_3, weights, l, tm):
    B, T, D = x.shape
    wa, wb, wc, wo, wr_t = weights

    def tok(width, col=0):
        return pl.BlockSpec((1, tm, width), lambda b, i: (b, i, col // width))

    def mod_spec(k):
        return pl.BlockSpec((1, 1, D), lambda b, i: (mod_row(b) * 6 + k, 0, 0))

    in_specs = [tok(256), tok(256), tok(512), tok(D, Z_MA), tok(D, Z_MB), tok(D, Z_MC), tok(D),
                mod_spec(2), mod_spec(3), mod_spec(4), _layer(norm2_3, l)] + [_layer(w, l) for w in weights]
    return pl.pallas_call(
        _merge_kernel,
        grid=(B, T // tm), in_specs=in_specs,
        out_specs=[tok(D), tok(D), pl.BlockSpec((1, N_EXPERTS, tm), lambda b, i: (b, 0, i))],
        out_shape=[jax.ShapeDtypeStruct((B, T, D), F32), jax.ShapeDtypeStruct((B, T, D), BF),
                   jax.ShapeDtypeStruct((B, N_EXPERTS, T), F32)],
        scratch_shapes=[pltpu.VMEM((tm, D), BF)],
        compiler_params=_cparams("arbitrary", "arbitrary"),
        name="merge",
    )(oa, ob, oc, z, z, z, x, mods3, mods3, mods3, norm2_3, wa, wb, wc, wo, wr_t)


def _route_kernel(aff_ref, tri_ref, pos_ref, post_ref, *, cap, slot_stride):
    aff = aff_ref[0]
    E, T = aff.shape

    def search(i, thr):
        cand = thr | (jnp.int32(1) << (30 - i))
        cnt = jnp.sum((aff >= pltpu.bitcast(cand, F32)).astype(F32), axis=1, keepdims=True)
        return jnp.where(cnt >= cap, cand, thr)

    thr = lax.fori_loop(0, 31, search, jnp.zeros((E, 1), jnp.int32))
    thr_f = pltpu.bitcast(thr, F32)
    gt = aff > thr_f
    eq = aff == thr_f
    need = cap - jnp.sum(gt.astype(F32), axis=1, keepdims=True)
    tri = tri_ref[...]

    def excl_prefix(flags):
        run = jnp.zeros((E, 1), F32)
        blocks = []
        for j in range(T // LANES):
            blk = flags[:, j * LANES:(j + 1) * LANES].astype(F32)
            blocks.append(_dot(blk.astype(BF), tri) + run)
            run = run + jnp.sum(blk, axis=1, keepdims=True)
        return jnp.concatenate(blocks, axis=1)

    sel = gt | (eq & (excl_prefix(eq) < need))
    base = (slot_stride * pl.program_id(0)).astype(F32)
    pos = jnp.where(sel, excl_prefix(sel) + base, -1.0)
    pos_ref[0] = pos.astype(jnp.int32)
    post_ref[0] = jnp.concatenate([pos, jnp.full((LANES - E, T), -1.0, F32)], axis=0).T


def _route(aff_t, cap, slot_stride):
    B, E, T = aff_t.shape
    r = np.arange(LANES)
    tri = jnp.asarray(r[:, None] < r[None, :], BF)
    return pl.pallas_call(
        functools.partial(_route_kernel, cap=cap, slot_stride=slot_stride),
        grid=(B,),
        in_specs=[pl.BlockSpec((1, E, T), lambda b: (b, 0, 0)), pl.BlockSpec((LANES, LANES), lambda b: (0, 0))],
        out_specs=[pl.BlockSpec((1, E, T), lambda b: (b, 0, 0)), pl.BlockSpec((1, T, LANES), lambda b: (b, 0, 0))],
        out_shape=[jax.ShapeDtypeStruct((B, E, T), jnp.int32), jax.ShapeDtypeStruct((B, T, LANES), F32)],
        compiler_params=_cparams("arbitrary"),
        name="route",
    )(aff_t, tri)


def _moe_kernel(h_ref, pos_ref, post_ref, aff_ref, wg_ref, wu_ref, wd_ref, o_ref, *, n_slots):
    e = pl.program_id(1)
    pos = pos_ref[0, pl.ds(e, 1), :]
    aff = aff_ref[0, pl.ds(e, 1), :]
    T = pos.shape[1]
    hit = lax.broadcasted_iota(jnp.int32, (n_slots, T), 0) == pos
    w_slot = jnp.sum(jnp.where(hit, aff, 0.0), axis=1, keepdims=True)
    xg = _dot(hit.astype(BF), h_ref[0]).astype(BF)
    gt = _dot(xg, wg_ref[...])
    up = _dot(xg, wu_ref[...])
    hid = (gt * jax.nn.sigmoid(gt) * up * w_slot).astype(BF)
    y = _dot(hid, wd_ref[...]).astype(BF)

    @pl.when(e == 0)
    def _():
        o_ref[...] = jnp.zeros_like(o_ref)

    lane = lax.broadcasted_iota(jnp.int32, (1, LANES), 1)
    slot_row = lax.broadcasted_iota(jnp.int32, (1, n_slots), 1).astype(F32)
    for t0 in range(0, T, 256):
        pos_col = jnp.sum(jnp.where(lane == e, post_ref[0, t0:t0 + 256, :], 0.0), axis=1, keepdims=True)
        o_ref[0, t0:t0 + 256, :] += _dot((pos_col == slot_row).astype(BF), y)


def _moe(h, pos, pos_t, aff_t, weights, l, n_slots):
    B, T, D = h.shape
    E = pos.shape[1]
    wg, wu, wd = weights

    def wspec(w):
        return pl.BlockSpec((None, None) + w.shape[2:], lambda b, e: (l, e, 0, 0))

    return pl.pallas_call(
        functools.partial(_moe_kernel, n_slots=n_slots),
        grid=(B, E),
        in_specs=[pl.BlockSpec((1, T, D), lambda b, e: (b, 0, 0)),
                  pl.BlockSpec((1, E, T), lambda b, e: (b, 0, 0)),
                  pl.BlockSpec((1, T, LANES), lambda b, e: (b, 0, 0)),
                  pl.BlockSpec((1, E, T), lambda b, e: (b, 0, 0)),
                  wspec(wg), wspec(wu), wspec(wd)],
        out_specs=pl.BlockSpec((1, T, D), lambda b, e: (b, 0, 0)),
        out_shape=jax.ShapeDtypeStruct((B, T, D), F32),
        compiler_params=_cparams("arbitrary", "arbitrary"),
        name="moe",
    )(h, pos, pos_t, aff_t, wg, wu, wd)


def _final_kernel(x_ref, moe_ref, gate_ref, g_ref, o_ref):
    x = x_ref[0] + gate_ref[0] * moe_ref[0]
    o_ref[0] = _rms(x, g_ref[...])


def _final(x, moe, mods3, g, tm=256):
    B, T, D = x.shape
    tok = pl.BlockSpec((1, tm, D), lambda b, i: (b, i, 0))
    return pl.pallas_call(
        _final_kernel,
        grid=(B, T // tm),
        in_specs=[tok, tok, pl.BlockSpec((1, 1, D), lambda b, i: (b * 6 + 5, 0, 0)),
                  pl.BlockSpec((1, D), lambda b, i: (0, 0))],
        out_specs=tok,
        out_shape=jax.ShapeDtypeStruct((B, T, D), F32),
        compiler_params=_cparams("arbitrary", "arbitrary"),
        name="final_norm",
    )(x, moe, mods3, g.reshape(1, D))


def _reorder_w_in(w_in):
    parts, off = [], 0
    for s in IN_SIZES:
        parts.append(w_in[..., off:off + s])
        off += s
    qa, ka, va, dq, dkv, kr, gq, gk, gv, a_f, a_b, og, m_a, m_b, m_c = parts
    pad = jnp.zeros(w_in.shape[:-1] + (LANES - MLA_ROPE_DIM - 2 * GLA_GATE_RANK,), w_in.dtype)
    return jnp.concatenate([m_a, m_b, m_c, gv, og, qa, ka, va, gq, gk, dq, dkv, kr, a_f, a_b, pad], axis=-1).astype(BF)


def kernel(x, c, ctx, c_ctx, ada_w, ada_b, norm1, norm2, w_in, na_rpb, mla_q_norm, mla_w_uq, mla_kv_norm, mla_w_ukv,
           gla_w_af, gla_b_af, gla_w_ab, gla_b_ab, gla_o_norm, w_br_a, w_br_b, w_br_c, w_out,
           moe_router, moe_w_gate, moe_w_up, moe_w_down, final_norm):
    B, T, D = x.shape
    L = ctx.shape[1]
    depth = ada_w.shape[0]
    assert B < MOD_ROWS
    cap_l = EC_CAPACITY * T // N_EXPERTS
    cap_c = EC_CAPACITY * L // N_EXPERTS
    ctx_row = B
    cc = jnp.concatenate([c, c_ctx[None], jnp.zeros((MOD_ROWS - B - 1, D), F32)], axis=0)

    def lat_row(b):
        return b

    def ctx_row_fn(b):
        return ctx_row

    def rows3(a):
        return a.reshape(depth, 1, -1)

    ada_b3, norm1_3, norm2_3 = rows3(ada_b), rows3(norm1), rows3(norm2)
    w_in_r = _reorder_w_in(w_in)
    na_bias = _na_bias(na_rpb, T // GRID_W)
    mla_w = _mla_weights(mla_w_uq, mla_w_ukv)
    qn3, kvn3 = rows3(mla_q_norm), rows3(mla_kv_norm)
    rope_tabs = _rope_tables(T)
    gate_pad = lambda w, off: jnp.pad(w, ((0, 0), (off, LANES - off - GLA_GATE_RANK), (0, 0))).astype(BF)
    gla_params = (gate_pad(gla_w_af, SM_AF), rows3(gla_b_af), gate_pad(gla_w_ab, SM_AB), rows3(gla_b_ab),
                  rows3(gla_o_norm))
    merge_w = (w_br_a.astype(BF), w_br_b.astype(BF), w_br_c.astype(BF), w_out.astype(BF),
               jnp.swapaxes(moe_router, 1, 2))
    moe_w = (moe_w_gate.astype(BF), moe_w_up.astype(BF), moe_w_down.astype(BF))

    moe_l = moe_c = mods3 = None
    for l in range(depth):
        ctx_out = l < depth - 1
        prev_mods3 = mods3
        mods3 = _ada(cc, ada_w, ada_b3, l).reshape(MOD_ROWS * 6, 1, D)
        z_l, x = _in_proj(x, moe_l, prev_mods3, mods3, lat_row, norm1_3, w_in_r, l)
        z_c, ctx = _in_proj(ctx, moe_c, prev_mods3, mods3, ctx_row_fn, norm1_3, w_in_r, l)

        oa_l, oa_c = _na_attention(z_l, z_c, na_bias, l, ctx_out)

        qb_l, kb_l, vb_l = _mla_proj(z_l, qn3, kvn3, mla_w, rope_tabs, l)
        qb_c, kb_c, vb_c = _mla_proj(z_c, qn3, kvn3, mla_w, None, l)
        ob_l = _mla_attn(qb_l, kb_c, vb_c, kb_l, vb_l)

        oc_l, oc_c = _gla(z_l, z_c, gla_params, l, ctx_out)

        x, h_l, aff_l = _merge(oa_l, ob_l, oc_l, z_l, x, mods3, lat_row, norm2_3, merge_w, l, tm=512)
        pos_l, post_l = _route(aff_l, cap_l, 0)
        moe_l = _moe(h_l, pos_l, post_l, aff_l, moe_w, l, cap_l)

        if ctx_out:
            ob_c = _mla_attn(qb_c, kb_c, vb_c)
            ctx, h_c, aff_c = _merge(oa_c, ob_c, oc_c, z_c, ctx, mods3, ctx_row_fn, norm2_3, merge_w, l, tm=256)
            pos_c, post_c = _route(aff_c, cap_c, cap_c)
            flat = lambda a: jnp.swapaxes(a, 0, 1).reshape(1, N_EXPERTS, B * L)
            moe_c = _moe(h_c.reshape(1, B * L, D), flat(pos_c), post_c.reshape(1, B * L, LANES), flat(aff_c),
                         moe_w, l, B * cap_c).reshape(B, L, D)
    return _final(x, moe_l, mods3, final_norm)
```

```python
import functools

import numpy as np
import jax
import jax.numpy as jnp
from jax import lax
from jax.experimental import pallas as pl
from jax.experimental.pallas import tpu as pltpu

BF = jnp.bfloat16
F32 = jnp.float32

D_MODEL = 1024
GRID_W = 64
EPS = 1e-6
NA_HEADS = 4
NA_HEAD_DIM = 64
NA_WIN_H = 8
NA_WIN_W = 16
MLA_HEADS = 4
MLA_Q_RANK = 256
MLA_KV_RANK = 128
MLA_NOPE_DIM = 64
MLA_ROPE_DIM = 32
MLA_V_DIM = 64
ROPE_THETA = 10000.0
GLA_HEADS = 4
GLA_DK = 64
GLA_DV = 128
GLA_GATE_RANK = 16
GLA_TAU = 16.0
GLA_CHUNK = 64
N_EXPERTS = 16
EXPERT_FF = 1024
EC_CAPACITY = 2
IN_SIZES = (256, 256, 256, 256, 128, 32, 256, 256, 512, 16, 16, 512, 1024, 1024, 1024)

LANES = 128
VMEM_LIMIT = 56 * 1024 * 1024
MOD_ROWS = 16

Z_MA, Z_MB, Z_MC = 0, 1024, 2048
Z_GV, Z_OG = 3072, 3584
Z_QA, Z_KA, Z_VA = 4096, 4352, 4608
Z_GQ, Z_GK = 4864, 5120
Z_MLA = 5376
Z_SM = 5760
Z_COLS = 5888
SM_KR, SM_AF, SM_AB = 0, 32, 48

SUPER = 256
NEG = -1e30


def _cparams(*sem):
    return pltpu.CompilerParams(dimension_semantics=sem, vmem_limit_bytes=VMEM_LIMIT)


def _layer(a, l):
    return pl.BlockSpec((None,) + a.shape[1:], lambda *_: (l,) + (0,) * (a.ndim - 1))


def _whole(a):
    return pl.BlockSpec(a.shape, lambda *_: (0,) * a.ndim)


def _dot(a, b):
    return jnp.dot(a, b, preferred_element_type=F32)


def _dot_nt(a, b):
    return lax.dot_general(a, b, (((1,), (1,)), ((), ())), preferred_element_type=F32)


def _dot_tn(a, b):
    return lax.dot_general(a, b, (((0,), (0,)), ((), ())), preferred_element_type=F32)


def _rms(x, g):
    return x * lax.rsqrt(jnp.mean(x * x, axis=-1, keepdims=True) + EPS) * g


def _split3(x):
    hi = x.astype(BF)
    r = x - hi.astype(F32)
    mid = r.astype(BF)
    lo = (r - mid.astype(F32)).astype(BF)
    return hi, mid, lo


def _ada_kernel(c_ref, w_ref, b_ref, o_ref):
    c = c_ref[...]
    a = (c * jax.nn.sigmoid(c)).astype(BF)
    o_ref[...] = _dot(a, w_ref[...].astype(BF)) + b_ref[...]


def _ada(cc, ada_w, ada_b3, l):
    n = ada_w.shape[2]
    tn = 1536
    return pl.pallas_call(
        _ada_kernel,
        grid=(n // tn,),
        in_specs=[pl.BlockSpec((MOD_ROWS, D_MODEL), lambda j: (0, 0)),
                  pl.BlockSpec((None, D_MODEL, tn), lambda j: (l, 0, j)),
                  pl.BlockSpec((None, 1, tn), lambda j: (l, 0, j))],
        out_specs=pl.BlockSpec((MOD_ROWS, tn), lambda j: (0, j)),
        out_shape=jax.ShapeDtypeStruct((MOD_ROWS, n), F32),
        compiler_params=_cparams("arbitrary"),
        name="ada_mod",
    )(cc, ada_w, ada_b3)


def _in_proj_kernel(*refs, with_moe):
    if with_moe:
        x_ref, moe_ref, gate_ref, g_ref, shift_ref, scale_ref, w_ref, o_ref, xo_ref, h_ref = refs
        x = x_ref[0] + gate_ref[0] * moe_ref[0]
        xo_ref[0] = x
    else:
        x_ref, g_ref, shift_ref, scale_ref, w_ref, o_ref, h_ref = refs
        x = x_ref[0]
    h = _rms(x, g_ref[...]) * (1.0 + scale_ref[0]) + shift_ref[0]
    h_ref[...] = h.astype(BF)
    for n0 in range(0, Z_COLS, 256):
        o_ref[0, :, n0:n0 + 256] = _dot(h_ref[...], w_ref[:, n0:n0 + 256]).astype(BF)


def _in_proj(x, moe, prev_mods3, mods3, mod_row, norm1_3, w, l, tm=256):
    B, T, D = x.shape
    with_moe = moe is not None

    def mod_spec(k):
        return pl.BlockSpec((1, 1, D), lambda b, i: (mod_row(b) * 6 + k, 0, 0))

    x_spec = pl.BlockSpec((1, tm, D), lambda b, i: (b, i, 0))
    in_specs = [x_spec]
    args = [x]
    if with_moe:
        in_specs += [x_spec, mod_spec(5)]
        args += [moe, prev_mods3]
    in_specs += [_layer(norm1_3, l), mod_spec(0), mod_spec(1), _layer(w, l)]
    args += [norm1_3, mods3, mods3, w]
    z_spec = pl.BlockSpec((1, tm, Z_COLS), lambda b, i: (b, i, 0))
    z_shape = jax.ShapeDtypeStruct((B, T, Z_COLS), BF)
    if with_moe:
        out_specs, out_shape = [z_spec, x_spec], [z_shape, jax.ShapeDtypeStruct((B, T, D), F32)]
    else:
        out_specs, out_shape = z_spec, z_shape
    res = pl.pallas_call(
        functools.partial(_in_proj_kernel, with_moe=with_moe),
        grid=(B, T // tm),
        in_specs=in_specs, out_specs=out_specs, out_shape=out_shape,
        scratch_shapes=[pltpu.VMEM((tm, D), BF)],
        compiler_params=_cparams("arbitrary", "arbitrary"),
        name="in_proj",
    )(*args)
    return (res[0], res[1]) if with_moe else (res, x)


def _softmax_pv_many(heads):
    maxes = []
    for s_list, _ in heads:
        m = s_list[0].max(axis=-1, keepdims=True)
        for s in s_list[1:]:
            m = jnp.maximum(m, s.max(axis=-1, keepdims=True))
        maxes.append(m)
    probs = [[jnp.exp(s - m) for s in s_list] for (s_list, _), m in zip(heads, maxes)]
    outs = []
    for (_, v_list), p_list in zip(heads, probs):
        l = p_list[0].sum(axis=-1, keepdims=True)
        o = _dot(p_list[0].astype(BF), v_list[0])
        for p, v in zip(p_list[1:], v_list[1:]):
            l = l + p.sum(axis=-1, keepdims=True)
            o = o + _dot(p.astype(BF), v)
        outs.append(o / l)
    return outs


def _lane_masks():
    lane = lax.broadcasted_iota(jnp.int32, (1, LANES), 1)
    return lane < 64, lane >= 64


NA_ROWS_PER_STEP = 2
NA_BAND = 10
NA_CASE_ROWS = (0, 2, 4, 28, 30)


def _na_bias(rpb, rows):
    depth = rpb.shape[0]
    qc = np.arange(GRID_W)
    kj = np.arange(GRID_W)
    cs = np.clip(qc - NA_WIN_W // 2, 0, GRID_W - NA_WIN_W)
    col_ok = (kj[None, :] >= cs[:, None]) & (kj[None, :] < cs[:, None] + NA_WIN_W)
    co = kj[None, :] - qc[:, None] + (NA_WIN_W - 1)
    onehot = np.zeros((2 * NA_WIN_W - 1, GRID_W, GRID_W), np.float32)
    qi, ki = np.nonzero(col_ok)
    onehot[co[qi, ki], qi, ki] = 1.0
    toep = jnp.einsum("lhrc,cp->lhrp", rpb.astype(F32), jnp.asarray(onehot.reshape(2 * NA_WIN_W - 1, -1)),
                      precision=lax.Precision.HIGHEST)
    toep = (toep.reshape(depth, NA_HEADS, 2 * NA_WIN_H - 1, GRID_W, GRID_W)
            + jnp.asarray(np.where(col_ok, 0.0, NEG), F32))
    neg = jnp.full((depth, NA_HEADS, GRID_W, GRID_W), NEG, F32)
    cases = []
    for r0 in NA_CASE_ROWS:
        bs = int(np.clip(r0 - NA_WIN_H // 2, 0, rows - NA_BAND))
        q_rows = []
        for qr in range(NA_ROWS_PER_STEP):
            r = r0 + qr
            rs = int(np.clip(r - NA_WIN_H // 2, 0, rows - NA_WIN_H))
            blocks = []
            for i in range(NA_BAND):
                krow = bs + i
                blocks.append(toep[:, :, krow - r + NA_WIN_H - 1] if rs <= krow < rs + NA_WIN_H else neg)
            q_rows.append(jnp.concatenate(blocks, axis=-1))
        cases.append(jnp.concatenate(q_rows, axis=-2))
    return jnp.stack(cases, axis=2)


def _na_kernel(*refs, rows, ctx_out):
    if ctx_out:
        ql_ref, kl_ref, vl_ref, qc_ref, kc_ref, vc_ref, bias_ref, ol_ref, oc_ref = refs
    else:
        ql_ref, kl_ref, vl_ref, kc_ref, vc_ref, bias_ref, ol_ref = refs
    scale = NA_HEAD_DIM ** -0.5
    masks = _lane_masks()
    nq = NA_ROWS_PER_STEP * GRID_W
    nk = NA_BAND * GRID_W

    def body(rp, carry):
        r0 = rp * NA_ROWS_PER_STEP
        bs = jnp.clip(r0 - NA_WIN_H // 2, 0, rows - NA_BAND)
        case = jnp.where(r0 < 4, r0 // 2, jnp.where(r0 > rows - 6, (r0 - (rows - 4)) // 2 + 3, 2))
        q0 = pl.multiple_of(r0 * GRID_W, nq)
        k0 = pl.multiple_of(bs * GRID_W, 2 * GRID_W)
        heads = []
        for hp in range(NA_HEADS // 2):
            sl = slice(hp * LANES, (hp + 1) * LANES)
            q = ql_ref[0, pl.ds(q0, nq), sl]
            kb = kl_ref[0, pl.ds(k0, nk), sl]
            vb = vl_ref[0, pl.ds(k0, nk), sl]
            kc = kc_ref[0, :, sl]
            vc = vc_ref[0, :, sl]
            for hh in range(2):
                qm = jnp.where(masks[hh], q, jnp.zeros_like(q))
                s_loc = _dot_nt(qm, kb) * scale + bias_ref[2 * hp + hh, case]
                s_ctx = _dot_nt(qm, kc) * scale
                heads.append(([s_loc, s_ctx], [vb, vc]))
        outs = _softmax_pv_many(heads)
        for hp in range(NA_HEADS // 2):
            sl = slice(hp * LANES, (hp + 1) * LANES)
            ol_ref[0, pl.ds(q0, nq), sl] = jnp.where(masks[0], outs[2 * hp], outs[2 * hp + 1]).astype(BF)
        return carry

    lax.fori_loop(0, rows // NA_ROWS_PER_STEP, body, 0)

    if ctx_out:
        heads = []
        for hp in range(NA_HEADS // 2):
            sl = slice(hp * LANES, (hp + 1) * LANES)
            q = qc_ref[0, :, sl]
            kc = kc_ref[0, :, sl]
            for hh in range(2):
                qm = jnp.where(masks[hh], q, jnp.zeros_like(q))
                heads.append(([_dot_nt(qm, kc) * scale], [vc_ref[0, :, sl]]))
        outs = _softmax_pv_many(heads)
        for hp in range(NA_HEADS // 2):
            sl = slice(hp * LANES, (hp + 1) * LANES)
            oc_ref[0, :, sl] = jnp.where(masks[0], outs[2 * hp], outs[2 * hp + 1]).astype(BF)


def _na_attention(z_l, z_c, bias, l, ctx_out):
    B, T, _ = z_l.shape
    L = z_c.shape[1]
    rows = T // GRID_W

    def zl(col):
        return pl.BlockSpec((1, T, 256), lambda b: (b, 0, col // 256))

    def zc(col):
        return pl.BlockSpec((1, L, 256), lambda b: (b, 0, col // 256))

    ol_spec = pl.BlockSpec((1, T, 256), lambda b: (b, 0, 0))
    ol_shape = jax.ShapeDtypeStruct((B, T, 256), BF)
    if ctx_out:
        in_specs = [zl(Z_QA), zl(Z_KA), zl(Z_VA), zc(Z_QA), zc(Z_KA), zc(Z_VA), _layer(bias, l)]
        args = [z_l, z_l, z_l, z_c, z_c, z_c, bias]
        out_specs = [ol_spec, pl.BlockSpec((1, L, 256), lambda b: (b, 0, 0))]
        out_shape = [ol_shape, jax.ShapeDtypeStruct((B, L, 256), BF)]
    else:
        in_specs = [zl(Z_QA), zl(Z_KA), zl(Z_VA), zc(Z_KA), zc(Z_VA), _layer(bias, l)]
        args = [z_l, z_l, z_l, z_c, z_c, bias]
        out_specs, out_shape = ol_spec, ol_shape
    res = pl.pallas_call(
        functools.partial(_na_kernel, rows=rows, ctx_out=ctx_out),
        grid=(B,), in_specs=in_specs, out_specs=out_specs, out_shape=out_shape,
        compiler_params=_cparams("arbitrary"),
        name="na_attn",
    )(*args)
    return (res[0], res[1]) if ctx_out else (res, None)


MLA_QK = MLA_NOPE_DIM + MLA_ROPE_DIM
MLA_W = MLA_HEADS * LANES


def _mla_proj_kernel(*refs, rope):
    if rope:
        (zm_ref, sm_ref, qn_ref, kvn_ref, wq_ref, wqs_ref, wk_ref, wv_ref, e_ref, es_ref, cos_ref, sin_ref,
         q_ref, k_ref, v_ref) = refs
    else:
        zm_ref, sm_ref, qn_ref, kvn_ref, wq_ref, wk_ref, wv_ref, e_ref, q_ref, k_ref, v_ref = refs
    zm = zm_ref[0].astype(F32)
    nq = _rms(zm[:, :MLA_Q_RANK], qn_ref[...]).astype(BF)
    nkv = _rms(zm[:, MLA_Q_RANK:], kvn_ref[...]).astype(BF)
    sm = sm_ref[0]
    scale = MLA_QK ** -0.5
    q = _dot(nq, wq_ref[...])
    k = _dot(nkv, wk_ref[...]) + _dot(sm, e_ref[...])
    if rope:
        cos = cos_ref[...]
        sin = sin_ref[...]
        q = q * cos + _dot(nq, wqs_ref[...]) * sin
        k = k * cos + _dot(sm, es_ref[...]) * sin
    q_ref[0] = (q * scale).astype(BF)
    k_ref[0] = k.astype(BF)
    v_ref[0] = _dot(nkv, wv_ref[...]).astype(BF)


def _rope_swap(e):
    return np.where(e % 16 < 8, e + 8, e - 8)


def _mla_weights(w_uq, w_ukv):
    depth = w_uq.shape[0]
    quarter = MLA_ROPE_DIM // 4
    q4 = w_uq.reshape(depth, MLA_Q_RANK, MLA_HEADS, MLA_QK)
    nope, rope = q4[..., :MLA_NOPE_DIM], q4[..., MLA_NOPE_DIM:]
    rope_sw = jnp.flip(rope.reshape(depth, MLA_Q_RANK, MLA_HEADS, 2, 2, quarter), axis=4).reshape(rope.shape)
    zpad = jnp.zeros(rope.shape, w_uq.dtype)
    wq = jnp.concatenate([nope, rope, zpad], axis=-1).reshape(depth, MLA_Q_RANK, MLA_W)
    wqs = jnp.concatenate([jnp.zeros(nope.shape, w_uq.dtype), rope_sw, zpad], axis=-1).reshape(depth, MLA_Q_RANK, MLA_W)
    kv4 = w_ukv.reshape(depth, MLA_KV_RANK, MLA_HEADS, MLA_NOPE_DIM + MLA_V_DIM)
    k_nope, v = kv4[..., :MLA_NOPE_DIM], kv4[..., MLA_NOPE_DIM:]
    wk = jnp.concatenate([k_nope, jnp.zeros(k_nope.shape, w_ukv.dtype)], axis=-1).reshape(depth, MLA_KV_RANK, MLA_W)
    wv = v.reshape(depth, MLA_KV_RANK, MLA_HEADS * MLA_V_DIM)
    e = np.arange(MLA_ROPE_DIM)
    em = np.zeros((LANES, MLA_W), np.float32)
    ems = np.zeros((LANES, MLA_W), np.float32)
    for h in range(MLA_HEADS):
        em[SM_KR + e, h * LANES + MLA_NOPE_DIM + e] = 1.0
        ems[SM_KR + _rope_swap(e), h * LANES + MLA_NOPE_DIM + e] = 1.0
    return (wq.astype(BF), wqs.astype(BF), wk.astype(BF), wv.astype(BF), jnp.asarray(em, BF), jnp.asarray(ems, BF))


def _rope_tables(T):
    t = jnp.arange(T)
    quarter = MLA_ROPE_DIM // 4
    inv_freq = ROPE_THETA ** (-jnp.arange(quarter, dtype=F32) / quarter)
    ang_r = (t // GRID_W).astype(F32)[:, None] * inv_freq
    ang_c = (t % GRID_W).astype(F32)[:, None] * inv_freq
    cr, sr, ccol, scol = jnp.cos(ang_r), jnp.sin(ang_r), jnp.cos(ang_c), jnp.sin(ang_c)
    ones = jnp.ones((T, MLA_NOPE_DIM), F32)
    pad1 = jnp.ones((T, LANES - MLA_QK), F32)
    cos_h = jnp.concatenate([ones, cr, cr, ccol, ccol, pad1], axis=1)
    sin_h = jnp.concatenate([0.0 * ones, -sr, sr, -scol, scol, 0.0 * pad1], axis=1)
    return jnp.tile(cos_h, (1, MLA_HEADS)), jnp.tile(sin_h, (1, MLA_HEADS))


def _mla_proj(z, qn3, kvn3, weights, rope_tabs, l, tm=256):
    B, T, _ = z.shape
    wq, wqs, wk, wv, em, ems = weights
    rope = rope_tabs is not None
    zm_spec = pl.BlockSpec((1, tm, 384), lambda b, i: (b, i, Z_MLA // 384))
    sm_spec = pl.BlockSpec((1, tm, LANES), lambda b, i: (b, i, Z_SM // LANES))
    if rope:
        tab_spec = pl.BlockSpec((tm, MLA_W), lambda b, i: (i, 0))
        ins = [z, z, qn3, kvn3, wq, wqs, wk, wv, em, ems, rope_tabs[0], rope_tabs[1]]
        in_specs = ([zm_spec, sm_spec] + [_layer(a, l) for a in ins[2:8]] + [_whole(em), _whole(ems)]
                    + [tab_spec, tab_spec])
    else:
        ins = [z, z, qn3, kvn3, wq, wk, wv, em]
        in_specs = [zm_spec, sm_spec] + [_layer(a, l) for a in ins[2:7]] + [_whole(em)]
    return pl.pallas_call(
        functools.partial(_mla_proj_kernel, rope=rope),
        grid=(B, T // tm), in_specs=in_specs,
        out_specs=[pl.BlockSpec((1, tm, MLA_W), lambda b, i: (b, i, 0)),
                   pl.BlockSpec((1, tm, MLA_W), lambda b, i: (b, i, 0)),
                   pl.BlockSpec((1, tm, 256), lambda b, i: (b, i, 0))],
        out_shape=[jax.ShapeDtypeStruct((B, T, MLA_W), BF), jax.ShapeDtypeStruct((B, T, MLA_W), BF),
                   jax.ShapeDtypeStruct((B, T, 256), BF)],
        compiler_params=_cparams("arbitrary", "arbitrary"),
        name="mla_proj",
    )(*ins)


def _mla_attn_kernel(*refs, with_latent):
    if with_latent:
        q_ref, kc_ref, vc_ref, kl_ref, vl_ref, o_ref = refs
    else:
        q_ref, kc_ref, vc_ref, o_ref = refs
    masks = _lane_masks()
    heads = []
    for h in range(MLA_HEADS):
        hsl = slice(h * LANES, (h + 1) * LANES)
        vsl = slice(h // 2 * LANES, (h // 2 + 1) * LANES)
        q = q_ref[0, :, hsl]
        s_list = [_dot_nt(q, kc_ref[0, :, hsl])]
        v_list = [vc_ref[0, :, vsl]]
        if with_latent:
            s_list.append(_dot_nt(q, kl_ref[0, :, hsl]))
            v_list.append(vl_ref[0, :, vsl])
        heads.append((s_list, v_list))
    outs = _softmax_pv_many(heads)
    for hp in range(MLA_HEADS // 2):
        vsl = slice(hp * LANES, (hp + 1) * LANES)
        o_ref[0, :, vsl] = jnp.where(masks[0], outs[2 * hp], outs[2 * hp + 1]).astype(BF)


def _mla_attn(q, k_c, v_c, k_l=None, v_l=None, tq=256):
    B, Tq, W = q.shape
    L = k_c.shape[1]
    with_latent = k_l is not None
    in_specs = [pl.BlockSpec((1, tq, W), lambda b, i: (b, i, 0)),
                pl.BlockSpec((1, L, W), lambda b, i: (b, 0, 0)),
                pl.BlockSpec((1, L, 256), lambda b, i: (b, 0, 0))]
    args = [q, k_c, v_c]
    if with_latent:
        T = k_l.shape[1]
        in_specs += [pl.BlockSpec((1, T, W), lambda b, i: (b, 0, 0)),
                     pl.BlockSpec((1, T, 256), lambda b, i: (b, 0, 0))]
        args += [k_l, v_l]
    return pl.pallas_call(
        functools.partial(_mla_attn_kernel, with_latent=with_latent),
        grid=(B, Tq // tq), in_specs=in_specs,
        out_specs=pl.BlockSpec((1, tq, 256), lambda b, i: (b, i, 0)),
        out_shape=jax.ShapeDtypeStruct((B, Tq, 256), BF),
        compiler_params=_cparams("arbitrary", "arbitrary"),
        name="mla_attn",
    )(*args)


def _gla_superchunks(jobs, st_ref):
    n_chunk = SUPER // GLA_CHUNK
    masks = _lane_masks()
    rowblk = lax.broadcasted_iota(jnp.int32, (SUPER, LANES), 0) // GLA_CHUNK
    own = ((lax.broadcasted_iota(jnp.int32, (2 * GLA_DV, LANES), 0) < GLA_DV)
           == (lax.broadcasted_iota(jnp.int32, (2 * GLA_DV, LANES), 1) < GLA_DK))
    gs = [jax.nn.log_sigmoid(_dot(sm, w_a) + b_a) * (1.0 / GLA_TAU) for (_, _, _, sm, w_a, b_a, _, _, _) in jobs]
    pre = []
    for (q, k, v, sm, w_a, b_a, tri, d, reverse), g in zip(jobs, gs):
        g_hi = g.astype(BF)
        g_lo = (g - g_hi.astype(F32)).astype(BF)
        b = _dot(tri, g_hi) + _dot(tri, g_lo)
        ends = [i * GLA_CHUNK if reverse else (i + 1) * GLA_CHUNK - 1 for i in range(n_chunk)]
        tot = jnp.concatenate([jnp.broadcast_to(b[r:r + 1, :], (GLA_CHUNK, b.shape[1])) for r in ends], axis=0)
        pre.append((q.astype(F32) * (GLA_DK ** -0.5) * jnp.exp(b), k.astype(F32) * jnp.exp(-b),
                    k.astype(F32) * jnp.exp(tot - b), jnp.exp(tot), tri > 0))
    outs = [[] for _ in jobs]
    for hp in range(GLA_HEADS // 2):
        sl = slice(hp * LANES, (hp + 1) * LANES)
        stage = []
        for (q, k, v, sm, w_a, b_a, tri, d, reverse), (qd_all, kd_all, ke_all, dec_all, keep) in zip(jobs, pre):
            qd = qd_all[:, sl]
            kd = kd_all[:, sl].astype(BF)
            ke = ke_all[:, sl].astype(BF)
            zero = jnp.zeros_like(ke)
            ke_bd = jnp.concatenate([jnp.where(rowblk == i, ke, zero) for i in range(n_chunk)], axis=1)
            v_pair = v[:, hp * 2 * GLA_DV:(hp + 1) * 2 * GLA_DV]
            kv_t = _dot_tn(v_pair, ke_bd)
            a = [jnp.where(keep, _dot_nt(jnp.where(masks[hh], qd, 0.0).astype(BF), kd), 0.0).astype(BF)
                 for hh in range(2)]
            stage.append((qd, dec_all[:, sl], v_pair, kv_t, a))
        inters = []
        for (q, k, v, sm, w_a, b_a, tri, d, reverse), (qd, dec, v_pair, kv_t, a) in zip(jobs, stage):
            s = st_ref[d, hp]
            inter = [None] * n_chunk
            for i in (range(n_chunk - 1, -1, -1) if reverse else range(n_chunk)):
                rows = slice(i * GLA_CHUNK, (i + 1) * GLA_CHUNK)
                inter[i] = _dot_nt(qd[rows].astype(BF), s.astype(BF))
                s = s * dec[i * GLA_CHUNK:i * GLA_CHUNK + 1, :] + jnp.where(own, kv_t[:, i * LANES:(i + 1) * LANES], 0.0)
            st_ref[d, hp] = s
            inters.append(jnp.concatenate(inter, axis=0))
        for n, ((qd, dec, v_pair, kv_t, a), o_inter) in enumerate(zip(stage, inters)):
            for hh in range(2):
                hv = slice(hh * GLA_DV, (hh + 1) * GLA_DV)
                outs[n].append(_dot(a[hh], v_pair[:, hv]) + o_inter[:, hv])
    return [jnp.concatenate(o, axis=1) for o in outs]


def _gla_finish(o, og, onorm):
    ys = []
    for h in range(GLA_HEADS):
        ys.append(_rms(o[:, h * GLA_DV:(h + 1) * GLA_DV], onorm))
    y = jnp.concatenate(ys, axis=1)
    ogf = og.astype(F32)
    return (y * (ogf * jax.nn.sigmoid(ogf))).astype(BF)


def _gla_kernel(*refs, n_lat, ctx_out):
    (ql_ref, kl_ref, vl_ref, sml_ref, ogl_ref, qc_ref, kc_ref, vc_ref, smc_ref, ogc_ref,
     waf_ref, baf_ref, wab_ref, bab_ref, onorm_ref, trif_ref, trib_ref) = refs[:17]
    if ctx_out:
        ol_ref, oc_ref, of_ref, ob_ref, st_ref = refs[17:]
    else:
        ol_ref, of_ref, ob_ref, st_ref = refs[17:]
    onorm = onorm_ref[...]

    def both(fwd_in, bwd_in):
        return _gla_superchunks([fwd_in + (waf_ref[...], baf_ref[...], trif_ref[...], 0, False),
                                 bwd_in + (wab_ref[...], bab_ref[...], trib_ref[...], 1, True)], st_ref)

    st_ref[...] = jnp.zeros_like(st_ref)
    ctx_in = (qc_ref[0], kc_ref[0], vc_ref[0], smc_ref[0])
    o_cf, o_cb = both(ctx_in, ctx_in)
    if ctx_out:
        oc_ref[0] = _gla_finish(o_cf + o_cb, ogc_ref[0], onorm)

    def body(j, carry):
        rf = pl.ds(pl.multiple_of(j * SUPER, SUPER), SUPER)
        rb = pl.ds(pl.multiple_of((n_lat - 1 - j) * SUPER, SUPER), SUPER)
        o_f, o_b = both((ql_ref[0, rf, :], kl_ref[0, rf, :], vl_ref[0, rf, :], sml_ref[0, rf, :]),
                        (ql_ref[0, rb, :], kl_ref[0, rb, :], vl_ref[0, rb, :], sml_ref[0, rb, :]))
        of_ref[rf, :] = o_f
        ob_ref[rb, :] = o_b
        return carry

    lax.fori_loop(0, n_lat, body, 0)

    def finish(j, carry):
        r = pl.ds(pl.multiple_of(j * SUPER, SUPER), SUPER)
        ol_ref[0, r, :] = _gla_finish(of_ref[r, :] + ob_ref[r, :], ogl_ref[0, r, :], onorm)
        return carry

    lax.fori_loop(0, n_lat, finish, 0)


def _gla_consts():
    r = np.arange(SUPER)
    same = (r[:, None] // GLA_CHUNK) == (r[None, :] // GLA_CHUNK)
    return jnp.asarray(same & (r[None, :] <= r[:, None]), BF), jnp.asarray(same & (r[None, :] >= r[:, None]), BF)


def _gla(z_l, z_c, params, l, ctx_out):
    B, T, _ = z_l.shape
    L = z_c.shape[1]
    assert L == SUPER and T % SUPER == 0
    kw = GLA_HEADS * GLA_DK
    vw = GLA_HEADS * GLA_DV
    consts = _gla_consts()

    def zspec(n, width, col):
        return pl.BlockSpec((1, n, width), lambda b: (b, 0, col // width))

    in_specs = ([zspec(T, kw, Z_GQ), zspec(T, kw, Z_GK), zspec(T, vw, Z_GV), zspec(T, LANES, Z_SM), zspec(T, vw, Z_OG),
                 zspec(L, kw, Z_GQ), zspec(L, kw, Z_GK), zspec(L, vw, Z_GV), zspec(L, LANES, Z_SM), zspec(L, vw, Z_OG)]
                + [_layer(a, l) for a in params] + [_whole(a) for a in consts])
    ol_spec = pl.BlockSpec((1, T, vw), lambda b: (b, 0, 0))
    ol_shape = jax.ShapeDtypeStruct((B, T, vw), BF)
    if ctx_out:
        out_specs = [ol_spec, pl.BlockSpec((1, L, vw), lambda b: (b, 0, 0))]
        out_shape = [ol_shape, jax.ShapeDtypeStruct((B, L, vw), BF)]
    else:
        out_specs, out_shape = ol_spec, ol_shape
    scratch = [pltpu.VMEM((T, vw), F32), pltpu.VMEM((T, vw), F32),
               pltpu.VMEM((2, GLA_HEADS // 2, 2 * GLA_DV, LANES), F32)]
    res = pl.pallas_call(
        functools.partial(_gla_kernel, n_lat=T // SUPER, ctx_out=ctx_out),
        grid=(B,), in_specs=in_specs, out_specs=out_specs, out_shape=out_shape,
        scratch_shapes=scratch,
        compiler_params=_cparams("arbitrary"),
        name="gla",
    )(*([z_l] * 5 + [z_c] * 5 + list(params) + list(consts)))
    return (res[0], res[1]) if ctx_out else (res, None)


MERGE_SUB = 256


def _merge_kernel(oa_ref, ob_ref, oc_ref, ma_ref, mb_ref, mc_ref, x_ref, gate_ref, shift_ref, scale_ref, g_ref,
                  wa_ref, wb_ref, wc_ref, wo_ref, wr_ref, xo_ref, h_ref, aff_ref, y_ref):
    tm = x_ref.shape[1]
    gate = gate_ref[0]
    subs = [slice(r0, r0 + MERGE_SUB) for r0 in range(0, tm, MERGE_SUB)]

    def branch(o_ref, m_ref, w_ref, rows, n0):
        m = m_ref[0, rows, n0:n0 + 256].astype(F32)
        return jax.nn.sigmoid(m) * _dot(o_ref[0, rows, :], w_ref[:, n0:n0 + 256])

    for n0 in range(0, D_MODEL, 256):
        for rows in subs:
            y = (branch(oa_ref, ma_ref, wa_ref, rows, n0) + branch(ob_ref, mb_ref, wb_ref, rows, n0)
                 + branch(oc_ref, mc_ref, wc_ref, rows, n0))
            y_ref[rows, n0:n0 + 256] = y.astype(BF)
    for n0 in range(0, D_MODEL, 256):
        for rows in subs:
            xo_ref[0, rows, n0:n0 + 256] = (x_ref[0, rows, n0:n0 + 256]
                                            + gate[:, n0:n0 + 256] * _dot(y_ref[rows, :], wo_ref[:, n0:n0 + 256]))
    hs = [_rms(xo_ref[0, rows, :], g_ref[...]) * (1.0 + scale_ref[0]) + shift_ref[0] for rows in subs]
    wr = wr_ref[...]
    for rows, h in zip(subs, hs):
        h_hi = h.astype(BF)
        h_lo = (h - h_hi.astype(F32)).astype(BF)
        h_ref[0, rows, :] = h_hi
        part = _dot_nt(wr, h_hi)
        logits = part[:N_EXPERTS] + part[N_EXPERTS:] + _dot_nt(wr[:N_EXPERTS], h_lo)
        e = jnp.exp(logits - logits.max(axis=0, keepdims=True))
        aff_ref[0, :, rows] = e / e.sum(axis=0, keepdims=True)


def _merge(oa, ob, oc, z, x, mods3, mod_row, norm2_3, weights, l, tm):
    B, T, D = x.shape
    wa, wb, wc, wo, wr_t = weights

    def tok(width, col=0):
        return pl.BlockSpec((1, tm, width), lambda b, i: (b, i, col // width))

    def mod_spec(k):
        return pl.BlockSpec((1, 1, D), lambda b, i: (mod_row(b) * 6 + k, 0, 0))

    in_specs = [tok(256), tok(256), tok(512), tok(D, Z_MA), tok(D, Z_MB), tok(D, Z_MC), tok(D),
                mod_spec(2), mod_spec(3), mod_spec(4), _layer(norm2_3, l)] + [_layer(w, l) for w in weights]
    return pl.pallas_call(
        _merge_kernel,
        grid=(B, T // tm), in_specs=in_specs,
        out_specs=[tok(D), tok(D), pl.BlockSpec((1, N_EXPERTS, tm), lambda b, i: (b, 0, i))],
        out_shape=[jax.ShapeDtypeStruct((B, T, D), F32), jax.ShapeDtypeStruct((B, T, D), BF),
                   jax.ShapeDtypeStruct((B, N_EXPERTS, T), F32)],
        scratch_shapes=[pltpu.VMEM((tm, D), BF)],
        compiler_params=_cparams("arbitrary", "arbitrary"),
        name="merge",
    )(oa, ob, oc, z, z, z, x, mods3, mods3, mods3, norm2_3, wa, wb, wc, wo, wr_t)


def _route_kernel(aff_ref, tri_ref, pos_ref, post_ref, *, cap, slot_stride):
    aff = aff_ref[0]
    E, T = aff.shape

    def search(i, thr):
        cand = thr | (jnp.int32(1) << (30 - i))
        cnt = jnp.sum((aff >= pltpu.bitcast(cand, F32)).astype(F32), axis=1, keepdims=True)
        return jnp.where(cnt >= cap, cand, thr)

    thr = lax.fori_loop(0, 31, search, jnp.zeros((E, 1), jnp.int32))
    thr_f = pltpu.bitcast(thr, F32)
    gt = aff > thr_f
    eq = aff == thr_f
    need = cap - jnp.sum(gt.astype(F32), axis=1, keepdims=True)
    tri = tri_ref[...]

    def excl_prefix(flags):
        run = jnp.zeros((E, 1), F32)
        blocks = []
        for j in range(T // LANES):
            blk = flags[:, j * LANES:(j + 1) * LANES].astype(F32)
            blocks.append(_dot(blk.astype(BF), tri) + run)
            run = run + jnp.sum(blk, axis=1, keepdims=True)
        return jnp.concatenate(blocks, axis=1)

    sel = gt | (eq & (excl_prefix(eq) < need))
    base = (slot_stride * pl.program_id(0)).astype(F32)
    pos = jnp.where(sel, excl_prefix(sel) + base, -1.0)
    pos_ref[0] = pos.astype(jnp.int32)
    post_ref[0] = jnp.concatenate([pos, jnp.full((LANES - E, T), -1.0, F32)], axis=0).T


def _route(aff_t, cap, slot_stride):
    B, E, T = aff_t.shape
    r = np.arange(LANES)
    tri = jnp.asarray(r[:, None] < r[None, :], BF)
    return pl.pallas_call(
        functools.partial(_route_kernel, cap=cap, slot_stride=slot_stride),
        grid=(B,),
        in_specs=[pl.BlockSpec((1, E, T), lambda b: (b, 0, 0)), pl.BlockSpec((LANES, LANES), lambda b: (0, 0))],
        out_specs=[pl.BlockSpec((1, E, T), lambda b: (b, 0, 0)), pl.BlockSpec((1, T, LANES), lambda b: (b, 0, 0))],
        out_shape=[jax.ShapeDtypeStruct((B, E, T), jnp.int32), jax.ShapeDtypeStruct((B, T, LANES), F32)],
        compiler_params=_cparams("arbitrary"),
        name="route",
    )(aff_t, tri)


def _moe_kernel(h_ref, pos_ref, post_ref, aff_ref, wg_ref, wu_ref, wd_ref, o_ref, *, n_slots):
    e = pl.program_id(1)
    pos = pos_ref[0, pl.ds(e, 1), :]
    aff = aff_ref[0, pl.ds(e, 1), :]
    T = pos.shape[1]
    hit = lax.broadcasted_iota(jnp.int32, (n_slots, T), 0) == pos
    w_slot = jnp.sum(jnp.where(hit, aff, 0.0), axis=1, keepdims=True)
    xg = _dot(hit.astype(BF), h_ref[0]).astype(BF)
    gt = _dot(xg, wg_ref[...])
    up = _dot(xg, wu_ref[...])
    hid = (gt * jax.nn.sigmoid(gt) * up * w_slot).astype(BF)
    y = _dot(hid, wd_ref[...]).astype(BF)

    @pl.when(e == 0)
    def _():
        o_ref[...] = jnp.zeros_like(o_ref)

    lane = lax.broadcasted_iota(jnp.int32, (1, LANES), 1)
    slot_row = lax.broadcasted_iota(jnp.int32, (1, n_slots), 1).astype(F32)
    for t0 in range(0, T, 256):
        pos_col = jnp.sum(jnp.where(lane == e, post_ref[0, t0:t0 + 256, :], 0.0), axis=1, keepdims=True)
        o_ref[0, t0:t0 + 256, :] += _dot((pos_col == slot_row).astype(BF), y)


def _moe(h, pos, pos_t, aff_t, weights, l, n_slots):
    B, T, D = h.shape
    E = pos.shape[1]
    wg, wu, wd = weights

    def wspec(w):
        return pl.BlockSpec((None, None) + w.shape[2:], lambda b, e: (l, e, 0, 0))

    return pl.pallas_call(
        functools.partial(_moe_kernel, n_slots=n_slots),
        grid=(B, E),
        in_specs=[pl.BlockSpec((1, T, D), lambda b, e: (b, 0, 0)),
                  pl.BlockSpec((1, E, T), lambda b, e: (b, 0, 0)),
                  pl.BlockSpec((1, T, LANES), lambda b, e: (b, 0, 0)),
                  pl.BlockSpec((1, E, T), lambda b, e: (b, 0, 0)),
                  wspec(wg), wspec(wu), wspec(wd)],
        out_specs=pl.BlockSpec((1, T, D), lambda b, e: (b, 0, 0)),
        out_shape=jax.ShapeDtypeStruct((B, T, D), F32),
        compiler_params=_cparams("arbitrary", "arbitrary"),
        name="moe",
    )(h, pos, pos_t, aff_t, wg, wu, wd)


def _final_kernel(x_ref, moe_ref, gate_ref, g_ref, o_ref):
    x = x_ref[0] + gate_ref[0] * moe_ref[0]
    o_ref[0] = _rms(x, g_ref[...])


def _final(x, moe, mods3, g, tm=256):
    B, T, D = x.shape
    tok = pl.BlockSpec((1, tm, D), lambda b, i: (b, i, 0))
    return pl.pallas_call(
        _final_kernel,
        grid=(B, T // tm),
        in_specs=[tok, tok, pl.BlockSpec((1, 1, D), lambda b, i: (b * 6 + 5, 0, 0)),
                  pl.BlockSpec((1, D), lambda b, i: (0, 0))],
        out_specs=tok,
        out_shape=jax.ShapeDtypeStruct((B, T, D), F32),
        compiler_params=_cparams("arbitrary", "arbitrary"),
        name="final_norm",
    )(x, moe, mods3, g.reshape(1, D))


def _reorder_w_in(w_in):
    parts, off = [], 0
    for s in IN_SIZES:
        parts.append(w_in[..., off:off + s])
        off += s
    qa, ka, va, dq, dkv, kr, gq, gk, gv, a_f, a_b, og, m_a, m_b, m_c = parts
    pad = jnp.zeros(w_in.shape[:-1] + (LANES - MLA_ROPE_DIM - 2 * GLA_GATE_RANK,), w_in.dtype)
    return jnp.concatenate([m_a, m_b, m_c, gv, og, qa, ka, va, gq, gk, dq, dkv, kr, a_f, a_b, pad], axis=-1).astype(BF)


def kernel(x, c, ctx, c_ctx, ada_w, ada_b, norm1, norm2, w_in, na_rpb, mla_q_norm, mla_w_uq, mla_kv_norm, mla_w_ukv,
           gla_w_af, gla_b_af, gla_w_ab, gla_b_ab, gla_o_norm, w_br_a, w_br_b, w_br_c, w_out,
           moe_router, moe_w_gate, moe_w_up, moe_w_down, final_norm):
    B, T, D = x.shape
    L = ctx.shape[1]
    depth = ada_w.shape[0]
    assert B < MOD_ROWS
    cap_l = EC_CAPACITY * T // N_EXPERTS
    cap_c = EC_CAPACITY * L // N_EXPERTS
    ctx_row = B
    cc = jnp.concatenate([c, c_ctx[None], jnp.zeros((MOD_ROWS - B - 1, D), F32)], axis=0)

    def lat_row(b):
        return b

    def ctx_row_fn(b):
        return ctx_row

    def rows3(a):
        return a.reshape(depth, 1, -1)

    ada_b3, norm1_3, norm2_3 = rows3(ada_b), rows3(norm1), rows3(norm2)
    w_in_r = _reorder_w_in(w_in)
    na_bias = _na_bias(na_rpb, T // GRID_W)
    mla_w = _mla_weights(mla_w_uq, mla_w_ukv)
    qn3, kvn3 = rows3(mla_q_norm), rows3(mla_kv_norm)
    rope_tabs = _rope_tables(T)
    gate_pad = lambda w, off: jnp.pad(w, ((0, 0), (off, LANES - off - GLA_GATE_RANK), (0, 0))).astype(BF)
    gla_params = (gate_pad(gla_w_af, SM_AF), rows3(gla_b_af), gate_pad(gla_w_ab, SM_AB), rows3(gla_b_ab),
                  rows3(gla_o_norm))
    wr_t = jnp.swapaxes(moe_router, 1, 2)
    wr_hi = wr_t.astype(BF)
    wr_lo = (wr_t - wr_hi.astype(F32)).astype(BF)
    merge_w = (w_br_a.astype(BF), w_br_b.astype(BF), w_br_c.astype(BF), w_out.astype(BF),
               jnp.concatenate([wr_hi, wr_lo], axis=1))
    moe_w = (moe_w_gate.astype(BF), moe_w_up.astype(BF), moe_w_down.astype(BF))

    moe_l = moe_c = mods3 = None
    for l in range(depth):
        ctx_out = l < depth - 1
        prev_mods3 = mods3
        mods3 = _ada(cc, ada_w, ada_b3, l).reshape(MOD_ROWS * 6, 1, D)
        z_l, x = _in_proj(x, moe_l, prev_mods3, mods3, lat_row, norm1_3, w_in_r, l)
        z_c, ctx = _in_proj(ctx, moe_c, prev_mods3, mods3, ctx_row_fn, norm1_3, w_in_r, l)

        oa_l, oa_c = _na_attention(z_l, z_c, na_bias, l, ctx_out)

        qb_l, kb_l, vb_l = _mla_proj(z_l, qn3, kvn3, mla_w, rope_tabs, l)
        qb_c, kb_c, vb_c = _mla_proj(z_c, qn3, kvn3, mla_w, None, l)
        ob_l = _mla_attn(qb_l, kb_c, vb_c, kb_l, vb_l)

        oc_l, oc_c = _gla(z_l, z_c, gla_params, l, ctx_out)

        x, h_l, aff_l = _merge(oa_l, ob_l, oc_l, z_l, x, mods3, lat_row, norm2_3, merge_w, l, tm=512)
        pos_l, post_l = _route(aff_l, cap_l, 0)
        moe_l = _moe(h_l, pos_l, post_l, aff_l, moe_w, l, cap_l)

        if ctx_out:
            ob_c = _mla_attn(qb_c, kb_c, vb_c)
            ctx, h_c, aff_c = _merge(oa_c, ob_c, oc_c, z_c, ctx, mods3, ctx_row_fn, norm2_3, merge_w, l, tm=256)
            pos_c, post_c = _route(aff_c, cap_c, cap_c)
            flat = lambda a: jnp.swapaxes(a, 0, 1).reshape(1, N_EXPERTS, B * L)
            moe_c = _moe(h_c.reshape(1, B * L, D), flat(pos_c), post_c.reshape(1, B * L, LANES), flat(aff_c),
                         moe_w, l, B * cap_c).reshape(B, L, D)
    return _final(x, moe_l, mods3, final_norm)
```

```python
import functools

import numpy as np
import jax
import jax.numpy as jnp
from jax import lax
from jax.experimental import pallas as pl
from jax.experimental.pallas import tpu as pltpu

BF = jnp.bfloat16
F32 = jnp.float32

D_MODEL = 1024
GRID_W = 64
EPS = 1e-6
NA_HEADS = 4
NA_HEAD_DIM = 64
NA_WIN_H = 8
NA_WIN_W = 16
MLA_HEADS = 4
MLA_Q_RANK = 256
MLA_KV_RANK = 128
MLA_NOPE_DIM = 64
MLA_ROPE_DIM = 32
MLA_V_DIM = 64
ROPE_THETA = 10000.0
GLA_HEADS = 4
GLA_DK = 64
GLA_DV = 128
GLA_GATE_RANK = 16
GLA_TAU = 16.0
GLA_CHUNK = 64
N_EXPERTS = 16
EXPERT_FF = 1024
EC_CAPACITY = 2
IN_SIZES = (256, 256, 256, 256, 128, 32, 256, 256, 512, 16, 16, 512, 1024, 1024, 1024)

LANES = 128
VMEM_LIMIT = 56 * 1024 * 1024
MOD_ROWS = 16

Z_MA, Z_MB, Z_MC = 0, 1024, 2048
Z_GV, Z_OG = 3072, 3584
Z_QA, Z_KA, Z_VA = 4096, 4352, 4608
Z_GQ, Z_GK = 4864, 5120
Z_MLA = 5376
Z_SM = 5760
Z_COLS = 5888
SM_KR, SM_AF, SM_AB = 0, 32, 48

SUPER = 256
NEG = -1e30


def _cparams(*sem):
    return pltpu.CompilerParams(dimension_semantics=sem, vmem_limit_bytes=VMEM_LIMIT)


def _layer(a, l):
    return pl.BlockSpec((None,) + a.shape[1:], lambda *_: (l,) + (0,) * (a.ndim - 1), pipeline_mode=pl.Buffered(1))


def _whole(a):
    return pl.BlockSpec(a.shape, lambda *_: (0,) * a.ndim, pipeline_mode=pl.Buffered(1))


def _dot(a, b):
    return jnp.dot(a, b, preferred_element_type=F32)


def _dot_nt(a, b):
    return lax.dot_general(a, b, (((1,), (1,)), ((), ())), preferred_element_type=F32)


def _dot_tn(a, b):
    return lax.dot_general(a, b, (((0,), (0,)), ((), ())), preferred_element_type=F32)


def _rms(x, g):
    return x * lax.rsqrt(jnp.mean(x * x, axis=-1, keepdims=True) + EPS) * g


def _split3(x):
    hi = x.astype(BF)
    r = x - hi.astype(F32)
    mid = r.astype(BF)
    lo = (r - mid.astype(F32)).astype(BF)
    return hi, mid, lo


def _ada_kernel(c_ref, w_ref, b_ref, o_ref):
    c = c_ref[...]
    a = (c * jax.nn.sigmoid(c)).astype(BF)
    o_ref[...] = _dot(a, w_ref[...].astype(BF)) + b_ref[...]


def _ada(cc, ada_w, ada_b3, l):
    n = ada_w.shape[2]
    tn = 1536
    return pl.pallas_call(
        _ada_kernel,
        grid=(n // tn,),
        in_specs=[pl.BlockSpec((MOD_ROWS, D_MODEL), lambda j: (0, 0)),
                  pl.BlockSpec((None, D_MODEL, tn), lambda j: (l, 0, j)),
                  pl.BlockSpec((None, 1, tn), lambda j: (l, 0, j))],
        out_specs=pl.BlockSpec((MOD_ROWS, tn), lambda j: (0, j)),
        out_shape=jax.ShapeDtypeStruct((MOD_ROWS, n), F32),
        compiler_params=_cparams("arbitrary"),
        name="ada_mod",
    )(cc, ada_w, ada_b3)


def _in_proj_kernel(*refs, with_moe):
    if with_moe:
        x_ref, moe_ref, gate_ref, g_ref, shift_ref, scale_ref, w_ref, o_ref, xo_ref, h_ref = refs
        x = x_ref[0] + gate_ref[0] * moe_ref[0]
        xo_ref[0] = x
    else:
        x_ref, g_ref, shift_ref, scale_ref, w_ref, o_ref, h_ref = refs
        x = x_ref[0]
    h = _rms(x, g_ref[...]) * (1.0 + scale_ref[0]) + shift_ref[0]
    h_ref[...] = h.astype(BF)
    for n0 in range(0, Z_COLS, 256):
        o_ref[0, :, n0:n0 + 256] = _dot(h_ref[...], w_ref[:, n0:n0 + 256]).astype(BF)


def _in_proj(x, moe, prev_mods3, mods3, mod_row, norm1_3, w, l, tm=512):
    B, T, D = x.shape
    tm = min(tm, T)
    with_moe = moe is not None

    def mod_spec(k):
        return pl.BlockSpec((1, 1, D), lambda b, i: (mod_row(b) * 6 + k, 0, 0))

    x_spec = pl.BlockSpec((1, tm, D), lambda b, i: (b, i, 0))
    in_specs = [x_spec]
    args = [x]
    if with_moe:
        in_specs += [x_spec, mod_spec(5)]
        args += [moe, prev_mods3]
    in_specs += [_layer(norm1_3, l), mod_spec(0), mod_spec(1), _layer(w, l)]
    args += [norm1_3, mods3, mods3, w]
    z_spec = pl.BlockSpec((1, tm, Z_COLS), lambda b, i: (b, i, 0))
    z_shape = jax.ShapeDtypeStruct((B, T, Z_COLS), BF)
    if with_moe:
        out_specs, out_shape = [z_spec, x_spec], [z_shape, jax.ShapeDtypeStruct((B, T, D), F32)]
    else:
        out_specs, out_shape = z_spec, z_shape
    res = pl.pallas_call(
        functools.partial(_in_proj_kernel, with_moe=with_moe),
        grid=(B, T // tm),
        in_specs=in_specs, out_specs=out_specs, out_shape=out_shape,
        scratch_shapes=[pltpu.VMEM((tm, D), BF)],
        compiler_params=_cparams("arbitrary", "arbitrary"),
        name="in_proj",
    )(*args)
    return (res[0], res[1]) if with_moe else (res, x)


def _softmax_pv_many(heads):
    maxes = []
    for s_list, _ in heads:
        m = s_list[0].max(axis=-1, keepdims=True)
        for s in s_list[1:]:
            m = jnp.maximum(m, s.max(axis=-1, keepdims=True))
        maxes.append(m)
    probs = [[jnp.exp(s - m) for s in s_list] for (s_list, _), m in zip(heads, maxes)]
    outs = []
    for (_, v_list), p_list in zip(heads, probs):
        l = p_list[0].sum(axis=-1, keepdims=True)
        o = _dot(p_list[0].astype(BF), v_list[0])
        for p, v in zip(p_list[1:], v_list[1:]):
            l = l + p.sum(axis=-1, keepdims=True)
            o = o + _dot(p.astype(BF), v)
        outs.append(o / l)
    return outs


def _lane_masks():
    lane = lax.broadcasted_iota(jnp.int32, (1, LANES), 1)
    return lane < 64, lane >= 64


NA_ROWS_PER_STEP = 2
NA_BAND = 10
NA_CASE_ROWS = (0, 2, 4, 28, 30)


def _na_bias(rpb, rows):
    depth = rpb.shape[0]
    qc = np.arange(GRID_W)
    kj = np.arange(GRID_W)
    cs = np.clip(qc - NA_WIN_W // 2, 0, GRID_W - NA_WIN_W)
    col_ok = (kj[None, :] >= cs[:, None]) & (kj[None, :] < cs[:, None] + NA_WIN_W)
    co = kj[None, :] - qc[:, None] + (NA_WIN_W - 1)
    onehot = np.zeros((2 * NA_WIN_W - 1, GRID_W, GRID_W), np.float32)
    qi, ki = np.nonzero(col_ok)
    onehot[co[qi, ki], qi, ki] = 1.0
    toep = jnp.einsum("lhrc,cp->lhrp", rpb.astype(F32), jnp.asarray(onehot.reshape(2 * NA_WIN_W - 1, -1)),
                      precision=lax.Precision.HIGHEST)
    toep = (toep.reshape(depth, NA_HEADS, 2 * NA_WIN_H - 1, GRID_W, GRID_W)
            + jnp.asarray(np.where(col_ok, 0.0, NEG), F32))
    neg = jnp.full((depth, NA_HEADS, GRID_W, GRID_W), NEG, F32)
    cases = []
    for r0 in NA_CASE_ROWS:
        bs = int(np.clip(r0 - NA_WIN_H // 2, 0, rows - NA_BAND))
        q_rows = []
        for qr in range(NA_ROWS_PER_STEP):
            r = r0 + qr
            rs = int(np.clip(r - NA_WIN_H // 2, 0, rows - NA_WIN_H))
            blocks = []
            for i in range(NA_BAND):
                krow = bs + i
                blocks.append(toep[:, :, krow - r + NA_WIN_H - 1] if rs <= krow < rs + NA_WIN_H else neg)
            q_rows.append(jnp.concatenate(blocks, axis=-1))
        cases.append(jnp.concatenate(q_rows, axis=-2))
    return jnp.stack(cases, axis=2)


def _na_kernel(*refs, rows, ctx_out):
    if ctx_out:
        ql_ref, kl_ref, vl_ref, qc_ref, kc_ref, vc_ref, bias_ref, ol_ref, oc_ref = refs
    else:
        ql_ref, kl_ref, vl_ref, kc_ref, vc_ref, bias_ref, ol_ref = refs
    scale = NA_HEAD_DIM ** -0.5
    masks = _lane_masks()
    nq = NA_ROWS_PER_STEP * GRID_W
    nk = NA_BAND * GRID_W

    def body(rp, carry):
        r0 = rp * NA_ROWS_PER_STEP
        bs = jnp.clip(r0 - NA_WIN_H // 2, 0, rows - NA_BAND)
        case = jnp.where(r0 < 4, r0 // 2, jnp.where(r0 > rows - 6, (r0 - (rows - 4)) // 2 + 3, 2))
        q0 = pl.multiple_of(r0 * GRID_W, nq)
        k0 = pl.multiple_of(bs * GRID_W, 2 * GRID_W)
        heads = []
        for hp in range(NA_HEADS // 2):
            sl = slice(hp * LANES, (hp + 1) * LANES)
            q = ql_ref[0, pl.ds(q0, nq), sl] * scale
            kb = kl_ref[0, pl.ds(k0, nk), sl]
            vb = vl_ref[0, pl.ds(k0, nk), sl]
            kc = kc_ref[0, :, sl]
            vc = vc_ref[0, :, sl]
            for hh in range(2):
                qm = jnp.where(masks[hh], q, jnp.zeros_like(q))
                s_loc = _dot_nt(qm, kb) + bias_ref[2 * hp + hh, case]
                s_ctx = _dot_nt(qm, kc)
                heads.append(([s_loc, s_ctx], [vb, vc]))
        outs = _softmax_pv_many(heads)
        for hp in range(NA_HEADS // 2):
            sl = slice(hp * LANES, (hp + 1) * LANES)
            ol_ref[0, pl.ds(q0, nq), sl] = jnp.where(masks[0], outs[2 * hp], outs[2 * hp + 1]).astype(BF)
        return carry

    lax.fori_loop(0, rows // NA_ROWS_PER_STEP, body, 0)

    if ctx_out:
        heads = []
        for hp in range(NA_HEADS // 2):
            sl = slice(hp * LANES, (hp + 1) * LANES)
            q = qc_ref[0, :, sl] * scale
            kc = kc_ref[0, :, sl]
            for hh in range(2):
                qm = jnp.where(masks[hh], q, jnp.zeros_like(q))
                heads.append(([_dot_nt(qm, kc)], [vc_ref[0, :, sl]]))
        outs = _softmax_pv_many(heads)
        for hp in range(NA_HEADS // 2):
            sl = slice(hp * LANES, (hp + 1) * LANES)
            oc_ref[0, :, sl] = jnp.where(masks[0], outs[2 * hp], outs[2 * hp + 1]).astype(BF)


def _na_attention(z_l, z_c, bias, l, ctx_out):
    B, T, _ = z_l.shape
    L = z_c.shape[1]
    rows = T // GRID_W

    def zl(col):
        return pl.BlockSpec((1, T, 256), lambda b: (b, 0, col // 256))

    def zc(col):
        return pl.BlockSpec((1, L, 256), lambda b: (b, 0, col // 256))

    ol_spec = pl.BlockSpec((1, T, 256), lambda b: (b, 0, 0))
    ol_shape = jax.ShapeDtypeStruct((B, T, 256), BF)
    if ctx_out:
        in_specs = [zl(Z_QA), zl(Z_KA), zl(Z_VA), zc(Z_QA), zc(Z_KA), zc(Z_VA), _layer(bias, l)]
        args = [z_l, z_l, z_l, z_c, z_c, z_c, bias]
        out_specs = [ol_spec, pl.BlockSpec((1, L, 256), lambda b: (b, 0, 0))]
        out_shape = [ol_shape, jax.ShapeDtypeStruct((B, L, 256), BF)]
    else:
        in_specs = [zl(Z_QA), zl(Z_KA), zl(Z_VA), zc(Z_KA), zc(Z_VA), _layer(bias, l)]
        args = [z_l, z_l, z_l, z_c, z_c, bias]
        out_specs, out_shape = ol_spec, ol_shape
    res = pl.pallas_call(
        functools.partial(_na_kernel, rows=rows, ctx_out=ctx_out),
        grid=(B,), in_specs=in_specs, out_specs=out_specs, out_shape=out_shape,
        compiler_params=_cparams("arbitrary"),
        name="na_attn",
    )(*args)
    return (res[0], res[1]) if ctx_out else (res, None)


MLA_QK = MLA_NOPE_DIM + MLA_ROPE_DIM
MLA_W = MLA_HEADS * LANES


def _mla_proj_kernel(*refs, rope):
    if rope:
        (zm_ref, sm_ref, qn_ref, kvn_ref, wq_ref, wqs_ref, wk_ref, wv_ref, e_ref, es_ref, cos_ref, sin_ref,
         q_ref, k_ref, v_ref) = refs
    else:
        zm_ref, sm_ref, qn_ref, kvn_ref, wq_ref, wk_ref, wv_ref, e_ref, q_ref, k_ref, v_ref = refs
    zm = zm_ref[0].astype(F32)
    nq = _rms(zm[:, :MLA_Q_RANK], qn_ref[...]).astype(BF)
    nkv = _rms(zm[:, MLA_Q_RANK:], kvn_ref[...]).astype(BF)
    sm = sm_ref[0]
    scale = MLA_QK ** -0.5
    q = _dot(nq, wq_ref[...])
    k = _dot(nkv, wk_ref[...]) + _dot(sm, e_ref[...])
    if rope:
        cos = cos_ref[...]
        sin = sin_ref[...]
        q = q * cos + _dot(nq, wqs_ref[...]) * sin
        k = k * cos + _dot(sm, es_ref[...]) * sin
    q_ref[0] = (q * scale).astype(BF)
    k_ref[0] = k.astype(BF)
    v_ref[0] = _dot(nkv, wv_ref[...]).astype(BF)


def _rope_swap(e):
    return np.where(e % 16 < 8, e + 8, e - 8)


def _mla_weights(w_uq, w_ukv):
    depth = w_uq.shape[0]
    quarter = MLA_ROPE_DIM // 4
    q4 = w_uq.reshape(depth, MLA_Q_RANK, MLA_HEADS, MLA_QK)
    nope, rope = q4[..., :MLA_NOPE_DIM], q4[..., MLA_NOPE_DIM:]
    rope_sw = jnp.flip(rope.reshape(depth, MLA_Q_RANK, MLA_HEADS, 2, 2, quarter), axis=4).reshape(rope.shape)
    zpad = jnp.zeros(rope.shape, w_uq.dtype)
    wq = jnp.concatenate([nope, rope, zpad], axis=-1).reshape(depth, MLA_Q_RANK, MLA_W)
    wqs = jnp.concatenate([jnp.zeros(nope.shape, w_uq.dtype), rope_sw, zpad], axis=-1).reshape(depth, MLA_Q_RANK, MLA_W)
    kv4 = w_ukv.reshape(depth, MLA_KV_RANK, MLA_HEADS, MLA_NOPE_DIM + MLA_V_DIM)
    k_nope, v = kv4[..., :MLA_NOPE_DIM], kv4[..., MLA_NOPE_DIM:]
    wk = jnp.concatenate([k_nope, jnp.zeros(k_nope.shape, w_ukv.dtype)], axis=-1).reshape(depth, MLA_KV_RANK, MLA_W)
    wv = v.reshape(depth, MLA_KV_RANK, MLA_HEADS * MLA_V_DIM)
    e = np.arange(MLA_ROPE_DIM)
    em = np.zeros((LANES, MLA_W), np.float32)
    ems = np.zeros((LANES, MLA_W), np.float32)
    for h in range(MLA_HEADS):
        em[SM_KR + e, h * LANES + MLA_NOPE_DIM + e] = 1.0
        ems[SM_KR + _rope_swap(e), h * LANES + MLA_NOPE_DIM + e] = 1.0
    return (wq.astype(BF), wqs.astype(BF), wk.astype(BF), wv.astype(BF), jnp.asarray(em, BF), jnp.asarray(ems, BF))


def _rope_tables(T):
    t = jnp.arange(T)
    quarter = MLA_ROPE_DIM // 4
    inv_freq = ROPE_THETA ** (-jnp.arange(quarter, dtype=F32) / quarter)
    ang_r = (t // GRID_W).astype(F32)[:, None] * inv_freq
    ang_c = (t % GRID_W).astype(F32)[:, None] * inv_freq
    cr, sr, ccol, scol = jnp.cos(ang_r), jnp.sin(ang_r), jnp.cos(ang_c), jnp.sin(ang_c)
    ones = jnp.ones((T, MLA_NOPE_DIM), F32)
    pad1 = jnp.ones((T, LANES - MLA_QK), F32)
    cos_h = jnp.concatenate([ones, cr, cr, ccol, ccol, pad1], axis=1)
    sin_h = jnp.concatenate([0.0 * ones, -sr, sr, -scol, scol, 0.0 * pad1], axis=1)
    return jnp.tile(cos_h, (1, MLA_HEADS)), jnp.tile(sin_h, (1, MLA_HEADS))


def _mla_proj(z, qn3, kvn3, weights, rope_tabs, l, tm=512):
    B, T, _ = z.shape
    wq, wqs, wk, wv, em, ems = weights
    rope = rope_tabs is not None
    tm = min(tm, T)
    zm_spec = pl.BlockSpec((1, tm, 384), lambda i, b: (b, i, Z_MLA // 384))
    sm_spec = pl.BlockSpec((1, tm, LANES), lambda i, b: (b, i, Z_SM // LANES))
    if rope:
        tab_spec = pl.BlockSpec((tm, MLA_W), lambda i, b: (i, 0))
        ins = [z, z, qn3, kvn3, wq, wqs, wk, wv, em, ems, rope_tabs[0], rope_tabs[1]]
        in_specs = ([zm_spec, sm_spec] + [_layer(a, l) for a in ins[2:8]] + [_whole(em), _whole(ems)]
                    + [tab_spec, tab_spec])
    else:
        ins = [z, z, qn3, kvn3, wq, wk, wv, em]
        in_specs = [zm_spec, sm_spec] + [_layer(a, l) for a in ins[2:7]] + [_whole(em)]
    return pl.pallas_call(
        functools.partial(_mla_proj_kernel, rope=rope),
        grid=(T // tm, B), in_specs=in_specs,
        out_specs=[pl.BlockSpec((1, tm, MLA_W), lambda i, b: (b, i, 0)),
                   pl.BlockSpec((1, tm, MLA_W), lambda i, b: (b, i, 0)),
                   pl.BlockSpec((1, tm, 256), lambda i, b: (b, i, 0))],
        out_shape=[jax.ShapeDtypeStruct((B, T, MLA_W), BF), jax.ShapeDtypeStruct((B, T, MLA_W), BF),
                   jax.ShapeDtypeStruct((B, T, 256), BF)],
        compiler_params=_cparams("arbitrary", "arbitrary"),
        name="mla_proj",
    )(*ins)


def _mla_attn_kernel(*refs, with_latent):
    if with_latent:
        q_ref, kc_ref, vc_ref, kl_ref, vl_ref, o_ref = refs
    else:
        q_ref, kc_ref, vc_ref, o_ref = refs
    masks = _lane_masks()
    heads = []
    for h in range(MLA_HEADS):
        hsl = slice(h * LANES, (h + 1) * LANES)
        vsl = slice(h // 2 * LANES, (h // 2 + 1) * LANES)
        q = q_ref[0, :, hsl]
        s_list = [_dot_nt(q, kc_ref[0, :, hsl])]
        v_list = [vc_ref[0, :, vsl]]
        if with_latent:
            s_list.append(_dot_nt(q, kl_ref[0, :, hsl]))
            v_list.append(vl_ref[0, :, vsl])
        heads.append((s_list, v_list))
    outs = _softmax_pv_many(heads)
    for hp in range(MLA_HEADS // 2):
        vsl = slice(hp * LANES, (hp + 1) * LANES)
        o_ref[0, :, vsl] = jnp.where(masks[0], outs[2 * hp], outs[2 * hp + 1]).astype(BF)


def _mla_attn(q, k_c, v_c, k_l=None, v_l=None, tq=256):
    B, Tq, W = q.shape
    L = k_c.shape[1]
    with_latent = k_l is not None
    in_specs = [pl.BlockSpec((1, tq, W), lambda b, i: (b, i, 0)),
                pl.BlockSpec((1, L, W), lambda b, i: (b, 0, 0)),
                pl.BlockSpec((1, L, 256), lambda b, i: (b, 0, 0))]
    args = [q, k_c, v_c]
    if with_latent:
        T = k_l.shape[1]
        in_specs += [pl.BlockSpec((1, T, W), lambda b, i: (b, 0, 0)),
                     pl.BlockSpec((1, T, 256), lambda b, i: (b, 0, 0))]
        args += [k_l, v_l]
    return pl.pallas_call(
        functools.partial(_mla_attn_kernel, with_latent=with_latent),
        grid=(B, Tq // tq), in_specs=in_specs,
        out_specs=pl.BlockSpec((1, tq, 256), lambda b, i: (b, i, 0)),
        out_shape=jax.ShapeDtypeStruct((B, Tq, 256), BF),
        compiler_params=_cparams("arbitrary", "arbitrary"),
        name="mla_attn",
    )(*args)


def _gla_superchunks(jobs, st_ref):
    n_chunk = SUPER // GLA_CHUNK
    masks = _lane_masks()
    rowblk = lax.broadcasted_iota(jnp.int32, (SUPER, LANES), 0) // GLA_CHUNK
    own = ((lax.broadcasted_iota(jnp.int32, (2 * GLA_DV, LANES), 0) < GLA_DV)
           == (lax.broadcasted_iota(jnp.int32, (2 * GLA_DV, LANES), 1) < GLA_DK))
    gs = [jax.nn.log_sigmoid(_dot(sm, w_a) + b_a) * (1.0 / GLA_TAU) for (_, _, _, sm, w_a, b_a, _, _, _) in jobs]
    pre = []
    for (q, k, v, sm, w_a, b_a, tri, d, reverse), g in zip(jobs, gs):
        g_hi = g.astype(BF)
        g_lo = (g - g_hi.astype(F32)).astype(BF)
        b = _dot(tri, g_hi) + _dot(tri, g_lo)
        ends = [i * GLA_CHUNK if reverse else (i + 1) * GLA_CHUNK - 1 for i in range(n_chunk)]
        tot = jnp.concatenate([jnp.broadcast_to(b[r:r + 1, :], (GLA_CHUNK, b.shape[1])) for r in ends], axis=0)
        pre.append((q.astype(F32) * (GLA_DK ** -0.5) * jnp.exp(b), k.astype(F32) * jnp.exp(-b),
                    k.astype(F32) * jnp.exp(tot - b), jnp.exp(tot), tri > 0))
    outs = [[] for _ in jobs]
    for hp in range(GLA_HEADS // 2):
        sl = slice(hp * LANES, (hp + 1) * LANES)
        stage = []
        for (q, k, v, sm, w_a, b_a, tri, d, reverse), (qd_all, kd_all, ke_all, dec_all, keep) in zip(jobs, pre):
            qd = qd_all[:, sl]
            kd = kd_all[:, sl].astype(BF)
            ke = ke_all[:, sl].astype(BF)
            zero = jnp.zeros_like(ke)
            ke_bd = jnp.concatenate([jnp.where(rowblk == i, ke, zero) for i in range(n_chunk)], axis=1)
            v_pair = v[:, hp * 2 * GLA_DV:(hp + 1) * 2 * GLA_DV]
            kv_t = _dot_tn(v_pair, ke_bd)
            a = [jnp.where(keep, _dot_nt(jnp.where(masks[hh], qd, 0.0).astype(BF), kd), 0.0).astype(BF)
                 for hh in range(2)]
            stage.append((qd, dec_all[:, sl], v_pair, kv_t, a))
        inters = []
        for (q, k, v, sm, w_a, b_a, tri, d, reverse), (qd, dec, v_pair, kv_t, a) in zip(jobs, stage):
            s = st_ref[d, hp]
            inter = [None] * n_chunk
            for i in (range(n_chunk - 1, -1, -1) if reverse else range(n_chunk)):
                rows = slice(i * GLA_CHUNK, (i + 1) * GLA_CHUNK)
                inter[i] = _dot_nt(qd[rows].astype(BF), s.astype(BF))
                s = s * dec[i * GLA_CHUNK:i * GLA_CHUNK + 1, :] + jnp.where(own, kv_t[:, i * LANES:(i + 1) * LANES], 0.0)
            st_ref[d, hp] = s
            inters.append(jnp.concatenate(inter, axis=0))
        for n, ((qd, dec, v_pair, kv_t, a), o_inter) in enumerate(zip(stage, inters)):
            for hh in range(2):
                hv = slice(hh * GLA_DV, (hh + 1) * GLA_DV)
                outs[n].append(_dot(a[hh], v_pair[:, hv]) + o_inter[:, hv])
    return [jnp.concatenate(o, axis=1) for o in outs]


def _gla_finish(o, og, onorm):
    ys = []
    for h in range(GLA_HEADS):
        ys.append(_rms(o[:, h * GLA_DV:(h + 1) * GLA_DV], onorm))
    y = jnp.concatenate(ys, axis=1)
    ogf = og.astype(F32)
    return (y * (ogf * jax.nn.sigmoid(ogf))).astype(BF)


def _gla_kernel(*refs, n_lat, ctx_out):
    (ql_ref, kl_ref, vl_ref, sml_ref, ogl_ref, qc_ref, kc_ref, vc_ref, smc_ref, ogc_ref,
     waf_ref, baf_ref, wab_ref, bab_ref, onorm_ref, trif_ref, trib_ref) = refs[:17]
    if ctx_out:
        ol_ref, oc_ref, of_ref, ob_ref, st_ref = refs[17:]
    else:
        ol_ref, of_ref, ob_ref, st_ref = refs[17:]
    onorm = onorm_ref[...]

    def both(fwd_in, bwd_in):
        return _gla_superchunks([fwd_in + (waf_ref[...], baf_ref[...], trif_ref[...], 0, False),
                                 bwd_in + (wab_ref[...], bab_ref[...], trib_ref[...], 1, True)], st_ref)

    st_ref[...] = jnp.zeros_like(st_ref)
    ctx_in = (qc_ref[0], kc_ref[0], vc_ref[0], smc_ref[0])
    o_cf, o_cb = both(ctx_in, ctx_in)
    if ctx_out:
        oc_ref[0] = _gla_finish(o_cf + o_cb, ogc_ref[0], onorm)

    def body(j, carry):
        rf = pl.ds(pl.multiple_of(j * SUPER, SUPER), SUPER)
        rb = pl.ds(pl.multiple_of((n_lat - 1 - j) * SUPER, SUPER), SUPER)
        o_f, o_b = both((ql_ref[0, rf, :], kl_ref[0, rf, :], vl_ref[0, rf, :], sml_ref[0, rf, :]),
                        (ql_ref[0, rb, :], kl_ref[0, rb, :], vl_ref[0, rb, :], sml_ref[0, rb, :]))
        of_ref[rf, :] = o_f
        ob_ref[rb, :] = o_b
        return carry

    lax.fori_loop(0, n_lat, body, 0)

    def finish(j, carry):
        r = pl.ds(pl.multiple_of(j * SUPER, SUPER), SUPER)
        ol_ref[0, r, :] = _gla_finish(of_ref[r, :] + ob_ref[r, :], ogl_ref[0, r, :], onorm)
        return carry

    lax.fori_loop(0, n_lat, finish, 0)


def _gla_consts():
    r = np.arange(SUPER)
    same = (r[:, None] // GLA_CHUNK) == (r[None, :] // GLA_CHUNK)
    return jnp.asarray(same & (r[None, :] <= r[:, None]), BF), jnp.asarray(same & (r[None, :] >= r[:, None]), BF)


def _gla(z_l, z_c, params, l, ctx_out):
    B, T, _ = z_l.shape
    L = z_c.shape[1]
    assert L == SUPER and T % SUPER == 0
    kw = GLA_HEADS * GLA_DK
    vw = GLA_HEADS * GLA_DV
    consts = _gla_consts()

    def zspec(n, width, col):
        return pl.BlockSpec((1, n, width), lambda b: (b, 0, col // width))

    in_specs = ([zspec(T, kw, Z_GQ), zspec(T, kw, Z_GK), zspec(T, vw, Z_GV), zspec(T, LANES, Z_SM), zspec(T, vw, Z_OG),
                 zspec(L, kw, Z_GQ), zspec(L, kw, Z_GK), zspec(L, vw, Z_GV), zspec(L, LANES, Z_SM), zspec(L, vw, Z_OG)]
                + [_layer(a, l) for a in params] + [_whole(a) for a in consts])
    ol_spec = pl.BlockSpec((1, T, vw), lambda b: (b, 0, 0))
    ol_shape = jax.ShapeDtypeStruct((B, T, vw), BF)
    if ctx_out:
        out_specs = [ol_spec, pl.BlockSpec((1, L, vw), lambda b: (b, 0, 0))]
        out_shape = [ol_shape, jax.ShapeDtypeStruct((B, L, vw), BF)]
    else:
        out_specs, out_shape = ol_spec, ol_shape
    scratch = [pltpu.VMEM((T, vw), F32), pltpu.VMEM((T, vw), F32),
               pltpu.VMEM((2, GLA_HEADS // 2, 2 * GLA_DV, LANES), F32)]
    res = pl.pallas_call(
        functools.partial(_gla_kernel, n_lat=T // SUPER, ctx_out=ctx_out),
        grid=(B,), in_specs=in_specs, out_specs=out_specs, out_shape=out_shape,
        scratch_shapes=scratch,
        compiler_params=_cparams("arbitrary"),
        name="gla",
    )(*([z_l] * 5 + [z_c] * 5 + list(params) + list(consts)))
    return (res[0], res[1]) if ctx_out else (res, None)


MERGE_SUB = 256


def _merge_kernel(oa_ref, ob_ref, oc_ref, ma_ref, mb_ref, mc_ref, x_ref, gate_ref, shift_ref, scale_ref, g_ref,
                  wa_ref, wb_ref, wc_ref, wo_ref, wr_ref, xo_ref, h_ref, aff_ref, y_ref):
    tm = x_ref.shape[1]
    gate = gate_ref[0]
    subs = [slice(r0, r0 + MERGE_SUB) for r0 in range(0, tm, MERGE_SUB)]

    def branch(o_ref, m_ref, w_ref, rows, n0):
        m = m_ref[0, rows, n0:n0 + 256].astype(F32)
        return jax.nn.sigmoid(m) * _dot(o_ref[0, rows, :], w_ref[:, n0:n0 + 256])

    for n0 in range(0, D_MODEL, 256):
        for rows in subs:
            y = (branch(oa_ref, ma_ref, wa_ref, rows, n0) + branch(ob_ref, mb_ref, wb_ref, rows, n0)
                 + branch(oc_ref, mc_ref, wc_ref, rows, n0))
            y_ref[rows, n0:n0 + 256] = y.astype(BF)
    for n0 in range(0, D_MODEL, 256):
        for rows in subs:
            xo_ref[0, rows, n0:n0 + 256] = (x_ref[0, rows, n0:n0 + 256]
                                            + gate[:, n0:n0 + 256] * _dot(y_ref[rows, :], wo_ref[:, n0:n0 + 256]))
    hs = [_rms(xo_ref[0, rows, :], g_ref[...]) * (1.0 + scale_ref[0]) + shift_ref[0] for rows in subs]
    wr = wr_ref[...]
    for rows, h in zip(subs, hs):
        h_hi = h.astype(BF)
        h_lo = (h - h_hi.astype(F32)).astype(BF)
        h_ref[0, rows, :] = h_hi
        part = _dot_nt(wr, h_hi)
        logits = part[:N_EXPERTS] + part[N_EXPERTS:] + _dot_nt(wr[:N_EXPERTS], h_lo)
        e = jnp.exp(logits - logits.max(axis=0, keepdims=True))
        aff_ref[0, :, rows] = e / e.sum(axis=0, keepdims=True)


def _merge(oa, ob, oc, z, x, mods3, mod_row, norm2_3, weights, l, tm):
    B, T, D = x.shape
    wa, wb, wc, wo, wr_t = weights

    def tok(width, col=0):
        return pl.BlockSpec((1, tm, width), lambda b, i: (b, i, col // width))

    def mod_spec(k):
        return pl.BlockSpec((1, 1, D), lambda b, i: (mod_row(b) * 6 + k, 0, 0))

    in_specs = [tok(256), tok(256), tok(512), tok(D, Z_MA), tok(D, Z_MB), tok(D, Z_MC), tok(D),
                mod_spec(2), mod_spec(3), mod_spec(4), _layer(norm2_3, l)] + [_layer(w, l) for w in weights]
    return pl.pallas_call(
        _merge_kernel,
        grid=(B, T // tm), in_specs=in_specs,
        out_specs=[tok(D), tok(D), pl.BlockSpec((1, N_EXPERTS, tm), lambda b, i: (b, 0, i))],
        out_shape=[jax.ShapeDtypeStruct((B, T, D), F32), jax.ShapeDtypeStruct((B, T, D), BF),
                   jax.ShapeDtypeStruct((B, N_EXPERTS, T), F32)],
        scratch_shapes=[pltpu.VMEM((tm, D), BF)],
        compiler_params=_cparams("arbitrary", "arbitrary"),
        name="merge",
    )(oa, ob, oc, z, z, z, x, mods3, mods3, mods3, norm2_3, wa, wb, wc, wo, wr_t)


def _route_kernel(aff_ref, tri_ref, pos_ref, post_ref, *, cap, slot_stride):
    aff = aff_ref[0]
    E, T = aff.shape

    def search(i, thr):
        cand = thr | (jnp.int32(1) << (30 - i))
        cnt = jnp.sum((aff >= pltpu.bitcast(cand, F32)).astype(F32), axis=1, keepdims=True)
        return jnp.where(cnt >= cap, cand, thr)

    thr = lax.fori_loop(0, 31, search, jnp.zeros((E, 1), jnp.int32))
    thr_f = pltpu.bitcast(thr, F32)
    gt = aff > thr_f
    eq = aff == thr_f
    need = cap - jnp.sum(gt.astype(F32), axis=1, keepdims=True)
    tri = tri_ref[...]

    def excl_prefix(flags):
        run = jnp.zeros((E, 1), F32)
        blocks = []
        for j in range(T // LANES):
            blk = flags[:, j * LANES:(j + 1) * LANES].astype(F32)
            blocks.append(_dot(blk.astype(BF), tri) + run)
            run = run + jnp.sum(blk, axis=1, keepdims=True)
        return jnp.concatenate(blocks, axis=1)

    sel = gt | (eq & (excl_prefix(eq) < need))
    base = (slot_stride * pl.program_id(0)).astype(F32)
    pos = jnp.where(sel, excl_prefix(sel) + base, -1.0)
    pos_ref[0] = pos.astype(jnp.int32)
    post_ref[0] = jnp.concatenate([pos, jnp.full((LANES - E, T), -1.0, F32)], axis=0).T


def _route(aff_t, cap, slot_stride):
    B, E, T = aff_t.shape
    r = np.arange(LANES)
    tri = jnp.asarray(r[:, None] < r[None, :], BF)
    return pl.pallas_call(
        functools.partial(_route_kernel, cap=cap, slot_stride=slot_stride),
        grid=(B,),
        in_specs=[pl.BlockSpec((1, E, T), lambda b: (b, 0, 0)), pl.BlockSpec((LANES, LANES), lambda b: (0, 0))],
        out_specs=[pl.BlockSpec((1, E, T), lambda b: (b, 0, 0)), pl.BlockSpec((1, T, LANES), lambda b: (b, 0, 0))],
        out_shape=[jax.ShapeDtypeStruct((B, E, T), jnp.int32), jax.ShapeDtypeStruct((B, T, LANES), F32)],
        compiler_params=_cparams("arbitrary"),
        name="route",
    )(aff_t, tri)


def _moe_kernel(*refs, n_slots, final):
    if final:
        h_ref, pos_ref, post_ref, aff_ref, wg_ref, wu_ref, wd_ref, x_ref, gate_ref, g_ref, o_ref = refs
    else:
        h_ref, pos_ref, post_ref, aff_ref, wg_ref, wu_ref, wd_ref, o_ref = refs
    e = pl.program_id(1)
    pos = pos_ref[0, pl.ds(e, 1), :]
    aff = aff_ref[0, pl.ds(e, 1), :]
    T = pos.shape[1]
    hit = lax.broadcasted_iota(jnp.int32, (n_slots, T), 0) == pos
    w_slot = jnp.sum(jnp.where(hit, aff, 0.0), axis=1, keepdims=True)
    xg = _dot(hit.astype(BF), h_ref[0]).astype(BF)
    gt = _dot(xg, wg_ref[...])
    up = _dot(xg, wu_ref[...])
    hid = (gt * jax.nn.sigmoid(gt) * up * w_slot).astype(BF)
    y = _dot(hid, wd_ref[...]).astype(BF)

    @pl.when(e == 0)
    def _():
        o_ref[...] = jnp.zeros_like(o_ref)

    lane = lax.broadcasted_iota(jnp.int32, (1, LANES), 1)
    slot_row = lax.broadcasted_iota(jnp.int32, (1, n_slots), 1).astype(F32)
    for t0 in range(0, T, 256):
        pos_col = jnp.sum(jnp.where(lane == e, post_ref[0, t0:t0 + 256, :], 0.0), axis=1, keepdims=True)
        o_ref[0, t0:t0 + 256, :] += _dot((pos_col == slot_row).astype(BF), y)

    if final:
        @pl.when(e == pl.num_programs(1) - 1)
        def _():
            for t0 in range(0, T, 256):
                rows = slice(t0, t0 + 256)
                o_ref[0, rows, :] = _rms(x_ref[0, rows, :] + gate_ref[0] * o_ref[0, rows, :], g_ref[...])


def _moe(h, pos, pos_t, aff_t, weights, l, n_slots, final=None):
    B, T, D = h.shape
    E = pos.shape[1]
    wg, wu, wd = weights

    def wspec(w):
        return pl.BlockSpec((None, None) + w.shape[2:], lambda b, e: (l, e, 0, 0))

    in_specs = [pl.BlockSpec((1, T, D), lambda b, e: (b, 0, 0)),
                pl.BlockSpec((1, E, T), lambda b, e: (b, 0, 0)),
                pl.BlockSpec((1, T, LANES), lambda b, e: (b, 0, 0)),
                pl.BlockSpec((1, E, T), lambda b, e: (b, 0, 0)),
                wspec(wg), wspec(wu), wspec(wd)]
    args = [h, pos, pos_t, aff_t, wg, wu, wd]
    if final is not None:
        x, mods3, g = final
        in_specs += [pl.BlockSpec((1, T, D), lambda b, e: (b, 0, 0), pipeline_mode=pl.Buffered(1)),
                     pl.BlockSpec((1, 1, D), lambda b, e: (b * 6 + 5, 0, 0)),
                     pl.BlockSpec((1, D), lambda b, e: (0, 0))]
        args += [x, mods3, g.reshape(1, D)]
    return pl.pallas_call(
        functools.partial(_moe_kernel, n_slots=n_slots, final=final is not None),
        grid=(B, E),
        in_specs=in_specs,
        out_specs=pl.BlockSpec((1, T, D), lambda b, e: (b, 0, 0)),
        out_shape=jax.ShapeDtypeStruct((B, T, D), F32),
        compiler_params=_cparams("arbitrary", "arbitrary"),
        name="moe",
    )(*args)


def _reorder_w_in(w_in):
    parts, off = [], 0
    for s in IN_SIZES:
        parts.append(w_in[..., off:off + s])
        off += s
    qa, ka, va, dq, dkv, kr, gq, gk, gv, a_f, a_b, og, m_a, m_b, m_c = parts
    pad = jnp.zeros(w_in.shape[:-1] + (LANES - MLA_ROPE_DIM - 2 * GLA_GATE_RANK,), w_in.dtype)
    return jnp.concatenate([m_a, m_b, m_c, gv, og, qa, ka, va, gq, gk, dq, dkv, kr, a_f, a_b, pad], axis=-1).astype(BF)


def kernel(x, c, ctx, c_ctx, ada_w, ada_b, norm1, norm2, w_in, na_rpb, mla_q_norm, mla_w_uq, mla_kv_norm, mla_w_ukv,
           gla_w_af, gla_b_af, gla_w_ab, gla_b_ab, gla_o_norm, w_br_a, w_br_b, w_br_c, w_out,
           moe_router, moe_w_gate, moe_w_up, moe_w_down, final_norm):
    B, T, D = x.shape
    L = ctx.shape[1]
    depth = ada_w.shape[0]
    assert B < MOD_ROWS
    cap_l = EC_CAPACITY * T // N_EXPERTS
    cap_c = EC_CAPACITY * L // N_EXPERTS
    ctx_row = B
    cc = jnp.concatenate([c, c_ctx[None], jnp.zeros((MOD_ROWS - B - 1, D), F32)], axis=0)

    def lat_row(b):
        return b

    def ctx_row_fn(b):
        return ctx_row

    def rows3(a):
        return a.reshape(depth, 1, -1)

    ada_b3, norm1_3, norm2_3 = rows3(ada_b), rows3(norm1), rows3(norm2)
    w_in_r = _reorder_w_in(w_in)
    na_bias = _na_bias(na_rpb, T // GRID_W)
    mla_w = _mla_weights(mla_w_uq, mla_w_ukv)
    qn3, kvn3 = rows3(mla_q_norm), rows3(mla_kv_norm)
    rope_tabs = _rope_tables(T)
    gate_pad = lambda w, off: jnp.pad(w, ((0, 0), (off, LANES - off - GLA_GATE_RANK), (0, 0))).astype(BF)
    gla_params = (gate_pad(gla_w_af, SM_AF), rows3(gla_b_af), gate_pad(gla_w_ab, SM_AB), rows3(gla_b_ab),
                  rows3(gla_o_norm))
    wr_t = jnp.swapaxes(moe_router, 1, 2)
    wr_hi = wr_t.astype(BF)
    wr_lo = (wr_t - wr_hi.astype(F32)).astype(BF)
    merge_w = (w_br_a.astype(BF), w_br_b.astype(BF), w_br_c.astype(BF), w_out.astype(BF),
               jnp.concatenate([wr_hi, wr_lo], axis=1))
    moe_w = (moe_w_gate.astype(BF), moe_w_up.astype(BF), moe_w_down.astype(BF))

    moe_l = moe_c = mods3 = None
    for l in range(depth):
        ctx_out = l < depth - 1
        prev_mods3 = mods3
        mods3 = _ada(cc, ada_w, ada_b3, l).reshape(MOD_ROWS * 6, 1, D)
        z_l, x = _in_proj(x, moe_l, prev_mods3, mods3, lat_row, norm1_3, w_in_r, l)
        z_c, ctx = _in_proj(ctx, moe_c, prev_mods3, mods3, ctx_row_fn, norm1_3, w_in_r, l)

        oa_l, oa_c = _na_attention(z_l, z_c, na_bias, l, ctx_out)

        qb_l, kb_l, vb_l = _mla_proj(z_l, qn3, kvn3, mla_w, rope_tabs, l)
        qb_c, kb_c, vb_c = _mla_proj(z_c, qn3, kvn3, mla_w, None, l)
        ob_l = _mla_attn(qb_l, kb_c, vb_c, kb_l, vb_l)

        oc_l, oc_c = _gla(z_l, z_c, gla_params, l, ctx_out)

        x, h_l, aff_l = _merge(oa_l, ob_l, oc_l, z_l, x, mods3, lat_row, norm2_3, merge_w, l, tm=512)
        pos_l, post_l = _route(aff_l, cap_l, 0)
        last = (x, mods3, final_norm) if l == depth - 1 else None
        moe_l = _moe(h_l, pos_l, post_l, aff_l, moe_w, l, cap_l, final=last)

        if ctx_out:
            ob_c = _mla_attn(qb_c, kb_c, vb_c)
            ctx, h_c, aff_c = _merge(oa_c, ob_c, oc_c, z_c, ctx, mods3, ctx_row_fn, norm2_3, merge_w, l, tm=256)
            pos_c, post_c = _route(aff_c, cap_c, cap_c)
            flat = lambda a: jnp.swapaxes(a, 0, 1).reshape(1, N_EXPERTS, B * L)
            moe_c = _moe(h_c.reshape(1, B * L, D), flat(pos_c), post_c.reshape(1, B * L, LANES), flat(aff_c),
                         moe_w, l, B * cap_c).reshape(B, L, D)
    return moe_l
```

```python
import functools

import numpy as np
import jax
import jax.numpy as jnp
from jax import lax
from jax.experimental import pallas as pl
from jax.experimental.pallas import tpu as pltpu

BF = jnp.bfloat16
F32 = jnp.float32

D_MODEL = 1024
GRID_W = 64
EPS = 1e-6
NA_HEADS = 4
NA_HEAD_DIM = 64
NA_WIN_H = 8
NA_WIN_W = 16
MLA_HEADS = 4
MLA_Q_RANK = 256
MLA_KV_RANK = 128
MLA_NOPE_DIM = 64
MLA_ROPE_DIM = 32
MLA_V_DIM = 64
ROPE_THETA = 10000.0
GLA_HEADS = 4
GLA_DK = 64
GLA_DV = 128
GLA_GATE_RANK = 16
GLA_TAU = 16.0
GLA_CHUNK = 64
N_EXPERTS = 16
EXPERT_FF = 1024
EC_CAPACITY = 2
IN_SIZES = (256, 256, 256, 256, 128, 32, 256, 256, 512, 16, 16, 512, 1024, 1024, 1024)

LANES = 128
VMEM_LIMIT = 56 * 1024 * 1024
MOD_ROWS = 16

Z_MA, Z_MB, Z_MC = 0, 1024, 2048
Z_GV, Z_OG = 3072, 3584
Z_QA, Z_KA, Z_VA = 4096, 4352, 4608
Z_GQ, Z_GK = 4864, 5120
Z_MLA = 5376
Z_SM = 5760
Z_COLS = 5888
SM_KR, SM_AF, SM_AB = 0, 32, 48

SUPER = 256
NEG = -1e30


def _cparams(*sem):
    return pltpu.CompilerParams(dimension_semantics=sem, vmem_limit_bytes=VMEM_LIMIT)


def _layer(a, l):
    return pl.BlockSpec((None,) + a.shape[1:], lambda *_: (l,) + (0,) * (a.ndim - 1), pipeline_mode=pl.Buffered(1))


def _whole(a):
    return pl.BlockSpec(a.shape, lambda *_: (0,) * a.ndim, pipeline_mode=pl.Buffered(1))


def _dot(a, b):
    return jnp.dot(a, b, preferred_element_type=F32)


def _dot_nt(a, b):
    return lax.dot_general(a, b, (((1,), (1,)), ((), ())), preferred_element_type=F32)


def _dot_tn(a, b):
    return lax.dot_general(a, b, (((0,), (0,)), ((), ())), preferred_element_type=F32)


def _rms(x, g):
    return x * lax.rsqrt(jnp.mean(x * x, axis=-1, keepdims=True) + EPS) * g


def _split3(x):
    hi = x.astype(BF)
    r = x - hi.astype(F32)
    mid = r.astype(BF)
    lo = (r - mid.astype(F32)).astype(BF)
    return hi, mid, lo


def _ada_kernel(c_ref, w_ref, b_ref, o_ref):
    c = c_ref[...]
    a = (c * jax.nn.sigmoid(c)).astype(BF)
    o_ref[...] = _dot(a, w_ref[...].astype(BF)) + b_ref[...]


def _ada(cc, ada_w, ada_b3, l):
    n = ada_w.shape[2]
    tn = 1536
    return pl.pallas_call(
        _ada_kernel,
        grid=(n // tn,),
        in_specs=[pl.BlockSpec((MOD_ROWS, D_MODEL), lambda j: (0, 0)),
                  pl.BlockSpec((None, D_MODEL, tn), lambda j: (l, 0, j)),
                  pl.BlockSpec((None, 1, tn), lambda j: (l, 0, j))],
        out_specs=pl.BlockSpec((MOD_ROWS, tn), lambda j: (0, j)),
        out_shape=jax.ShapeDtypeStruct((MOD_ROWS, n), F32),
        compiler_params=_cparams("arbitrary"),
        name="ada_mod",
    )(cc, ada_w, ada_b3)


def _in_proj_kernel(*refs, with_moe):
    if with_moe:
        x_ref, moe_ref, gate_ref, g_ref, shift_ref, scale_ref, w_ref, o_ref, xo_ref, h_ref = refs
        x = x_ref[0] + gate_ref[0] * moe_ref[0]
        xo_ref[0] = x
    else:
        x_ref, g_ref, shift_ref, scale_ref, w_ref, o_ref, h_ref = refs
        x = x_ref[0]
    h = _rms(x, g_ref[...]) * (1.0 + scale_ref[0]) + shift_ref[0]
    h_ref[...] = h.astype(BF)
    for n0 in range(0, Z_COLS, 256):
        o_ref[0, :, n0:n0 + 256] = _dot(h_ref[...], w_ref[:, n0:n0 + 256]).astype(BF)


def _in_proj(x, moe, prev_mods3, mods3, mod_row, norm1_3, w, l, tm=512):
    B, T, D = x.shape
    tm = min(tm, T)
    with_moe = moe is not None

    def mod_spec(k):
        return pl.BlockSpec((1, 1, D), lambda b, i: (mod_row(b) * 6 + k, 0, 0))

    x_spec = pl.BlockSpec((1, tm, D), lambda b, i: (b, i, 0))
    in_specs = [x_spec]
    args = [x]
    if with_moe:
        in_specs += [x_spec, mod_spec(5)]
        args += [moe, prev_mods3]
    in_specs += [_layer(norm1_3, l), mod_spec(0), mod_spec(1), _layer(w, l)]
    args += [norm1_3, mods3, mods3, w]
    z_spec = pl.BlockSpec((1, tm, Z_COLS), lambda b, i: (b, i, 0))
    z_shape = jax.ShapeDtypeStruct((B, T, Z_COLS), BF)
    if with_moe:
        out_specs, out_shape = [z_spec, x_spec], [z_shape, jax.ShapeDtypeStruct((B, T, D), F32)]
    else:
        out_specs, out_shape = z_spec, z_shape
    res = pl.pallas_call(
        functools.partial(_in_proj_kernel, with_moe=with_moe),
        grid=(B, T // tm),
        in_specs=in_specs, out_specs=out_specs, out_shape=out_shape,
        scratch_shapes=[pltpu.VMEM((tm, D), BF)],
        compiler_params=_cparams("arbitrary", "arbitrary"),
        name="in_proj",
    )(*args)
    return (res[0], res[1]) if with_moe else (res, x)


def _softmax_pv_many(heads):
    maxes = []
    for s_list, _ in heads:
        m = s_list[0].max(axis=-1, keepdims=True)
        for s in s_list[1:]:
            m = jnp.maximum(m, s.max(axis=-1, keepdims=True))
        maxes.append(m)
    probs = [[jnp.exp((s - m).astype(BF)) for s in s_list] for (s_list, _), m in zip(heads, maxes)]
    outs = []
    for (_, v_list), p_list in zip(heads, probs):
        o = _dot(p_list[0], v_list[0])
        for p, v in zip(p_list[1:], v_list[1:]):
            o = o + _dot(p, v)
        outs.append(o / pltpu.roll(o, LANES // 2, axis=1))
    return outs


def _softmax_pv_sum(heads):
    maxes = []
    for s_list, _ in heads:
        m = s_list[0].max(axis=-1, keepdims=True)
        for s in s_list[1:]:
            m = jnp.maximum(m, s.max(axis=-1, keepdims=True))
        maxes.append(m)
    probs = [[jnp.exp(s - m) for s in s_list] for (s_list, _), m in zip(heads, maxes)]
    outs = []
    for (_, v_list), p_list in zip(heads, probs):
        l = p_list[0].sum(axis=-1, keepdims=True)
        o = _dot(p_list[0].astype(BF), v_list[0])
        for p, v in zip(p_list[1:], v_list[1:]):
            l = l + p.sum(axis=-1, keepdims=True)
            o = o + _dot(p.astype(BF), v)
        outs.append(o / l)
    return outs


def _lane_masks():
    lane = lax.broadcasted_iota(jnp.int32, (1, LANES), 1)
    return lane < 64, lane >= 64


NA_ROWS_PER_STEP = 2
NA_BAND = 10
NA_CASE_ROWS = (0, 2, 4, 28, 30)


def _na_bias(rpb, rows):
    depth = rpb.shape[0]
    qc = np.arange(GRID_W)
    kj = np.arange(GRID_W)
    cs = np.clip(qc - NA_WIN_W // 2, 0, GRID_W - NA_WIN_W)
    col_ok = (kj[None, :] >= cs[:, None]) & (kj[None, :] < cs[:, None] + NA_WIN_W)
    co = kj[None, :] - qc[:, None] + (NA_WIN_W - 1)
    onehot = np.zeros((2 * NA_WIN_W - 1, GRID_W, GRID_W), np.float32)
    qi, ki = np.nonzero(col_ok)
    onehot[co[qi, ki], qi, ki] = 1.0
    toep = jnp.einsum("lhrc,cp->lhrp", rpb.astype(F32), jnp.asarray(onehot.reshape(2 * NA_WIN_W - 1, -1)),
                      precision=lax.Precision.HIGHEST)
    toep = (toep.reshape(depth, NA_HEADS, 2 * NA_WIN_H - 1, GRID_W, GRID_W)
            + jnp.asarray(np.where(col_ok, 0.0, NEG), F32))
    neg = jnp.full((depth, NA_HEADS, GRID_W, GRID_W), NEG, F32)
    cases = []
    for r0 in NA_CASE_ROWS:
        bs = int(np.clip(r0 - NA_WIN_H // 2, 0, rows - NA_BAND))
        q_rows = []
        for qr in range(NA_ROWS_PER_STEP):
            r = r0 + qr
            rs = int(np.clip(r - NA_WIN_H // 2, 0, rows - NA_WIN_H))
            blocks = []
            for i in range(NA_BAND):
                krow = bs + i
                blocks.append(toep[:, :, krow - r + NA_WIN_H - 1] if rs <= krow < rs + NA_WIN_H else neg)
            q_rows.append(jnp.concatenate(blocks, axis=-1))
        cases.append(jnp.concatenate(q_rows, axis=-2))
    return jnp.stack(cases, axis=2)


def _na_kernel(*refs, rows, ctx_out):
    if ctx_out:
        ql_ref, kl_ref, vl_ref, qc_ref, kc_ref, vc_ref, bias_ref, ol_ref, oc_ref = refs
    else:
        ql_ref, kl_ref, vl_ref, kc_ref, vc_ref, bias_ref, ol_ref = refs
    scale = NA_HEAD_DIM ** -0.5
    masks = _lane_masks()
    nq = NA_ROWS_PER_STEP * GRID_W
    nk = NA_BAND * GRID_W

    def body(rp, carry):
        r0 = rp * NA_ROWS_PER_STEP
        bs = jnp.clip(r0 - NA_WIN_H // 2, 0, rows - NA_BAND)
        case = jnp.where(r0 < 4, r0 // 2, jnp.where(r0 > rows - 6, (r0 - (rows - 4)) // 2 + 3, 2))
        q0 = pl.multiple_of(r0 * GRID_W, nq)
        k0 = pl.multiple_of(bs * GRID_W, 2 * GRID_W)
        heads = []
        for hp in range(NA_HEADS // 2):
            sl = slice(hp * LANES, (hp + 1) * LANES)
            q = ql_ref[0, pl.ds(q0, nq), sl] * scale
            kb = kl_ref[0, pl.ds(k0, nk), sl]
            vb = vl_ref[0, pl.ds(k0, nk), sl]
            kc = kc_ref[0, :, sl]
            vc = vc_ref[0, :, sl]
            for hh in range(2):
                qm = jnp.where(masks[hh], q, jnp.zeros_like(q))
                s_loc = _dot_nt(qm, kb) + bias_ref[2 * hp + hh, case]
                s_ctx = _dot_nt(qm, kc)
                heads.append(([s_loc, s_ctx], [vb, vc]))
        outs = _softmax_pv_sum(heads)
        for hp in range(NA_HEADS // 2):
            sl = slice(hp * LANES, (hp + 1) * LANES)
            ol_ref[0, pl.ds(q0, nq), sl] = jnp.where(masks[0], outs[2 * hp], outs[2 * hp + 1]).astype(BF)
        return carry

    lax.fori_loop(0, rows // NA_ROWS_PER_STEP, body, 0)

    if ctx_out:
        heads = []
        for hp in range(NA_HEADS // 2):
            sl = slice(hp * LANES, (hp + 1) * LANES)
            q = qc_ref[0, :, sl] * scale
            kc = kc_ref[0, :, sl]
            for hh in range(2):
                qm = jnp.where(masks[hh], q, jnp.zeros_like(q))
                heads.append(([_dot_nt(qm, kc)], [vc_ref[0, :, sl]]))
        outs = _softmax_pv_sum(heads)
        for hp in range(NA_HEADS // 2):
            sl = slice(hp * LANES, (hp + 1) * LANES)
            oc_ref[0, :, sl] = jnp.where(masks[0], outs[2 * hp], outs[2 * hp + 1]).astype(BF)


def _na_attention(z_l, z_c, bias, l, ctx_out):
    B, T, _ = z_l.shape
    L = z_c.shape[1]
    rows = T // GRID_W

    def zl(col):
        return pl.BlockSpec((1, T, 256), lambda b: (b, 0, col // 256))

    def zc(col):
        return pl.BlockSpec((1, L, 256), lambda b: (b, 0, col // 256))

    ol_spec = pl.BlockSpec((1, T, 256), lambda b: (b, 0, 0))
    ol_shape = jax.ShapeDtypeStruct((B, T, 256), BF)
    if ctx_out:
        in_specs = [zl(Z_QA), zl(Z_KA), zl(Z_VA), zc(Z_QA), zc(Z_KA), zc(Z_VA), _layer(bias, l)]
        args = [z_l, z_l, z_l, z_c, z_c, z_c, bias]
        out_specs = [ol_spec, pl.BlockSpec((1, L, 256), lambda b: (b, 0, 0))]
        out_shape = [ol_shape, jax.ShapeDtypeStruct((B, L, 256), BF)]
    else:
        in_specs = [zl(Z_QA), zl(Z_KA), zl(Z_VA), zc(Z_KA), zc(Z_VA), _layer(bias, l)]
        args = [z_l, z_l, z_l, z_c, z_c, bias]
        out_specs, out_shape = ol_spec, ol_shape
    res = pl.pallas_call(
        functools.partial(_na_kernel, rows=rows, ctx_out=ctx_out),
        grid=(B,), in_specs=in_specs, out_specs=out_specs, out_shape=out_shape,
        compiler_params=_cparams("arbitrary"),
        name="na_attn",
    )(*args)
    return (res[0], res[1]) if ctx_out else (res, None)


MLA_QK = MLA_NOPE_DIM + MLA_ROPE_DIM
MLA_W = MLA_HEADS * LANES


def _mla_proj_kernel(*refs, rope):
    if rope:
        (zm_ref, sm_ref, qn_ref, kvn_ref, wq_ref, wqs_ref, wk_ref, wv_ref, e_ref, es_ref, vone_ref, cos_ref, sin_ref,
         q_ref, k_ref, v_ref) = refs
    else:
        zm_ref, sm_ref, qn_ref, kvn_ref, wq_ref, wk_ref, wv_ref, e_ref, vone_ref, q_ref, k_ref, v_ref = refs
    zm = zm_ref[0].astype(F32)
    nq = _rms(zm[:, :MLA_Q_RANK], qn_ref[...]).astype(BF)
    nkv = _rms(zm[:, MLA_Q_RANK:], kvn_ref[...]).astype(BF)
    sm = sm_ref[0]
    scale = MLA_QK ** -0.5
    q = _dot(nq, wq_ref[...])
    k = _dot(nkv, wk_ref[...]) + _dot(sm, e_ref[...])
    if rope:
        cos = cos_ref[...]
        sin = sin_ref[...]
        q = q * cos + _dot(nq, wqs_ref[...]) * sin
        k = k * cos + _dot(sm, es_ref[...]) * sin
    q_ref[0] = (q * scale).astype(BF)
    k_ref[0] = k.astype(BF)
    v_ref[0] = (_dot(nkv, wv_ref[...]) + vone_ref[...]).astype(BF)


def _rope_swap(e):
    return np.where(e % 16 < 8, e + 8, e - 8)


def _mla_weights(w_uq, w_ukv):
    depth = w_uq.shape[0]
    quarter = MLA_ROPE_DIM // 4
    q4 = w_uq.reshape(depth, MLA_Q_RANK, MLA_HEADS, MLA_QK)
    nope, rope = q4[..., :MLA_NOPE_DIM], q4[..., MLA_NOPE_DIM:]
    rope_sw = jnp.flip(rope.reshape(depth, MLA_Q_RANK, MLA_HEADS, 2, 2, quarter), axis=4).reshape(rope.shape)
    zpad = jnp.zeros(rope.shape, w_uq.dtype)
    wq = jnp.concatenate([nope, rope, zpad], axis=-1).reshape(depth, MLA_Q_RANK, MLA_W)
    wqs = jnp.concatenate([jnp.zeros(nope.shape, w_uq.dtype), rope_sw, zpad], axis=-1).reshape(depth, MLA_Q_RANK, MLA_W)
    kv4 = w_ukv.reshape(depth, MLA_KV_RANK, MLA_HEADS, MLA_NOPE_DIM + MLA_V_DIM)
    k_nope, v = kv4[..., :MLA_NOPE_DIM], kv4[..., MLA_NOPE_DIM:]
    wk = jnp.concatenate([k_nope, jnp.zeros(k_nope.shape, w_ukv.dtype)], axis=-1).reshape(depth, MLA_KV_RANK, MLA_W)
    vz = jnp.zeros(v.shape[:2] + (1, MLA_V_DIM), w_ukv.dtype)
    wv = jnp.concatenate([jnp.concatenate([v[:, :, h:h + 1], vz] if h % 2 == 0 else [vz, v[:, :, h:h + 1]], axis=-1)
                          for h in range(MLA_HEADS)], axis=2).reshape(depth, MLA_KV_RANK, MLA_W)
    e = np.arange(MLA_ROPE_DIM)
    em = np.zeros((LANES, MLA_W), np.float32)
    ems = np.zeros((LANES, MLA_W), np.float32)
    for h in range(MLA_HEADS):
        em[SM_KR + e, h * LANES + MLA_NOPE_DIM + e] = 1.0
        ems[SM_KR + _rope_swap(e), h * LANES + MLA_NOPE_DIM + e] = 1.0
    lane = np.arange(MLA_W) % LANES
    v_ones = ((lane >= MLA_V_DIM) == ((np.arange(MLA_W) // LANES) % 2 == 0)).astype(np.float32)[None, :]
    return (wq.astype(BF), wqs.astype(BF), wk.astype(BF), wv.astype(BF), jnp.asarray(em, BF), jnp.asarray(ems, BF),
            jnp.asarray(v_ones, F32))


def _rope_tables(T):
    t = jnp.arange(T)
    quarter = MLA_ROPE_DIM // 4
    inv_freq = ROPE_THETA ** (-jnp.arange(quarter, dtype=F32) / quarter)
    ang_r = (t // GRID_W).astype(F32)[:, None] * inv_freq
    ang_c = (t % GRID_W).astype(F32)[:, None] * inv_freq
    cr, sr, ccol, scol = jnp.cos(ang_r), jnp.sin(ang_r), jnp.cos(ang_c), jnp.sin(ang_c)
    ones = jnp.ones((T, MLA_NOPE_DIM), F32)
    pad1 = jnp.ones((T, LANES - MLA_QK), F32)
    cos_h = jnp.concatenate([ones, cr, cr, ccol, ccol, pad1], axis=1)
    sin_h = jnp.concatenate([0.0 * ones, -sr, sr, -scol, scol, 0.0 * pad1], axis=1)
    return jnp.tile(cos_h, (1, MLA_HEADS)), jnp.tile(sin_h, (1, MLA_HEADS))


def _mla_proj(z, qn3, kvn3, weights, rope_tabs, l, tm=512):
    B, T, _ = z.shape
    wq, wqs, wk, wv, em, ems, v_ones = weights
    rope = rope_tabs is not None
    tm = min(tm, T)
    zm_spec = pl.BlockSpec((1, tm, 384), lambda i, b: (b, i, Z_MLA // 384))
    sm_spec = pl.BlockSpec((1, tm, LANES), lambda i, b: (b, i, Z_SM // LANES))
    if rope:
        tab_spec = pl.BlockSpec((tm, MLA_W), lambda i, b: (i, 0))
        ins = [z, z, qn3, kvn3, wq, wqs, wk, wv, em, ems, v_ones, rope_tabs[0], rope_tabs[1]]
        in_specs = ([zm_spec, sm_spec] + [_layer(a, l) for a in ins[2:8]] + [_whole(em), _whole(ems), _whole(v_ones)]
                    + [tab_spec, tab_spec])
    else:
        ins = [z, z, qn3, kvn3, wq, wk, wv, em, v_ones]
        in_specs = [zm_spec, sm_spec] + [_layer(a, l) for a in ins[2:7]] + [_whole(em), _whole(v_ones)]
    return pl.pallas_call(
        functools.partial(_mla_proj_kernel, rope=rope),
        grid=(T // tm, B), in_specs=in_specs,
        out_specs=[pl.BlockSpec((1, tm, MLA_W), lambda i, b: (b, i, 0)),
                   pl.BlockSpec((1, tm, MLA_W), lambda i, b: (b, i, 0)),
                   pl.BlockSpec((1, tm, MLA_W), lambda i, b: (b, i, 0))],
        out_shape=[jax.ShapeDtypeStruct((B, T, MLA_W), BF)] * 3,
        compiler_params=_cparams("arbitrary", "arbitrary"),
        name="mla_proj",
    )(*ins)


def _mla_attn_kernel(*refs, with_latent):
    if with_latent:
        q_ref, kc_ref, vc_ref, kl_ref, vl_ref, o_ref = refs
    else:
        q_ref, kc_ref, vc_ref, o_ref = refs
    masks = _lane_masks()
    heads = []
    for h in range(MLA_HEADS):
        hsl = slice(h * LANES, (h + 1) * LANES)
        q = q_ref[0, :, hsl]
        s_list = [_dot_nt(q, kc_ref[0, :, hsl])]
        v_list = [vc_ref[0, :, hsl]]
        if with_latent:
            s_list.append(_dot_nt(q, kl_ref[0, :, hsl]))
            v_list.append(vl_ref[0, :, hsl])
        heads.append((s_list, v_list))
    outs = _softmax_pv_many(heads)
    for hp in range(MLA_HEADS // 2):
        vsl = slice(hp * LANES, (hp + 1) * LANES)
        o_ref[0, :, vsl] = jnp.where(masks[0], outs[2 * hp], outs[2 * hp + 1]).astype(BF)


def _mla_attn(q, k_c, v_c, k_l=None, v_l=None, tq=256):
    B, Tq, W = q.shape
    L = k_c.shape[1]
    with_latent = k_l is not None
    in_specs = [pl.BlockSpec((1, tq, W), lambda b, i: (b, i, 0)),
                pl.BlockSpec((1, L, W), lambda b, i: (b, 0, 0)),
                pl.BlockSpec((1, L, W), lambda b, i: (b, 0, 0))]
    args = [q, k_c, v_c]
    if with_latent:
        T = k_l.shape[1]
        in_specs += [pl.BlockSpec((1, T, W), lambda b, i: (b, 0, 0)),
                     pl.BlockSpec((1, T, W), lambda b, i: (b, 0, 0))]
        args += [k_l, v_l]
    return pl.pallas_call(
        functools.partial(_mla_attn_kernel, with_latent=with_latent),
        grid=(B, Tq // tq), in_specs=in_specs,
        out_specs=pl.BlockSpec((1, tq, 256), lambda b, i: (b, i, 0)),
        out_shape=jax.ShapeDtypeStruct((B, Tq, 256), BF),
        compiler_params=_cparams("arbitrary", "arbitrary"),
        name="mla_attn",
    )(*args)


def _gla_superchunks(jobs, st_ref):
    n_chunk = SUPER // GLA_CHUNK
    masks = _lane_masks()
    rowblk = lax.broadcasted_iota(jnp.int32, (SUPER, LANES), 0) // GLA_CHUNK
    own = ((lax.broadcasted_iota(jnp.int32, (2 * GLA_DV, LANES), 0) < GLA_DV)
           == (lax.broadcasted_iota(jnp.int32, (2 * GLA_DV, LANES), 1) < GLA_DK))
    gs = [jax.nn.log_sigmoid(_dot(sm, w_a) + b_a) * (1.0 / GLA_TAU) for (_, _, _, sm, w_a, b_a, _, _, _) in jobs]
    pre = []
    for (q, k, v, sm, w_a, b_a, tri, d, reverse), g in zip(jobs, gs):
        g_hi = g.astype(BF)
        g_lo = (g - g_hi.astype(F32)).astype(BF)
        b = _dot(tri, g_hi) + _dot(tri, g_lo)
        ends = [i * GLA_CHUNK if reverse else (i + 1) * GLA_CHUNK - 1 for i in range(n_chunk)]
        tot = jnp.concatenate([jnp.broadcast_to(b[r:r + 1, :], (GLA_CHUNK, b.shape[1])) for r in ends], axis=0)
        pre.append((q.astype(F32) * (GLA_DK ** -0.5) * jnp.exp(b), k.astype(F32) * jnp.exp(-b),
                    k.astype(F32) * jnp.exp(tot - b), jnp.exp(tot), tri > 0))
    outs = [[] for _ in jobs]
    for hp in range(GLA_HEADS // 2):
        sl = slice(hp * LANES, (hp + 1) * LANES)
        stage = []
        for (q, k, v, sm, w_a, b_a, tri, d, reverse), (qd_all, kd_all, ke_all, dec_all, keep) in zip(jobs, pre):
            qd = qd_all[:, sl]
            kd = kd_all[:, sl].astype(BF)
            ke = ke_all[:, sl].astype(BF)
            zero = jnp.zeros_like(ke)
            ke_bd = jnp.concatenate([jnp.where(rowblk == i, ke, zero) for i in range(n_chunk)], axis=1)
            v_pair = v[:, hp * 2 * GLA_DV:(hp + 1) * 2 * GLA_DV]
            kv_t = _dot_tn(v_pair, ke_bd)
            a = [jnp.where(keep, _dot_nt(jnp.where(masks[hh], qd, 0.0).astype(BF), kd), 0.0).astype(BF)
                 for hh in range(2)]
            stage.append((qd, dec_all[:, sl], v_pair, kv_t, a))
        inters = []
        for (q, k, v, sm, w_a, b_a, tri, d, reverse), (qd, dec, v_pair, kv_t, a) in zip(jobs, stage):
            s = st_ref[d, hp]
            inter = [None] * n_chunk
            for i in (range(n_chunk - 1, -1, -1) if reverse else range(n_chunk)):
                rows = slice(i * GLA_CHUNK, (i + 1) * GLA_CHUNK)
                inter[i] = _dot_nt(qd[rows].astype(BF), s.astype(BF))
                s = s * dec[i * GLA_CHUNK:i * GLA_CHUNK + 1, :] + jnp.where(own, kv_t[:, i * LANES:(i + 1) * LANES], 0.0)
            st_ref[d, hp] = s
            inters.append(jnp.concatenate(inter, axis=0))
        for n, ((qd, dec, v_pair, kv_t, a), o_inter) in enumerate(zip(stage, inters)):
            for hh in range(2):
                hv = slice(hh * GLA_DV, (hh + 1) * GLA_DV)
                outs[n].append(_dot(a[hh], v_pair[:, hv]) + o_inter[:, hv])
    return [jnp.concatenate(o, axis=1) for o in outs]


def _gla_finish(o, og, onorm):
    ys = []
    for h in range(GLA_HEADS):
        ys.append(_rms(o[:, h * GLA_DV:(h + 1) * GLA_DV], onorm))
    y = jnp.concatenate(ys, axis=1)
    ogf = og.astype(F32)
    return (y * (ogf * jax.nn.sigmoid(ogf))).astype(BF)


def _gla_kernel(*refs, n_lat, ctx_out):
    (ql_ref, kl_ref, vl_ref, sml_ref, ogl_ref, qc_ref, kc_ref, vc_ref, smc_ref, ogc_ref,
     waf_ref, baf_ref, wab_ref, bab_ref, onorm_ref, trif_ref, trib_ref) = refs[:17]
    if ctx_out:
        ol_ref, oc_ref, of_ref, ob_ref, st_ref = refs[17:]
    else:
        ol_ref, of_ref, ob_ref, st_ref = refs[17:]
    onorm = onorm_ref[...]

    def both(fwd_in, bwd_in):
        return _gla_superchunks([fwd_in + (waf_ref[...], baf_ref[...], trif_ref[...], 0, False),
                                 bwd_in + (wab_ref[...], bab_ref[...], trib_ref[...], 1, True)], st_ref)

    st_ref[...] = jnp.zeros_like(st_ref)
    ctx_in = (qc_ref[0], kc_ref[0], vc_ref[0], smc_ref[0])
    o_cf, o_cb = both(ctx_in, ctx_in)
    if ctx_out:
        oc_ref[0] = _gla_finish(o_cf + o_cb, ogc_ref[0], onorm)

    def body(j, carry):
        rf = pl.ds(pl.multiple_of(j * SUPER, SUPER), SUPER)
        rb = pl.ds(pl.multiple_of((n_lat - 1 - j) * SUPER, SUPER), SUPER)
        o_f, o_b = both((ql_ref[0, rf, :], kl_ref[0, rf, :], vl_ref[0, rf, :], sml_ref[0, rf, :]),
                        (ql_ref[0, rb, :], kl_ref[0, rb, :], vl_ref[0, rb, :], sml_ref[0, rb, :]))
        of_ref[rf, :] = o_f
        ob_ref[rb, :] = o_b
        return carry

    lax.fori_loop(0, n_lat, body, 0)

    def finish(j, carry):
        r = pl.ds(pl.multiple_of(j * SUPER, SUPER), SUPER)
        ol_ref[0, r, :] = _gla_finish(of_ref[r, :] + ob_ref[r, :], ogl_ref[0, r, :], onorm)
        return carry

    lax.fori_loop(0, n_lat, finish, 0)


def _gla_consts():
    r = np.arange(SUPER)
    same = (r[:, None] // GLA_CHUNK) == (r[None, :] // GLA_CHUNK)
    return jnp.asarray(same & (r[None, :] <= r[:, None]), BF), jnp.asarray(same & (r[None, :] >= r[:, None]), BF)


def _gla(z_l, z_c, params, l, ctx_out):
    B, T, _ = z_l.shape
    L = z_c.shape[1]
    assert L == SUPER and T % SUPER == 0
    kw = GLA_HEADS * GLA_DK
    vw = GLA_HEADS * GLA_DV
    consts = _gla_consts()

    def zspec(n, width, col):
        return pl.BlockSpec((1, n, width), lambda b: (b, 0, col // width))

    in_specs = ([zspec(T, kw, Z_GQ), zspec(T, kw, Z_GK), zspec(T, vw, Z_GV), zspec(T, LANES, Z_SM), zspec(T, vw, Z_OG),
                 zspec(L, kw, Z_GQ), zspec(L, kw, Z_GK), zspec(L, vw, Z_GV), zspec(L, LANES, Z_SM), zspec(L, vw, Z_OG)]
                + [_layer(a, l) for a in params] + [_whole(a) for a in consts])
    ol_spec = pl.BlockSpec((1, T, vw), lambda b: (b, 0, 0))
    ol_shape = jax.ShapeDtypeStruct((B, T, vw), BF)
    if ctx_out:
        out_specs = [ol_spec, pl.BlockSpec((1, L, vw), lambda b: (b, 0, 0))]
        out_shape = [ol_shape, jax.ShapeDtypeStruct((B, L, vw), BF)]
    else:
        out_specs, out_shape = ol_spec, ol_shape
    scratch = [pltpu.VMEM((T, vw), F32), pltpu.VMEM((T, vw), F32),
               pltpu.VMEM((2, GLA_HEADS // 2, 2 * GLA_DV, LANES), F32)]
    res = pl.pallas_call(
        functools.partial(_gla_kernel, n_lat=T // SUPER, ctx_out=ctx_out),
        grid=(B,), in_specs=in_specs, out_specs=out_specs, out_shape=out_shape,
        scratch_shapes=scratch,
        compiler_params=_cparams("arbitrary"),
        name="gla",
    )(*([z_l] * 5 + [z_c] * 5 + list(params) + list(consts)))
    return (res[0], res[1]) if ctx_out else (res, None)


MERGE_SUB = 256


def _merge_kernel(oa_ref, ob_ref, oc_ref, ma_ref, mb_ref, mc_ref, x_ref, gate_ref, shift_ref, scale_ref, g_ref,
                  wa_ref, wb_ref, wc_ref, wo_ref, wr_ref, xo_ref, h_ref, aff_ref, y_ref):
    tm = x_ref.shape[1]
    gate = gate_ref[0]
    subs = [slice(r0, r0 + MERGE_SUB) for r0 in range(0, tm, MERGE_SUB)]

    def branch(o_ref, m_ref, w_ref, rows, n0):
        m = m_ref[0, rows, n0:n0 + 256].astype(F32)
        return jax.nn.sigmoid(m) * _dot(o_ref[0, rows, :], w_ref[:, n0:n0 + 256])

    for n0 in range(0, D_MODEL, 256):
        for rows in subs:
            y = (branch(oa_ref, ma_ref, wa_ref, rows, n0) + branch(ob_ref, mb_ref, wb_ref, rows, n0)
                 + branch(oc_ref, mc_ref, wc_ref, rows, n0))
            y_ref[rows, n0:n0 + 256] = y.astype(BF)
    for n0 in range(0, D_MODEL, 256):
        for rows in subs:
            xo_ref[0, rows, n0:n0 + 256] = (x_ref[0, rows, n0:n0 + 256]
                                            + gate[:, n0:n0 + 256] * _dot(y_ref[rows, :], wo_ref[:, n0:n0 + 256]))
    hs = [_rms(xo_ref[0, rows, :], g_ref[...]) * (1.0 + scale_ref[0]) + shift_ref[0] for rows in subs]
    wr = wr_ref[...]
    for rows, h in zip(subs, hs):
        h_hi = h.astype(BF)
        h_lo = (h - h_hi.astype(F32)).astype(BF)
        h_ref[0, rows, :] = h_hi
        part = _dot_nt(wr, h_hi)
        logits = part[:N_EXPERTS] + part[N_EXPERTS:] + _dot_nt(wr[:N_EXPERTS], h_lo)
        e = jnp.exp(logits - logits.max(axis=0, keepdims=True))
        aff_ref[0, :, rows] = e / e.sum(axis=0, keepdims=True)


def _merge(oa, ob, oc, z, x, mods3, mod_row, norm2_3, weights, l, tm):
    B, T, D = x.shape
    wa, wb, wc, wo, wr_t = weights

    def tok(width, col=0):
        return pl.BlockSpec((1, tm, width), lambda b, i: (b, i, col // width))

    def mod_spec(k):
        return pl.BlockSpec((1, 1, D), lambda b, i: (mod_row(b) * 6 + k, 0, 0))

    in_specs = [tok(256), tok(256), tok(512), tok(D, Z_MA), tok(D, Z_MB), tok(D, Z_MC), tok(D),
                mod_spec(2), mod_spec(3), mod_spec(4), _layer(norm2_3, l)] + [_layer(w, l) for w in weights]
    return pl.pallas_call(
        _merge_kernel,
        grid=(B, T // tm), in_specs=in_specs,
        out_specs=[tok(D), tok(D), pl.BlockSpec((1, N_EXPERTS, tm), lambda b, i: (b, 0, i))],
        out_shape=[jax.ShapeDtypeStruct((B, T, D), F32), jax.ShapeDtypeStruct((B, T, D), BF),
                   jax.ShapeDtypeStruct((B, N_EXPERTS, T), F32)],
        scratch_shapes=[pltpu.VMEM((tm, D), BF)],
        compiler_params=_cparams("arbitrary", "arbitrary"),
        name="merge",
    )(oa, ob, oc, z, z, z, x, mods3, mods3, mods3, norm2_3, wa, wb, wc, wo, wr_t)


def _route_kernel(aff_ref, tri_ref, pos_ref, post_ref, *, cap, slot_stride):
    aff = aff_ref[0]
    E, T = aff.shape

    def refine(thr, shift, patterns):
        best = thr
        for c in patterns:
            cand = thr | (jnp.int32(c) << shift)
            cnt = jnp.sum((aff >= pltpu.bitcast(cand, F32)).astype(F32), axis=1, keepdims=True)
            best = jnp.where(cnt >= cap, cand, best)
        return best

    thr = lax.fori_loop(0, 15, lambda i, t: refine(t, 29 - 2 * i, (1, 2, 3)), jnp.zeros((E, 1), jnp.int32))
    thr = refine(thr, 0, (1,))
    thr_f = pltpu.bitcast(thr, F32)
    gt = aff > thr_f
    eq = aff == thr_f
    need = cap - jnp.sum(gt.astype(F32), axis=1, keepdims=True)
    tri = tri_ref[...]

    def excl_prefix(flags):
        run = jnp.zeros((E, 1), F32)
        blocks = []
        for j in range(T // LANES):
            blk = flags[:, j * LANES:(j + 1) * LANES].astype(F32)
            blocks.append(_dot(blk.astype(BF), tri) + run)
            run = run + jnp.sum(blk, axis=1, keepdims=True)
        return jnp.concatenate(blocks, axis=1)

    sel = gt | (eq & (excl_prefix(eq) < need))
    base = (slot_stride * pl.program_id(0)).astype(F32)
    pos = jnp.where(sel, excl_prefix(sel) + base, -1.0)
    pos_ref[0] = pos.astype(jnp.int32)
    post_ref[0] = jnp.concatenate([pos, jnp.full((LANES - E, T), -1.0, F32)], axis=0).T


def _route(aff_t, cap, slot_stride):
    B, E, T = aff_t.shape
    r = np.arange(LANES)
    tri = jnp.asarray(r[:, None] < r[None, :], BF)
    return pl.pallas_call(
        functools.partial(_route_kernel, cap=cap, slot_stride=slot_stride),
        grid=(B,),
        in_specs=[pl.BlockSpec((1, E, T), lambda b: (b, 0, 0)), pl.BlockSpec((LANES, LANES), lambda b: (0, 0))],
        out_specs=[pl.BlockSpec((1, E, T), lambda b: (b, 0, 0)), pl.BlockSpec((1, T, LANES), lambda b: (b, 0, 0))],
        out_shape=[jax.ShapeDtypeStruct((B, E, T), jnp.int32), jax.ShapeDtypeStruct((B, T, LANES), F32)],
        compiler_params=_cparams("arbitrary"),
        name="route",
    )(aff_t, tri)


def _moe_kernel(*refs, n_slots, final):
    if final:
        h_ref, pos_ref, post_ref, aff_ref, wg_ref, wu_ref, wd_ref, x_ref, gate_ref, g_ref, o_ref = refs
    else:
        h_ref, pos_ref, post_ref, aff_ref, wg_ref, wu_ref, wd_ref, o_ref = refs
    e = pl.program_id(1)
    pos = pos_ref[0, pl.ds(e, 1), :]
    aff = aff_ref[0, pl.ds(e, 1), :]
    T = pos.shape[1]
    hit = lax.broadcasted_iota(jnp.int32, (n_slots, T), 0) == pos
    w_slot = jnp.sum(jnp.where(hit, aff, 0.0), axis=1, keepdims=True)
    xg = _dot(hit.astype(BF), h_ref[0]).astype(BF)
    gt = _dot(xg, wg_ref[...])
    up = _dot(xg, wu_ref[...])
    hid = (gt * jax.nn.sigmoid(gt) * up * w_slot).astype(BF)
    y = _dot(hid, wd_ref[...]).astype(BF)

    @pl.when(e == 0)
    def _():
        o_ref[...] = jnp.zeros_like(o_ref)

    lane = lax.broadcasted_iota(jnp.int32, (1, LANES), 1)
    slot_row = lax.broadcasted_iota(jnp.int32, (1, n_slots), 1).astype(F32)
    for t0 in range(0, T, 256):
        pos_col = jnp.sum(jnp.where(lane == e, post_ref[0, t0:t0 + 256, :], 0.0), axis=1, keepdims=True)
        o_ref[0, t0:t0 + 256, :] += _dot((pos_col == slot_row).astype(BF), y)

    if final:
        @pl.when(e == pl.num_programs(1) - 1)
        def _():
            for t0 in range(0, T, 256):
                rows = slice(t0, t0 + 256)
                o_ref[0, rows, :] = _rms(x_ref[0, rows, :] + gate_ref[0] * o_ref[0, rows, :], g_ref[...])


def _moe(h, pos, pos_t, aff_t, weights, l, n_slots, final=None):
    B, T, D = h.shape
    E = pos.shape[1]
    wg, wu, wd = weights

    def wspec(w):
        return pl.BlockSpec((None, None) + w.shape[2:], lambda b, e: (l, e, 0, 0))

    in_specs = [pl.BlockSpec((1, T, D), lambda b, e: (b, 0, 0)),
                pl.BlockSpec((1, E, T), lambda b, e: (b, 0, 0)),
                pl.BlockSpec((1, T, LANES), lambda b, e: (b, 0, 0)),
                pl.BlockSpec((1, E, T), lambda b, e: (b, 0, 0)),
                wspec(wg), wspec(wu), wspec(wd)]
    args = [h, pos, pos_t, aff_t, wg, wu, wd]
    if final is not None:
        x, mods3, g = final
        in_specs += [pl.BlockSpec((1, T, D), lambda b, e: (b, 0, 0), pipeline_mode=pl.Buffered(1)),
                     pl.BlockSpec((1, 1, D), lambda b, e: (b * 6 + 5, 0, 0)),
                     pl.BlockSpec((1, D), lambda b, e: (0, 0))]
        args += [x, mods3, g.reshape(1, D)]
    return pl.pallas_call(
        functools.partial(_moe_kernel, n_slots=n_slots, final=final is not None),
        grid=(B, E),
        in_specs=in_specs,
        out_specs=pl.BlockSpec((1, T, D), lambda b, e: (b, 0, 0)),
        out_shape=jax.ShapeDtypeStruct((B, T, D), F32),
        compiler_params=_cparams("arbitrary", "arbitrary"),
        name="moe",
    )(*args)


def _reorder_w_in(w_in):
    parts, off = [], 0
    for s in IN_SIZES:
        parts.append(w_in[..., off:off + s])
        off += s
    qa, ka, va, dq, dkv, kr, gq, gk, gv, a_f, a_b, og, m_a, m_b, m_c = parts
    pad = jnp.zeros(w_in.shape[:-1] + (LANES - MLA_ROPE_DIM - 2 * GLA_GATE_RANK,), w_in.dtype)
    return jnp.concatenate([m_a, m_b, m_c, gv, og, qa, ka, va, gq, gk, dq, dkv, kr, a_f, a_b, pad], axis=-1).astype(BF)


def kernel(x, c, ctx, c_ctx, ada_w, ada_b, norm1, norm2, w_in, na_rpb, mla_q_norm, mla_w_uq, mla_kv_norm, mla_w_ukv,
           gla_w_af, gla_b_af, gla_w_ab, gla_b_ab, gla_o_norm, w_br_a, w_br_b, w_br_c, w_out,
           moe_router, moe_w_gate, moe_w_up, moe_w_down, final_norm):
    B, T, D = x.shape
    L = ctx.shape[1]
    depth = ada_w.shape[0]
    assert B < MOD_ROWS
    cap_l = EC_CAPACITY * T // N_EXPERTS
    cap_c = EC_CAPACITY * L // N_EXPERTS
    ctx_row = B
    cc = jnp.concatenate([c, c_ctx[None], jnp.zeros((MOD_ROWS - B - 1, D), F32)], axis=0)

    def lat_row(b):
        return b

    def ctx_row_fn(b):
        return ctx_row

    def rows3(a):
        return a.reshape(depth, 1, -1)

    ada_b3, norm1_3, norm2_3 = rows3(ada_b), rows3(norm1), rows3(norm2)
    w_in_r = _reorder_w_in(w_in)
    na_bias = _na_bias(na_rpb, T // GRID_W)
    mla_w = _mla_weights(mla_w_uq, mla_w_ukv)
    qn3, kvn3 = rows3(mla_q_norm), rows3(mla_kv_norm)
    rope_tabs = _rope_tables(T)
    gate_pad = lambda w, off: jnp.pad(w, ((0, 0), (off, LANES - off - GLA_GATE_RANK), (0, 0))).astype(BF)
    gla_params = (gate_pad(gla_w_af, SM_AF), rows3(gla_b_af), gate_pad(gla_w_ab, SM_AB), rows3(gla_b_ab),
                  rows3(gla_o_norm))
    wr_t = jnp.swapaxes(moe_router, 1, 2)
    wr_hi = wr_t.astype(BF)
    wr_lo = (wr_t - wr_hi.astype(F32)).astype(BF)
    merge_w = (w_br_a.astype(BF), w_br_b.astype(BF), w_br_c.astype(BF), w_out.astype(BF),
               jnp.concatenate([wr_hi, wr_lo], axis=1))
    moe_w = (moe_w_gate.astype(BF), moe_w_up.astype(BF), moe_w_down.astype(BF))

    moe_l = moe_c = mods3 = None
    for l in range(depth):
        ctx_out = l < depth - 1
        prev_mods3 = mods3
        mods3 = _ada(cc, ada_w, ada_b3, l).reshape(MOD_ROWS * 6, 1, D)
        z_l, x = _in_proj(x, moe_l, prev_mods3, mods3, lat_row, norm1_3, w_in_r, l)
        z_c, ctx = _in_proj(ctx, moe_c, prev_mods3, mods3, ctx_row_fn, norm1_3, w_in_r, l)

        oa_l, oa_c = _na_attention(z_l, z_c, na_bias, l, ctx_out)

        qb_l, kb_l, vb_l = _mla_proj(z_l, qn3, kvn3, mla_w, rope_tabs, l)
        qb_c, kb_c, vb_c = _mla_proj(z_c, qn3, kvn3, mla_w, None, l)
        ob_l = _mla_attn(qb_l, kb_c, vb_c, kb_l, vb_l)

        oc_l, oc_c = _gla(z_l, z_c, gla_params, l, ctx_out)

        x, h_l, aff_l = _merge(oa_l, ob_l, oc_l, z_l, x, mods3, lat_row, norm2_3, merge_w, l, tm=512)
        pos_l, post_l = _route(aff_l, cap_l, 0)
        last = (x, mods3, final_norm) if l == depth - 1 else None
        moe_l = _moe(h_l, pos_l, post_l, aff_l, moe_w, l, cap_l, final=last)

        if ctx_out:
            ob_c = _mla_attn(qb_c, kb_c, vb_c)
            ctx, h_c, aff_c = _merge(oa_c, ob_c, oc_c, z_c, ctx, mods3, ctx_row_fn, norm2_3, merge_w, l, tm=256)
            pos_c, post_c = _route(aff_c, cap_c, cap_c)
            flat = lambda a: jnp.swapaxes(a, 0, 1).reshape(1, N_EXPERTS, B * L)
            moe_c = _moe(h_c.reshape(1, B * L, D), flat(pos_c), post_c.reshape(1, B * L, LANES), flat(aff_c),
                         moe_w, l, B * cap_c).reshape(B, L, D)
    return moe_l
```

```python
import functools

import numpy as np
import jax
import jax.numpy as jnp
from jax import lax
from jax.experimental import pallas as pl
from jax.experimental.pallas import tpu as pltpu

BF = jnp.bfloat16
F32 = jnp.float32

D_MODEL = 1024
GRID_W = 64
EPS = 1e-6
NA_HEADS = 4
NA_HEAD_DIM = 64
NA_WIN_H = 8
NA_WIN_W = 16
MLA_HEADS = 4
MLA_Q_RANK = 256
MLA_KV_RANK = 128
MLA_NOPE_DIM = 64
MLA_ROPE_DIM = 32
MLA_V_DIM = 64
ROPE_THETA = 10000.0
GLA_HEADS = 4
GLA_DK = 64
GLA_DV = 128
GLA_GATE_RANK = 16
GLA_TAU = 16.0
GLA_CHUNK = 64
N_EXPERTS = 16
EXPERT_FF = 1024
EC_CAPACITY = 2
IN_SIZES = (256, 256, 256, 256, 128, 32, 256, 256, 512, 16, 16, 512, 1024, 1024, 1024)

LANES = 128
VMEM_LIMIT = 56 * 1024 * 1024
MOD_ROWS = 16

Z_MA, Z_MB, Z_MC = 0, 1024, 2048
Z_GV, Z_OG = 3072, 3584
Z_QA, Z_KA, Z_VA = 4096, 4352, 4608
Z_GQ, Z_GK = 4864, 5120
Z_MLA = 5376
Z_SM = 5760
Z_COLS = 5888
SM_KR, SM_AF, SM_AB = 0, 32, 48

SUPER = 256
NEG = -1e30


def _cparams(*sem):
    return pltpu.CompilerParams(dimension_semantics=sem, vmem_limit_bytes=VMEM_LIMIT)


def _layer(a, l):
    return pl.BlockSpec((None,) + a.shape[1:], lambda *_: (l,) + (0,) * (a.ndim - 1), pipeline_mode=pl.Buffered(1))


def _whole(a):
    return pl.BlockSpec(a.shape, lambda *_: (0,) * a.ndim, pipeline_mode=pl.Buffered(1))


def _dot(a, b):
    return jnp.dot(a, b, preferred_element_type=F32)


def _dot_nt(a, b):
    return lax.dot_general(a, b, (((1,), (1,)), ((), ())), preferred_element_type=F32)


def _dot_tn(a, b):
    return lax.dot_general(a, b, (((0,), (0,)), ((), ())), preferred_element_type=F32)


def _rms(x, g):
    return x * lax.rsqrt(jnp.mean(x * x, axis=-1, keepdims=True) + EPS) * g


def _split3(x):
    hi = x.astype(BF)
    r = x - hi.astype(F32)
    mid = r.astype(BF)
    lo = (r - mid.astype(F32)).astype(BF)
    return hi, mid, lo


def _ada_kernel(c_ref, w_ref, b_ref, o_ref):
    c = c_ref[...]
    a = (c * jax.nn.sigmoid(c)).astype(BF)
    o_ref[...] = _dot(a, w_ref[...].astype(BF)) + b_ref[...]


def _ada(cc, ada_w, ada_b3, l):
    n = ada_w.shape[2]
    tn = 1536
    return pl.pallas_call(
        _ada_kernel,
        grid=(n // tn,),
        in_specs=[pl.BlockSpec((MOD_ROWS, D_MODEL), lambda j: (0, 0)),
                  pl.BlockSpec((None, D_MODEL, tn), lambda j: (l, 0, j)),
                  pl.BlockSpec((None, 1, tn), lambda j: (l, 0, j))],
        out_specs=pl.BlockSpec((MOD_ROWS, tn), lambda j: (0, j)),
        out_shape=jax.ShapeDtypeStruct((MOD_ROWS, n), F32),
        compiler_params=_cparams("arbitrary"),
        name="ada_mod",
    )(cc, ada_w, ada_b3)


def _in_proj_kernel(*refs, with_moe):
    if with_moe:
        x_ref, moe_ref, gate_ref, g_ref, shift_ref, scale_ref, w_ref, o_ref, xo_ref, h_ref = refs
        x = x_ref[0] + gate_ref[0] * moe_ref[0]
        xo_ref[0] = x
    else:
        x_ref, g_ref, shift_ref, scale_ref, w_ref, o_ref, h_ref = refs
        x = x_ref[0]
    h = _rms(x, g_ref[...]) * (1.0 + scale_ref[0]) + shift_ref[0]
    h_ref[...] = h.astype(BF)
    for n0 in range(0, Z_COLS, 256):
        o_ref[0, :, n0:n0 + 256] = _dot(h_ref[...], w_ref[:, n0:n0 + 256]).astype(BF)


def _in_proj(x, moe, prev_mods3, mods3, mod_row, norm1_3, w, l, tm=512):
    B, T, D = x.shape
    tm = min(tm, T)
    with_moe = moe is not None

    def mod_spec(k):
        return pl.BlockSpec((1, 1, D), lambda b, i: (mod_row(b) * 6 + k, 0, 0))

    x_spec = pl.BlockSpec((1, tm, D), lambda b, i: (b, i, 0))
    in_specs = [x_spec]
    args = [x]
    if with_moe:
        in_specs += [x_spec, mod_spec(5)]
        args += [moe, prev_mods3]
    in_specs += [_layer(norm1_3, l), mod_spec(0), mod_spec(1), _layer(w, l)]
    args += [norm1_3, mods3, mods3, w]
    z_spec = pl.BlockSpec((1, tm, Z_COLS), lambda b, i: (b, i, 0))
    z_shape = jax.ShapeDtypeStruct((B, T, Z_COLS), BF)
    if with_moe:
        out_specs, out_shape = [z_spec, x_spec], [z_shape, jax.ShapeDtypeStruct((B, T, D), F32)]
    else:
        out_specs, out_shape = z_spec, z_shape
    res = pl.pallas_call(
        functools.partial(_in_proj_kernel, with_moe=with_moe),
        grid=(B, T // tm),
        in_specs=in_specs, out_specs=out_specs, out_shape=out_shape,
        scratch_shapes=[pltpu.VMEM((tm, D), BF)],
        compiler_params=_cparams("arbitrary", "arbitrary"),
        name="in_proj",
    )(*args)
    return (res[0], res[1]) if with_moe else (res, x)


def _softmax_pv_many(heads):
    maxes = []
    for s_list, _ in heads:
        m = s_list[0].max(axis=-1, keepdims=True)
        for s in s_list[1:]:
            m = jnp.maximum(m, s.max(axis=-1, keepdims=True))
        maxes.append(m)
    probs = [[jnp.exp((s - m).astype(BF)) for s in s_list] for (s_list, _), m in zip(heads, maxes)]
    outs = []
    for (_, v_list), p_list in zip(heads, probs):
        o = _dot(p_list[0], v_list[0])
        for p, v in zip(p_list[1:], v_list[1:]):
            o = o + _dot(p, v)
        outs.append(o / pltpu.roll(o, LANES // 2, axis=1))
    return outs


def _softmax_pv_sum(heads):
    maxes = []
    for s_list, _ in heads:
        m = s_list[0].max(axis=-1, keepdims=True)
        for s in s_list[1:]:
            m = jnp.maximum(m, s.max(axis=-1, keepdims=True))
        maxes.append(m)
    probs = [[jnp.exp(s - m) for s in s_list] for (s_list, _), m in zip(heads, maxes)]
    outs = []
    for (_, v_list), p_list in zip(heads, probs):
        l = p_list[0].sum(axis=-1, keepdims=True)
        o = _dot(p_list[0].astype(BF), v_list[0])
        for p, v in zip(p_list[1:], v_list[1:]):
            l = l + p.sum(axis=-1, keepdims=True)
            o = o + _dot(p.astype(BF), v)
        outs.append(o / l)
    return outs


def _lane_masks():
    lane = lax.broadcasted_iota(jnp.int32, (1, LANES), 1)
    return lane < 64, lane >= 64


NA_ROWS_PER_STEP = 2
NA_BAND = 10
NA_CASE_ROWS = (0, 2, 4, 28, 30)


def _na_bias(rpb, rows):
    depth = rpb.shape[0]
    qc = np.arange(GRID_W)
    kj = np.arange(GRID_W)
    cs = np.clip(qc - NA_WIN_W // 2, 0, GRID_W - NA_WIN_W)
    col_ok = (kj[None, :] >= cs[:, None]) & (kj[None, :] < cs[:, None] + NA_WIN_W)
    co = kj[None, :] - qc[:, None] + (NA_WIN_W - 1)
    onehot = np.zeros((2 * NA_WIN_W - 1, GRID_W, GRID_W), np.float32)
    qi, ki = np.nonzero(col_ok)
    onehot[co[qi, ki], qi, ki] = 1.0
    toep = jnp.einsum("lhrc,cp->lhrp", rpb.astype(F32), jnp.asarray(onehot.reshape(2 * NA_WIN_W - 1, -1)),
                      precision=lax.Precision.HIGHEST)
    toep = (toep.reshape(depth, NA_HEADS, 2 * NA_WIN_H - 1, GRID_W, GRID_W)
            + jnp.asarray(np.where(col_ok, 0.0, NEG), F32))
    neg = jnp.full((depth, NA_HEADS, GRID_W, GRID_W), NEG, F32)
    cases = []
    for r0 in NA_CASE_ROWS:
        bs = int(np.clip(r0 - NA_WIN_H // 2, 0, rows - NA_BAND))
        q_rows = []
        for qr in range(NA_ROWS_PER_STEP):
            r = r0 + qr
            rs = int(np.clip(r - NA_WIN_H // 2, 0, rows - NA_WIN_H))
            blocks = []
            for i in range(NA_BAND):
                krow = bs + i
                blocks.append(toep[:, :, krow - r + NA_WIN_H - 1] if rs <= krow < rs + NA_WIN_H else neg)
            q_rows.append(jnp.concatenate(blocks, axis=-1))
        cases.append(jnp.concatenate(q_rows, axis=-2))
    return jnp.stack(cases, axis=2)


def _na_kernel(*refs, rows, ctx_out):
    if ctx_out:
        ql_ref, kl_ref, vl_ref, qc_ref, kc_ref, vc_ref, bias_ref, ol_ref, oc_ref = refs
    else:
        ql_ref, kl_ref, vl_ref, kc_ref, vc_ref, bias_ref, ol_ref = refs
    scale = NA_HEAD_DIM ** -0.5
    masks = _lane_masks()
    nq = NA_ROWS_PER_STEP * GRID_W
    nk = NA_BAND * GRID_W

    def body(rp, carry):
        r0 = rp * NA_ROWS_PER_STEP
        bs = jnp.clip(r0 - NA_WIN_H // 2, 0, rows - NA_BAND)
        case = jnp.where(r0 < 4, r0 // 2, jnp.where(r0 > rows - 6, (r0 - (rows - 4)) // 2 + 3, 2))
        q0 = pl.multiple_of(r0 * GRID_W, nq)
        k0 = pl.multiple_of(bs * GRID_W, 2 * GRID_W)
        heads = []
        for hp in range(NA_HEADS // 2):
            sl = slice(hp * LANES, (hp + 1) * LANES)
            q = ql_ref[0, pl.ds(q0, nq), sl] * scale
            kb = kl_ref[0, pl.ds(k0, nk), sl]
            vb = vl_ref[0, pl.ds(k0, nk), sl]
            kc = kc_ref[0, :, sl]
            vc = vc_ref[0, :, sl]
            for hh in range(2):
                qm = jnp.where(masks[hh], q, jnp.zeros_like(q))
                s_loc = _dot_nt(qm, kb) + bias_ref[2 * hp + hh, case]
                s_ctx = _dot_nt(qm, kc)
                heads.append(([s_loc, s_ctx], [vb, vc]))
        outs = _softmax_pv_sum(heads)
        for hp in range(NA_HEADS // 2):
            sl = slice(hp * LANES, (hp + 1) * LANES)
            ol_ref[0, pl.ds(q0, nq), sl] = jnp.where(masks[0], outs[2 * hp], outs[2 * hp + 1]).astype(BF)
        return carry

    lax.fori_loop(0, rows // NA_ROWS_PER_STEP, body, 0)

    if ctx_out:
        heads = []
        for hp in range(NA_HEADS // 2):
            sl = slice(hp * LANES, (hp + 1) * LANES)
            q = qc_ref[0, :, sl] * scale
            kc = kc_ref[0, :, sl]
            for hh in range(2):
                qm = jnp.where(masks[hh], q, jnp.zeros_like(q))
                heads.append(([_dot_nt(qm, kc)], [vc_ref[0, :, sl]]))
        outs = _softmax_pv_sum(heads)
        for hp in range(NA_HEADS // 2):
            sl = slice(hp * LANES, (hp + 1) * LANES)
            oc_ref[0, :, sl] = jnp.where(masks[0], outs[2 * hp], outs[2 * hp + 1]).astype(BF)


def _na_attention(z_l, z_c, bias, l, ctx_out):
    B, T, _ = z_l.shape
    L = z_c.shape[1]
    rows = T // GRID_W

    def zl(col):
        return pl.BlockSpec((1, T, 256), lambda b: (b, 0, col // 256))

    def zc(col):
        return pl.BlockSpec((1, L, 256), lambda b: (b, 0, col // 256))

    ol_spec = pl.BlockSpec((1, T, 256), lambda b: (b, 0, 0))
    ol_shape = jax.ShapeDtypeStruct((B, T, 256), BF)
    if ctx_out:
        in_specs = [zl(Z_QA), zl(Z_KA), zl(Z_VA), zc(Z_QA), zc(Z_KA), zc(Z_VA), _layer(bias, l)]
        args = [z_l, z_l, z_l, z_c, z_c, z_c, bias]
        out_specs = [ol_spec, pl.BlockSpec((1, L, 256), lambda b: (b, 0, 0))]
        out_shape = [ol_shape, jax.ShapeDtypeStruct((B, L, 256), BF)]
    else:
        in_specs = [zl(Z_QA), zl(Z_KA), zl(Z_VA), zc(Z_KA), zc(Z_VA), _layer(bias, l)]
        args = [z_l, z_l, z_l, z_c, z_c, bias]
        out_specs, out_shape = ol_spec, ol_shape
    res = pl.pallas_call(
        functools.partial(_na_kernel, rows=rows, ctx_out=ctx_out),
        grid=(B,), in_specs=in_specs, out_specs=out_specs, out_shape=out_shape,
        compiler_params=_cparams("arbitrary"),
        name="na_attn",
    )(*args)
    return (res[0], res[1]) if ctx_out else (res, None)


MLA_QK = MLA_NOPE_DIM + MLA_ROPE_DIM
MLA_W = MLA_HEADS * LANES


def _mla_proj_kernel(*refs, rope):
    if rope:
        (zm_ref, sm_ref, qn_ref, kvn_ref, wq_ref, wqs_ref, wk_ref, wv_ref, e_ref, es_ref, vone_ref, cos_ref, sin_ref,
         q_ref, k_ref, v_ref) = refs
    else:
        zm_ref, sm_ref, qn_ref, kvn_ref, wq_ref, wk_ref, wv_ref, e_ref, vone_ref, q_ref, k_ref, v_ref = refs
    zm = zm_ref[0].astype(F32)
    nq = _rms(zm[:, :MLA_Q_RANK], qn_ref[...]).astype(BF)
    nkv = _rms(zm[:, MLA_Q_RANK:], kvn_ref[...]).astype(BF)
    sm = sm_ref[0]
    scale = MLA_QK ** -0.5
    q = _dot(nq, wq_ref[...])
    k = _dot(nkv, wk_ref[...]) + _dot(sm, e_ref[...])
    if rope:
        cos = cos_ref[...]
        sin = sin_ref[...]
        q = q * cos + _dot(nq, wqs_ref[...]) * sin
        k = k * cos + _dot(sm, es_ref[...]) * sin
    q_ref[0] = (q * scale).astype(BF)
    k_ref[0] = k.astype(BF)
    v_ref[0] = (_dot(nkv, wv_ref[...]) + vone_ref[...]).astype(BF)


def _rope_swap(e):
    return np.where(e % 16 < 8, e + 8, e - 8)


def _mla_weights(w_uq, w_ukv):
    depth = w_uq.shape[0]
    quarter = MLA_ROPE_DIM // 4
    q4 = w_uq.reshape(depth, MLA_Q_RANK, MLA_HEADS, MLA_QK)
    nope, rope = q4[..., :MLA_NOPE_DIM], q4[..., MLA_NOPE_DIM:]
    rope_sw = jnp.flip(rope.reshape(depth, MLA_Q_RANK, MLA_HEADS, 2, 2, quarter), axis=4).reshape(rope.shape)
    zpad = jnp.zeros(rope.shape, w_uq.dtype)
    wq = jnp.concatenate([nope, rope, zpad], axis=-1).reshape(depth, MLA_Q_RANK, MLA_W)
    wqs = jnp.concatenate([jnp.zeros(nope.shape, w_uq.dtype), rope_sw, zpad], axis=-1).reshape(depth, MLA_Q_RANK, MLA_W)
    kv4 = w_ukv.reshape(depth, MLA_KV_RANK, MLA_HEADS, MLA_NOPE_DIM + MLA_V_DIM)
    k_nope, v = kv4[..., :MLA_NOPE_DIM], kv4[..., MLA_NOPE_DIM:]
    wk = jnp.concatenate([k_nope, jnp.zeros(k_nope.shape, w_ukv.dtype)], axis=-1).reshape(depth, MLA_KV_RANK, MLA_W)
    vz = jnp.zeros(v.shape[:2] + (1, MLA_V_DIM), w_ukv.dtype)
    wv = jnp.concatenate([jnp.concatenate([v[:, :, h:h + 1], vz] if h % 2 == 0 else [vz, v[:, :, h:h + 1]], axis=-1)
                          for h in range(MLA_HEADS)], axis=2).reshape(depth, MLA_KV_RANK, MLA_W)
    e = np.arange(MLA_ROPE_DIM)
    em = np.zeros((LANES, MLA_W), np.float32)
    ems = np.zeros((LANES, MLA_W), np.float32)
    for h in range(MLA_HEADS):
        em[SM_KR + e, h * LANES + MLA_NOPE_DIM + e] = 1.0
        ems[SM_KR + _rope_swap(e), h * LANES + MLA_NOPE_DIM + e] = 1.0
    lane = np.arange(MLA_W) % LANES
    v_ones = ((lane >= MLA_V_DIM) == ((np.arange(MLA_W) // LANES) % 2 == 0)).astype(np.float32)[None, :]
    return (wq.astype(BF), wqs.astype(BF), wk.astype(BF), wv.astype(BF), jnp.asarray(em, BF), jnp.asarray(ems, BF),
            jnp.asarray(v_ones, F32))


def _rope_tables(T):
    t = jnp.arange(T)
    quarter = MLA_ROPE_DIM // 4
    inv_freq = ROPE_THETA ** (-jnp.arange(quarter, dtype=F32) / quarter)
    ang_r = (t // GRID_W).astype(F32)[:, None] * inv_freq
    ang_c = (t % GRID_W).astype(F32)[:, None] * inv_freq
    cr, sr, ccol, scol = jnp.cos(ang_r), jnp.sin(ang_r), jnp.cos(ang_c), jnp.sin(ang_c)
    ones = jnp.ones((T, MLA_NOPE_DIM), F32)
    pad1 = jnp.ones((T, LANES - MLA_QK), F32)
    cos_h = jnp.concatenate([ones, cr, cr, ccol, ccol, pad1], axis=1)
    sin_h = jnp.concatenate([0.0 * ones, -sr, sr, -scol, scol, 0.0 * pad1], axis=1)
    return jnp.tile(cos_h, (1, MLA_HEADS)), jnp.tile(sin_h, (1, MLA_HEADS))


def _mla_proj(z, qn3, kvn3, weights, rope_tabs, l, tm=512):
    B, T, _ = z.shape
    wq, wqs, wk, wv, em, ems, v_ones = weights
    rope = rope_tabs is not None
    tm = min(tm, T)
    zm_spec = pl.BlockSpec((1, tm, 384), lambda i, b: (b, i, Z_MLA // 384))
    sm_spec = pl.BlockSpec((1, tm, LANES), lambda i, b: (b, i, Z_SM // LANES))
    if rope:
        tab_spec = pl.BlockSpec((tm, MLA_W), lambda i, b: (i, 0))
        ins = [z, z, qn3, kvn3, wq, wqs, wk, wv, em, ems, v_ones, rope_tabs[0], rope_tabs[1]]
        in_specs = ([zm_spec, sm_spec] + [_layer(a, l) for a in ins[2:8]] + [_whole(em), _whole(ems), _whole(v_ones)]
                    + [tab_spec, tab_spec])
    else:
        ins = [z, z, qn3, kvn3, wq, wk, wv, em, v_ones]
        in_specs = [zm_spec, sm_spec] + [_layer(a, l) for a in ins[2:7]] + [_whole(em), _whole(v_ones)]
    return pl.pallas_call(
        functools.partial(_mla_proj_kernel, rope=rope),
        grid=(T // tm, B), in_specs=in_specs,
        out_specs=[pl.BlockSpec((1, tm, MLA_W), lambda i, b: (b, i, 0)),
                   pl.BlockSpec((1, tm, MLA_W), lambda i, b: (b, i, 0)),
                   pl.BlockSpec((1, tm, MLA_W), lambda i, b: (b, i, 0))],
        out_shape=[jax.ShapeDtypeStruct((B, T, MLA_W), BF)] * 3,
        compiler_params=_cparams("arbitrary", "arbitrary"),
        name="mla_proj",
    )(*ins)


def _mla_attn_kernel(*refs, with_latent):
    if with_latent:
        q_ref, kc_ref, vc_ref, kl_ref, vl_ref, o_ref = refs
    else:
        q_ref, kc_ref, vc_ref, o_ref = refs
    masks = _lane_masks()
    heads = []
    for h in range(MLA_HEADS):
        hsl = slice(h * LANES, (h + 1) * LANES)
        q = q_ref[0, :, hsl]
        s_list = [_dot_nt(q, kc_ref[0, :, hsl])]
        v_list = [vc_ref[0, :, hsl]]
        if with_latent:
            s_list.append(_dot_nt(q, kl_ref[0, :, hsl]))
            v_list.append(vl_ref[0, :, hsl])
        heads.append((s_list, v_list))
    outs = _softmax_pv_many(heads)
    for hp in range(MLA_HEADS // 2):
        vsl = slice(hp * LANES, (hp + 1) * LANES)
        o_ref[0, :, vsl] = jnp.where(masks[0], outs[2 * hp], outs[2 * hp + 1]).astype(BF)


def _mla_attn(q, k_c, v_c, k_l=None, v_l=None, tq=256):
    B, Tq, W = q.shape
    L = k_c.shape[1]
    with_latent = k_l is not None
    in_specs = [pl.BlockSpec((1, tq, W), lambda b, i: (b, i, 0)),
                pl.BlockSpec((1, L, W), lambda b, i: (b, 0, 0)),
                pl.BlockSpec((1, L, W), lambda b, i: (b, 0, 0))]
    args = [q, k_c, v_c]
    if with_latent:
        T = k_l.shape[1]
        in_specs += [pl.BlockSpec((1, T, W), lambda b, i: (b, 0, 0)),
                     pl.BlockSpec((1, T, W), lambda b, i: (b, 0, 0))]
        args += [k_l, v_l]
    return pl.pallas_call(
        functools.partial(_mla_attn_kernel, with_latent=with_latent),
        grid=(B, Tq // tq), in_specs=in_specs,
        out_specs=pl.BlockSpec((1, tq, 256), lambda b, i: (b, i, 0)),
        out_shape=jax.ShapeDtypeStruct((B, Tq, 256), BF),
        compiler_params=_cparams("arbitrary", "arbitrary"),
        name="mla_attn",
    )(*args)


def _gla_superchunks(jobs, st_ref):
    n_chunk = SUPER // GLA_CHUNK
    masks = _lane_masks()
    rowblk = lax.broadcasted_iota(jnp.int32, (SUPER, LANES), 0) // GLA_CHUNK
    own = ((lax.broadcasted_iota(jnp.int32, (2 * GLA_DV, LANES), 0) < GLA_DV)
           == (lax.broadcasted_iota(jnp.int32, (2 * GLA_DV, LANES), 1) < GLA_DK))
    gs = [jax.nn.log_sigmoid(_dot(sm, w_a) + b_a) * (1.0 / GLA_TAU) for (_, _, _, sm, w_a, b_a, _, _, _) in jobs]
    pre = []
    for (q, k, v, sm, w_a, b_a, tri, d, reverse), g in zip(jobs, gs):
        g_hi = g.astype(BF)
        g_lo = (g - g_hi.astype(F32)).astype(BF)
        b = _dot(tri, g_hi) + _dot(tri, g_lo)
        ends = [i * GLA_CHUNK if reverse else (i + 1) * GLA_CHUNK - 1 for i in range(n_chunk)]
        tot = jnp.concatenate([jnp.broadcast_to(b[r:r + 1, :], (GLA_CHUNK, b.shape[1])) for r in ends], axis=0)
        pre.append((q.astype(F32) * (GLA_DK ** -0.5) * jnp.exp(b), k.astype(F32) * jnp.exp(-b),
                    k.astype(F32) * jnp.exp(tot - b), jnp.exp(tot), tri > 0))
    outs = [[] for _ in jobs]
    for hp in range(GLA_HEADS // 2):
        sl = slice(hp * LANES, (hp + 1) * LANES)
        stage = []
        for (q, k, v, sm, w_a, b_a, tri, d, reverse), (qd_all, kd_all, ke_all, dec_all, keep) in zip(jobs, pre):
            qd = qd_all[:, sl]
            kd = kd_all[:, sl].astype(BF)
            ke = ke_all[:, sl].astype(BF)
            zero = jnp.zeros_like(ke)
            ke_bd = jnp.concatenate([jnp.where(rowblk == i, ke, zero) for i in range(n_chunk)], axis=1)
            v_pair = v[:, hp * 2 * GLA_DV:(hp + 1) * 2 * GLA_DV]
            kv_t = _dot_tn(v_pair, ke_bd)
            a = [jnp.where(keep, _dot_nt(jnp.where(masks[hh], qd, 0.0).astype(BF), kd), 0.0).astype(BF)
                 for hh in range(2)]
            stage.append((qd, dec_all[:, sl], v_pair, kv_t, a))
        inters = []
        for (q, k, v, sm, w_a, b_a, tri, d, reverse), (qd, dec, v_pair, kv_t, a) in zip(jobs, stage):
            s = st_ref[d, hp]
            inter = [None] * n_chunk
            for i in (range(n_chunk - 1, -1, -1) if reverse else range(n_chunk)):
                rows = slice(i * GLA_CHUNK, (i + 1) * GLA_CHUNK)
                inter[i] = _dot_nt(qd[rows].astype(BF), s.astype(BF))
                s = s * dec[i * GLA_CHUNK:i * GLA_CHUNK + 1, :] + jnp.where(own, kv_t[:, i * LANES:(i + 1) * LANES], 0.0)
            st_ref[d, hp] = s
            inters.append(jnp.concatenate(inter, axis=0))
        for n, ((qd, dec, v_pair, kv_t, a), o_inter) in enumerate(zip(stage, inters)):
            for hh in range(2):
                hv = slice(hh * GLA_DV, (hh + 1) * GLA_DV)
                outs[n].append(_dot(a[hh], v_pair[:, hv]) + o_inter[:, hv])
    return [jnp.concatenate(o, axis=1) for o in outs]


def _gla_finish(o, og, onorm):
    ys = []
    for h in range(GLA_HEADS):
        ys.append(_rms(o[:, h * GLA_DV:(h + 1) * GLA_DV], onorm))
    y = jnp.concatenate(ys, axis=1)
    ogf = og.astype(F32)
    return (y * (ogf * jax.nn.sigmoid(ogf))).astype(BF)


def _gla_kernel(*refs, n_lat, ctx_out):
    (ql_ref, kl_ref, vl_ref, sml_ref, ogl_ref, qc_ref, kc_ref, vc_ref, smc_ref, ogc_ref,
     waf_ref, baf_ref, wab_ref, bab_ref, onorm_ref, trif_ref, trib_ref) = refs[:17]
    if ctx_out:
        ol_ref, oc_ref, of_ref, ob_ref, st_ref = refs[17:]
    else:
        ol_ref, of_ref, ob_ref, st_ref = refs[17:]
    onorm = onorm_ref[...]

    def both(fwd_in, bwd_in):
        return _gla_superchunks([fwd_in + (waf_ref[...], baf_ref[...], trif_ref[...], 0, False),
                                 bwd_in + (wab_ref[...], bab_ref[...], trib_ref[...], 1, True)], st_ref)

    st_ref[...] = jnp.zeros_like(st_ref)
    ctx_in = (qc_ref[0], kc_ref[0], vc_ref[0], smc_ref[0])
    o_cf, o_cb = both(ctx_in, ctx_in)
    if ctx_out:
        oc_ref[0] = _gla_finish(o_cf + o_cb, ogc_ref[0], onorm)

    def body(j, carry):
        rf = pl.ds(pl.multiple_of(j * SUPER, SUPER), SUPER)
        rb = pl.ds(pl.multiple_of((n_lat - 1 - j) * SUPER, SUPER), SUPER)
        o_f, o_b = both((ql_ref[0, rf, :], kl_ref[0, rf, :], vl_ref[0, rf, :], sml_ref[0, rf, :]),
                        (ql_ref[0, rb, :], kl_ref[0, rb, :], vl_ref[0, rb, :], sml_ref[0, rb, :]))
        of_ref[rf, :] = o_f
        ob_ref[rb, :] = o_b
        return carry

    lax.fori_loop(0, n_lat, body, 0)

    def finish(j, carry):
        r = pl.ds(pl.multiple_of(j * SUPER, SUPER), SUPER)
        ol_ref[0, r, :] = _gla_finish(of_ref[r, :] + ob_ref[r, :], ogl_ref[0, r, :], onorm)
        return carry

    lax.fori_loop(0, n_lat, finish, 0)


def _gla_consts():
    r = np.arange(SUPER)
    same = (r[:, None] // GLA_CHUNK) == (r[None, :] // GLA_CHUNK)
    return jnp.asarray(same & (r[None, :] <= r[:, None]), BF), jnp.asarray(same & (r[None, :] >= r[:, None]), BF)


def _gla(z_l, z_c, params, l, ctx_out):
    B, T, _ = z_l.shape
    L = z_c.shape[1]
    assert L == SUPER and T % SUPER == 0
    kw = GLA_HEADS * GLA_DK
    vw = GLA_HEADS * GLA_DV
    consts = _gla_consts()

    def zspec(n, width, col):
        return pl.BlockSpec((1, n, width), lambda b: (b, 0, col // width))

    in_specs = ([zspec(T, kw, Z_GQ), zspec(T, kw, Z_GK), zspec(T, vw, Z_GV), zspec(T, LANES, Z_SM), zspec(T, vw, Z_OG),
                 zspec(L, kw, Z_GQ), zspec(L, kw, Z_GK), zspec(L, vw, Z_GV), zspec(L, LANES, Z_SM), zspec(L, vw, Z_OG)]
                + [_layer(a, l) for a in params] + [_whole(a) for a in consts])
    ol_spec = pl.BlockSpec((1, T, vw), lambda b: (b, 0, 0))
    ol_shape = jax.ShapeDtypeStruct((B, T, vw), BF)
    if ctx_out:
        out_specs = [ol_spec, pl.BlockSpec((1, L, vw), lambda b: (b, 0, 0))]
        out_shape = [ol_shape, jax.ShapeDtypeStruct((B, L, vw), BF)]
    else:
        out_specs, out_shape = ol_spec, ol_shape
    scratch = [pltpu.VMEM((T, vw), F32), pltpu.VMEM((T, vw), F32),
               pltpu.VMEM((2, GLA_HEADS // 2, 2 * GLA_DV, LANES), F32)]
    res = pl.pallas_call(
        functools.partial(_gla_kernel, n_lat=T // SUPER, ctx_out=ctx_out),
        grid=(B,), in_specs=in_specs, out_specs=out_specs, out_shape=out_shape,
        scratch_shapes=scratch,
        compiler_params=_cparams("arbitrary"),
        name="gla",
    )(*([z_l] * 5 + [z_c] * 5 + list(params) + list(consts)))
    return (res[0], res[1]) if ctx_out else (res, None)


MERGE_SUB = 256


def _merge_kernel(oa_ref, ob_ref, oc_ref, ma_ref, mb_ref, mc_ref, x_ref, gate_ref, shift_ref, scale_ref, g_ref,
                  wa_ref, wb_ref, wc_ref, wo_ref, wr_ref, xo_ref, h_ref, aff_ref, y_ref):
    tm = x_ref.shape[1]
    gate = gate_ref[0]
    subs = [slice(r0, r0 + MERGE_SUB) for r0 in range(0, tm, MERGE_SUB)]

    def branch(o_ref, m_ref, w_ref, rows, n0):
        m = m_ref[0, rows, n0:n0 + 256].astype(F32)
        return jax.nn.sigmoid(m) * _dot(o_ref[0, rows, :], w_ref[:, n0:n0 + 256])

    for n0 in range(0, D_MODEL, 256):
        for rows in subs:
            y = (branch(oa_ref, ma_ref, wa_ref, rows, n0) + branch(ob_ref, mb_ref, wb_ref, rows, n0)
                 + branch(oc_ref, mc_ref, wc_ref, rows, n0))
            y_ref[rows, n0:n0 + 256] = y.astype(BF)
    for n0 in range(0, D_MODEL, 256):
        for rows in subs:
            xo_ref[0, rows, n0:n0 + 256] = (x_ref[0, rows, n0:n0 + 256]
                                            + gate[:, n0:n0 + 256] * _dot(y_ref[rows, :], wo_ref[:, n0:n0 + 256]))
    hs = [_rms(xo_ref[0, rows, :], g_ref[...]) * (1.0 + scale_ref[0]) + shift_ref[0] for rows in subs]
    wr = wr_ref[...]
    for rows, h in zip(subs, hs):
        h_hi = h.astype(BF)
        h_lo = (h - h_hi.astype(F32)).astype(BF)
        h_ref[0, rows, :] = h
        part = _dot_nt(wr, h_hi)
        logits = part[:N_EXPERTS] + part[N_EXPERTS:] + _dot_nt(wr[:N_EXPERTS], h_lo)
        e = jnp.exp(logits - logits.max(axis=0, keepdims=True))
        aff_ref[0, :, rows] = e / e.sum(axis=0, keepdims=True)


def _merge(oa, ob, oc, z, x, mods3, mod_row, norm2_3, weights, l, tm):
    B, T, D = x.shape
    wa, wb, wc, wo, wr_t = weights

    def tok(width, col=0):
        return pl.BlockSpec((1, tm, width), lambda b, i: (b, i, col // width))

    def mod_spec(k):
        return pl.BlockSpec((1, 1, D), lambda b, i: (mod_row(b) * 6 + k, 0, 0))

    in_specs = [tok(256), tok(256), tok(512), tok(D, Z_MA), tok(D, Z_MB), tok(D, Z_MC), tok(D),
                mod_spec(2), mod_spec(3), mod_spec(4), _layer(norm2_3, l)] + [_layer(w, l) for w in weights]
    return pl.pallas_call(
        _merge_kernel,
        grid=(B, T // tm), in_specs=in_specs,
        out_specs=[tok(D), tok(D), pl.BlockSpec((1, N_EXPERTS, tm), lambda b, i: (b, 0, i))],
        out_shape=[jax.ShapeDtypeStruct((B, T, D), F32), jax.ShapeDtypeStruct((B, T, D), F32),
                   jax.ShapeDtypeStruct((B, N_EXPERTS, T), F32)],
        scratch_shapes=[pltpu.VMEM((tm, D), BF)],
        compiler_params=_cparams("arbitrary", "arbitrary"),
        name="merge",
    )(oa, ob, oc, z, z, z, x, mods3, mods3, mods3, norm2_3, wa, wb, wc, wo, wr_t)


def _route_kernel(aff_ref, tri_ref, pos_ref, post_ref, tok_ref, *, cap, slot_stride, tok_stride):
    aff = aff_ref[0]
    E, T = aff.shape

    def refine(thr, shift, patterns):
        best = thr
        for c in patterns:
            cand = thr | (jnp.int32(c) << shift)
            cnt = jnp.sum((aff >= pltpu.bitcast(cand, F32)).astype(F32), axis=1, keepdims=True)
            best = jnp.where(cnt >= cap, cand, best)
        return best

    thr = lax.fori_loop(0, 15, lambda i, t: refine(t, 29 - 2 * i, (1, 2, 3)), jnp.zeros((E, 1), jnp.int32))
    thr = refine(thr, 0, (1,))
    thr_f = pltpu.bitcast(thr, F32)
    gt = aff > thr_f
    eq = aff == thr_f
    need = cap - jnp.sum(gt.astype(F32), axis=1, keepdims=True)
    tri = tri_ref[...]

    def excl_prefix(flags):
        run = jnp.zeros((E, 1), F32)
        blocks = []
        for j in range(T // LANES):
            blk = flags[:, j * LANES:(j + 1) * LANES].astype(F32)
            blocks.append(_dot(blk.astype(BF), tri) + run)
            run = run + jnp.sum(blk, axis=1, keepdims=True)
        return jnp.concatenate(blocks, axis=1)

    sel = gt | (eq & (excl_prefix(eq) < need))
    local = jnp.where(sel, excl_prefix(sel), -1.0)
    base = (slot_stride * pl.program_id(0)).astype(F32)
    pos = jnp.where(sel, local + base, -1.0)
    pos_ref[0] = pos.astype(jnp.int32)
    post_ref[0] = jnp.concatenate([pos, jnp.full((LANES - E, T), -1.0, F32)], axis=0).T
    n_pad = -(-cap // LANES) * LANES
    slot_col = lax.broadcasted_iota(jnp.int32, (n_pad, 1), 0).astype(F32)
    t_row = (lax.broadcasted_iota(jnp.int32, (1, T), 1) + tok_stride * pl.program_id(0)).astype(F32)
    cols = [jnp.sum(jnp.where(local[e:e + 1, :] == slot_col, t_row, 0.0), axis=1, keepdims=True) for e in range(E)]
    tok_t = jnp.concatenate(cols + [jnp.zeros((n_pad, LANES - E), F32)], axis=1)
    tok_ref[0] = tok_t.T[:E, :cap].astype(jnp.int32)


def _route(aff_t, cap, slot_stride, tok_stride):
    B, E, T = aff_t.shape
    r = np.arange(LANES)
    tri = jnp.asarray(r[:, None] < r[None, :], BF)
    return pl.pallas_call(
        functools.partial(_route_kernel, cap=cap, slot_stride=slot_stride, tok_stride=tok_stride),
        grid=(B,),
        in_specs=[pl.BlockSpec((1, E, T), lambda b: (b, 0, 0)), pl.BlockSpec((LANES, LANES), lambda b: (0, 0))],
        out_specs=[pl.BlockSpec((1, E, T), lambda b: (b, 0, 0)), pl.BlockSpec((1, T, LANES), lambda b: (b, 0, 0)),
                   pl.BlockSpec((1, E, cap), lambda b: (b, 0, 0))],
        out_shape=[jax.ShapeDtypeStruct((B, E, T), jnp.int32), jax.ShapeDtypeStruct((B, T, LANES), F32),
                   jax.ShapeDtypeStruct((B, E, cap), jnp.int32)],
        compiler_params=_cparams("arbitrary"),
        name="route",
    )(aff_t, tri)


def _moe_kernel(*refs, n_slots, final):
    if final:
        (tok_ref, tokn_ref, h_hbm, pos_ref, post_ref, aff_ref, wg_ref, wu_ref, wd_ref, x_ref, gate_ref, g_ref,
         o_ref, xg_ref, sem) = refs
    else:
        tok_ref, tokn_ref, h_hbm, pos_ref, post_ref, aff_ref, wg_ref, wu_ref, wd_ref, o_ref, xg_ref, sem = refs
    b = pl.program_id(0)
    e = pl.program_id(1)
    n_e = pl.num_programs(1)
    step = b * n_e + e
    cur = step % 2

    def gather(tok, sample, buf, slots=None):
        for s in (range(n_slots) if slots is None else slots):
            pltpu.make_async_copy(h_hbm.at[sample, pl.ds(tok[0, s], 1), :], xg_ref.at[buf, pl.ds(s, 1), :],
                                  sem.at[buf]).start()

    @pl.when(step == 0)
    def _():
        gather(tok_ref, b, cur)

    @pl.when(e == 0)
    def _():
        o_ref[...] = jnp.zeros_like(o_ref)

    def wait(buf):
        pltpu.make_async_copy(h_hbm.at[0, pl.ds(0, n_slots), :], xg_ref.at[buf], sem.at[buf]).wait()

    wait(cur)
    xg = xg_ref[cur].astype(BF)
    last = step + 1 == pl.num_programs(0) * n_e
    nxt_sample = jnp.minimum(jnp.where(e == n_e - 1, b + 1, b), pl.num_programs(0) - 1)
    quarter = n_slots // 4

    def gather_next(part):
        gather(tokn_ref, nxt_sample, 1 - cur, range(part * quarter, (part + 1) * quarter))

    pos = pos_ref[0, pl.ds(e, 1), :]
    aff = aff_ref[0, pl.ds(e, 1), :]
    T = pos.shape[1]
    hit = lax.broadcasted_iota(jnp.int32, (n_slots, T), 0) == pos
    w_slot = jnp.sum(jnp.where(hit, aff, 0.0), axis=1, keepdims=True)
    gather_next(0)
    gt = _dot(xg, wg_ref[...])
    gather_next(1)
    up = _dot(xg, wu_ref[...])
    hid = (gt * jax.nn.sigmoid(gt) * up * w_slot).astype(BF)
    gather_next(2)
    y = _dot(hid, wd_ref[...]).astype(BF)
    gather_next(3)

    lane = lax.broadcasted_iota(jnp.int32, (1, LANES), 1)
    slot_row = lax.broadcasted_iota(jnp.int32, (1, n_slots), 1).astype(F32)
    for t0 in range(0, T, 256):
        pos_col = jnp.sum(jnp.where(lane == e, post_ref[0, t0:t0 + 256, :], 0.0), axis=1, keepdims=True)
        o_ref[0, t0:t0 + 256, :] += _dot((pos_col == slot_row).astype(BF), y)

    @pl.when(last)
    def _():
        wait(1 - cur)

    if final:
        @pl.when(e == n_e - 1)
        def _():
            for t0 in range(0, T, 256):
                rows = slice(t0, t0 + 256)
                o_ref[0, rows, :] = _rms(x_ref[0, rows, :] + gate_ref[0] * o_ref[0, rows, :], g_ref[...])


def _moe(h, tok, pos, pos_t, aff_t, weights, l, final=None):
    B, T, D = h.shape
    E, n_slots = tok.shape[1:]
    wg, wu, wd = weights

    def wspec(w):
        return pl.BlockSpec((None, None) + w.shape[2:], lambda b, e: (l, e, 0, 0))

    tok = tok.reshape(B * E, 1, n_slots)
    in_specs = [pl.BlockSpec((None, 1, n_slots), lambda b, e: (b * E + e, 0, 0), memory_space=pltpu.SMEM),
                pl.BlockSpec((None, 1, n_slots), lambda b, e: (jnp.minimum(b * E + e + 1, B * E - 1), 0, 0),
                             memory_space=pltpu.SMEM),
                pl.BlockSpec(memory_space=pl.ANY),
                pl.BlockSpec((1, E, T), lambda b, e: (b, 0, 0)),
                pl.BlockSpec((1, T, LANES), lambda b, e: (b, 0, 0)),
                pl.BlockSpec((1, E, T), lambda b, e: (b, 0, 0)),
                wspec(wg), wspec(wu), wspec(wd)]
    args = [tok, tok, h, pos, pos_t, aff_t, wg, wu, wd]
    if final is not None:
        x, mods3, g = final
        in_specs += [pl.BlockSpec((1, T, D), lambda b, e: (b, 0, 0), pipeline_mode=pl.Buffered(1)),
                     pl.BlockSpec((1, 1, D), lambda b, e: (b * 6 + 5, 0, 0)),
                     pl.BlockSpec((1, D), lambda b, e: (0, 0))]
        args += [x, mods3, g.reshape(1, D)]
    return pl.pallas_call(
        functools.partial(_moe_kernel, n_slots=n_slots, final=final is not None),
        grid=(B, E),
        in_specs=in_specs,
        out_specs=pl.BlockSpec((1, T, D), lambda b, e: (b, 0, 0)),
        out_shape=jax.ShapeDtypeStruct((B, T, D), F32),
        scratch_shapes=[pltpu.VMEM((2, n_slots, D), F32), pltpu.SemaphoreType.DMA((2,))],
        compiler_params=_cparams("arbitrary", "arbitrary"),
        name="moe",
    )(*args)


def _reorder_w_in(w_in):
    parts, off = [], 0
    for s in IN_SIZES:
        parts.append(w_in[..., off:off + s])
        off += s
    qa, ka, va, dq, dkv, kr, gq, gk, gv, a_f, a_b, og, m_a, m_b, m_c = parts
    pad = jnp.zeros(w_in.shape[:-1] + (LANES - MLA_ROPE_DIM - 2 * GLA_GATE_RANK,), w_in.dtype)
    return jnp.concatenate([m_a, m_b, m_c, gv, og, qa, ka, va, gq, gk, dq, dkv, kr, a_f, a_b, pad], axis=-1).astype(BF)


def kernel(x, c, ctx, c_ctx, ada_w, ada_b, norm1, norm2, w_in, na_rpb, mla_q_norm, mla_w_uq, mla_kv_norm, mla_w_ukv,
           gla_w_af, gla_b_af, gla_w_ab, gla_b_ab, gla_o_norm, w_br_a, w_br_b, w_br_c, w_out,
           moe_router, moe_w_gate, moe_w_up, moe_w_down, final_norm):
    B, T, D = x.shape
    L = ctx.shape[1]
    depth = ada_w.shape[0]
    assert B < MOD_ROWS
    cap_l = EC_CAPACITY * T // N_EXPERTS
    cap_c = EC_CAPACITY * L // N_EXPERTS
    ctx_row = B
    cc = jnp.concatenate([c, c_ctx[None], jnp.zeros((MOD_ROWS - B - 1, D), F32)], axis=0)

    def lat_row(b):
        return b

    def ctx_row_fn(b):
        return ctx_row

    def rows3(a):
        return a.reshape(depth, 1, -1)

    ada_b3, norm1_3, norm2_3 = rows3(ada_b), rows3(norm1), rows3(norm2)
    w_in_r = _reorder_w_in(w_in)
    na_bias = _na_bias(na_rpb, T // GRID_W)
    mla_w = _mla_weights(mla_w_uq, mla_w_ukv)
    qn3, kvn3 = rows3(mla_q_norm), rows3(mla_kv_norm)
    rope_tabs = _rope_tables(T)
    gate_pad = lambda w, off: jnp.pad(w, ((0, 0), (off, LANES - off - GLA_GATE_RANK), (0, 0))).astype(BF)
    gla_params = (gate_pad(gla_w_af, SM_AF), rows3(gla_b_af), gate_pad(gla_w_ab, SM_AB), rows3(gla_b_ab),
                  rows3(gla_o_norm))
    wr_t = jnp.swapaxes(moe_router, 1, 2)
    wr_hi = wr_t.astype(BF)
    wr_lo = (wr_t - wr_hi.astype(F32)).astype(BF)
    merge_w = (w_br_a.astype(BF), w_br_b.astype(BF), w_br_c.astype(BF), w_out.astype(BF),
               jnp.concatenate([wr_hi, wr_lo], axis=1))
    moe_w = (moe_w_gate.astype(BF), moe_w_up.astype(BF), moe_w_down.astype(BF))

    moe_l = moe_c = mods3 = None
    for l in range(depth):
        ctx_out = l < depth - 1
        prev_mods3 = mods3
        mods3 = _ada(cc, ada_w, ada_b3, l).reshape(MOD_ROWS * 6, 1, D)
        z_l, x = _in_proj(x, moe_l, prev_mods3, mods3, lat_row, norm1_3, w_in_r, l)
        z_c, ctx = _in_proj(ctx, moe_c, prev_mods3, mods3, ctx_row_fn, norm1_3, w_in_r, l)

        oa_l, oa_c = _na_attention(z_l, z_c, na_bias, l, ctx_out)

        qb_l, kb_l, vb_l = _mla_proj(z_l, qn3, kvn3, mla_w, rope_tabs, l)
        qb_c, kb_c, vb_c = _mla_proj(z_c, qn3, kvn3, mla_w, None, l)
        ob_l = _mla_attn(qb_l, kb_c, vb_c, kb_l, vb_l)

        oc_l, oc_c = _gla(z_l, z_c, gla_params, l, ctx_out)

        x, h_l, aff_l = _merge(oa_l, ob_l, oc_l, z_l, x, mods3, lat_row, norm2_3, merge_w, l, tm=512)
        pos_l, post_l, tok_l = _route(aff_l, cap_l, 0, 0)
        last = (x, mods3, final_norm) if l == depth - 1 else None
        moe_l = _moe(h_l, tok_l, pos_l, post_l, aff_l, moe_w, l, final=last)

        if ctx_out:
            ob_c = _mla_attn(qb_c, kb_c, vb_c)
            ctx, h_c, aff_c = _merge(oa_c, ob_c, oc_c, z_c, ctx, mods3, ctx_row_fn, norm2_3, merge_w, l, tm=256)
            pos_c, post_c, tok_c = _route(aff_c, cap_c, cap_c, L)
            flat = lambda a: jnp.swapaxes(a, 0, 1).reshape(1, N_EXPERTS, -1)
            moe_c = _moe(h_c.reshape(1, B * L, D), flat(tok_c), flat(pos_c), post_c.reshape(1, B * L, LANES),
                         flat(aff_c), moe_w, l).reshape(B, L, D)
    return moe_l
```

```python
import functools

import numpy as np
import jax
import jax.numpy as jnp
from jax import lax
from jax.experimental import pallas as pl
from jax.experimental.pallas import tpu as pltpu

BF = jnp.bfloat16
F32 = jnp.float32

D_MODEL = 1024
GRID_W = 64
EPS = 1e-6
NA_HEADS = 4
NA_HEAD_DIM = 64
NA_WIN_H = 8
NA_WIN_W = 16
MLA_HEADS = 4
MLA_Q_RANK = 256
MLA_KV_RANK = 128
MLA_NOPE_DIM = 64
MLA_ROPE_DIM = 32
MLA_V_DIM = 64
ROPE_THETA = 10000.0
GLA_HEADS = 4
GLA_DK = 64
GLA_DV = 128
GLA_GATE_RANK = 16
GLA_TAU = 16.0
GLA_CHUNK = 64
N_EXPERTS = 16
EXPERT_FF = 1024
EC_CAPACITY = 2
IN_SIZES = (256, 256, 256, 256, 128, 32, 256, 256, 512, 16, 16, 512, 1024, 1024, 1024)

LANES = 128
VMEM_LIMIT = 56 * 1024 * 1024
MOD_ROWS = 16

Z_MA, Z_MB, Z_MC = 0, 1024, 2048
Z_GV, Z_OG = 3072, 3584
Z_QA, Z_KA, Z_VA = 4096, 4352, 4608
Z_GQ, Z_GK = 4864, 5120
Z_MLA = 5376
Z_SM = 5760
Z_COLS = 5888
SM_KR, SM_AF, SM_AB = 0, 32, 48

CTX_KEY_COLS = (Z_GV, Z_GV + 256, Z_KA, Z_VA, Z_GK, Z_MLA, Z_MLA + 256)

SUPER = 256
NEG = -1e30


def _cparams(*sem):
    return pltpu.CompilerParams(dimension_semantics=sem, vmem_limit_bytes=VMEM_LIMIT)


def _layer(a, l):
    return pl.BlockSpec((None,) + a.shape[1:], lambda *_: (l,) + (0,) * (a.ndim - 1), pipeline_mode=pl.Buffered(1))


def _whole(a):
    return pl.BlockSpec(a.shape, lambda *_: (0,) * a.ndim, pipeline_mode=pl.Buffered(1))


def _dot(a, b):
    return jnp.dot(a, b, preferred_element_type=F32)


def _dot_nt(a, b):
    return lax.dot_general(a, b, (((1,), (1,)), ((), ())), preferred_element_type=F32)


def _dot_tn(a, b):
    return lax.dot_general(a, b, (((0,), (0,)), ((), ())), preferred_element_type=F32)


def _rms(x, g):
    return x * lax.rsqrt(jnp.mean(x * x, axis=-1, keepdims=True) + EPS) * g


def _split3(x):
    hi = x.astype(BF)
    r = x - hi.astype(F32)
    mid = r.astype(BF)
    lo = (r - mid.astype(F32)).astype(BF)
    return hi, mid, lo


def _ada_kernel(c_ref, w_ref, b_ref, o_ref):
    c = c_ref[...]
    a = (c * jax.nn.sigmoid(c)).astype(BF)
    o_ref[...] = _dot(a, w_ref[...]) + b_ref[...]


def _ada(cc, ada_w, ada_b3, l):
    n = ada_w.shape[2]
    tn = 1536
    return pl.pallas_call(
        _ada_kernel,
        grid=(n // tn,),
        in_specs=[pl.BlockSpec((MOD_ROWS, D_MODEL), lambda j: (0, 0)),
                  pl.BlockSpec((None, D_MODEL, tn), lambda j: (l, 0, j)),
                  pl.BlockSpec((None, 1, tn), lambda j: (l, 0, j))],
        out_specs=pl.BlockSpec((MOD_ROWS, tn), lambda j: (0, j)),
        out_shape=jax.ShapeDtypeStruct((MOD_ROWS, n), F32),
        compiler_params=_cparams("arbitrary"),
        name="ada_mod",
    )(cc, ada_w, ada_b3)


def _in_proj_kernel(*refs, with_moe, cols):
    if with_moe:
        x_ref, moe_ref, gate_ref, g_ref, shift_ref, scale_ref, w_ref, o_ref, xo_ref, h_ref = refs
        x = x_ref[0] + gate_ref[0] * moe_ref[0]
        xo_ref[0] = x
    else:
        x_ref, g_ref, shift_ref, scale_ref, w_ref, o_ref, h_ref = refs
        x = x_ref[0]
    h = _rms(x, g_ref[...]) * (1.0 + scale_ref[0]) + shift_ref[0]
    h_ref[...] = h.astype(BF)
    for n0 in range(0, Z_COLS, 256):
        if cols is None or n0 in cols:
            o_ref[0, :, n0:n0 + 256] = _dot(h_ref[...], w_ref[:, n0:n0 + 256]).astype(BF)
        else:
            o_ref[0, :, n0:n0 + 256] = jnp.zeros((o_ref.shape[1], 256), BF)


def _in_proj(x, moe, prev_mods3, mods3, mod_row, norm1_3, w, l, tm=512, cols=None):
    B, T, D = x.shape
    tm = min(tm, T)
    with_moe = moe is not None

    def mod_spec(k):
        return pl.BlockSpec((1, 1, D), lambda b, i: (mod_row(b) * 6 + k, 0, 0))

    x_spec = pl.BlockSpec((1, tm, D), lambda b, i: (b, i, 0))
    in_specs = [x_spec]
    args = [x]
    if with_moe:
        in_specs += [x_spec, mod_spec(5)]
        args += [moe, prev_mods3]
    in_specs += [_layer(norm1_3, l), mod_spec(0), mod_spec(1), _layer(w, l)]
    args += [norm1_3, mods3, mods3, w]
    z_spec = pl.BlockSpec((1, tm, Z_COLS), lambda b, i: (b, i, 0))
    z_shape = jax.ShapeDtypeStruct((B, T, Z_COLS), BF)
    if with_moe:
        out_specs, out_shape = [z_spec, x_spec], [z_shape, jax.ShapeDtypeStruct((B, T, D), F32)]
    else:
        out_specs, out_shape = z_spec, z_shape
    res = pl.pallas_call(
        functools.partial(_in_proj_kernel, with_moe=with_moe, cols=cols),
        grid=(B, T // tm),
        in_specs=in_specs, out_specs=out_specs, out_shape=out_shape,
        scratch_shapes=[pltpu.VMEM((tm, D), BF)],
        compiler_params=_cparams("arbitrary", "arbitrary"),
        name="in_proj",
    )(*args)
    return (res[0], res[1]) if with_moe else (res, x)


def _softmax_pv_many(heads):
    maxes = []
    for s_list, _ in heads:
        m = s_list[0].max(axis=-1, keepdims=True)
        for s in s_list[1:]:
            m = jnp.maximum(m, s.max(axis=-1, keepdims=True))
        maxes.append(m)
    probs = [[jnp.exp((s - m).astype(BF)) for s in s_list] for (s_list, _), m in zip(heads, maxes)]
    outs = []
    for (_, v_list), p_list in zip(heads, probs):
        o = _dot(p_list[0], v_list[0])
        for p, v in zip(p_list[1:], v_list[1:]):
            o = o + _dot(p, v)
        outs.append(o / pltpu.roll(o, LANES // 2, axis=1))
    return outs


def _softmax_pv_sum(heads):
    maxes = []
    for s_list, _ in heads:
        m = s_list[0].max(axis=-1, keepdims=True)
        for s in s_list[1:]:
            m = jnp.maximum(m, s.max(axis=-1, keepdims=True))
        maxes.append(m)
    probs = [[jnp.exp(s - m) for s in s_list] for (s_list, _), m in zip(heads, maxes)]
    outs = []
    for (_, v_list), p_list in zip(heads, probs):
        l = p_list[0].sum(axis=-1, keepdims=True)
        o = _dot(p_list[0].astype(BF), v_list[0])
        for p, v in zip(p_list[1:], v_list[1:]):
            l = l + p.sum(axis=-1, keepdims=True)
            o = o + _dot(p.astype(BF), v)
        outs.append(o / l)
    return outs


def _lane_masks():
    lane = lax.broadcasted_iota(jnp.int32, (1, LANES), 1)
    return lane < 64, lane >= 64


NA_ROWS_PER_STEP = 2
NA_BAND = 10
NA_CASE_ROWS = (0, 2, 4, 28, 30)


def _na_bias(rpb, rows):
    depth = rpb.shape[0]
    qc = np.arange(GRID_W)
    kj = np.arange(GRID_W)
    cs = np.clip(qc - NA_WIN_W // 2, 0, GRID_W - NA_WIN_W)
    col_ok = (kj[None, :] >= cs[:, None]) & (kj[None, :] < cs[:, None] + NA_WIN_W)
    co = kj[None, :] - qc[:, None] + (NA_WIN_W - 1)
    onehot = np.zeros((2 * NA_WIN_W - 1, GRID_W, GRID_W), np.float32)
    qi, ki = np.nonzero(col_ok)
    onehot[co[qi, ki], qi, ki] = 1.0
    toep = jnp.einsum("lhrc,cp->lhrp", rpb.astype(F32), jnp.asarray(onehot.reshape(2 * NA_WIN_W - 1, -1)),
                      precision=lax.Precision.HIGHEST)
    toep = (toep.reshape(depth, NA_HEADS, 2 * NA_WIN_H - 1, GRID_W, GRID_W)
            + jnp.asarray(np.where(col_ok, 0.0, NEG), F32))
    neg = jnp.full((depth, NA_HEADS, GRID_W, GRID_W), NEG, F32)
    cases = []
    for r0 in NA_CASE_ROWS:
        bs = int(np.clip(r0 - NA_WIN_H // 2, 0, rows - NA_BAND))
        q_rows = []
        for qr in range(NA_ROWS_PER_STEP):
            r = r0 + qr
            rs = int(np.clip(r - NA_WIN_H // 2, 0, rows - NA_WIN_H))
            blocks = []
            for i in range(NA_BAND):
                krow = bs + i
                blocks.append(toep[:, :, krow - r + NA_WIN_H - 1] if rs <= krow < rs + NA_WIN_H else neg)
            q_rows.append(jnp.concatenate(blocks, axis=-1))
        cases.append(jnp.concatenate(q_rows, axis=-2))
    return jnp.stack(cases, axis=2)


def _na_kernel(*refs, rows, ctx_out):
    if ctx_out:
        ql_ref, kl_ref, vl_ref, qc_ref, kc_ref, vc_ref, bias_ref, ol_ref, oc_ref = refs
    else:
        ql_ref, kl_ref, vl_ref, kc_ref, vc_ref, bias_ref, ol_ref = refs
    scale = NA_HEAD_DIM ** -0.5
    masks = _lane_masks()
    nq = NA_ROWS_PER_STEP * GRID_W
    nk = NA_BAND * GRID_W

    def body(rp, carry):
        r0 = rp * NA_ROWS_PER_STEP
        bs = jnp.clip(r0 - NA_WIN_H // 2, 0, rows - NA_BAND)
        case = jnp.where(r0 < 4, r0 // 2, jnp.where(r0 > rows - 6, (r0 - (rows - 4)) // 2 + 3, 2))
        q0 = pl.multiple_of(r0 * GRID_W, nq)
        k0 = pl.multiple_of(bs * GRID_W, 2 * GRID_W)
        heads = []
        for hp in range(NA_HEADS // 2):
            sl = slice(hp * LANES, (hp + 1) * LANES)
            q = ql_ref[0, pl.ds(q0, nq), sl] * scale
            kb = kl_ref[0, pl.ds(k0, nk), sl]
            vb = vl_ref[0, pl.ds(k0, nk), sl]
            kc = kc_ref[0, :, sl]
            vc = vc_ref[0, :, sl]
            for hh in range(2):
                qm = jnp.where(masks[hh], q, jnp.zeros_like(q))
                s_loc = _dot_nt(qm, kb) + bias_ref[2 * hp + hh, case]
                s_ctx = _dot_nt(qm, kc)
                heads.append(([s_loc, s_ctx], [vb, vc]))
        outs = _softmax_pv_sum(heads)
        for hp in range(NA_HEADS // 2):
            sl = slice(hp * LANES, (hp + 1) * LANES)
            ol_ref[0, pl.ds(q0, nq), sl] = jnp.where(masks[0], outs[2 * hp], outs[2 * hp + 1]).astype(BF)
        return carry

    lax.fori_loop(0, rows // NA_ROWS_PER_STEP, body, 0)

    if ctx_out:
        heads = []
        for hp in range(NA_HEADS // 2):
            sl = slice(hp * LANES, (hp + 1) * LANES)
            q = qc_ref[0, :, sl] * scale
            kc = kc_ref[0, :, sl]
            for hh in range(2):
                qm = jnp.where(masks[hh], q, jnp.zeros_like(q))
                heads.append(([_dot_nt(qm, kc)], [vc_ref[0, :, sl]]))
        outs = _softmax_pv_sum(heads)
        for hp in range(NA_HEADS // 2):
            sl = slice(hp * LANES, (hp + 1) * LANES)
            oc_ref[0, :, sl] = jnp.where(masks[0], outs[2 * hp], outs[2 * hp + 1]).astype(BF)


def _na_attention(z_l, z_c, bias, l, ctx_out):
    B, T, _ = z_l.shape
    L = z_c.shape[1]
    rows = T // GRID_W

    def zl(col):
        return pl.BlockSpec((1, T, 256), lambda b: (b, 0, col // 256))

    def zc(col):
        return pl.BlockSpec((1, L, 256), lambda b: (b, 0, col // 256))

    ol_spec = pl.BlockSpec((1, T, 256), lambda b: (b, 0, 0))
    ol_shape = jax.ShapeDtypeStruct((B, T, 256), BF)
    if ctx_out:
        in_specs = [zl(Z_QA), zl(Z_KA), zl(Z_VA), zc(Z_QA), zc(Z_KA), zc(Z_VA), _layer(bias, l)]
        args = [z_l, z_l, z_l, z_c, z_c, z_c, bias]
        out_specs = [ol_spec, pl.BlockSpec((1, L, 256), lambda b: (b, 0, 0))]
        out_shape = [ol_shape, jax.ShapeDtypeStruct((B, L, 256), BF)]
    else:
        in_specs = [zl(Z_QA), zl(Z_KA), zl(Z_VA), zc(Z_KA), zc(Z_VA), _layer(bias, l)]
        args = [z_l, z_l, z_l, z_c, z_c, bias]
        out_specs, out_shape = ol_spec, ol_shape
    res = pl.pallas_call(
        functools.partial(_na_kernel, rows=rows, ctx_out=ctx_out),
        grid=(B,), in_specs=in_specs, out_specs=out_specs, out_shape=out_shape,
        compiler_params=_cparams("arbitrary"),
        name="na_attn",
    )(*args)
    return (res[0], res[1]) if ctx_out else (res, None)


MLA_QK = MLA_NOPE_DIM + MLA_ROPE_DIM
MLA_W = MLA_HEADS * LANES


def _mla_proj_kernel(*refs, rope):
    if rope:
        (zm_ref, sm_ref, qn_ref, kvn_ref, wq_ref, wqs_ref, wk_ref, wv_ref, e_ref, es_ref, vone_ref, cos_ref, sin_ref,
         q_ref, k_ref, v_ref) = refs
    else:
        zm_ref, sm_ref, qn_ref, kvn_ref, wq_ref, wk_ref, wv_ref, e_ref, vone_ref, q_ref, k_ref, v_ref = refs
    zm = zm_ref[0].astype(F32)
    nq = _rms(zm[:, :MLA_Q_RANK], qn_ref[...]).astype(BF)
    nkv = _rms(zm[:, MLA_Q_RANK:], kvn_ref[...]).astype(BF)
    sm = sm_ref[0]
    scale = MLA_QK ** -0.5
    q = _dot(nq, wq_ref[...])
    k = _dot(nkv, wk_ref[...]) + _dot(sm, e_ref[...])
    if rope:
        cos = cos_ref[...]
        sin = sin_ref[...]
        q = q * cos + _dot(nq, wqs_ref[...]) * sin
        k = k * cos + _dot(sm, es_ref[...]) * sin
    q_ref[0] = (q * scale).astype(BF)
    k_ref[0] = k.astype(BF)
    v_ref[0] = (_dot(nkv, wv_ref[...]) + vone_ref[...]).astype(BF)


def _rope_swap(e):
    return np.where(e % 16 < 8, e + 8, e - 8)


def _mla_weights(w_uq, w_ukv):
    depth = w_uq.shape[0]
    quarter = MLA_ROPE_DIM // 4
    q4 = w_uq.reshape(depth, MLA_Q_RANK, MLA_HEADS, MLA_QK)
    nope, rope = q4[..., :MLA_NOPE_DIM], q4[..., MLA_NOPE_DIM:]
    rope_sw = jnp.flip(rope.reshape(depth, MLA_Q_RANK, MLA_HEADS, 2, 2, quarter), axis=4).reshape(rope.shape)
    zpad = jnp.zeros(rope.shape, w_uq.dtype)
    wq = jnp.concatenate([nope, rope, zpad], axis=-1).reshape(depth, MLA_Q_RANK, MLA_W)
    wqs = jnp.concatenate([jnp.zeros(nope.shape, w_uq.dtype), rope_sw, zpad], axis=-1).reshape(depth, MLA_Q_RANK, MLA_W)
    kv4 = w_ukv.reshape(depth, MLA_KV_RANK, MLA_HEADS, MLA_NOPE_DIM + MLA_V_DIM)
    k_nope, v = kv4[..., :MLA_NOPE_DIM], kv4[..., MLA_NOPE_DIM:]
    wk = jnp.concatenate([k_nope, jnp.zeros(k_nope.shape, w_ukv.dtype)], axis=-1).reshape(depth, MLA_KV_RANK, MLA_W)
    vz = jnp.zeros(v.shape[:2] + (1, MLA_V_DIM), w_ukv.dtype)
    wv = jnp.concatenate([jnp.concatenate([v[:, :, h:h + 1], vz] if h % 2 == 0 else [vz, v[:, :, h:h + 1]], axis=-1)
                          for h in range(MLA_HEADS)], axis=2).reshape(depth, MLA_KV_RANK, MLA_W)
    e = np.arange(MLA_ROPE_DIM)
    em = np.zeros((LANES, MLA_W), np.float32)
    ems = np.zeros((LANES, MLA_W), np.float32)
    for h in range(MLA_HEADS):
        em[SM_KR + e, h * LANES + MLA_NOPE_DIM + e] = 1.0
        ems[SM_KR + _rope_swap(e), h * LANES + MLA_NOPE_DIM + e] = 1.0
    lane = np.arange(MLA_W) % LANES
    v_ones = ((lane >= MLA_V_DIM) == ((np.arange(MLA_W) // LANES) % 2 == 0)).astype(np.float32)[None, :]
    return (wq.astype(BF), wqs.astype(BF), wk.astype(BF), wv.astype(BF), jnp.asarray(em, BF), jnp.asarray(ems, BF),
            jnp.asarray(v_ones, F32))


def _rope_tables(T):
    t = jnp.arange(T)
    quarter = MLA_ROPE_DIM // 4
    inv_freq = ROPE_THETA ** (-jnp.arange(quarter, dtype=F32) / quarter)
    ang_r = (t // GRID_W).astype(F32)[:, None] * inv_freq
    ang_c = (t % GRID_W).astype(F32)[:, None] * inv_freq
    cr, sr, ccol, scol = jnp.cos(ang_r), jnp.sin(ang_r), jnp.cos(ang_c), jnp.sin(ang_c)
    ones = jnp.ones((T, MLA_NOPE_DIM), F32)
    pad1 = jnp.ones((T, LANES - MLA_QK), F32)
    cos_h = jnp.concatenate([ones, cr, cr, ccol, ccol, pad1], axis=1)
    sin_h = jnp.concatenate([0.0 * ones, -sr, sr, -scol, scol, 0.0 * pad1], axis=1)
    return jnp.tile(cos_h, (1, MLA_HEADS)), jnp.tile(sin_h, (1, MLA_HEADS))


def _mla_proj(z, qn3, kvn3, weights, rope_tabs, l, tm=512):
    B, T, _ = z.shape
    wq, wqs, wk, wv, em, ems, v_ones = weights
    rope = rope_tabs is not None
    tm = min(tm, T)
    zm_spec = pl.BlockSpec((1, tm, 384), lambda i, b: (b, i, Z_MLA // 384))
    sm_spec = pl.BlockSpec((1, tm, LANES), lambda i, b: (b, i, Z_SM // LANES))
    if rope:
        tab_spec = pl.BlockSpec((tm, MLA_W), lambda i, b: (i, 0))
        ins = [z, z, qn3, kvn3, wq, wqs, wk, wv, em, ems, v_ones, rope_tabs[0], rope_tabs[1]]
        in_specs = ([zm_spec, sm_spec] + [_layer(a, l) for a in ins[2:8]] + [_whole(em), _whole(ems), _whole(v_ones)]
                    + [tab_spec, tab_spec])
    else:
        ins = [z, z, qn3, kvn3, wq, wk, wv, em, v_ones]
        in_specs = [zm_spec, sm_spec] + [_layer(a, l) for a in ins[2:7]] + [_whole(em), _whole(v_ones)]
    return pl.pallas_call(
        functools.partial(_mla_proj_kernel, rope=rope),
        grid=(T // tm, B), in_specs=in_specs,
        out_specs=[pl.BlockSpec((1, tm, MLA_W), lambda i, b: (b, i, 0)),
                   pl.BlockSpec((1, tm, MLA_W), lambda i, b: (b, i, 0)),
                   pl.BlockSpec((1, tm, MLA_W), lambda i, b: (b, i, 0))],
        out_shape=[jax.ShapeDtypeStruct((B, T, MLA_W), BF)] * 3,
        compiler_params=_cparams("arbitrary", "arbitrary"),
        name="mla_proj",
    )(*ins)


def _mla_attn_kernel(*refs, with_latent):
    if with_latent:
        q_ref, kc_ref, vc_ref, kl_ref, vl_ref, o_ref = refs
    else:
        q_ref, kc_ref, vc_ref, o_ref = refs
    masks = _lane_masks()
    heads = []
    for h in range(MLA_HEADS):
        hsl = slice(h * LANES, (h + 1) * LANES)
        q = q_ref[0, :, hsl]
        s_list = [_dot_nt(q, kc_ref[0, :, hsl])]
        v_list = [vc_ref[0, :, hsl]]
        if with_latent:
            s_list.append(_dot_nt(q, kl_ref[0, :, hsl]))
            v_list.append(vl_ref[0, :, hsl])
        heads.append((s_list, v_list))
    outs = _softmax_pv_many(heads)
    for hp in range(MLA_HEADS // 2):
        vsl = slice(hp * LANES, (hp + 1) * LANES)
        o_ref[0, :, vsl] = jnp.where(masks[0], outs[2 * hp], outs[2 * hp + 1]).astype(BF)


def _mla_attn(q, k_c, v_c, k_l=None, v_l=None, tq=256):
    B, Tq, W = q.shape
    L = k_c.shape[1]
    with_latent = k_l is not None
    in_specs = [pl.BlockSpec((1, tq, W), lambda b, i: (b, i, 0)),
                pl.BlockSpec((1, L, W), lambda b, i: (b, 0, 0)),
                pl.BlockSpec((1, L, W), lambda b, i: (b, 0, 0))]
    args = [q, k_c, v_c]
    if with_latent:
        T = k_l.shape[1]
        in_specs += [pl.BlockSpec((1, T, W), lambda b, i: (b, 0, 0)),
                     pl.BlockSpec((1, T, W), lambda b, i: (b, 0, 0))]
        args += [k_l, v_l]
    return pl.pallas_call(
        functools.partial(_mla_attn_kernel, with_latent=with_latent),
        grid=(B, Tq // tq), in_specs=in_specs,
        out_specs=pl.BlockSpec((1, tq, 256), lambda b, i: (b, i, 0)),
        out_shape=jax.ShapeDtypeStruct((B, Tq, 256), BF),
        compiler_params=_cparams("arbitrary", "arbitrary"),
        name="mla_attn",
    )(*args)


def _gla_superchunks(jobs, st_ref):
    n_chunk = SUPER // GLA_CHUNK
    masks = _lane_masks()
    rowblk = lax.broadcasted_iota(jnp.int32, (SUPER, LANES), 0) // GLA_CHUNK
    own = ((lax.broadcasted_iota(jnp.int32, (2 * GLA_DV, LANES), 0) < GLA_DV)
           == (lax.broadcasted_iota(jnp.int32, (2 * GLA_DV, LANES), 1) < GLA_DK))
    gs = [jax.nn.log_sigmoid(_dot(sm, w_a) + b_a) * (1.0 / GLA_TAU) for (_, _, _, sm, w_a, b_a, _, _, _) in jobs]
    pre = []
    for (q, k, v, sm, w_a, b_a, tri, d, reverse), g in zip(jobs, gs):
        g_hi = g.astype(BF)
        g_lo = (g - g_hi.astype(F32)).astype(BF)
        b = _dot(tri, g_hi) + _dot(tri, g_lo)
        ends = [i * GLA_CHUNK if reverse else (i + 1) * GLA_CHUNK - 1 for i in range(n_chunk)]
        tot = jnp.concatenate([jnp.broadcast_to(b[r:r + 1, :], (GLA_CHUNK, b.shape[1])) for r in ends], axis=0)
        pre.append((q.astype(F32) * (GLA_DK ** -0.5) * jnp.exp(b), k.astype(F32) * jnp.exp(-b),
                    k.astype(F32) * jnp.exp(tot - b), jnp.exp(tot), tri > 0))
    outs = [[] for _ in jobs]
    for hp in range(GLA_HEADS // 2):
        sl = slice(hp * LANES, (hp + 1) * LANES)
        stage = []
        for (q, k, v, sm, w_a, b_a, tri, d, reverse), (qd_all, kd_all, ke_all, dec_all, keep) in zip(jobs, pre):
            qd = qd_all[:, sl]
            kd = kd_all[:, sl].astype(BF)
            ke = ke_all[:, sl].astype(BF)
            zero = jnp.zeros_like(ke)
            ke_bd = jnp.concatenate([jnp.where(rowblk == i, ke, zero) for i in range(n_chunk)], axis=1)
            v_pair = v[:, hp * 2 * GLA_DV:(hp + 1) * 2 * GLA_DV]
            kv_t = _dot_tn(v_pair, ke_bd)
            a = [jnp.where(keep, _dot_nt(jnp.where(masks[hh], qd, 0.0).astype(BF), kd), 0.0).astype(BF)
                 for hh in range(2)]
            stage.append((qd, dec_all[:, sl], v_pair, kv_t, a))
        inters = []
        for (q, k, v, sm, w_a, b_a, tri, d, reverse), (qd, dec, v_pair, kv_t, a) in zip(jobs, stage):
            s = st_ref[d, hp]
            inter = [None] * n_chunk
            for i in (range(n_chunk - 1, -1, -1) if reverse else range(n_chunk)):
                rows = slice(i * GLA_CHUNK, (i + 1) * GLA_CHUNK)
                inter[i] = _dot_nt(qd[rows].astype(BF), s.astype(BF))
                s = s * dec[i * GLA_CHUNK:i * GLA_CHUNK + 1, :] + jnp.where(own, kv_t[:, i * LANES:(i + 1) * LANES], 0.0)
            st_ref[d, hp] = s
            inters.append(jnp.concatenate(inter, axis=0))
        for n, ((qd, dec, v_pair, kv_t, a), o_inter) in enumerate(zip(stage, inters)):
            for hh in range(2):
                hv = slice(hh * GLA_DV, (hh + 1) * GLA_DV)
                outs[n].append(_dot(a[hh], v_pair[:, hv]) + o_inter[:, hv])
    return [jnp.concatenate(o, axis=1) for o in outs]


def _gla_finish(o, og, onorm):
    ys = []
    for h in range(GLA_HEADS):
        ys.append(_rms(o[:, h * GLA_DV:(h + 1) * GLA_DV], onorm))
    y = jnp.concatenate(ys, axis=1)
    ogf = og.astype(F32)
    return (y * (ogf * jax.nn.sigmoid(ogf))).astype(BF)


def _gla_kernel(*refs, n_lat, ctx_out):
    (ql_ref, kl_ref, vl_ref, sml_ref, ogl_ref, qc_ref, kc_ref, vc_ref, smc_ref, ogc_ref,
     waf_ref, baf_ref, wab_ref, bab_ref, onorm_ref, trif_ref, trib_ref) = refs[:17]
    if ctx_out:
        ol_ref, oc_ref, of_ref, ob_ref, st_ref = refs[17:]
    else:
        ol_ref, of_ref, ob_ref, st_ref = refs[17:]
    onorm = onorm_ref[...]

    def both(fwd_in, bwd_in):
        return _gla_superchunks([fwd_in + (waf_ref[...], baf_ref[...], trif_ref[...], 0, False),
                                 bwd_in + (wab_ref[...], bab_ref[...], trib_ref[...], 1, True)], st_ref)

    st_ref[...] = jnp.zeros_like(st_ref)
    ctx_in = (qc_ref[0], kc_ref[0], vc_ref[0], smc_ref[0])
    o_cf, o_cb = both(ctx_in, ctx_in)
    if ctx_out:
        oc_ref[0] = _gla_finish(o_cf + o_cb, ogc_ref[0], onorm)

    def body(j, carry):
        rf = pl.ds(pl.multiple_of(j * SUPER, SUPER), SUPER)
        rb = pl.ds(pl.multiple_of((n_lat - 1 - j) * SUPER, SUPER), SUPER)
        o_f, o_b = both((ql_ref[0, rf, :], kl_ref[0, rf, :], vl_ref[0, rf, :], sml_ref[0, rf, :]),
                        (ql_ref[0, rb, :], kl_ref[0, rb, :], vl_ref[0, rb, :], sml_ref[0, rb, :]))
        of_ref[rf, :] = o_f
        ob_ref[rb, :] = o_b
        return carry

    lax.fori_loop(0, n_lat, body, 0)

    def finish(j, carry):
        r = pl.ds(pl.multiple_of(j * SUPER, SUPER), SUPER)
        ol_ref[0, r, :] = _gla_finish(of_ref[r, :] + ob_ref[r, :], ogl_ref[0, r, :], onorm)
        return carry

    lax.fori_loop(0, n_lat, finish, 0)


def _gla_consts():
    r = np.arange(SUPER)
    same = (r[:, None] // GLA_CHUNK) == (r[None, :] // GLA_CHUNK)
    return jnp.asarray(same & (r[None, :] <= r[:, None]), BF), jnp.asarray(same & (r[None, :] >= r[:, None]), BF)


def _gla(z_l, z_c, params, l, ctx_out):
    B, T, _ = z_l.shape
    L = z_c.shape[1]
    assert L == SUPER and T % SUPER == 0
    kw = GLA_HEADS * GLA_DK
    vw = GLA_HEADS * GLA_DV
    consts = _gla_consts()

    def zspec(n, width, col):
        return pl.BlockSpec((1, n, width), lambda b: (b, 0, col // width))

    in_specs = ([zspec(T, kw, Z_GQ), zspec(T, kw, Z_GK), zspec(T, vw, Z_GV), zspec(T, LANES, Z_SM), zspec(T, vw, Z_OG),
                 zspec(L, kw, Z_GQ), zspec(L, kw, Z_GK), zspec(L, vw, Z_GV), zspec(L, LANES, Z_SM), zspec(L, vw, Z_OG)]
                + [_layer(a, l) for a in params] + [_whole(a) for a in consts])
    ol_spec = pl.BlockSpec((1, T, vw), lambda b: (b, 0, 0))
    ol_shape = jax.ShapeDtypeStruct((B, T, vw), BF)
    if ctx_out:
        out_specs = [ol_spec, pl.BlockSpec((1, L, vw), lambda b: (b, 0, 0))]
        out_shape = [ol_shape, jax.ShapeDtypeStruct((B, L, vw), BF)]
    else:
        out_specs, out_shape = ol_spec, ol_shape
    scratch = [pltpu.VMEM((T, vw), F32), pltpu.VMEM((T, vw), F32),
               pltpu.VMEM((2, GLA_HEADS // 2, 2 * GLA_DV, LANES), F32)]
    res = pl.pallas_call(
        functools.partial(_gla_kernel, n_lat=T // SUPER, ctx_out=ctx_out),
        grid=(B,), in_specs=in_specs, out_specs=out_specs, out_shape=out_shape,
        scratch_shapes=scratch,
        compiler_params=_cparams("arbitrary"),
        name="gla",
    )(*([z_l] * 5 + [z_c] * 5 + list(params) + list(consts)))
    return (res[0], res[1]) if ctx_out else (res, None)


MERGE_SUB = 256


def _merge_kernel(oa_ref, ob_ref, oc_ref, ma_ref, mb_ref, mc_ref, x_ref, gate_ref, shift_ref, scale_ref, g_ref,
                  wa_ref, wb_ref, wc_ref, wo_ref, wr_ref, xo_ref, h_ref, aff_ref, y_ref):
    tm = x_ref.shape[1]
    gate = gate_ref[0]
    subs = [slice(r0, r0 + MERGE_SUB) for r0 in range(0, tm, MERGE_SUB)]

    def branch(o_ref, m_ref, w_ref, rows, n0):
        m = m_ref[0, rows, n0:n0 + 256].astype(F32)
        return jax.nn.sigmoid(m) * _dot(o_ref[0, rows, :], w_ref[:, n0:n0 + 256])

    for n0 in range(0, D_MODEL, 256):
        for rows in subs:
            y = (branch(oa_ref, ma_ref, wa_ref, rows, n0) + branch(ob_ref, mb_ref, wb_ref, rows, n0)
                 + branch(oc_ref, mc_ref, wc_ref, rows, n0))
            y_ref[rows, n0:n0 + 256] = y.astype(BF)
    for n0 in range(0, D_MODEL, 256):
        for rows in subs:
            xo_ref[0, rows, n0:n0 + 256] = (x_ref[0, rows, n0:n0 + 256]
                                            + gate[:, n0:n0 + 256] * _dot(y_ref[rows, :], wo_ref[:, n0:n0 + 256]))
    hs = [_rms(xo_ref[0, rows, :], g_ref[...]) * (1.0 + scale_ref[0]) + shift_ref[0] for rows in subs]
    wr = wr_ref[...]
    for rows, h in zip(subs, hs):
        h_hi = h.astype(BF)
        h_lo = (h - h_hi.astype(F32)).astype(BF)
        h_ref[0, rows, :] = h_hi
        part = _dot_nt(wr, h_hi)
        logits = part[:N_EXPERTS] + part[N_EXPERTS:] + _dot_nt(wr[:N_EXPERTS], h_lo)
        e = jnp.exp(logits - logits.max(axis=0, keepdims=True))
        aff_ref[0, :, rows] = e / e.sum(axis=0, keepdims=True)


def _merge(oa, ob, oc, z, x, mods3, mod_row, norm2_3, weights, l, tm):
    B, T, D = x.shape
    wa, wb, wc, wo, wr_t = weights

    def tok(width, col=0):
        return pl.BlockSpec((1, tm, width), lambda b, i: (b, i, col // width))

    def mod_spec(k):
        return pl.BlockSpec((1, 1, D), lambda b, i: (mod_row(b) * 6 + k, 0, 0))

    in_specs = [tok(256), tok(256), tok(512), tok(D, Z_MA), tok(D, Z_MB), tok(D, Z_MC), tok(D),
                mod_spec(2), mod_spec(3), mod_spec(4), _layer(norm2_3, l)] + [_layer(w, l) for w in weights]
    return pl.pallas_call(
        _merge_kernel,
        grid=(B, T // tm), in_specs=in_specs,
        out_specs=[tok(D), tok(D), pl.BlockSpec((1, N_EXPERTS, tm), lambda b, i: (b, 0, i))],
        out_shape=[jax.ShapeDtypeStruct((B, T, D), F32), jax.ShapeDtypeStruct((B, T, D), BF),
                   jax.ShapeDtypeStruct((B, N_EXPERTS, T), F32)],
        scratch_shapes=[pltpu.VMEM((tm, D), BF)],
        compiler_params=_cparams("arbitrary", "arbitrary"),
        name="merge",
    )(oa, ob, oc, z, z, z, x, mods3, mods3, mods3, norm2_3, wa, wb, wc, wo, wr_t)


def _route_kernel(aff_ref, tri_ref, pos_ref, post_ref, *, cap, slot_stride):
    aff = aff_ref[0]
    E, T = aff.shape

    def refine(thr, shift, patterns):
        best = thr
        for c in patterns:
            cand = thr | (jnp.int32(c) << shift)
            cnt = jnp.sum((aff >= pltpu.bitcast(cand, F32)).astype(F32), axis=1, keepdims=True)
            best = jnp.where(cnt >= cap, cand, best)
        return best

    thr = lax.fori_loop(0, 15, lambda i, t: refine(t, 29 - 2 * i, (1, 2, 3)), jnp.zeros((E, 1), jnp.int32))
    thr = refine(thr, 0, (1,))
    thr_f = pltpu.bitcast(thr, F32)
    gt = aff > thr_f
    eq = aff == thr_f
    need = cap - jnp.sum(gt.astype(F32), axis=1, keepdims=True)
    tri = tri_ref[...]

    def excl_prefix(flags):
        run = jnp.zeros((E, 1), F32)
        blocks = []
        for j in range(T // LANES):
            blk = flags[:, j * LANES:(j + 1) * LANES].astype(F32)
            blocks.append(_dot(blk.astype(BF), tri) + run)
            run = run + jnp.sum(blk, axis=1, keepdims=True)
        return jnp.concatenate(blocks, axis=1)

    sel = gt | (eq & (excl_prefix(eq) < need))
    base = (slot_stride * pl.program_id(0)).astype(F32)
    pos = jnp.where(sel, excl_prefix(sel) + base, -1.0)
    pos_ref[0] = pos.astype(jnp.int32)
    post_ref[0] = jnp.concatenate([pos, jnp.full((LANES - E, T), -1.0, F32)], axis=0).T


def _route(aff_t, cap, slot_stride):
    B, E, T = aff_t.shape
    r = np.arange(LANES)
    tri = jnp.asarray(r[:, None] < r[None, :], BF)
    return pl.pallas_call(
        functools.partial(_route_kernel, cap=cap, slot_stride=slot_stride),
        grid=(B,),
        in_specs=[pl.BlockSpec((1, E, T), lambda b: (b, 0, 0)), pl.BlockSpec((LANES, LANES), lambda b: (0, 0))],
        out_specs=[pl.BlockSpec((1, E, T), lambda b: (b, 0, 0)), pl.BlockSpec((1, T, LANES), lambda b: (b, 0, 0))],
        out_shape=[jax.ShapeDtypeStruct((B, E, T), jnp.int32), jax.ShapeDtypeStruct((B, T, LANES), F32)],
        compiler_params=_cparams("arbitrary"),
        name="route",
    )(aff_t, tri)


def _moe_kernel(*refs, n_slots, final):
    if final:
        h_ref, pos_ref, post_ref, aff_ref, wg_ref, wu_ref, wd_ref, x_ref, gate_ref, g_ref, o_ref = refs
    else:
        h_ref, pos_ref, post_ref, aff_ref, wg_ref, wu_ref, wd_ref, o_ref = refs
    e = pl.program_id(1)

    @pl.when(e == 0)
    def _():
        o_ref[...] = jnp.zeros_like(o_ref)

    pos = pos_ref[0, pl.ds(e, 1), :]
    aff = aff_ref[0, pl.ds(e, 1), :]
    T = pos.shape[1]
    hit = lax.broadcasted_iota(jnp.int32, (n_slots, T), 0) == pos
    w_slot = jnp.sum(jnp.where(hit, aff, 0.0), axis=1, keepdims=True)
    xg = _dot(hit.astype(BF), h_ref[0]).astype(BF)
    gt = _dot(xg, wg_ref[...])
    up = _dot(xg, wu_ref[...])
    hid = (gt * jax.nn.sigmoid(gt) * up * w_slot).astype(BF)
    y = _dot(hid, wd_ref[...]).astype(BF)

    lane = lax.broadcasted_iota(jnp.int32, (1, LANES), 1)
    slot_row = lax.broadcasted_iota(jnp.int32, (1, n_slots), 1).astype(F32)
    for t0 in range(0, T, 256):
        pos_col = jnp.sum(jnp.where(lane == e, post_ref[0, t0:t0 + 256, :], 0.0), axis=1, keepdims=True)
        o_ref[0, t0:t0 + 256, :] += _dot((pos_col == slot_row).astype(BF), y)

    if final:
        @pl.when(e == pl.num_programs(1) - 1)
        def _():
            for t0 in range(0, T, 256):
                rows = slice(t0, t0 + 256)
                o_ref[0, rows, :] = _rms(x_ref[0, rows, :] + gate_ref[0] * o_ref[0, rows, :], g_ref[...])


def _moe(h, pos, pos_t, aff_t, weights, l, n_slots, final=None):
    B, T, D = h.shape
    E = pos.shape[1]
    wg, wu, wd = weights

    def wspec(w):
        return pl.BlockSpec((None, None) + w.shape[2:], lambda b, e: (l, e, 0, 0))

    in_specs = [pl.BlockSpec((1, T, D), lambda b, e: (b, 0, 0)),
                pl.BlockSpec((1, E, T), lambda b, e: (b, 0, 0)),
                pl.BlockSpec((1, T, LANES), lambda b, e: (b, 0, 0)),
                pl.BlockSpec((1, E, T), lambda b, e: (b, 0, 0)),
                wspec(wg), wspec(wu), wspec(wd)]
    args = [h, pos, pos_t, aff_t, wg, wu, wd]
    if final is not None:
        x, mods3, g = final
        in_specs += [pl.BlockSpec((1, T, D), lambda b, e: (b, 0, 0), pipeline_mode=pl.Buffered(1)),
                     pl.BlockSpec((1, 1, D), lambda b, e: (b * 6 + 5, 0, 0)),
                     pl.BlockSpec((1, D), lambda b, e: (0, 0))]
        args += [x, mods3, g.reshape(1, D)]
    return pl.pallas_call(
        functools.partial(_moe_kernel, n_slots=n_slots, final=final is not None),
        grid=(B, E),
        in_specs=in_specs,
        out_specs=pl.BlockSpec((1, T, D), lambda b, e: (b, 0, 0)),
        out_shape=jax.ShapeDtypeStruct((B, T, D), F32),
        compiler_params=_cparams("arbitrary", "arbitrary"),
        name="moe",
    )(*args)


def _reorder_w_in(w_in):
    parts, off = [], 0
    for s in IN_SIZES:
        parts.append(w_in[..., off:off + s])
        off += s
    qa, ka, va, dq, dkv, kr, gq, gk, gv, a_f, a_b, og, m_a, m_b, m_c = parts
    pad = jnp.zeros(w_in.shape[:-1] + (LANES - MLA_ROPE_DIM - 2 * GLA_GATE_RANK,), w_in.dtype)
    return jnp.concatenate([m_a, m_b, m_c, gv, og, qa, ka, va, gq, gk, dq, dkv, kr, a_f, a_b, pad], axis=-1).astype(BF)


def kernel(x, c, ctx, c_ctx, ada_w, ada_b, norm1, norm2, w_in, na_rpb, mla_q_norm, mla_w_uq, mla_kv_norm, mla_w_ukv,
           gla_w_af, gla_b_af, gla_w_ab, gla_b_ab, gla_o_norm, w_br_a, w_br_b, w_br_c, w_out,
           moe_router, moe_w_gate, moe_w_up, moe_w_down, final_norm):
    B, T, D = x.shape
    L = ctx.shape[1]
    depth = ada_w.shape[0]
    assert B < MOD_ROWS
    cap_l = EC_CAPACITY * T // N_EXPERTS
    cap_c = EC_CAPACITY * L // N_EXPERTS
    ctx_row = B
    cc = jnp.concatenate([c, c_ctx[None], jnp.zeros((MOD_ROWS - B - 1, D), F32)], axis=0)

    def lat_row(b):
        return b

    def ctx_row_fn(b):
        return ctx_row

    def rows3(a):
        return a.reshape(depth, 1, -1)

    ada_b3, norm1_3, norm2_3 = rows3(ada_b), rows3(norm1), rows3(norm2)
    ada_w_bf = ada_w.astype(BF)
    w_in_r = _reorder_w_in(w_in)
    na_bias = _na_bias(na_rpb, T // GRID_W)
    mla_w = _mla_weights(mla_w_uq, mla_w_ukv)
    qn3, kvn3 = rows3(mla_q_norm), rows3(mla_kv_norm)
    rope_tabs = _rope_tables(T)
    gate_pad = lambda w, off: jnp.pad(w, ((0, 0), (off, LANES - off - GLA_GATE_RANK), (0, 0))).astype(BF)
    gla_params = (gate_pad(gla_w_af, SM_AF), rows3(gla_b_af), gate_pad(gla_w_ab, SM_AB), rows3(gla_b_ab),
                  rows3(gla_o_norm))
    wr_t = jnp.swapaxes(moe_router, 1, 2)
    wr_hi = wr_t.astype(BF)
    wr_lo = (wr_t - wr_hi.astype(F32)).astype(BF)
    merge_w = (w_br_a.astype(BF), w_br_b.astype(BF), w_br_c.astype(BF), w_out.astype(BF),
               jnp.concatenate([wr_hi, wr_lo], axis=1))
    moe_w = (moe_w_gate.astype(BF), moe_w_up.astype(BF), moe_w_down.astype(BF))

    moe_l = moe_c = mods3 = None
    for l in range(depth):
        ctx_out = l < depth - 1
        prev_mods3 = mods3
        mods3 = _ada(cc, ada_w_bf, ada_b3, l).reshape(MOD_ROWS * 6, 1, D)
        z_l, x = _in_proj(x, moe_l, prev_mods3, mods3, lat_row, norm1_3, w_in_r, l)
        z_c, ctx = _in_proj(ctx, moe_c, prev_mods3, mods3, ctx_row_fn, norm1_3, w_in_r, l,
                            cols=None if ctx_out else CTX_KEY_COLS)

        oa_l, oa_c = _na_attention(z_l, z_c, na_bias, l, ctx_out)

        qb_l, kb_l, vb_l = _mla_proj(z_l, qn3, kvn3, mla_w, rope_tabs, l)
        qb_c, kb_c, vb_c = _mla_proj(z_c, qn3, kvn3, mla_w, None, l)
        ob_l = _mla_attn(qb_l, kb_c, vb_c, kb_l, vb_l)

        oc_l, oc_c = _gla(z_l, z_c, gla_params, l, ctx_out)

        x, h_l, aff_l = _merge(oa_l, ob_l, oc_l, z_l, x, mods3, lat_row, norm2_3, merge_w, l, tm=512)
        pos_l, post_l = _route(aff_l, cap_l, 0)
        last = (x, mods3, final_norm) if l == depth - 1 else None
        moe_l = _moe(h_l, pos_l, post_l, aff_l, moe_w, l, cap_l, final=last)

        if ctx_out:
            ob_c = _mla_attn(qb_c, kb_c, vb_c)
            ctx, h_c, aff_c = _merge(oa_c, ob_c, oc_c, z_c, ctx, mods3, ctx_row_fn, norm2_3, merge_w, l, tm=256)
            pos_c, post_c = _route(aff_c, cap_c, cap_c)
            flat = lambda a: jnp.swapaxes(a, 0, 1).reshape(1, N_EXPERTS, B * L)
            moe_c = _moe(h_c.reshape(1, B * L, D), flat(pos_c), post_c.reshape(1, B * L, LANES), flat(aff_c),
                         moe_w, l, B * cap_c).reshape(B, L, D)
    return moe_l
```

```python
import functools

import numpy as np
import jax
import jax.numpy as jnp
from jax import lax
from jax.experimental import pallas as pl
from jax.experimental.pallas import tpu as pltpu

BF = jnp.bfloat16
F32 = jnp.float32

D_MODEL = 1024
GRID_W = 64
EPS = 1e-6
NA_HEADS = 4
NA_HEAD_DIM = 64
NA_WIN_H = 8
NA_WIN_W = 16
MLA_HEADS = 4
MLA_Q_RANK = 256
MLA_KV_RANK = 128
MLA_NOPE_DIM = 64
MLA_ROPE_DIM = 32
MLA_V_DIM = 64
ROPE_THETA = 10000.0
GLA_HEADS = 4
GLA_DK = 64
GLA_DV = 128
GLA_GATE_RANK = 16
GLA_TAU = 16.0
GLA_CHUNK = 64
N_EXPERTS = 16
EXPERT_FF = 1024
EC_CAPACITY = 2
IN_SIZES = (256, 256, 256, 256, 128, 32, 256, 256, 512, 16, 16, 512, 1024, 1024, 1024)

LANES = 128
VMEM_LIMIT = 56 * 1024 * 1024
MOD_ROWS = 16

Z_MA, Z_MB, Z_MC = 0, 1024, 2048
Z_GV, Z_OG = 3072, 3584
Z_QA, Z_KA, Z_VA = 4096, 4352, 4608
Z_GQ, Z_GK = 4864, 5120
Z_MLA = 5376
Z_SM = 5760
Z_COLS = 5888
SM_KR, SM_AF, SM_AB = 0, 32, 48

CTX_KEY_COLS = (Z_GV, Z_GV + 256, Z_KA, Z_VA, Z_GK, Z_MLA, Z_MLA + 256)

SUPER = 256
NEG = -1e30


def _cparams(*sem):
    return pltpu.CompilerParams(dimension_semantics=sem, vmem_limit_bytes=VMEM_LIMIT)


def _layer(a, l):
    return pl.BlockSpec((None,) + a.shape[1:], lambda *_: (l,) + (0,) * (a.ndim - 1), pipeline_mode=pl.Buffered(1))


def _whole(a):
    return pl.BlockSpec(a.shape, lambda *_: (0,) * a.ndim, pipeline_mode=pl.Buffered(1))


def _dot(a, b):
    return jnp.dot(a, b, preferred_element_type=F32)


def _dot_nt(a, b):
    return lax.dot_general(a, b, (((1,), (1,)), ((), ())), preferred_element_type=F32)


def _dot_tn(a, b):
    return lax.dot_general(a, b, (((0,), (0,)), ((), ())), preferred_element_type=F32)


def _rms(x, g):
    return x * lax.rsqrt(jnp.mean(x * x, axis=-1, keepdims=True) + EPS) * g


def _split3(x):
    hi = x.astype(BF)
    r = x - hi.astype(F32)
    mid = r.astype(BF)
    lo = (r - mid.astype(F32)).astype(BF)
    return hi, mid, lo


def _ada_kernel(c_ref, w_ref, b_ref, o_ref):
    c = c_ref[...]
    a = (c * jax.nn.sigmoid(c)).astype(BF)
    o_ref[...] = _dot(a, w_ref[...]) + b_ref[...]


def _ada(cc, ada_w, ada_b3, l):
    n = ada_w.shape[2]
    tn = 1536
    return pl.pallas_call(
        _ada_kernel,
        grid=(n // tn,),
        in_specs=[pl.BlockSpec((MOD_ROWS, D_MODEL), lambda j: (0, 0)),
                  pl.BlockSpec((None, D_MODEL, tn), lambda j: (l, 0, j)),
                  pl.BlockSpec((None, 1, tn), lambda j: (l, 0, j))],
        out_specs=pl.BlockSpec((MOD_ROWS, tn), lambda j: (0, j)),
        out_shape=jax.ShapeDtypeStruct((MOD_ROWS, n), F32),
        compiler_params=_cparams("arbitrary"),
        name="ada_mod",
    )(cc, ada_w, ada_b3)


def _in_proj_kernel(*refs, with_moe, cols):
    if with_moe:
        x_ref, moe_ref, gate_ref, g_ref, shift_ref, scale_ref, w_ref, o_ref, xo_ref, h_ref = refs
        x = x_ref[0] + gate_ref[0] * moe_ref[0]
        xo_ref[0] = x
    else:
        x_ref, g_ref, shift_ref, scale_ref, w_ref, o_ref, h_ref = refs
        x = x_ref[0]
    h = _rms(x, g_ref[...]) * (1.0 + scale_ref[0]) + shift_ref[0]
    h_ref[...] = h.astype(BF)
    for n0 in range(0, Z_COLS, 256):
        if cols is None or n0 in cols:
            o_ref[0, :, n0:n0 + 256] = _dot(h_ref[...], w_ref[:, n0:n0 + 256]).astype(BF)
        else:
            o_ref[0, :, n0:n0 + 256] = jnp.zeros((o_ref.shape[1], 256), BF)


def _in_proj(x, moe, prev_mods3, mods3, mod_row, norm1_3, w, l, tm=512, cols=None):
    B, T, D = x.shape
    tm = min(tm, T)
    with_moe = moe is not None

    def mod_spec(k):
        return pl.BlockSpec((1, 1, D), lambda b, i: (mod_row(b) * 6 + k, 0, 0))

    x_spec = pl.BlockSpec((1, tm, D), lambda b, i: (b, i, 0))
    in_specs = [x_spec]
    args = [x]
    if with_moe:
        in_specs += [x_spec, mod_spec(5)]
        args += [moe, prev_mods3]
    in_specs += [_layer(norm1_3, l), mod_spec(0), mod_spec(1), _layer(w, l)]
    args += [norm1_3, mods3, mods3, w]
    z_spec = pl.BlockSpec((1, tm, Z_COLS), lambda b, i: (b, i, 0))
    z_shape = jax.ShapeDtypeStruct((B, T, Z_COLS), BF)
    if with_moe:
        out_specs, out_shape = [z_spec, x_spec], [z_shape, jax.ShapeDtypeStruct((B, T, D), F32)]
    else:
        out_specs, out_shape = z_spec, z_shape
    res = pl.pallas_call(
        functools.partial(_in_proj_kernel, with_moe=with_moe, cols=cols),
        grid=(B, T // tm),
        in_specs=in_specs, out_specs=out_specs, out_shape=out_shape,
        scratch_shapes=[pltpu.VMEM((tm, D), BF)],
        compiler_params=_cparams("arbitrary", "arbitrary"),
        name="in_proj",
    )(*args)
    return (res[0], res[1]) if with_moe else (res, x)


def _softmax_pv_many(heads):
    maxes = []
    for s_list, _ in heads:
        m = s_list[0].max(axis=-1, keepdims=True)
        for s in s_list[1:]:
            m = jnp.maximum(m, s.max(axis=-1, keepdims=True))
        maxes.append(m)
    probs = [[jnp.exp((s - m).astype(BF)) for s in s_list] for (s_list, _), m in zip(heads, maxes)]
    outs = []
    for (_, v_list), p_list in zip(heads, probs):
        o = _dot(p_list[0], v_list[0])
        for p, v in zip(p_list[1:], v_list[1:]):
            o = o + _dot(p, v)
        outs.append(o / pltpu.roll(o, LANES // 2, axis=1))
    return outs


def _softmax_pv_sum(heads):
    maxes = []
    for s_list, _ in heads:
        m = s_list[0].max(axis=-1, keepdims=True)
        for s in s_list[1:]:
            m = jnp.maximum(m, s.max(axis=-1, keepdims=True))
        maxes.append(m)
    probs = [[jnp.exp(s - m) for s in s_list] for (s_list, _), m in zip(heads, maxes)]
    outs = []
    for (_, v_list), p_list in zip(heads, probs):
        l = p_list[0].sum(axis=-1, keepdims=True)
        o = _dot(p_list[0].astype(BF), v_list[0])
        for p, v in zip(p_list[1:], v_list[1:]):
            l = l + p.sum(axis=-1, keepdims=True)
            o = o + _dot(p.astype(BF), v)
        outs.append(o / l)
    return outs


def _lane_masks():
    lane = lax.broadcasted_iota(jnp.int32, (1, LANES), 1)
    return lane < 64, lane >= 64


NA_ROWS_PER_STEP = 2
NA_BAND = 10
NA_CASE_ROWS = (0, 2, 4, 28, 30)


def _na_bias(rpb, rows):
    depth = rpb.shape[0]
    qc = np.arange(GRID_W)
    kj = np.arange(GRID_W)
    cs = np.clip(qc - NA_WIN_W // 2, 0, GRID_W - NA_WIN_W)
    col_ok = (kj[None, :] >= cs[:, None]) & (kj[None, :] < cs[:, None] + NA_WIN_W)
    co = kj[None, :] - qc[:, None] + (NA_WIN_W - 1)
    onehot = np.zeros((2 * NA_WIN_W - 1, GRID_W, GRID_W), np.float32)
    qi, ki = np.nonzero(col_ok)
    onehot[co[qi, ki], qi, ki] = 1.0
    toep = jnp.einsum("lhrc,cp->lhrp", rpb.astype(F32), jnp.asarray(onehot.reshape(2 * NA_WIN_W - 1, -1)),
                      precision=lax.Precision.HIGHEST)
    toep = (toep.reshape(depth, NA_HEADS, 2 * NA_WIN_H - 1, GRID_W, GRID_W)
            + jnp.asarray(np.where(col_ok, 0.0, NEG), F32))
    neg = jnp.full((depth, NA_HEADS, GRID_W, GRID_W), NEG, F32)
    cases = []
    for r0 in NA_CASE_ROWS:
        bs = int(np.clip(r0 - NA_WIN_H // 2, 0, rows - NA_BAND))
        q_rows = []
        for qr in range(NA_ROWS_PER_STEP):
            r = r0 + qr
            rs = int(np.clip(r - NA_WIN_H // 2, 0, rows - NA_WIN_H))
            blocks = []
            for i in range(NA_BAND):
                krow = bs + i
                blocks.append(toep[:, :, krow - r + NA_WIN_H - 1] if rs <= krow < rs + NA_WIN_H else neg)
            q_rows.append(jnp.concatenate(blocks, axis=-1))
        cases.append(jnp.concatenate(q_rows, axis=-2))
    return jnp.stack(cases, axis=2)


def _na_kernel(*refs, rows, ctx_out):
    if ctx_out:
        ql_ref, kl_ref, vl_ref, qc_ref, kc_ref, vc_ref, bias_ref, ol_ref, oc_ref = refs
    else:
        ql_ref, kl_ref, vl_ref, kc_ref, vc_ref, bias_ref, ol_ref = refs
    scale = NA_HEAD_DIM ** -0.5
    masks = _lane_masks()
    nq = NA_ROWS_PER_STEP * GRID_W
    nk = NA_BAND * GRID_W

    def body(rp, carry):
        r0 = rp * NA_ROWS_PER_STEP
        bs = jnp.clip(r0 - NA_WIN_H // 2, 0, rows - NA_BAND)
        case = jnp.where(r0 < 4, r0 // 2, jnp.where(r0 > rows - 6, (r0 - (rows - 4)) // 2 + 3, 2))
        q0 = pl.multiple_of(r0 * GRID_W, nq)
        k0 = pl.multiple_of(bs * GRID_W, 2 * GRID_W)
        heads = []
        for hp in range(NA_HEADS // 2):
            sl = slice(hp * LANES, (hp + 1) * LANES)
            q = ql_ref[0, pl.ds(q0, nq), sl] * scale
            kb = kl_ref[0, pl.ds(k0, nk), sl]
            vb = vl_ref[0, pl.ds(k0, nk), sl]
            kc = kc_ref[0, :, sl]
            vc = vc_ref[0, :, sl]
            for hh in range(2):
                qm = jnp.where(masks[hh], q, jnp.zeros_like(q))
                s_loc = _dot_nt(qm, kb) + bias_ref[2 * hp + hh, case]
                s_ctx = _dot_nt(qm, kc)
                heads.append(([s_loc, s_ctx], [vb, vc]))
        outs = _softmax_pv_sum(heads)
        for hp in range(NA_HEADS // 2):
            sl = slice(hp * LANES, (hp + 1) * LANES)
            ol_ref[0, pl.ds(q0, nq), sl] = jnp.where(masks[0], outs[2 * hp], outs[2 * hp + 1]).astype(BF)
        return carry

    lax.fori_loop(0, rows // NA_ROWS_PER_STEP, body, 0)

    if ctx_out:
        heads = []
        for hp in range(NA_HEADS // 2):
            sl = slice(hp * LANES, (hp + 1) * LANES)
            q = qc_ref[0, :, sl] * scale
            kc = kc_ref[0, :, sl]
            for hh in range(2):
                qm = jnp.where(masks[hh], q, jnp.zeros_like(q))
                heads.append(([_dot_nt(qm, kc)], [vc_ref[0, :, sl]]))
        outs = _softmax_pv_sum(heads)
        for hp in range(NA_HEADS // 2):
            sl = slice(hp * LANES, (hp + 1) * LANES)
            oc_ref[0, :, sl] = jnp.where(masks[0], outs[2 * hp], outs[2 * hp + 1]).astype(BF)


def _na_attention(z_l, z_c, bias, l, ctx_out):
    B, T, _ = z_l.shape
    L = z_c.shape[1]
    rows = T // GRID_W

    def zl(col):
        return pl.BlockSpec((1, T, 256), lambda b: (b, 0, col // 256))

    def zc(col):
        return pl.BlockSpec((1, L, 256), lambda b: (b, 0, col // 256))

    ol_spec = pl.BlockSpec((1, T, 256), lambda b: (b, 0, 0))
    ol_shape = jax.ShapeDtypeStruct((B, T, 256), BF)
    if ctx_out:
        in_specs = [zl(Z_QA), zl(Z_KA), zl(Z_VA), zc(Z_QA), zc(Z_KA), zc(Z_VA), _layer(bias, l)]
        args = [z_l, z_l, z_l, z_c, z_c, z_c, bias]
        out_specs = [ol_spec, pl.BlockSpec((1, L, 256), lambda b: (b, 0, 0))]
        out_shape = [ol_shape, jax.ShapeDtypeStruct((B, L, 256), BF)]
    else:
        in_specs = [zl(Z_QA), zl(Z_KA), zl(Z_VA), zc(Z_KA), zc(Z_VA), _layer(bias, l)]
        args = [z_l, z_l, z_l, z_c, z_c, bias]
        out_specs, out_shape = ol_spec, ol_shape
    res = pl.pallas_call(
        functools.partial(_na_kernel, rows=rows, ctx_out=ctx_out),
        grid=(B,), in_specs=in_specs, out_specs=out_specs, out_shape=out_shape,
        compiler_params=_cparams("arbitrary"),
        name="na_attn",
    )(*args)
    return (res[0], res[1]) if ctx_out else (res, None)


MLA_QK = MLA_NOPE_DIM + MLA_ROPE_DIM
MLA_W = MLA_HEADS * LANES


def _mla_proj_kernel(*refs, rope):
    if rope:
        (zm_ref, sm_ref, qn_ref, kvn_ref, wq_ref, wqs_ref, wk_ref, wv_ref, e_ref, es_ref, vone_ref, cos_ref, sin_ref,
         q_ref, k_ref, v_ref) = refs
    else:
        zm_ref, sm_ref, qn_ref, kvn_ref, wq_ref, wk_ref, wv_ref, e_ref, vone_ref, q_ref, k_ref, v_ref = refs
    zm = zm_ref[0].astype(F32)
    nq = _rms(zm[:, :MLA_Q_RANK], qn_ref[...]).astype(BF)
    nkv = _rms(zm[:, MLA_Q_RANK:], kvn_ref[...]).astype(BF)
    sm = sm_ref[0]
    scale = MLA_QK ** -0.5
    q = _dot(nq, wq_ref[...])
    k = _dot(nkv, wk_ref[...]) + _dot(sm, e_ref[...])
    if rope:
        cos = cos_ref[...]
        sin = sin_ref[...]
        q = q * cos + _dot(nq, wqs_ref[...]) * sin
        k = k * cos + _dot(sm, es_ref[...]) * sin
    q_ref[0] = (q * scale).astype(BF)
    k_ref[0] = k.astype(BF)
    v_ref[0] = (_dot(nkv, wv_ref[...]) + vone_ref[...]).astype(BF)


def _rope_swap(e):
    return np.where(e % 16 < 8, e + 8, e - 8)


def _mla_weights(w_uq, w_ukv):
    depth = w_uq.shape[0]
    quarter = MLA_ROPE_DIM // 4
    q4 = w_uq.reshape(depth, MLA_Q_RANK, MLA_HEADS, MLA_QK)
    nope, rope = q4[..., :MLA_NOPE_DIM], q4[..., MLA_NOPE_DIM:]
    rope_sw = jnp.flip(rope.reshape(depth, MLA_Q_RANK, MLA_HEADS, 2, 2, quarter), axis=4).reshape(rope.shape)
    zpad = jnp.zeros(rope.shape, w_uq.dtype)
    wq = jnp.concatenate([nope, rope, zpad], axis=-1).reshape(depth, MLA_Q_RANK, MLA_W)
    wqs = jnp.concatenate([jnp.zeros(nope.shape, w_uq.dtype), rope_sw, zpad], axis=-1).reshape(depth, MLA_Q_RANK, MLA_W)
    kv4 = w_ukv.reshape(depth, MLA_KV_RANK, MLA_HEADS, MLA_NOPE_DIM + MLA_V_DIM)
    k_nope, v = kv4[..., :MLA_NOPE_DIM], kv4[..., MLA_NOPE_DIM:]
    wk = jnp.concatenate([k_nope, jnp.zeros(k_nope.shape, w_ukv.dtype)], axis=-1).reshape(depth, MLA_KV_RANK, MLA_W)
    vz = jnp.zeros(v.shape[:2] + (1, MLA_V_DIM), w_ukv.dtype)
    wv = jnp.concatenate([jnp.concatenate([v[:, :, h:h + 1], vz] if h % 2 == 0 else [vz, v[:, :, h:h + 1]], axis=-1)
                          for h in range(MLA_HEADS)], axis=2).reshape(depth, MLA_KV_RANK, MLA_W)
    e = np.arange(MLA_ROPE_DIM)
    em = np.zeros((LANES, MLA_W), np.float32)
    ems = np.zeros((LANES, MLA_W), np.float32)
    for h in range(MLA_HEADS):
        em[SM_KR + e, h * LANES + MLA_NOPE_DIM + e] = 1.0
        ems[SM_KR + _rope_swap(e), h * LANES + MLA_NOPE_DIM + e] = 1.0
    lane = np.arange(MLA_W) % LANES
    v_ones = ((lane >= MLA_V_DIM) == ((np.arange(MLA_W) // LANES) % 2 == 0)).astype(np.float32)[None, :]
    return (wq.astype(BF), wqs.astype(BF), wk.astype(BF), wv.astype(BF), jnp.asarray(em, BF), jnp.asarray(ems, BF),
            jnp.asarray(v_ones, F32))


def _rope_tables(T):
    t = jnp.arange(T)
    quarter = MLA_ROPE_DIM // 4
    inv_freq = ROPE_THETA ** (-jnp.arange(quarter, dtype=F32) / quarter)
    ang_r = (t // GRID_W).astype(F32)[:, None] * inv_freq
    ang_c = (t % GRID_W).astype(F32)[:, None] * inv_freq
    cr, sr, ccol, scol = jnp.cos(ang_r), jnp.sin(ang_r), jnp.cos(ang_c), jnp.sin(ang_c)
    ones = jnp.ones((T, MLA_NOPE_DIM), F32)
    pad1 = jnp.ones((T, LANES - MLA_QK), F32)
    cos_h = jnp.concatenate([ones, cr, cr, ccol, ccol, pad1], axis=1)
    sin_h = jnp.concatenate([0.0 * ones, -sr, sr, -scol, scol, 0.0 * pad1], axis=1)
    return jnp.tile(cos_h, (1, MLA_HEADS)), jnp.tile(sin_h, (1, MLA_HEADS))


def _mla_proj(z, qn3, kvn3, weights, rope_tabs, l, tm=512):
    B, T, _ = z.shape
    wq, wqs, wk, wv, em, ems, v_ones = weights
    rope = rope_tabs is not None
    tm = min(tm, T)
    zm_spec = pl.BlockSpec((1, tm, 384), lambda i, b: (b, i, Z_MLA // 384))
    sm_spec = pl.BlockSpec((1, tm, LANES), lambda i, b: (b, i, Z_SM // LANES))
    if rope:
        tab_spec = pl.BlockSpec((tm, MLA_W), lambda i, b: (i, 0))
        ins = [z, z, qn3, kvn3, wq, wqs, wk, wv, em, ems, v_ones, rope_tabs[0], rope_tabs[1]]
        in_specs = ([zm_spec, sm_spec] + [_layer(a, l) for a in ins[2:8]] + [_whole(em), _whole(ems), _whole(v_ones)]
                    + [tab_spec, tab_spec])
    else:
        ins = [z, z, qn3, kvn3, wq, wk, wv, em, v_ones]
        in_specs = [zm_spec, sm_spec] + [_layer(a, l) for a in ins[2:7]] + [_whole(em), _whole(v_ones)]
    return pl.pallas_call(
        functools.partial(_mla_proj_kernel, rope=rope),
        grid=(T // tm, B), in_specs=in_specs,
        out_specs=[pl.BlockSpec((1, tm, MLA_W), lambda i, b: (b, i, 0)),
                   pl.BlockSpec((1, tm, MLA_W), lambda i, b: (b, i, 0)),
                   pl.BlockSpec((1, tm, MLA_W), lambda i, b: (b, i, 0))],
        out_shape=[jax.ShapeDtypeStruct((B, T, MLA_W), BF)] * 3,
        compiler_params=_cparams("arbitrary", "arbitrary"),
        name="mla_proj",
    )(*ins)


def _mla_attn_kernel(*refs, with_latent):
    if with_latent:
        q_ref, kc_ref, vc_ref, kl_ref, vl_ref, o_ref = refs
    else:
        q_ref, kc_ref, vc_ref, o_ref = refs
    masks = _lane_masks()
    heads = []
    for h in range(MLA_HEADS):
        hsl = slice(h * LANES, (h + 1) * LANES)
        q = q_ref[0, :, hsl]
        s_list = [_dot_nt(q, kc_ref[0, :, hsl])]
        v_list = [vc_ref[0, :, hsl]]
        if with_latent:
            s_list.append(_dot_nt(q, kl_ref[0, :, hsl]))
            v_list.append(vl_ref[0, :, hsl])
        heads.append((s_list, v_list))
    outs = _softmax_pv_many(heads)
    for hp in range(MLA_HEADS // 2):
        vsl = slice(hp * LANES, (hp + 1) * LANES)
        o_ref[0, :, vsl] = jnp.where(masks[0], outs[2 * hp], outs[2 * hp + 1]).astype(BF)


def _mla_attn(q, k_c, v_c, k_l=None, v_l=None, tq=256):
    B, Tq, W = q.shape
    L = k_c.shape[1]
    with_latent = k_l is not None
    in_specs = [pl.BlockSpec((1, tq, W), lambda b, i: (b, i, 0)),
                pl.BlockSpec((1, L, W), lambda b, i: (b, 0, 0)),
                pl.BlockSpec((1, L, W), lambda b, i: (b, 0, 0))]
    args = [q, k_c, v_c]
    if with_latent:
        T = k_l.shape[1]
        in_specs += [pl.BlockSpec((1, T, W), lambda b, i: (b, 0, 0)),
                     pl.BlockSpec((1, T, W), lambda b, i: (b, 0, 0))]
        args += [k_l, v_l]
    return pl.pallas_call(
        functools.partial(_mla_attn_kernel, with_latent=with_latent),
        grid=(B, Tq // tq), in_specs=in_specs,
        out_specs=pl.BlockSpec((1, tq, 256), lambda b, i: (b, i, 0)),
        out_shape=jax.ShapeDtypeStruct((B, Tq, 256), BF),
        compiler_params=_cparams("arbitrary", "arbitrary"),
        name="mla_attn",
    )(*args)


def _gla_superchunks(jobs, st_ref):
    n_chunk = SUPER // GLA_CHUNK
    masks = _lane_masks()
    rowblk = lax.broadcasted_iota(jnp.int32, (SUPER, LANES), 0) // GLA_CHUNK
    own = ((lax.broadcasted_iota(jnp.int32, (2 * GLA_DV, LANES), 0) < GLA_DV)
           == (lax.broadcasted_iota(jnp.int32, (2 * GLA_DV, LANES), 1) < GLA_DK))
    gs = [jax.nn.log_sigmoid(_dot(sm, w_a) + b_a) * (1.0 / GLA_TAU) for (_, _, _, sm, w_a, b_a, _, _, _) in jobs]
    pre = []
    for (q, k, v, sm, w_a, b_a, tri, d, reverse), g in zip(jobs, gs):
        g_hi = g.astype(BF)
        g_lo = (g - g_hi.astype(F32)).astype(BF)
        b = _dot(tri, g_hi) + _dot(tri, g_lo)
        ends = [i * GLA_CHUNK if reverse else (i + 1) * GLA_CHUNK - 1 for i in range(n_chunk)]
        tot = jnp.concatenate([jnp.broadcast_to(b[r:r + 1, :], (GLA_CHUNK, b.shape[1])) for r in ends], axis=0)
        pre.append((q.astype(F32) * (GLA_DK ** -0.5) * jnp.exp(b), k.astype(F32) * jnp.exp(-b),
                    k.astype(F32) * jnp.exp(tot - b), jnp.exp(tot), tri > 0))
    outs = [[] for _ in jobs]
    for hp in range(GLA_HEADS // 2):
        sl = slice(hp * LANES, (hp + 1) * LANES)
        stage = []
        for (q, k, v, sm, w_a, b_a, tri, d, reverse), (qd_all, kd_all, ke_all, dec_all, keep) in zip(jobs, pre):
            qd = qd_all[:, sl]
            kd = kd_all[:, sl].astype(BF)
            ke = ke_all[:, sl].astype(BF)
            zero = jnp.zeros_like(ke)
            ke_bd = jnp.concatenate([jnp.where(rowblk == i, ke, zero) for i in range(n_chunk)], axis=1)
            v_pair = v[:, hp * 2 * GLA_DV:(hp + 1) * 2 * GLA_DV]
            kv_t = _dot_tn(v_pair, ke_bd)
            a = [jnp.where(keep, _dot_nt(jnp.where(masks[hh], qd, 0.0).astype(BF), kd), 0.0).astype(BF)
                 for hh in range(2)]
            stage.append((qd, dec_all[:, sl], v_pair, kv_t, a))
        inters = []
        for (q, k, v, sm, w_a, b_a, tri, d, reverse), (qd, dec, v_pair, kv_t, a) in zip(jobs, stage):
            s = st_ref[d, hp]
            inter = [None] * n_chunk
            for i in (range(n_chunk - 1, -1, -1) if reverse else range(n_chunk)):
                rows = slice(i * GLA_CHUNK, (i + 1) * GLA_CHUNK)
                inter[i] = _dot_nt(qd[rows].astype(BF), s.astype(BF))
                s = s * dec[i * GLA_CHUNK:i * GLA_CHUNK + 1, :] + jnp.where(own, kv_t[:, i * LANES:(i + 1) * LANES], 0.0)
            st_ref[d, hp] = s
            inters.append(jnp.concatenate(inter, axis=0))
        for n, ((qd, dec, v_pair, kv_t, a), o_inter) in enumerate(zip(stage, inters)):
            for hh in range(2):
                hv = slice(hh * GLA_DV, (hh + 1) * GLA_DV)
                outs[n].append(_dot(a[hh], v_pair[:, hv]) + o_inter[:, hv])
    return [jnp.concatenate(o, axis=1) for o in outs]


def _gla_finish(o, og, onorm):
    ys = []
    for h in range(GLA_HEADS):
        ys.append(_rms(o[:, h * GLA_DV:(h + 1) * GLA_DV], onorm))
    y = jnp.concatenate(ys, axis=1)
    ogf = og.astype(F32)
    return (y * (ogf * jax.nn.sigmoid(ogf))).astype(BF)


def _gla_kernel(*refs, n_lat, ctx_out):
    (ql_ref, kl_ref, vl_ref, sml_ref, ogl_ref, qc_ref, kc_ref, vc_ref, smc_ref, ogc_ref,
     waf_ref, baf_ref, wab_ref, bab_ref, onorm_ref, trif_ref, trib_ref) = refs[:17]
    if ctx_out:
        ol_ref, oc_ref, of_ref, ob_ref, st_ref = refs[17:]
    else:
        ol_ref, of_ref, ob_ref, st_ref = refs[17:]
    onorm = onorm_ref[...]

    def both(fwd_in, bwd_in):
        return _gla_superchunks([fwd_in + (waf_ref[...], baf_ref[...], trif_ref[...], 0, False),
                                 bwd_in + (wab_ref[...], bab_ref[...], trib_ref[...], 1, True)], st_ref)

    st_ref[...] = jnp.zeros_like(st_ref)
    ctx_in = (qc_ref[0], kc_ref[0], vc_ref[0], smc_ref[0])
    o_cf, o_cb = both(ctx_in, ctx_in)
    if ctx_out:
        oc_ref[0] = _gla_finish(o_cf + o_cb, ogc_ref[0], onorm)

    def body(j, carry):
        rf = pl.ds(pl.multiple_of(j * SUPER, SUPER), SUPER)
        rb = pl.ds(pl.multiple_of((n_lat - 1 - j) * SUPER, SUPER), SUPER)
        o_f, o_b = both((ql_ref[0, rf, :], kl_ref[0, rf, :], vl_ref[0, rf, :], sml_ref[0, rf, :]),
                        (ql_ref[0, rb, :], kl_ref[0, rb, :], vl_ref[0, rb, :], sml_ref[0, rb, :]))
        of_ref[rf, :] = o_f
        ob_ref[rb, :] = o_b
        return carry

    lax.fori_loop(0, n_lat, body, 0)

    def finish(j, carry):
        r = pl.ds(pl.multiple_of(j * SUPER, SUPER), SUPER)
        ol_ref[0, r, :] = _gla_finish(of_ref[r, :] + ob_ref[r, :], ogl_ref[0, r, :], onorm)
        return carry

    lax.fori_loop(0, n_lat, finish, 0)


def _gla_consts():
    r = np.arange(SUPER)
    same = (r[:, None] // GLA_CHUNK) == (r[None, :] // GLA_CHUNK)
    return jnp.asarray(same & (r[None, :] <= r[:, None]), BF), jnp.asarray(same & (r[None, :] >= r[:, None]), BF)


def _gla(z_l, z_c, params, l, ctx_out):
    B, T, _ = z_l.shape
    L = z_c.shape[1]
    assert L == SUPER and T % SUPER == 0
    kw = GLA_HEADS * GLA_DK
    vw = GLA_HEADS * GLA_DV
    consts = _gla_consts()

    def zspec(n, width, col):
        return pl.BlockSpec((1, n, width), lambda b: (b, 0, col // width))

    in_specs = ([zspec(T, kw, Z_GQ), zspec(T, kw, Z_GK), zspec(T, vw, Z_GV), zspec(T, LANES, Z_SM), zspec(T, vw, Z_OG),
                 zspec(L, kw, Z_GQ), zspec(L, kw, Z_GK), zspec(L, vw, Z_GV), zspec(L, LANES, Z_SM), zspec(L, vw, Z_OG)]
                + [_layer(a, l) for a in params] + [_whole(a) for a in consts])
    ol_spec = pl.BlockSpec((1, T, vw), lambda b: (b, 0, 0))
    ol_shape = jax.ShapeDtypeStruct((B, T, vw), BF)
    if ctx_out:
        out_specs = [ol_spec, pl.BlockSpec((1, L, vw), lambda b: (b, 0, 0))]
        out_shape = [ol_shape, jax.ShapeDtypeStruct((B, L, vw), BF)]
    else:
        out_specs, out_shape = ol_spec, ol_shape
    scratch = [pltpu.VMEM((T, vw), F32), pltpu.VMEM((T, vw), F32),
               pltpu.VMEM((2, GLA_HEADS // 2, 2 * GLA_DV, LANES), F32)]
    res = pl.pallas_call(
        functools.partial(_gla_kernel, n_lat=T // SUPER, ctx_out=ctx_out),
        grid=(B,), in_specs=in_specs, out_specs=out_specs, out_shape=out_shape,
        scratch_shapes=scratch,
        compiler_params=_cparams("arbitrary"),
        name="gla",
    )(*([z_l] * 5 + [z_c] * 5 + list(params) + list(consts)))
    return (res[0], res[1]) if ctx_out else (res, None)


MERGE_SUB = 256


def _merge_kernel(oa_ref, ob_ref, oc_ref, ma_ref, mb_ref, mc_ref, x_ref, gate_ref, shift_ref, scale_ref, g_ref,
                  wa_ref, wb_ref, wc_ref, wo_ref, wr_ref, xo_ref, h_ref, aff_ref, y_ref):
    tm = x_ref.shape[1]
    gate = gate_ref[0]
    subs = [slice(r0, r0 + MERGE_SUB) for r0 in range(0, tm, MERGE_SUB)]

    def branch(o_ref, m_ref, w_ref, rows, n0):
        m = m_ref[0, rows, n0:n0 + 256].astype(F32)
        return jax.nn.sigmoid(m) * _dot(o_ref[0, rows, :], w_ref[:, n0:n0 + 256])

    for n0 in range(0, D_MODEL, 256):
        for rows in subs:
            y = (branch(oa_ref, ma_ref, wa_ref, rows, n0) + branch(ob_ref, mb_ref, wb_ref, rows, n0)
                 + branch(oc_ref, mc_ref, wc_ref, rows, n0))
            y_ref[rows, n0:n0 + 256] = y.astype(BF)
    for n0 in range(0, D_MODEL, 256):
        for rows in subs:
            xo_ref[0, rows, n0:n0 + 256] = (x_ref[0, rows, n0:n0 + 256]
                                            + gate[:, n0:n0 + 256] * _dot(y_ref[rows, :], wo_ref[:, n0:n0 + 256]))
    hs = [_rms(xo_ref[0, rows, :], g_ref[...]) * (1.0 + scale_ref[0]) + shift_ref[0] for rows in subs]
    wr = wr_ref[...]
    for rows, h in zip(subs, hs):
        h_hi = h.astype(BF)
        h_lo = (h - h_hi.astype(F32)).astype(BF)
        h_ref[0, rows, :] = h_hi
        part = _dot_nt(wr, h_hi)
        logits = part[:N_EXPERTS] + part[N_EXPERTS:] + _dot_nt(wr[:N_EXPERTS], h_lo)
        e = jnp.exp(logits - logits.max(axis=0, keepdims=True))
        aff_ref[0, :, rows] = e / e.sum(axis=0, keepdims=True)


def _merge(oa, ob, oc, z, x, mods3, mod_row, norm2_3, weights, l, tm):
    B, T, D = x.shape
    wa, wb, wc, wo, wr_t = weights

    def tok(width, col=0):
        return pl.BlockSpec((1, tm, width), lambda b, i: (b, i, col // width))

    def mod_spec(k):
        return pl.BlockSpec((1, 1, D), lambda b, i: (mod_row(b) * 6 + k, 0, 0))

    in_specs = [tok(256), tok(256), tok(512), tok(D, Z_MA), tok(D, Z_MB), tok(D, Z_MC), tok(D),
                mod_spec(2), mod_spec(3), mod_spec(4), _layer(norm2_3, l)] + [_layer(w, l) for w in weights]
    return pl.pallas_call(
        _merge_kernel,
        grid=(B, T // tm), in_specs=in_specs,
        out_specs=[tok(D), tok(D), pl.BlockSpec((1, N_EXPERTS, tm), lambda b, i: (b, 0, i))],
        out_shape=[jax.ShapeDtypeStruct((B, T, D), F32), jax.ShapeDtypeStruct((B, T, D), BF),
                   jax.ShapeDtypeStruct((B, N_EXPERTS, T), F32)],
        scratch_shapes=[pltpu.VMEM((tm, D), BF)],
        compiler_params=_cparams("arbitrary", "arbitrary"),
        name="merge",
    )(oa, ob, oc, z, z, z, x, mods3, mods3, mods3, norm2_3, wa, wb, wc, wo, wr_t)


def _route_kernel(aff_ref, tri_ref, pos_ref, post_ref, *, cap, slot_stride):
    aff = aff_ref[0]
    E, T = aff.shape

    def refine(thr, shift, patterns):
        best = thr
        for c in patterns:
            cand = thr | (jnp.int32(c) << shift)
            cnt = jnp.sum((aff >= pltpu.bitcast(cand, F32)).astype(F32), axis=1, keepdims=True)
            best = jnp.where(cnt >= cap, cand, best)
        return best

    thr = lax.fori_loop(0, 15, lambda i, t: refine(t, 29 - 2 * i, (1, 2, 3)), jnp.zeros((E, 1), jnp.int32))
    thr = refine(thr, 0, (1,))
    thr_f = pltpu.bitcast(thr, F32)
    gt = aff > thr_f
    eq = aff == thr_f
    need = cap - jnp.sum(gt.astype(F32), axis=1, keepdims=True)
    tri = tri_ref[...]

    def excl_prefix(flags):
        run = jnp.zeros((E, 1), F32)
        blocks = []
        for j in range(T // LANES):
            blk = flags[:, j * LANES:(j + 1) * LANES].astype(F32)
            blocks.append(_dot(blk.astype(BF), tri) + run)
            run = run + jnp.sum(blk, axis=1, keepdims=True)
        return jnp.concatenate(blocks, axis=1)

    sel = gt | (eq & (excl_prefix(eq) < need))
    base = (slot_stride * pl.program_id(0)).astype(F32)
    pos = jnp.where(sel, excl_prefix(sel) + base, -1.0)
    pos_ref[0] = pos.astype(jnp.int32)
    post_ref[0] = jnp.concatenate([pos, jnp.full((LANES - E, T), -1.0, F32)], axis=0).T


def _route(aff_t, cap, slot_stride):
    B, E, T = aff_t.shape
    r = np.arange(LANES)
    tri = jnp.asarray(r[:, None] < r[None, :], BF)
    return pl.pallas_call(
        functools.partial(_route_kernel, cap=cap, slot_stride=slot_stride),
        grid=(B,),
        in_specs=[pl.BlockSpec((1, E, T), lambda b: (b, 0, 0)), pl.BlockSpec((LANES, LANES), lambda b: (0, 0))],
        out_specs=[pl.BlockSpec((1, E, T), lambda b: (b, 0, 0)), pl.BlockSpec((1, T, LANES), lambda b: (b, 0, 0))],
        out_shape=[jax.ShapeDtypeStruct((B, E, T), jnp.int32), jax.ShapeDtypeStruct((B, T, LANES), F32)],
        compiler_params=_cparams("arbitrary"),
        name="route",
    )(aff_t, tri)


def _moe_kernel(*refs, n_slots, final):
    if final:
        h_ref, pos_ref, post_ref, aff_ref, wg_ref, wu_ref, wd_ref, x_ref, gate_ref, g_ref, o_ref = refs
    else:
        h_ref, pos_ref, post_ref, aff_ref, wg_ref, wu_ref, wd_ref, o_ref = refs
    e = pl.program_id(1)

    @pl.when(e == 0)
    def _():
        o_ref[...] = jnp.zeros_like(o_ref)

    pos = pos_ref[0, pl.ds(e, 1), :]
    aff = aff_ref[0, pl.ds(e, 1), :]
    T = pos.shape[1]
    hit = lax.broadcasted_iota(jnp.int32, (n_slots, T), 0) == pos
    w_slot = jnp.sum(jnp.where(hit, aff, 0.0), axis=1, keepdims=True)
    xg = _dot(hit.astype(BF), h_ref[0]).astype(BF)
    gt = _dot(xg, wg_ref[...])
    up = _dot(xg, wu_ref[...])
    hid = (gt * jax.nn.sigmoid(gt) * up * w_slot).astype(BF)
    y = _dot(hid, wd_ref[...]).astype(BF)

    lane = lax.broadcasted_iota(jnp.int32, (1, LANES), 1)
    slot_row = lax.broadcasted_iota(jnp.int32, (1, n_slots), 1).astype(F32)
    for t0 in range(0, T, 256):
        pos_col = jnp.sum(jnp.where(lane == e, post_ref[0, t0:t0 + 256, :], 0.0), axis=1, keepdims=True)
        o_ref[0, t0:t0 + 256, :] += _dot((pos_col == slot_row).astype(BF), y)

    if final:
        @pl.when(e == pl.num_programs(1) - 1)
        def _():
            for t0 in range(0, T, 256):
                rows = slice(t0, t0 + 256)
                o_ref[0, rows, :] = _rms(x_ref[0, rows, :] + gate_ref[0] * o_ref[0, rows, :], g_ref[...])


def _moe(h, pos, pos_t, aff_t, weights, l, n_slots, final=None):
    B, T, D = h.shape
    E = pos.shape[1]
    wg, wu, wd = weights

    def wspec(w):
        return pl.BlockSpec((None, None) + w.shape[2:], lambda b, e: (l, e, 0, 0))

    in_specs = [pl.BlockSpec((1, T, D), lambda b, e: (b, 0, 0)),
                pl.BlockSpec((1, E, T), lambda b, e: (b, 0, 0)),
                pl.BlockSpec((1, T, LANES), lambda b, e: (b, 0, 0)),
                pl.BlockSpec((1, E, T), lambda b, e: (b, 0, 0)),
                wspec(wg), wspec(wu), wspec(wd)]
    args = [h, pos, pos_t, aff_t, wg, wu, wd]
    if final is not None:
        x, mods3, g = final
        in_specs += [pl.BlockSpec((1, T, D), lambda b, e: (b, 0, 0), pipeline_mode=pl.Buffered(1)),
                     pl.BlockSpec((1, 1, D), lambda b, e: (b * 6 + 5, 0, 0)),
                     pl.BlockSpec((1, D), lambda b, e: (0, 0))]
        args += [x, mods3, g.reshape(1, D)]
    return pl.pallas_call(
        functools.partial(_moe_kernel, n_slots=n_slots, final=final is not None),
        grid=(B, E),
        in_specs=in_specs,
        out_specs=pl.BlockSpec((1, T, D), lambda b, e: (b, 0, 0)),
        out_shape=jax.ShapeDtypeStruct((B, T, D), F32),
        compiler_params=_cparams("arbitrary", "arbitrary"),
        name="moe",
    )(*args)


IN_NAMES = ("qa", "ka", "va", "dq", "dkv", "kr", "gq", "gk", "gv", "a_f", "a_b", "og", "m_a", "m_b", "m_c")
Z_ORDER = ("m_a", "m_b", "m_c", "gv", "og", "qa", "ka", "va", "gq", "gk", "dq", "dkv", "kr", "a_f", "a_b")


def _reorder_kernel(w_ref, o_ref):
    src, off = {}, 0
    for name, size in zip(IN_NAMES, IN_SIZES):
        src[name] = (off, size)
        off += size
    dst = 0
    for name in Z_ORDER:
        s0, n = src[name]
        o_ref[0, :, dst:dst + n] = w_ref[0, :, s0:s0 + n].astype(BF)
        dst += n
    o_ref[0, :, dst:] = jnp.zeros((o_ref.shape[1], Z_COLS - dst), BF)


def _reorder_w_in(w_in, tr=256):
    depth, d, d_in = w_in.shape
    return pl.pallas_call(
        _reorder_kernel,
        grid=(depth, d // tr),
        in_specs=[pl.BlockSpec((1, tr, d_in), lambda l, i: (l, i, 0))],
        out_specs=pl.BlockSpec((1, tr, Z_COLS), lambda l, i: (l, i, 0)),
        out_shape=jax.ShapeDtypeStruct((depth, d, Z_COLS), BF),
        compiler_params=_cparams("arbitrary", "arbitrary"),
        name="w_in_layout",
    )(w_in)


def kernel(x, c, ctx, c_ctx, ada_w, ada_b, norm1, norm2, w_in, na_rpb, mla_q_norm, mla_w_uq, mla_kv_norm, mla_w_ukv,
           gla_w_af, gla_b_af, gla_w_ab, gla_b_ab, gla_o_norm, w_br_a, w_br_b, w_br_c, w_out,
           moe_router, moe_w_gate, moe_w_up, moe_w_down, final_norm):
    B, T, D = x.shape
    L = ctx.shape[1]
    depth = ada_w.shape[0]
    assert B < MOD_ROWS
    cap_l = EC_CAPACITY * T // N_EXPERTS
    cap_c = EC_CAPACITY * L // N_EXPERTS
    ctx_row = B
    cc = jnp.concatenate([c, c_ctx[None], jnp.zeros((MOD_ROWS - B - 1, D), F32)], axis=0)

    def lat_row(b):
        return b

    def ctx_row_fn(b):
        return ctx_row

    def rows3(a):
        return a.reshape(depth, 1, -1)

    ada_b3, norm1_3, norm2_3 = rows3(ada_b), rows3(norm1), rows3(norm2)
    ada_w_bf = ada_w.astype(BF)
    w_in_r = _reorder_w_in(w_in)
    na_bias = _na_bias(na_rpb, T // GRID_W)
    mla_w = _mla_weights(mla_w_uq, mla_w_ukv)
    qn3, kvn3 = rows3(mla_q_norm), rows3(mla_kv_norm)
    rope_tabs = _rope_tables(T)
    gate_pad = lambda w, off: jnp.pad(w, ((0, 0), (off, LANES - off - GLA_GATE_RANK), (0, 0))).astype(BF)
    gla_params = (gate_pad(gla_w_af, SM_AF), rows3(gla_b_af), gate_pad(gla_w_ab, SM_AB), rows3(gla_b_ab),
                  rows3(gla_o_norm))
    wr_t = jnp.swapaxes(moe_router, 1, 2)
    wr_hi = wr_t.astype(BF)
    wr_lo = (wr_t - wr_hi.astype(F32)).astype(BF)
    merge_w = (w_br_a.astype(BF), w_br_b.astype(BF), w_br_c.astype(BF), w_out.astype(BF),
               jnp.concatenate([wr_hi, wr_lo], axis=1))
    moe_w = (moe_w_gate.astype(BF), moe_w_up.astype(BF), moe_w_down.astype(BF))

    moe_l = moe_c = mods3 = None
    for l in range(depth):
        ctx_out = l < depth - 1
        prev_mods3 = mods3
        mods3 = _ada(cc, ada_w_bf, ada_b3, l).reshape(MOD_ROWS * 6, 1, D)
        z_l, x = _in_proj(x, moe_l, prev_mods3, mods3, lat_row, norm1_3, w_in_r, l)
        z_c, ctx = _in_proj(ctx, moe_c, prev_mods3, mods3, ctx_row_fn, norm1_3, w_in_r, l,
                            cols=None if ctx_out else CTX_KEY_COLS)

        oa_l, oa_c = _na_attention(z_l, z_c, na_bias, l, ctx_out)

        qb_l, kb_l, vb_l = _mla_proj(z_l, qn3, kvn3, mla_w, rope_tabs, l)
        qb_c, kb_c, vb_c = _mla_proj(z_c, qn3, kvn3, mla_w, None, l)
        ob_l = _mla_attn(qb_l, kb_c, vb_c, kb_l, vb_l)

        oc_l, oc_c = _gla(z_l, z_c, gla_params, l, ctx_out)

        x, h_l, aff_l = _merge(oa_l, ob_l, oc_l, z_l, x, mods3, lat_row, norm2_3, merge_w, l, tm=512)
        pos_l, post_l = _route(aff_l, cap_l, 0)
        last = (x, mods3, final_norm) if l == depth - 1 else None
        moe_l = _moe(h_l, pos_l, post_l, aff_l, moe_w, l, cap_l, final=last)

        if ctx_out:
            ob_c = _mla_attn(qb_c, kb_c, vb_c)
            ctx, h_c, aff_c = _merge(oa_c, ob_c, oc_c, z_c, ctx, mods3, ctx_row_fn, norm2_3, merge_w, l, tm=256)
            pos_c, post_c = _route(aff_c, cap_c, cap_c)
            flat = lambda a: jnp.swapaxes(a, 0, 1).reshape(1, N_EXPERTS, B * L)
            moe_c = _moe(h_c.reshape(1, B * L, D), flat(pos_c), post_c.reshape(1, B * L, LANES), flat(aff_c),
                         moe_w, l, B * cap_c).reshape(B, L, D)
    return moe_l
```

```python
import functools

import numpy as np
import jax
import jax.numpy as jnp
from jax import lax
from jax.experimental import pallas as pl
from jax.experimental.pallas import tpu as pltpu

BF = jnp.bfloat16
F32 = jnp.float32

D_MODEL = 1024
GRID_W = 64
EPS = 1e-6
NA_HEADS = 4
NA_HEAD_DIM = 64
NA_WIN_H = 8
NA_WIN_W = 16
MLA_HEADS = 4
MLA_Q_RANK = 256
MLA_KV_RANK = 128
MLA_NOPE_DIM = 64
MLA_ROPE_DIM = 32
MLA_V_DIM = 64
ROPE_THETA = 10000.0
GLA_HEADS = 4
GLA_DK = 64
GLA_DV = 128
GLA_GATE_RANK = 16
GLA_TAU = 16.0
GLA_CHUNK = 64
N_EXPERTS = 16
EXPERT_FF = 1024
EC_CAPACITY = 2
IN_SIZES = (256, 256, 256, 256, 128, 32, 256, 256, 512, 16, 16, 512, 1024, 1024, 1024)

LANES = 128
VMEM_LIMIT = 56 * 1024 * 1024
MOD_ROWS = 16

Z_MA, Z_MB, Z_MC = 0, 1024, 2048
Z_GV, Z_OG = 3072, 3584
Z_QA, Z_KA, Z_VA = 4096, 4352, 4608
Z_GQ, Z_GK = 4864, 5120
Z_MLA = 5376
Z_SM = 5760
Z_COLS = 5888
SM_KR, SM_AF, SM_AB = 0, 32, 48

CTX_KEY_COLS = (Z_GV, Z_GV + 256, Z_KA, Z_VA, Z_GK, Z_MLA, Z_MLA + 256)

IN_NAMES = ("qa", "ka", "va", "dq", "dkv", "kr", "gq", "gk", "gv", "a_f", "a_b", "og", "m_a", "m_b", "m_c")
Z_ORDER = ("m_a", "m_b", "m_c", "gv", "og", "qa", "ka", "va", "gq", "gk", "dq", "dkv", "kr", "a_f", "a_b")


def _z_chunk_sources(width=256):
    start, off = {}, 0
    for name, size in zip(IN_NAMES, IN_SIZES):
        start[name] = off
        off += size
    pieces = []
    z = 0
    for name in Z_ORDER:
        s0, n = start[name], IN_SIZES[IN_NAMES.index(name)]
        while n:
            take = min(n, width - z % width)
            pieces.append((z, s0, take))
            z, s0, n = z + take, s0 + take, n - take
    chunks = [[] for _ in range(Z_COLS // width)]
    for zc, s0, n in pieces:
        if chunks[zc // width] and sum(chunks[zc // width][-1]) == s0:
            chunks[zc // width][-1] = (chunks[zc // width][-1][0], chunks[zc // width][-1][1] + n)
        else:
            chunks[zc // width].append((s0, n))
    return chunks


SUPER = 256
NEG = -1e30


def _cparams(*sem):
    return pltpu.CompilerParams(dimension_semantics=sem, vmem_limit_bytes=VMEM_LIMIT)


def _layer(a, l):
    return pl.BlockSpec((None,) + a.shape[1:], lambda *_: (l,) + (0,) * (a.ndim - 1), pipeline_mode=pl.Buffered(1))


def _whole(a):
    return pl.BlockSpec(a.shape, lambda *_: (0,) * a.ndim, pipeline_mode=pl.Buffered(1))


def _dot(a, b):
    return jnp.dot(a, b, preferred_element_type=F32)


def _dot_nt(a, b):
    return lax.dot_general(a, b, (((1,), (1,)), ((), ())), preferred_element_type=F32)


def _dot_tn(a, b):
    return lax.dot_general(a, b, (((0,), (0,)), ((), ())), preferred_element_type=F32)


def _rms(x, g):
    return x * lax.rsqrt(jnp.mean(x * x, axis=-1, keepdims=True) + EPS) * g


def _split3(x):
    hi = x.astype(BF)
    r = x - hi.astype(F32)
    mid = r.astype(BF)
    lo = (r - mid.astype(F32)).astype(BF)
    return hi, mid, lo


def _ada_kernel(c_ref, w_ref, b_ref, o_ref):
    c = c_ref[...]
    a = (c * jax.nn.sigmoid(c)).astype(BF)
    o_ref[...] = _dot(a, w_ref[...]) + b_ref[...]


def _ada(cc, ada_w, ada_b3, l):
    n = ada_w.shape[2]
    tn = 1536
    return pl.pallas_call(
        _ada_kernel,
        grid=(n // tn,),
        in_specs=[pl.BlockSpec((MOD_ROWS, D_MODEL), lambda j: (0, 0)),
                  pl.BlockSpec((None, D_MODEL, tn), lambda j: (l, 0, j)),
                  pl.BlockSpec((None, 1, tn), lambda j: (l, 0, j))],
        out_specs=pl.BlockSpec((MOD_ROWS, tn), lambda j: (0, j)),
        out_shape=jax.ShapeDtypeStruct((MOD_ROWS, n), F32),
        compiler_params=_cparams("arbitrary"),
        name="ada_mod",
    )(cc, ada_w, ada_b3)


def _in_proj_kernel(*refs, with_moe, cols):
    if with_moe:
        x_ref, moe_ref, gate_ref, g_ref, shift_ref, scale_ref, w_ref, o_ref, xo_ref, h_ref = refs
        x = x_ref[0] + gate_ref[0] * moe_ref[0]
        xo_ref[0] = x
    else:
        x_ref, g_ref, shift_ref, scale_ref, w_ref, o_ref, h_ref = refs
        x = x_ref[0]
    h = _rms(x, g_ref[...]) * (1.0 + scale_ref[0]) + shift_ref[0]
    h_ref[...] = h.astype(BF)
    for n0, srcs in zip(range(0, Z_COLS, 256), _z_chunk_sources()):
        if cols is None or n0 in cols:
            rows = [w_ref[s0:s0 + n, :] for s0, n in srcs]
            pad = 256 - sum(n for _, n in srcs)
            if pad:
                rows.append(jnp.zeros((pad, w_ref.shape[1]), BF))
            wt = rows[0] if len(rows) == 1 else jnp.concatenate(rows, axis=0)
            o_ref[0, :, n0:n0 + 256] = _dot_nt(h_ref[...], wt).astype(BF)
        else:
            o_ref[0, :, n0:n0 + 256] = jnp.zeros((o_ref.shape[1], 256), BF)


def _in_proj(x, moe, prev_mods3, mods3, mod_row, norm1_3, w, l, tm=512, cols=None):
    B, T, D = x.shape
    tm = min(tm, T)
    with_moe = moe is not None

    def mod_spec(k):
        return pl.BlockSpec((1, 1, D), lambda b, i: (mod_row(b) * 6 + k, 0, 0))

    x_spec = pl.BlockSpec((1, tm, D), lambda b, i: (b, i, 0))
    in_specs = [x_spec]
    args = [x]
    if with_moe:
        in_specs += [x_spec, mod_spec(5)]
        args += [moe, prev_mods3]
    in_specs += [_layer(norm1_3, l), mod_spec(0), mod_spec(1), _layer(w, l)]
    args += [norm1_3, mods3, mods3, w]
    z_spec = pl.BlockSpec((1, tm, Z_COLS), lambda b, i: (b, i, 0))
    z_shape = jax.ShapeDtypeStruct((B, T, Z_COLS), BF)
    if with_moe:
        out_specs, out_shape = [z_spec, x_spec], [z_shape, jax.ShapeDtypeStruct((B, T, D), F32)]
    else:
        out_specs, out_shape = z_spec, z_shape
    res = pl.pallas_call(
        functools.partial(_in_proj_kernel, with_moe=with_moe, cols=cols),
        grid=(B, T // tm),
        in_specs=in_specs, out_specs=out_specs, out_shape=out_shape,
        scratch_shapes=[pltpu.VMEM((tm, D), BF)],
        compiler_params=_cparams("arbitrary", "arbitrary"),
        name="in_proj",
    )(*args)
    return (res[0], res[1]) if with_moe else (res, x)


def _softmax_pv_many(heads):
    maxes = []
    for s_list, _ in heads:
        m = s_list[0].max(axis=-1, keepdims=True)
        for s in s_list[1:]:
            m = jnp.maximum(m, s.max(axis=-1, keepdims=True))
        maxes.append(m)
    probs = [[jnp.exp((s - m).astype(BF)) for s in s_list] for (s_list, _), m in zip(heads, maxes)]
    outs = []
    for (_, v_list), p_list in zip(heads, probs):
        o = _dot(p_list[0], v_list[0])
        for p, v in zip(p_list[1:], v_list[1:]):
            o = o + _dot(p, v)
        outs.append(o / pltpu.roll(o, LANES // 2, axis=1))
    return outs


def _softmax_pv_sum(heads):
    maxes = []
    for s_list, _ in heads:
        m = s_list[0].max(axis=-1, keepdims=True)
        for s in s_list[1:]:
            m = jnp.maximum(m, s.max(axis=-1, keepdims=True))
        maxes.append(m)
    probs = [[jnp.exp(s - m) for s in s_list] for (s_list, _), m in zip(heads, maxes)]
    outs = []
    for (_, v_list), p_list in zip(heads, probs):
        l = p_list[0].sum(axis=-1, keepdims=True)
        o = _dot(p_list[0].astype(BF), v_list[0])
        for p, v in zip(p_list[1:], v_list[1:]):
            l = l + p.sum(axis=-1, keepdims=True)
            o = o + _dot(p.astype(BF), v)
        outs.append(o / l)
    return outs


def _lane_masks():
    lane = lax.broadcasted_iota(jnp.int32, (1, LANES), 1)
    return lane < 64, lane >= 64


NA_ROWS_PER_STEP = 2
NA_BAND = 10
NA_CASE_ROWS = (0, 2, 4, 28, 30)


def _na_bias(rpb, rows):
    depth = rpb.shape[0]
    qc = np.arange(GRID_W)
    kj = np.arange(GRID_W)
    cs = np.clip(qc - NA_WIN_W // 2, 0, GRID_W - NA_WIN_W)
    col_ok = (kj[None, :] >= cs[:, None]) & (kj[None, :] < cs[:, None] + NA_WIN_W)
    co = kj[None, :] - qc[:, None] + (NA_WIN_W - 1)
    onehot = np.zeros((2 * NA_WIN_W - 1, GRID_W, GRID_W), np.float32)
    qi, ki = np.nonzero(col_ok)
    onehot[co[qi, ki], qi, ki] = 1.0
    toep = jnp.einsum("lhrc,cp->lhrp", rpb.astype(F32), jnp.asarray(onehot.reshape(2 * NA_WIN_W - 1, -1)),
                      precision=lax.Precision.HIGHEST)
    toep = (toep.reshape(depth, NA_HEADS, 2 * NA_WIN_H - 1, GRID_W, GRID_W)
            + jnp.asarray(np.where(col_ok, 0.0, NEG), F32))
    neg = jnp.full((depth, NA_HEADS, GRID_W, GRID_W), NEG, F32)
    cases = []
    for r0 in NA_CASE_ROWS:
        bs = int(np.clip(r0 - NA_WIN_H // 2, 0, rows - NA_BAND))
        q_rows = []
        for qr in range(NA_ROWS_PER_STEP):
            r = r0 + qr
            rs = int(np.clip(r - NA_WIN_H // 2, 0, rows - NA_WIN_H))
            blocks = []
            for i in range(NA_BAND):
                krow = bs + i
                blocks.append(toep[:, :, krow - r + NA_WIN_H - 1] if rs <= krow < rs + NA_WIN_H else neg)
            q_rows.append(jnp.concatenate(blocks, axis=-1))
        cases.append(jnp.concatenate(q_rows, axis=-2))
    return jnp.stack(cases, axis=2)


def _na_kernel(*refs, rows, ctx_out):
    if ctx_out:
        ql_ref, kl_ref, vl_ref, qc_ref, kc_ref, vc_ref, bias_ref, ol_ref, oc_ref = refs
    else:
        ql_ref, kl_ref, vl_ref, kc_ref, vc_ref, bias_ref, ol_ref = refs
    scale = NA_HEAD_DIM ** -0.5
    masks = _lane_masks()
    nq = NA_ROWS_PER_STEP * GRID_W
    nk = NA_BAND * GRID_W

    def body(rp, carry):
        r0 = rp * NA_ROWS_PER_STEP
        bs = jnp.clip(r0 - NA_WIN_H // 2, 0, rows - NA_BAND)
        case = jnp.where(r0 < 4, r0 // 2, jnp.where(r0 > rows - 6, (r0 - (rows - 4)) // 2 + 3, 2))
        q0 = pl.multiple_of(r0 * GRID_W, nq)
        k0 = pl.multiple_of(bs * GRID_W, 2 * GRID_W)
        heads = []
        for hp in range(NA_HEADS // 2):
            sl = slice(hp * LANES, (hp + 1) * LANES)
            q = ql_ref[0, pl.ds(q0, nq), sl] * scale
            kb = kl_ref[0, pl.ds(k0, nk), sl]
            vb = vl_ref[0, pl.ds(k0, nk), sl]
            kc = kc_ref[0, :, sl]
            vc = vc_ref[0, :, sl]
            for hh in range(2):
                qm = jnp.where(masks[hh], q, jnp.zeros_like(q))
                s_loc = _dot_nt(qm, kb) + bias_ref[2 * hp + hh, case]
                s_ctx = _dot_nt(qm, kc)
                heads.append(([s_loc, s_ctx], [vb, vc]))
        outs = _softmax_pv_sum(heads)
        for hp in range(NA_HEADS // 2):
            sl = slice(hp * LANES, (hp + 1) * LANES)
            ol_ref[0, pl.ds(q0, nq), sl] = jnp.where(masks[0], outs[2 * hp], outs[2 * hp + 1]).astype(BF)
        return carry

    lax.fori_loop(0, rows // NA_ROWS_PER_STEP, body, 0)

    if ctx_out:
        heads = []
        for hp in range(NA_HEADS // 2):
            sl = slice(hp * LANES, (hp + 1) * LANES)
            q = qc_ref[0, :, sl] * scale
            kc = kc_ref[0, :, sl]
            for hh in range(2):
                qm = jnp.where(masks[hh], q, jnp.zeros_like(q))
                heads.append(([_dot_nt(qm, kc)], [vc_ref[0, :, sl]]))
        outs = _softmax_pv_sum(heads)
        for hp in range(NA_HEADS // 2):
            sl = slice(hp * LANES, (hp + 1) * LANES)
            oc_ref[0, :, sl] = jnp.where(masks[0], outs[2 * hp], outs[2 * hp + 1]).astype(BF)


def _na_attention(z_l, z_c, bias, l, ctx_out):
    B, T, _ = z_l.shape
    L = z_c.shape[1]
    rows = T // GRID_W

    def zl(col):
        return pl.BlockSpec((1, T, 256), lambda b: (b, 0, col // 256))

    def zc(col):
        return pl.BlockSpec((1, L, 256), lambda b: (b, 0, col // 256))

    ol_spec = pl.BlockSpec((1, T, 256), lambda b: (b, 0, 0))
    ol_shape = jax.ShapeDtypeStruct((B, T, 256), BF)
    if ctx_out:
        in_specs = [zl(Z_QA), zl(Z_KA), zl(Z_VA), zc(Z_QA), zc(Z_KA), zc(Z_VA), _layer(bias, l)]
        args = [z_l, z_l, z_l, z_c, z_c, z_c, bias]
        out_specs = [ol_spec, pl.BlockSpec((1, L, 256), lambda b: (b, 0, 0))]
        out_shape = [ol_shape, jax.ShapeDtypeStruct((B, L, 256), BF)]
    else:
        in_specs = [zl(Z_QA), zl(Z_KA), zl(Z_VA), zc(Z_KA), zc(Z_VA), _layer(bias, l)]
        args = [z_l, z_l, z_l, z_c, z_c, bias]
        out_specs, out_shape = ol_spec, ol_shape
    res = pl.pallas_call(
        functools.partial(_na_kernel, rows=rows, ctx_out=ctx_out),
        grid=(B,), in_specs=in_specs, out_specs=out_specs, out_shape=out_shape,
        compiler_params=_cparams("arbitrary"),
        name="na_attn",
    )(*args)
    return (res[0], res[1]) if ctx_out else (res, None)


MLA_QK = MLA_NOPE_DIM + MLA_ROPE_DIM
MLA_W = MLA_HEADS * LANES


def _mla_proj_kernel(*refs, rope):
    if rope:
        (zm_ref, sm_ref, qn_ref, kvn_ref, wq_ref, wqs_ref, wk_ref, wv_ref, e_ref, es_ref, vone_ref, cos_ref, sin_ref,
         q_ref, k_ref, v_ref) = refs
    else:
        zm_ref, sm_ref, qn_ref, kvn_ref, wq_ref, wk_ref, wv_ref, e_ref, vone_ref, q_ref, k_ref, v_ref = refs
    zm = zm_ref[0].astype(F32)
    nq = _rms(zm[:, :MLA_Q_RANK], qn_ref[...]).astype(BF)
    nkv = _rms(zm[:, MLA_Q_RANK:], kvn_ref[...]).astype(BF)
    sm = sm_ref[0]
    scale = MLA_QK ** -0.5
    q = _dot(nq, wq_ref[...])
    k = _dot(nkv, wk_ref[...]) + _dot(sm, e_ref[...])
    if rope:
        cos = cos_ref[...]
        sin = sin_ref[...]
        q = q * cos + _dot(nq, wqs_ref[...]) * sin
        k = k * cos + _dot(sm, es_ref[...]) * sin
    q_ref[0] = (q * scale).astype(BF)
    k_ref[0] = k.astype(BF)
    v_ref[0] = (_dot(nkv, wv_ref[...]) + vone_ref[...]).astype(BF)


def _rope_swap(e):
    return np.where(e % 16 < 8, e + 8, e - 8)


def _mla_weights(w_uq, w_ukv):
    depth = w_uq.shape[0]
    quarter = MLA_ROPE_DIM // 4
    q4 = w_uq.reshape(depth, MLA_Q_RANK, MLA_HEADS, MLA_QK)
    nope, rope = q4[..., :MLA_NOPE_DIM], q4[..., MLA_NOPE_DIM:]
    rope_sw = jnp.flip(rope.reshape(depth, MLA_Q_RANK, MLA_HEADS, 2, 2, quarter), axis=4).reshape(rope.shape)
    zpad = jnp.zeros(rope.shape, w_uq.dtype)
    wq = jnp.concatenate([nope, rope, zpad], axis=-1).reshape(depth, MLA_Q_RANK, MLA_W)
    wqs = jnp.concatenate([jnp.zeros(nope.shape, w_uq.dtype), rope_sw, zpad], axis=-1).reshape(depth, MLA_Q_RANK, MLA_W)
    kv4 = w_ukv.reshape(depth, MLA_KV_RANK, MLA_HEADS, MLA_NOPE_DIM + MLA_V_DIM)
    k_nope, v = kv4[..., :MLA_NOPE_DIM], kv4[..., MLA_NOPE_DIM:]
    wk = jnp.concatenate([k_nope, jnp.zeros(k_nope.shape, w_ukv.dtype)], axis=-1).reshape(depth, MLA_KV_RANK, MLA_W)
    vz = jnp.zeros(v.shape[:2] + (1, MLA_V_DIM), w_ukv.dtype)
    wv = jnp.concatenate([jnp.concatenate([v[:, :, h:h + 1], vz] if h % 2 == 0 else [vz, v[:, :, h:h + 1]], axis=-1)
                          for h in range(MLA_HEADS)], axis=2).reshape(depth, MLA_KV_RANK, MLA_W)
    e = np.arange(MLA_ROPE_DIM)
    em = np.zeros((LANES, MLA_W), np.float32)
    ems = np.zeros((LANES, MLA_W), np.float32)
    for h in range(MLA_HEADS):
        em[SM_KR + e, h * LANES + MLA_NOPE_DIM + e] = 1.0
        ems[SM_KR + _rope_swap(e), h * LANES + MLA_NOPE_DIM + e] = 1.0
    lane = np.arange(MLA_W) % LANES
    v_ones = ((lane >= MLA_V_DIM) == ((np.arange(MLA_W) // LANES) % 2 == 0)).astype(np.float32)[None, :]
    return (wq.astype(BF), wqs.astype(BF), wk.astype(BF), wv.astype(BF), jnp.asarray(em, BF), jnp.asarray(ems, BF),
            jnp.asarray(v_ones, F32))


def _rope_tables(T):
    t = jnp.arange(T)
    quarter = MLA_ROPE_DIM // 4
    inv_freq = ROPE_THETA ** (-jnp.arange(quarter, dtype=F32) / quarter)
    ang_r = (t // GRID_W).astype(F32)[:, None] * inv_freq
    ang_c = (t % GRID_W).astype(F32)[:, None] * inv_freq
    cr, sr, ccol, scol = jnp.cos(ang_r), jnp.sin(ang_r), jnp.cos(ang_c), jnp.sin(ang_c)
    ones = jnp.ones((T, MLA_NOPE_DIM), F32)
    pad1 = jnp.ones((T, LANES - MLA_QK), F32)
    cos_h = jnp.concatenate([ones, cr, cr, ccol, ccol, pad1], axis=1)
    sin_h = jnp.concatenate([0.0 * ones, -sr, sr, -scol, scol, 0.0 * pad1], axis=1)
    return jnp.tile(cos_h, (1, MLA_HEADS)), jnp.tile(sin_h, (1, MLA_HEADS))


def _mla_proj(z, qn3, kvn3, weights, rope_tabs, l, tm=512):
    B, T, _ = z.shape
    wq, wqs, wk, wv, em, ems, v_ones = weights
    rope = rope_tabs is not None
    tm = min(tm, T)
    zm_spec = pl.BlockSpec((1, tm, 384), lambda i, b: (b, i, Z_MLA // 384))
    sm_spec = pl.BlockSpec((1, tm, LANES), lambda i, b: (b, i, Z_SM // LANES))
    if rope:
        tab_spec = pl.BlockSpec((tm, MLA_W), lambda i, b: (i, 0))
        ins = [z, z, qn3, kvn3, wq, wqs, wk, wv, em, ems, v_ones, rope_tabs[0], rope_tabs[1]]
        in_specs = ([zm_spec, sm_spec] + [_layer(a, l) for a in ins[2:8]] + [_whole(em), _whole(ems), _whole(v_ones)]
                    + [tab_spec, tab_spec])
    else:
        ins = [z, z, qn3, kvn3, wq, wk, wv, em, v_ones]
        in_specs = [zm_spec, sm_spec] + [_layer(a, l) for a in ins[2:7]] + [_whole(em), _whole(v_ones)]
    return pl.pallas_call(
        functools.partial(_mla_proj_kernel, rope=rope),
        grid=(T // tm, B), in_specs=in_specs,
        out_specs=[pl.BlockSpec((1, tm, MLA_W), lambda i, b: (b, i, 0)),
                   pl.BlockSpec((1, tm, MLA_W), lambda i, b: (b, i, 0)),
                   pl.BlockSpec((1, tm, MLA_W), lambda i, b: (b, i, 0))],
        out_shape=[jax.ShapeDtypeStruct((B, T, MLA_W), BF)] * 3,
        compiler_params=_cparams("arbitrary", "arbitrary"),
        name="mla_proj",
    )(*ins)


def _mla_attn_kernel(*refs, with_latent):
    if with_latent:
        q_ref, kc_ref, vc_ref, kl_ref, vl_ref, o_ref = refs
    else:
        q_ref, kc_ref, vc_ref, o_ref = refs
    masks = _lane_masks()
    heads = []
    for h in range(MLA_HEADS):
        hsl = slice(h * LANES, (h + 1) * LANES)
        q = q_ref[0, :, hsl]
        s_list = [_dot_nt(q, kc_ref[0, :, hsl])]
        v_list = [vc_ref[0, :, hsl]]
        if with_latent:
            s_list.append(_dot_nt(q, kl_ref[0, :, hsl]))
            v_list.append(vl_ref[0, :, hsl])
        heads.append((s_list, v_list))
    outs = _softmax_pv_many(heads)
    for hp in range(MLA_HEADS // 2):
        vsl = slice(hp * LANES, (hp + 1) * LANES)
        o_ref[0, :, vsl] = jnp.where(masks[0], outs[2 * hp], outs[2 * hp + 1]).astype(BF)


def _mla_attn(q, k_c, v_c, k_l=None, v_l=None, tq=256):
    B, Tq, W = q.shape
    L = k_c.shape[1]
    with_latent = k_l is not None
    in_specs = [pl.BlockSpec((1, tq, W), lambda b, i: (b, i, 0)),
                pl.BlockSpec((1, L, W), lambda b, i: (b, 0, 0)),
                pl.BlockSpec((1, L, W), lambda b, i: (b, 0, 0))]
    args = [q, k_c, v_c]
    if with_latent:
        T = k_l.shape[1]
        in_specs += [pl.BlockSpec((1, T, W), lambda b, i: (b, 0, 0)),
                     pl.BlockSpec((1, T, W), lambda b, i: (b, 0, 0))]
        args += [k_l, v_l]
    return pl.pallas_call(
        functools.partial(_mla_attn_kernel, with_latent=with_latent),
        grid=(B, Tq // tq), in_specs=in_specs,
        out_specs=pl.BlockSpec((1, tq, 256), lambda b, i: (b, i, 0)),
        out_shape=jax.ShapeDtypeStruct((B, Tq, 256), BF),
        compiler_params=_cparams("arbitrary", "arbitrary"),
        name="mla_attn",
    )(*args)


def _gla_superchunks(jobs, st_ref):
    n_chunk = SUPER // GLA_CHUNK
    masks = _lane_masks()
    rowblk = lax.broadcasted_iota(jnp.int32, (SUPER, LANES), 0) // GLA_CHUNK
    own = ((lax.broadcasted_iota(jnp.int32, (2 * GLA_DV, LANES), 0) < GLA_DV)
           == (lax.broadcasted_iota(jnp.int32, (2 * GLA_DV, LANES), 1) < GLA_DK))
    gs = [jax.nn.log_sigmoid(_dot(sm, w_a) + b_a) * (1.0 / GLA_TAU) for (_, _, _, sm, w_a, b_a, _, _, _) in jobs]
    pre = []
    for (q, k, v, sm, w_a, b_a, tri, d, reverse), g in zip(jobs, gs):
        g_hi = g.astype(BF)
        g_lo = (g - g_hi.astype(F32)).astype(BF)
        b = _dot(tri, g_hi) + _dot(tri, g_lo)
        ends = [i * GLA_CHUNK if reverse else (i + 1) * GLA_CHUNK - 1 for i in range(n_chunk)]
        tot = jnp.concatenate([jnp.broadcast_to(b[r:r + 1, :], (GLA_CHUNK, b.shape[1])) for r in ends], axis=0)
        pre.append((q.astype(F32) * (GLA_DK ** -0.5) * jnp.exp(b), k.astype(F32) * jnp.exp(-b),
                    k.astype(F32) * jnp.exp(tot - b), jnp.exp(tot), tri > 0))
    outs = [[] for _ in jobs]
    for hp in range(GLA_HEADS // 2):
        sl = slice(hp * LANES, (hp + 1) * LANES)
        stage = []
        for (q, k, v, sm, w_a, b_a, tri, d, reverse), (qd_all, kd_all, ke_all, dec_all, keep) in zip(jobs, pre):
            qd = qd_all[:, sl]
            kd = kd_all[:, sl].astype(BF)
            ke = ke_all[:, sl].astype(BF)
            zero = jnp.zeros_like(ke)
            ke_bd = jnp.concatenate([jnp.where(rowblk == i, ke, zero) for i in range(n_chunk)], axis=1)
            v_pair = v[:, hp * 2 * GLA_DV:(hp + 1) * 2 * GLA_DV]
            kv_t = _dot_tn(v_pair, ke_bd)
            a = [jnp.where(keep, _dot_nt(jnp.where(masks[hh], qd, 0.0).astype(BF), kd), 0.0).astype(BF)
                 for hh in range(2)]
            stage.append((qd, dec_all[:, sl], v_pair, kv_t, a))
        inters = []
        for (q, k, v, sm, w_a, b_a, tri, d, reverse), (qd, dec, v_pair, kv_t, a) in zip(jobs, stage):
            s = st_ref[d, hp]
            inter = [None] * n_chunk
            for i in (range(n_chunk - 1, -1, -1) if reverse else range(n_chunk)):
                rows = slice(i * GLA_CHUNK, (i + 1) * GLA_CHUNK)
                inter[i] = _dot_nt(qd[rows].astype(BF), s.astype(BF))
                s = s * dec[i * GLA_CHUNK:i * GLA_CHUNK + 1, :] + jnp.where(own, kv_t[:, i * LANES:(i + 1) * LANES], 0.0)
            st_ref[d, hp] = s
            inters.append(jnp.concatenate(inter, axis=0))
        for n, ((qd, dec, v_pair, kv_t, a), o_inter) in enumerate(zip(stage, inters)):
            for hh in range(2):
                hv = slice(hh * GLA_DV, (hh + 1) * GLA_DV)
                outs[n].append(_dot(a[hh], v_pair[:, hv]) + o_inter[:, hv])
    return [jnp.concatenate(o, axis=1) for o in outs]


def _gla_finish(o, og, onorm):
    ys = []
    for h in range(GLA_HEADS):
        ys.append(_rms(o[:, h * GLA_DV:(h + 1) * GLA_DV], onorm))
    y = jnp.concatenate(ys, axis=1)
    ogf = og.astype(F32)
    return (y * (ogf * jax.nn.sigmoid(ogf))).astype(BF)


def _gla_kernel(*refs, n_lat, ctx_out):
    (ql_ref, kl_ref, vl_ref, sml_ref, ogl_ref, qc_ref, kc_ref, vc_ref, smc_ref, ogc_ref,
     waf_ref, baf_ref, wab_ref, bab_ref, onorm_ref, trif_ref, trib_ref) = refs[:17]
    if ctx_out:
        ol_ref, oc_ref, of_ref, ob_ref, st_ref = refs[17:]
    else:
        ol_ref, of_ref, ob_ref, st_ref = refs[17:]
    onorm = onorm_ref[...]

    def both(fwd_in, bwd_in):
        return _gla_superchunks([fwd_in + (waf_ref[...], baf_ref[...], trif_ref[...], 0, False),
                                 bwd_in + (wab_ref[...], bab_ref[...], trib_ref[...], 1, True)], st_ref)

    st_ref[...] = jnp.zeros_like(st_ref)
    ctx_in = (qc_ref[0], kc_ref[0], vc_ref[0], smc_ref[0])
    o_cf, o_cb = both(ctx_in, ctx_in)
    if ctx_out:
        oc_ref[0] = _gla_finish(o_cf + o_cb, ogc_ref[0], onorm)

    def body(j, carry):
        rf = pl.ds(pl.multiple_of(j * SUPER, SUPER), SUPER)
        rb = pl.ds(pl.multiple_of((n_lat - 1 - j) * SUPER, SUPER), SUPER)
        o_f, o_b = both((ql_ref[0, rf, :], kl_ref[0, rf, :], vl_ref[0, rf, :], sml_ref[0, rf, :]),
                        (ql_ref[0, rb, :], kl_ref[0, rb, :], vl_ref[0, rb, :], sml_ref[0, rb, :]))
        of_ref[rf, :] = o_f
        ob_ref[rb, :] = o_b
        return carry

    lax.fori_loop(0, n_lat, body, 0)

    def finish(j, carry):
        r = pl.ds(pl.multiple_of(j * SUPER, SUPER), SUPER)
        ol_ref[0, r, :] = _gla_finish(of_ref[r, :] + ob_ref[r, :], ogl_ref[0, r, :], onorm)
        return carry

    lax.fori_loop(0, n_lat, finish, 0)


def _gla_consts():
    r = np.arange(SUPER)
    same = (r[:, None] // GLA_CHUNK) == (r[None, :] // GLA_CHUNK)
    return jnp.asarray(same & (r[None, :] <= r[:, None]), BF), jnp.asarray(same & (r[None, :] >= r[:, None]), BF)


def _gla(z_l, z_c, params, l, ctx_out):
    B, T, _ = z_l.shape
    L = z_c.shape[1]
    assert L == SUPER and T % SUPER == 0
    kw = GLA_HEADS * GLA_DK
    vw = GLA_HEADS * GLA_DV
    consts = _gla_consts()

    def zspec(n, width, col):
        return pl.BlockSpec((1, n, width), lambda b: (b, 0, col // width))

    in_specs = ([zspec(T, kw, Z_GQ), zspec(T, kw, Z_GK), zspec(T, vw, Z_GV), zspec(T, LANES, Z_SM), zspec(T, vw, Z_OG),
                 zspec(L, kw, Z_GQ), zspec(L, kw, Z_GK), zspec(L, vw, Z_GV), zspec(L, LANES, Z_SM), zspec(L, vw, Z_OG)]
                + [_layer(a, l) for a in params] + [_whole(a) for a in consts])
    ol_spec = pl.BlockSpec((1, T, vw), lambda b: (b, 0, 0))
    ol_shape = jax.ShapeDtypeStruct((B, T, vw), BF)
    if ctx_out:
        out_specs = [ol_spec, pl.BlockSpec((1, L, vw), lambda b: (b, 0, 0))]
        out_shape = [ol_shape, jax.ShapeDtypeStruct((B, L, vw), BF)]
    else:
        out_specs, out_shape = ol_spec, ol_shape
    scratch = [pltpu.VMEM((T, vw), F32), pltpu.VMEM((T, vw), F32),
               pltpu.VMEM((2, GLA_HEADS // 2, 2 * GLA_DV, LANES), F32)]
    res = pl.pallas_call(
        functools.partial(_gla_kernel, n_lat=T // SUPER, ctx_out=ctx_out),
        grid=(B,), in_specs=in_specs, out_specs=out_specs, out_shape=out_shape,
        scratch_shapes=scratch,
        compiler_params=_cparams("arbitrary"),
        name="gla",
    )(*([z_l] * 5 + [z_c] * 5 + list(params) + list(consts)))
    return (res[0], res[1]) if ctx_out else (res, None)


MERGE_SUB = 256


def _merge_kernel(oa_ref, ob_ref, oc_ref, ma_ref, mb_ref, mc_ref, x_ref, gate_ref, shift_ref, scale_ref, g_ref,
                  wa_ref, wb_ref, wc_ref, wo_ref, wr_ref, xo_ref, h_ref, aff_ref, y_ref):
    tm = x_ref.shape[1]
    gate = gate_ref[0]
    subs = [slice(r0, r0 + MERGE_SUB) for r0 in range(0, tm, MERGE_SUB)]

    def branch(o_ref, m_ref, w_ref, rows, n0):
        m = m_ref[0, rows, n0:n0 + 256].astype(F32)
        return jax.nn.sigmoid(m) * _dot(o_ref[0, rows, :], w_ref[:, n0:n0 + 256])

    for n0 in range(0, D_MODEL, 256):
        for rows in subs:
            y = (branch(oa_ref, ma_ref, wa_ref, rows, n0) + branch(ob_ref, mb_ref, wb_ref, rows, n0)
                 + branch(oc_ref, mc_ref, wc_ref, rows, n0))
            y_ref[rows, n0:n0 + 256] = y.astype(BF)
    for n0 in range(0, D_MODEL, 256):
        for rows in subs:
            xo_ref[0, rows, n0:n0 + 256] = (x_ref[0, rows, n0:n0 + 256]
                                            + gate[:, n0:n0 + 256] * _dot(y_ref[rows, :], wo_ref[:, n0:n0 + 256]))
    hs = [_rms(xo_ref[0, rows, :], g_ref[...]) * (1.0 + scale_ref[0]) + shift_ref[0] for rows in subs]
    wr = wr_ref[...]
    for rows, h in zip(subs, hs):
        h_hi = h.astype(BF)
        h_lo = (h - h_hi.astype(F32)).astype(BF)
        h_ref[0, rows, :] = h_hi
        part = _dot_nt(wr, h_hi)
        logits = part[:N_EXPERTS] + part[N_EXPERTS:] + _dot_nt(wr[:N_EXPERTS], h_lo)
        e = jnp.exp(logits - logits.max(axis=0, keepdims=True))
        aff_ref[0, :, rows] = e / e.sum(axis=0, keepdims=True)


def _merge(oa, ob, oc, z, x, mods3, mod_row, norm2_3, weights, l, tm):
    B, T, D = x.shape
    wa, wb, wc, wo, wr_t = weights

    def tok(width, col=0):
        return pl.BlockSpec((1, tm, width), lambda b, i: (b, i, col // width))

    def mod_spec(k):
        return pl.BlockSpec((1, 1, D), lambda b, i: (mod_row(b) * 6 + k, 0, 0))

    in_specs = [tok(256), tok(256), tok(512), tok(D, Z_MA), tok(D, Z_MB), tok(D, Z_MC), tok(D),
                mod_spec(2), mod_spec(3), mod_spec(4), _layer(norm2_3, l)] + [_layer(w, l) for w in weights]
    return pl.pallas_call(
        _merge_kernel,
        grid=(B, T // tm), in_specs=in_specs,
        out_specs=[tok(D), tok(D), pl.BlockSpec((1, N_EXPERTS, tm), lambda b, i: (b, 0, i))],
        out_shape=[jax.ShapeDtypeStruct((B, T, D), F32), jax.ShapeDtypeStruct((B, T, D), BF),
                   jax.ShapeDtypeStruct((B, N_EXPERTS, T), F32)],
        scratch_shapes=[pltpu.VMEM((tm, D), BF)],
        compiler_params=_cparams("arbitrary", "arbitrary"),
        name="merge",
    )(oa, ob, oc, z, z, z, x, mods3, mods3, mods3, norm2_3, wa, wb, wc, wo, wr_t)


def _route_kernel(aff_ref, tri_ref, pos_ref, post_ref, *, cap, slot_stride):
    aff = aff_ref[0]
    E, T = aff.shape

    def refine(thr, shift, patterns):
        best = thr
        for c in patterns:
            cand = thr | (jnp.int32(c) << shift)
            cnt = jnp.sum((aff >= pltpu.bitcast(cand, F32)).astype(F32), axis=1, keepdims=True)
            best = jnp.where(cnt >= cap, cand, best)
        return best

    thr = lax.fori_loop(0, 15, lambda i, t: refine(t, 29 - 2 * i, (1, 2, 3)), jnp.zeros((E, 1), jnp.int32))
    thr = refine(thr, 0, (1,))
    thr_f = pltpu.bitcast(thr, F32)
    gt = aff > thr_f
    eq = aff == thr_f
    need = cap - jnp.sum(gt.astype(F32), axis=1, keepdims=True)
    tri = tri_ref[...]

    def excl_prefix(flags):
        run = jnp.zeros((E, 1), F32)
        blocks = []
        for j in range(T // LANES):
            blk = flags[:, j * LANES:(j + 1) * LANES].astype(F32)
            blocks.append(_dot(blk.astype(BF), tri) + run)
            run = run + jnp.sum(blk, axis=1, keepdims=True)
        return jnp.concatenate(blocks, axis=1)

    sel = gt | (eq & (excl_prefix(eq) < need))
    base = (slot_stride * pl.program_id(0)).astype(F32)
    pos = jnp.where(sel, excl_prefix(sel) + base, -1.0)
    pos_ref[0] = pos.astype(jnp.int32)
    post_ref[0] = jnp.concatenate([pos, jnp.full((LANES - E, T), -1.0, F32)], axis=0).T


def _route(aff_t, cap, slot_stride):
    B, E, T = aff_t.shape
    r = np.arange(LANES)
    tri = jnp.asarray(r[:, None] < r[None, :], BF)
    return pl.pallas_call(
        functools.partial(_route_kernel, cap=cap, slot_stride=slot_stride),
        grid=(B,),
        in_specs=[pl.BlockSpec((1, E, T), lambda b: (b, 0, 0)), pl.BlockSpec((LANES, LANES), lambda b: (0, 0))],
        out_specs=[pl.BlockSpec((1, E, T), lambda b: (b, 0, 0)), pl.BlockSpec((1, T, LANES), lambda b: (b, 0, 0))],
        out_shape=[jax.ShapeDtypeStruct((B, E, T), jnp.int32), jax.ShapeDtypeStruct((B, T, LANES), F32)],
        compiler_params=_cparams("arbitrary"),
        name="route",
    )(aff_t, tri)


def _moe_kernel(*refs, n_slots, final):
    if final:
        h_ref, pos_ref, post_ref, aff_ref, wg_ref, wu_ref, wd_ref, x_ref, gate_ref, g_ref, o_ref = refs
    else:
        h_ref, pos_ref, post_ref, aff_ref, wg_ref, wu_ref, wd_ref, o_ref = refs
    e = pl.program_id(1)

    @pl.when(e == 0)
    def _():
        o_ref[...] = jnp.zeros_like(o_ref)

    pos = pos_ref[0, pl.ds(e, 1), :]
    aff = aff_ref[0, pl.ds(e, 1), :]
    T = pos.shape[1]
    hit = lax.broadcasted_iota(jnp.int32, (n_slots, T), 0) == pos
    w_slot = jnp.sum(jnp.where(hit, aff, 0.0), axis=1, keepdims=True)
    xg = _dot(hit.astype(BF), h_ref[0]).astype(BF)
    gt = _dot(xg, wg_ref[...])
    up = _dot(xg, wu_ref[...])
    hid = (gt * jax.nn.sigmoid(gt) * up * w_slot).astype(BF)
    y = _dot(hid, wd_ref[...]).astype(BF)

    lane = lax.broadcasted_iota(jnp.int32, (1, LANES), 1)
    slot_row = lax.broadcasted_iota(jnp.int32, (1, n_slots), 1).astype(F32)
    for t0 in range(0, T, 256):
        pos_col = jnp.sum(jnp.where(lane == e, post_ref[0, t0:t0 + 256, :], 0.0), axis=1, keepdims=True)
        o_ref[0, t0:t0 + 256, :] += _dot((pos_col == slot_row).astype(BF), y)

    if final:
        @pl.when(e == pl.num_programs(1) - 1)
        def _():
            for t0 in range(0, T, 256):
                rows = slice(t0, t0 + 256)
                o_ref[0, rows, :] = _rms(x_ref[0, rows, :] + gate_ref[0] * o_ref[0, rows, :], g_ref[...])


def _moe(h, pos, pos_t, aff_t, weights, l, n_slots, final=None):
    B, T, D = h.shape
    E = pos.shape[1]
    wg, wu, wd = weights

    def wspec(w):
        return pl.BlockSpec((None, None) + w.shape[2:], lambda b, e: (l, e, 0, 0))

    in_specs = [pl.BlockSpec((1, T, D), lambda b, e: (b, 0, 0)),
                pl.BlockSpec((1, E, T), lambda b, e: (b, 0, 0)),
                pl.BlockSpec((1, T, LANES), lambda b, e: (b, 0, 0)),
                pl.BlockSpec((1, E, T), lambda b, e: (b, 0, 0)),
                wspec(wg), wspec(wu), wspec(wd)]
    args = [h, pos, pos_t, aff_t, wg, wu, wd]
    if final is not None:
        x, mods3, g = final
        in_specs += [pl.BlockSpec((1, T, D), lambda b, e: (b, 0, 0), pipeline_mode=pl.Buffered(1)),
                     pl.BlockSpec((1, 1, D), lambda b, e: (b * 6 + 5, 0, 0)),
                     pl.BlockSpec((1, D), lambda b, e: (0, 0))]
        args += [x, mods3, g.reshape(1, D)]
    return pl.pallas_call(
        functools.partial(_moe_kernel, n_slots=n_slots, final=final is not None),
        grid=(B, E),
        in_specs=in_specs,
        out_specs=pl.BlockSpec((1, T, D), lambda b, e: (b, 0, 0)),
        out_shape=jax.ShapeDtypeStruct((B, T, D), F32),
        compiler_params=_cparams("arbitrary", "arbitrary"),
        name="moe",
    )(*args)


def kernel(x, c, ctx, c_ctx, ada_w, ada_b, norm1, norm2, w_in, na_rpb, mla_q_norm, mla_w_uq, mla_kv_norm, mla_w_ukv,
           gla_w_af, gla_b_af, gla_w_ab, gla_b_ab, gla_o_norm, w_br_a, w_br_b, w_br_c, w_out,
           moe_router, moe_w_gate, moe_w_up, moe_w_down, final_norm):
    B, T, D = x.shape
    L = ctx.shape[1]
    depth = ada_w.shape[0]
    assert B < MOD_ROWS
    cap_l = EC_CAPACITY * T // N_EXPERTS
    cap_c = EC_CAPACITY * L // N_EXPERTS
    ctx_row = B
    cc = jnp.concatenate([c, c_ctx[None], jnp.zeros((MOD_ROWS - B - 1, D), F32)], axis=0)

    def lat_row(b):
        return b

    def ctx_row_fn(b):
        return ctx_row

    def rows3(a):
        return a.reshape(depth, 1, -1)

    ada_b3, norm1_3, norm2_3 = rows3(ada_b), rows3(norm1), rows3(norm2)
    ada_w_bf = ada_w.astype(BF)
    w_in_t = jnp.swapaxes(w_in, 1, 2).astype(BF)
    na_bias = _na_bias(na_rpb, T // GRID_W)
    mla_w = _mla_weights(mla_w_uq, mla_w_ukv)
    qn3, kvn3 = rows3(mla_q_norm), rows3(mla_kv_norm)
    rope_tabs = _rope_tables(T)
    gate_pad = lambda w, off: jnp.pad(w, ((0, 0), (off, LANES - off - GLA_GATE_RANK), (0, 0))).astype(BF)
    gla_params = (gate_pad(gla_w_af, SM_AF), rows3(gla_b_af), gate_pad(gla_w_ab, SM_AB), rows3(gla_b_ab),
                  rows3(gla_o_norm))
    wr_t = jnp.swapaxes(moe_router, 1, 2)
    wr_hi = wr_t.astype(BF)
    wr_lo = (wr_t - wr_hi.astype(F32)).astype(BF)
    merge_w = (w_br_a.astype(BF), w_br_b.astype(BF), w_br_c.astype(BF), w_out.astype(BF),
               jnp.concatenate([wr_hi, wr_lo], axis=1))
    moe_w = (moe_w_gate.astype(BF), moe_w_up.astype(BF), moe_w_down.astype(BF))

    moe_l = moe_c = mods3 = None
    for l in range(depth):
        ctx_out = l < depth - 1
        prev_mods3 = mods3
        mods3 = _ada(cc, ada_w_bf, ada_b3, l).reshape(MOD_ROWS * 6, 1, D)
        z_l, x = _in_proj(x, moe_l, prev_mods3, mods3, lat_row, norm1_3, w_in_t, l)
        z_c, ctx = _in_proj(ctx, moe_c, prev_mods3, mods3, ctx_row_fn, norm1_3, w_in_t, l,
                            cols=None if ctx_out else CTX_KEY_COLS)

        oa_l, oa_c = _na_attention(z_l, z_c, na_bias, l, ctx_out)

        qb_l, kb_l, vb_l = _mla_proj(z_l, qn3, kvn3, mla_w, rope_tabs, l)
        qb_c, kb_c, vb_c = _mla_proj(z_c, qn3, kvn3, mla_w, None, l)
        ob_l = _mla_attn(qb_l, kb_c, vb_c, kb_l, vb_l)

        oc_l, oc_c = _gla(z_l, z_c, gla_params, l, ctx_out)

        x, h_l, aff_l = _merge(oa_l, ob_l, oc_l, z_l, x, mods3, lat_row, norm2_3, merge_w, l, tm=512)
        pos_l, post_l = _route(aff_l, cap_l, 0)
        last = (x, mods3, final_norm) if l == depth - 1 else None
        moe_l = _moe(h_l, pos_l, post_l, aff_l, moe_w, l, cap_l, final=last)

        if ctx_out:
            ob_c = _mla_attn(qb_c, kb_c, vb_c)
            ctx, h_c, aff_c = _merge(oa_c, ob_c, oc_c, z_c, ctx, mods3, ctx_row_fn, norm2_3, merge_w, l, tm=256)
            pos_c, post_c = _route(aff_c, cap_c, cap_c)
            flat = lambda a: jnp.swapaxes(a, 0, 1).reshape(1, N_EXPERTS, B * L)
            moe_c = _moe(h_c.reshape(1, B * L, D), flat(pos_c), post_c.reshape(1, B * L, LANES), flat(aff_c),
                         moe_w, l, B * cap_c).reshape(B, L, D)
    return moe_l
```

```python
import functools

import numpy as np
import jax
import jax.numpy as jnp
from jax import lax
from jax.experimental import pallas as pl
from jax.experimental.pallas import tpu as pltpu

BF = jnp.bfloat16
F32 = jnp.float32

D_MODEL = 1024
GRID_W = 64
EPS = 1e-6
NA_HEADS = 4
NA_HEAD_DIM = 64
NA_WIN_H = 8
NA_WIN_W = 16
MLA_HEADS = 4
MLA_Q_RANK = 256
MLA_KV_RANK = 128
MLA_NOPE_DIM = 64
MLA_ROPE_DIM = 32
MLA_V_DIM = 64
ROPE_THETA = 10000.0
GLA_HEADS = 4
GLA_DK = 64
GLA_DV = 128
GLA_GATE_RANK = 16
GLA_TAU = 16.0
GLA_CHUNK = 64
N_EXPERTS = 16
EXPERT_FF = 1024
EC_CAPACITY = 2
IN_SIZES = (256, 256, 256, 256, 128, 32, 256, 256, 512, 16, 16, 512, 1024, 1024, 1024)

LANES = 128
VMEM_LIMIT = 56 * 1024 * 1024
MOD_ROWS = 16

Z_MA, Z_MB, Z_MC = 0, 1024, 2048
Z_GV, Z_OG = 3072, 3584
Z_QA, Z_KA, Z_VA = 4096, 4352, 4608
Z_GQ, Z_GK = 4864, 5120
Z_MLA = 5376
Z_SM = 5760
Z_COLS = 5888
SM_KR, SM_AF, SM_AB = 0, 32, 48

CTX_KEY_COLS = (Z_GV, Z_GV + 256, Z_KA, Z_VA, Z_GK, Z_MLA, Z_MLA + 256)

IN_NAMES = ("qa", "ka", "va", "dq", "dkv", "kr", "gq", "gk", "gv", "a_f", "a_b", "og", "m_a", "m_b", "m_c")
Z_ORDER = ("m_a", "m_b", "m_c", "gv", "og", "qa", "ka", "va", "gq", "gk", "dq", "dkv", "kr", "a_f", "a_b")


def _z_chunk_sources(width=256):
    start, off = {}, 0
    for name, size in zip(IN_NAMES, IN_SIZES):
        start[name] = off
        off += size
    pieces = []
    z = 0
    for name in Z_ORDER:
        s0, n = start[name], IN_SIZES[IN_NAMES.index(name)]
        while n:
            take = min(n, width - z % width)
            pieces.append((z, s0, take))
            z, s0, n = z + take, s0 + take, n - take
    chunks = [[] for _ in range(Z_COLS // width)]
    for zc, s0, n in pieces:
        if chunks[zc // width] and sum(chunks[zc // width][-1]) == s0:
            chunks[zc // width][-1] = (chunks[zc // width][-1][0], chunks[zc // width][-1][1] + n)
        else:
            chunks[zc // width].append((s0, n))
    return chunks


SUPER = 256
NEG = -1e30


def _cparams(*sem):
    return pltpu.CompilerParams(dimension_semantics=sem, vmem_limit_bytes=VMEM_LIMIT)


def _layer(a, l):
    return pl.BlockSpec((None,) + a.shape[1:], lambda *_: (l,) + (0,) * (a.ndim - 1), pipeline_mode=pl.Buffered(1))


def _whole(a):
    return pl.BlockSpec(a.shape, lambda *_: (0,) * a.ndim, pipeline_mode=pl.Buffered(1))


def _dot(a, b):
    return jnp.dot(a, b, preferred_element_type=F32)


def _dot_nt(a, b):
    return lax.dot_general(a, b, (((1,), (1,)), ((), ())), preferred_element_type=F32)


def _dot_tn(a, b):
    return lax.dot_general(a, b, (((0,), (0,)), ((), ())), preferred_element_type=F32)


def _rms(x, g):
    return x * lax.rsqrt(jnp.mean(x * x, axis=-1, keepdims=True) + EPS) * g


def _split3(x):
    hi = x.astype(BF)
    r = x - hi.astype(F32)
    mid = r.astype(BF)
    lo = (r - mid.astype(F32)).astype(BF)
    return hi, mid, lo


def _ada_kernel(c_ref, w_ref, b_ref, o_ref):
    c = c_ref[...]
    a = (c * jax.nn.sigmoid(c)).astype(BF)
    o_ref[...] = _dot(a, w_ref[...].astype(BF)) + b_ref[...]


def _ada(cc, ada_w, ada_b3, l):
    n = ada_w.shape[2]
    tn = 1536
    return pl.pallas_call(
        _ada_kernel,
        grid=(n // tn,),
        in_specs=[pl.BlockSpec((MOD_ROWS, D_MODEL), lambda j: (0, 0)),
                  pl.BlockSpec((None, D_MODEL, tn), lambda j: (l, 0, j)),
                  pl.BlockSpec((None, 1, tn), lambda j: (l, 0, j))],
        out_specs=pl.BlockSpec((MOD_ROWS, tn), lambda j: (0, j)),
        out_shape=jax.ShapeDtypeStruct((MOD_ROWS, n), F32),
        compiler_params=_cparams("arbitrary"),
        name="ada_mod",
    )(cc, ada_w, ada_b3)


def _in_proj_kernel(*refs, with_moe, cols):
    if with_moe:
        x_ref, moe_ref, gate_ref, g_ref, shift_ref, scale_ref, w_ref, o_ref, xo_ref, h_ref = refs
        x = x_ref[0] + gate_ref[0] * moe_ref[0]
        xo_ref[0] = x
    else:
        x_ref, g_ref, shift_ref, scale_ref, w_ref, o_ref, h_ref = refs
        x = x_ref[0]
    h = _rms(x, g_ref[...]) * (1.0 + scale_ref[0]) + shift_ref[0]
    h_ref[...] = h.astype(BF)
    for n0, srcs in zip(range(0, Z_COLS, 256), _z_chunk_sources()):
        if cols is None or n0 in cols:
            rows = [w_ref[s0:s0 + n, :] for s0, n in srcs]
            pad = 256 - sum(n for _, n in srcs)
            if pad:
                rows.append(jnp.zeros((pad, w_ref.shape[1]), BF))
            wt = rows[0] if len(rows) == 1 else jnp.concatenate(rows, axis=0)
            o_ref[0, :, n0:n0 + 256] = _dot_nt(h_ref[...], wt).astype(BF)
        else:
            o_ref[0, :, n0:n0 + 256] = jnp.zeros((o_ref.shape[1], 256), BF)


def _in_proj(x, moe, prev_mods3, mods3, mod_row, norm1_3, w, l, tm=512, cols=None):
    B, T, D = x.shape
    tm = min(tm, T)
    with_moe = moe is not None

    def mod_spec(k):
        return pl.BlockSpec((1, 1, D), lambda b, i: (mod_row(b) * 6 + k, 0, 0))

    x_spec = pl.BlockSpec((1, tm, D), lambda b, i: (b, i, 0))
    in_specs = [x_spec]
    args = [x]
    if with_moe:
        in_specs += [x_spec, mod_spec(5)]
        args += [moe, prev_mods3]
    in_specs += [_layer(norm1_3, l), mod_spec(0), mod_spec(1), _layer(w, l)]
    args += [norm1_3, mods3, mods3, w]
    z_spec = pl.BlockSpec((1, tm, Z_COLS), lambda b, i: (b, i, 0))
    z_shape = jax.ShapeDtypeStruct((B, T, Z_COLS), BF)
    if with_moe:
        out_specs, out_shape = [z_spec, x_spec], [z_shape, jax.ShapeDtypeStruct((B, T, D), F32)]
    else:
        out_specs, out_shape = z_spec, z_shape
    res = pl.pallas_call(
        functools.partial(_in_proj_kernel, with_moe=with_moe, cols=cols),
        grid=(B, T // tm),
        in_specs=in_specs, out_specs=out_specs, out_shape=out_shape,
        scratch_shapes=[pltpu.VMEM((tm, D), BF)],
        compiler_params=_cparams("arbitrary", "arbitrary"),
        name="in_proj",
    )(*args)
    return (res[0], res[1]) if with_moe else (res, x)


def _softmax_pv_many(heads):
    maxes = []
    for s_list, _ in heads:
        m = s_list[0].max(axis=-1, keepdims=True)
        for s in s_list[1:]:
            m = jnp.maximum(m, s.max(axis=-1, keepdims=True))
        maxes.append(m)
    probs = [[jnp.exp((s - m).astype(BF)) for s in s_list] for (s_list, _), m in zip(heads, maxes)]
    outs = []
    for (_, v_list), p_list in zip(heads, probs):
        o = _dot(p_list[0], v_list[0])
        for p, v in zip(p_list[1:], v_list[1:]):
            o = o + _dot(p, v)
        outs.append(o / pltpu.roll(o, LANES // 2, axis=1))
    return outs


def _softmax_pv_sum(heads):
    maxes = []
    for s_list, _ in heads:
        m = s_list[0].max(axis=-1, keepdims=True)
        for s in s_list[1:]:
            m = jnp.maximum(m, s.max(axis=-1, keepdims=True))
        maxes.append(m)
    probs = [[jnp.exp(s - m) for s in s_list] for (s_list, _), m in zip(heads, maxes)]
    outs = []
    for (_, v_list), p_list in zip(heads, probs):
        l = p_list[0].sum(axis=-1, keepdims=True)
        o = _dot(p_list[0].astype(BF), v_list[0])
        for p, v in zip(p_list[1:], v_list[1:]):
            l = l + p.sum(axis=-1, keepdims=True)
            o = o + _dot(p.astype(BF), v)
        outs.append(o / l)
    return outs


def _lane_masks():
    lane = lax.broadcasted_iota(jnp.int32, (1, LANES), 1)
    return lane < 64, lane >= 64


NA_ROWS_PER_STEP = 2
NA_BAND = 10
NA_CASE_ROWS = (0, 2, 4, 28, 30)


def _na_bias(rpb, rows):
    depth = rpb.shape[0]
    qc = np.arange(GRID_W)
    kj = np.arange(GRID_W)
    cs = np.clip(qc - NA_WIN_W // 2, 0, GRID_W - NA_WIN_W)
    col_ok = (kj[None, :] >= cs[:, None]) & (kj[None, :] < cs[:, None] + NA_WIN_W)
    co = kj[None, :] - qc[:, None] + (NA_WIN_W - 1)
    onehot = np.zeros((2 * NA_WIN_W - 1, GRID_W, GRID_W), np.float32)
    qi, ki = np.nonzero(col_ok)
    onehot[co[qi, ki], qi, ki] = 1.0
    toep = jnp.einsum("lhrc,cp->lhrp", rpb.astype(F32), jnp.asarray(onehot.reshape(2 * NA_WIN_W - 1, -1)),
                      precision=lax.Precision.HIGHEST)
    toep = (toep.reshape(depth, NA_HEADS, 2 * NA_WIN_H - 1, GRID_W, GRID_W)
            + jnp.asarray(np.where(col_ok, 0.0, NEG), F32))
    neg = jnp.full((depth, NA_HEADS, GRID_W, GRID_W), NEG, F32)
    cases = []
    for r0 in NA_CASE_ROWS:
        bs = int(np.clip(r0 - NA_WIN_H // 2, 0, rows - NA_BAND))
        q_rows = []
        for qr in range(NA_ROWS_PER_STEP):
            r = r0 + qr
            rs = int(np.clip(r - NA_WIN_H // 2, 0, rows - NA_WIN_H))
            blocks = []
            for i in range(NA_BAND):
                krow = bs + i
                blocks.append(toep[:, :, krow - r + NA_WIN_H - 1] if rs <= krow < rs + NA_WIN_H else neg)
            q_rows.append(jnp.concatenate(blocks, axis=-1))
        cases.append(jnp.concatenate(q_rows, axis=-2))
    return jnp.stack(cases, axis=2)


def _na_kernel(*refs, rows, ctx_out):
    if ctx_out:
        ql_ref, kl_ref, vl_ref, qc_ref, kc_ref, vc_ref, bias_ref, ol_ref, oc_ref = refs
    else:
        ql_ref, kl_ref, vl_ref, kc_ref, vc_ref, bias_ref, ol_ref = refs
    scale = NA_HEAD_DIM ** -0.5
    masks = _lane_masks()
    nq = NA_ROWS_PER_STEP * GRID_W
    nk = NA_BAND * GRID_W

    def body(rp, carry):
        r0 = rp * NA_ROWS_PER_STEP
        bs = jnp.clip(r0 - NA_WIN_H // 2, 0, rows - NA_BAND)
        case = jnp.where(r0 < 4, r0 // 2, jnp.where(r0 > rows - 6, (r0 - (rows - 4)) // 2 + 3, 2))
        q0 = pl.multiple_of(r0 * GRID_W, nq)
        k0 = pl.multiple_of(bs * GRID_W, 2 * GRID_W)
        heads = []
        for hp in range(NA_HEADS // 2):
            sl = slice(hp * LANES, (hp + 1) * LANES)
            q = ql_ref[0, pl.ds(q0, nq), sl] * scale
            kb = kl_ref[0, pl.ds(k0, nk), sl]
            vb = vl_ref[0, pl.ds(k0, nk), sl]
            kc = kc_ref[0, :, sl]
            vc = vc_ref[0, :, sl]
            for hh in range(2):
                qm = jnp.where(masks[hh], q, jnp.zeros_like(q))
                s_loc = _dot_nt(qm, kb) + bias_ref[2 * hp + hh, case]
                s_ctx = _dot_nt(qm, kc)
                heads.append(([s_loc, s_ctx], [vb, vc]))
        outs = _softmax_pv_sum(heads)
        for hp in range(NA_HEADS // 2):
            sl = slice(hp * LANES, (hp + 1) * LANES)
            ol_ref[0, pl.ds(q0, nq), sl] = jnp.where(masks[0], outs[2 * hp], outs[2 * hp + 1]).astype(BF)
        return carry

    lax.fori_loop(0, rows // NA_ROWS_PER_STEP, body, 0)

    if ctx_out:
        heads = []
        for hp in range(NA_HEADS // 2):
            sl = slice(hp * LANES, (hp + 1) * LANES)
            q = qc_ref[0, :, sl] * scale
            kc = kc_ref[0, :, sl]
            for hh in range(2):
                qm = jnp.where(masks[hh], q, jnp.zeros_like(q))
                heads.append(([_dot_nt(qm, kc)], [vc_ref[0, :, sl]]))
        outs = _softmax_pv_sum(heads)
        for hp in range(NA_HEADS // 2):
            sl = slice(hp * LANES, (hp + 1) * LANES)
            oc_ref[0, :, sl] = jnp.where(masks[0], outs[2 * hp], outs[2 * hp + 1]).astype(BF)


def _na_attention(z_l, z_c, bias, l, ctx_out):
    B, T, _ = z_l.shape
    L = z_c.shape[1]
    rows = T // GRID_W

    def zl(col):
        return pl.BlockSpec((1, T, 256), lambda b: (b, 0, col // 256))

    def zc(col):
        return pl.BlockSpec((1, L, 256), lambda b: (b, 0, col // 256))

    ol_spec = pl.BlockSpec((1, T, 256), lambda b: (b, 0, 0))
    ol_shape = jax.ShapeDtypeStruct((B, T, 256), BF)
    if ctx_out:
        in_specs = [zl(Z_QA), zl(Z_KA), zl(Z_VA), zc(Z_QA), zc(Z_KA), zc(Z_VA), _layer(bias, l)]
        args = [z_l, z_l, z_l, z_c, z_c, z_c, bias]
        out_specs = [ol_spec, pl.BlockSpec((1, L, 256), lambda b: (b, 0, 0))]
        out_shape = [ol_shape, jax.ShapeDtypeStruct((B, L, 256), BF)]
    else:
        in_specs = [zl(Z_QA), zl(Z_KA), zl(Z_VA), zc(Z_KA), zc(Z_VA), _layer(bias, l)]
        args = [z_l, z_l, z_l, z_c, z_c, bias]
        out_specs, out_shape = ol_spec, ol_shape
    res = pl.pallas_call(
        functools.partial(_na_kernel, rows=rows, ctx_out=ctx_out),
        grid=(B,), in_specs=in_specs, out_specs=out_specs, out_shape=out_shape,
        compiler_params=_cparams("arbitrary"),
        name="na_attn",
    )(*args)
    return (res[0], res[1]) if ctx_out else (res, None)


MLA_QK = MLA_NOPE_DIM + MLA_ROPE_DIM
MLA_W = MLA_HEADS * LANES


def _mla_proj_kernel(*refs, rope):
    if rope:
        (zm_ref, sm_ref, qn_ref, kvn_ref, wq_ref, wqs_ref, wk_ref, wv_ref, e_ref, es_ref, vone_ref, cos_ref, sin_ref,
         q_ref, k_ref, v_ref) = refs
    else:
        zm_ref, sm_ref, qn_ref, kvn_ref, wq_ref, wk_ref, wv_ref, e_ref, vone_ref, q_ref, k_ref, v_ref = refs
    zm = zm_ref[0].astype(F32)
    nq = _rms(zm[:, :MLA_Q_RANK], qn_ref[...]).astype(BF)
    nkv = _rms(zm[:, MLA_Q_RANK:], kvn_ref[...]).astype(BF)
    sm = sm_ref[0]
    scale = MLA_QK ** -0.5
    q = _dot(nq, wq_ref[...])
    k = _dot(nkv, wk_ref[...]) + _dot(sm, e_ref[...])
    if rope:
        cos = cos_ref[...]
        sin = sin_ref[...]
        q = q * cos + _dot(nq, wqs_ref[...]) * sin
        k = k * cos + _dot(sm, es_ref[...]) * sin
    q_ref[0] = (q * scale).astype(BF)
    k_ref[0] = k.astype(BF)
    v_ref[0] = (_dot(nkv, wv_ref[...]) + vone_ref[...]).astype(BF)


def _rope_swap(e):
    return np.where(e % 16 < 8, e + 8, e - 8)


def _mla_weights(w_uq, w_ukv):
    depth = w_uq.shape[0]
    quarter = MLA_ROPE_DIM // 4
    q4 = w_uq.reshape(depth, MLA_Q_RANK, MLA_HEADS, MLA_QK)
    nope, rope = q4[..., :MLA_NOPE_DIM], q4[..., MLA_NOPE_DIM:]
    rope_sw = jnp.flip(rope.reshape(depth, MLA_Q_RANK, MLA_HEADS, 2, 2, quarter), axis=4).reshape(rope.shape)
    zpad = jnp.zeros(rope.shape, w_uq.dtype)
    wq = jnp.concatenate([nope, rope, zpad], axis=-1).reshape(depth, MLA_Q_RANK, MLA_W)
    wqs = jnp.concatenate([jnp.zeros(nope.shape, w_uq.dtype), rope_sw, zpad], axis=-1).reshape(depth, MLA_Q_RANK, MLA_W)
    kv4 = w_ukv.reshape(depth, MLA_KV_RANK, MLA_HEADS, MLA_NOPE_DIM + MLA_V_DIM)
    k_nope, v = kv4[..., :MLA_NOPE_DIM], kv4[..., MLA_NOPE_DIM:]
    wk = jnp.concatenate([k_nope, jnp.zeros(k_nope.shape, w_ukv.dtype)], axis=-1).reshape(depth, MLA_KV_RANK, MLA_W)
    vz = jnp.zeros(v.shape[:2] + (1, MLA_V_DIM), w_ukv.dtype)
    wv = jnp.concatenate([jnp.concatenate([v[:, :, h:h + 1], vz] if h % 2 == 0 else [vz, v[:, :, h:h + 1]], axis=-1)
                          for h in range(MLA_HEADS)], axis=2).reshape(depth, MLA_KV_RANK, MLA_W)
    e = np.arange(MLA_ROPE_DIM)
    em = np.zeros((LANES, MLA_W), np.float32)
    ems = np.zeros((LANES, MLA_W), np.float32)
    for h in range(MLA_HEADS):
        em[SM_KR + e, h * LANES + MLA_NOPE_DIM + e] = 1.0
        ems[SM_KR + _rope_swap(e), h * LANES + MLA_NOPE_DIM + e] = 1.0
    lane = np.arange(MLA_W) % LANES
    v_ones = ((lane >= MLA_V_DIM) == ((np.arange(MLA_W) // LANES) % 2 == 0)).astype(np.float32)[None, :]
    return (wq.astype(BF), wqs.astype(BF), wk.astype(BF), wv.astype(BF), jnp.asarray(em, BF), jnp.asarray(ems, BF),
            jnp.asarray(v_ones, F32))


def _rope_tables(T):
    t = jnp.arange(T)
    quarter = MLA_ROPE_DIM // 4
    inv_freq = ROPE_THETA ** (-jnp.arange(quarter, dtype=F32) / quarter)
    ang_r = (t // GRID_W).astype(F32)[:, None] * inv_freq
    ang_c = (t % GRID_W).astype(F32)[:, None] * inv_freq
    cr, sr, ccol, scol = jnp.cos(ang_r), jnp.sin(ang_r), jnp.cos(ang_c), jnp.sin(ang_c)
    ones = jnp.ones((T, MLA_NOPE_DIM), F32)
    pad1 = jnp.ones((T, LANES - MLA_QK), F32)
    cos_h = jnp.concatenate([ones, cr, cr, ccol, ccol, pad1], axis=1)
    sin_h = jnp.concatenate([0.0 * ones, -sr, sr, -scol, scol, 0.0 * pad1], axis=1)
    return jnp.tile(cos_h, (1, MLA_HEADS)), jnp.tile(sin_h, (1, MLA_HEADS))


def _mla_proj(z, qn3, kvn3, weights, rope_tabs, l, tm=512):
    B, T, _ = z.shape
    wq, wqs, wk, wv, em, ems, v_ones = weights
    rope = rope_tabs is not None
    tm = min(tm, T)
    zm_spec = pl.BlockSpec((1, tm, 384), lambda i, b: (b, i, Z_MLA // 384))
    sm_spec = pl.BlockSpec((1, tm, LANES), lambda i, b: (b, i, Z_SM // LANES))
    if rope:
        tab_spec = pl.BlockSpec((tm, MLA_W), lambda i, b: (i, 0))
        ins = [z, z, qn3, kvn3, wq, wqs, wk, wv, em, ems, v_ones, rope_tabs[0], rope_tabs[1]]
        in_specs = ([zm_spec, sm_spec] + [_layer(a, l) for a in ins[2:8]] + [_whole(em), _whole(ems), _whole(v_ones)]
                    + [tab_spec, tab_spec])
    else:
        ins = [z, z, qn3, kvn3, wq, wk, wv, em, v_ones]
        in_specs = [zm_spec, sm_spec] + [_layer(a, l) for a in ins[2:7]] + [_whole(em), _whole(v_ones)]
    return pl.pallas_call(
        functools.partial(_mla_proj_kernel, rope=rope),
        grid=(T // tm, B), in_specs=in_specs,
        out_specs=[pl.BlockSpec((1, tm, MLA_W), lambda i, b: (b, i, 0)),
                   pl.BlockSpec((1, tm, MLA_W), lambda i, b: (b, i, 0)),
                   pl.BlockSpec((1, tm, MLA_W), lambda i, b: (b, i, 0))],
        out_shape=[jax.ShapeDtypeStruct((B, T, MLA_W), BF)] * 3,
        compiler_params=_cparams("arbitrary", "arbitrary"),
        name="mla_proj",
    )(*ins)


def _mla_attn_kernel(*refs, with_latent):
    if with_latent:
        q_ref, kc_ref, vc_ref, kl_ref, vl_ref, o_ref = refs
    else:
        q_ref, kc_ref, vc_ref, o_ref = refs
    masks = _lane_masks()
    heads = []
    for h in range(MLA_HEADS):
        hsl = slice(h * LANES, (h + 1) * LANES)
        q = q_ref[0, :, hsl]
        s_list = [_dot_nt(q, kc_ref[0, :, hsl])]
        v_list = [vc_ref[0, :, hsl]]
        if with_latent:
            s_list.append(_dot_nt(q, kl_ref[0, :, hsl]))
            v_list.append(vl_ref[0, :, hsl])
        heads.append((s_list, v_list))
    outs = _softmax_pv_many(heads)
    for hp in range(MLA_HEADS // 2):
        vsl = slice(hp * LANES, (hp + 1) * LANES)
        o_ref[0, :, vsl] = jnp.where(masks[0], outs[2 * hp], outs[2 * hp + 1]).astype(BF)


def _mla_attn(q, k_c, v_c, k_l=None, v_l=None, tq=256):
    B, Tq, W = q.shape
    L = k_c.shape[1]
    with_latent = k_l is not None
    in_specs = [pl.BlockSpec((1, tq, W), lambda b, i: (b, i, 0)),
                pl.BlockSpec((1, L, W), lambda b, i: (b, 0, 0)),
                pl.BlockSpec((1, L, W), lambda b, i: (b, 0, 0))]
    args = [q, k_c, v_c]
    if with_latent:
        T = k_l.shape[1]
        in_specs += [pl.BlockSpec((1, T, W), lambda b, i: (b, 0, 0)),
                     pl.BlockSpec((1, T, W), lambda b, i: (b, 0, 0))]
        args += [k_l, v_l]
    return pl.pallas_call(
        functools.partial(_mla_attn_kernel, with_latent=with_latent),
        grid=(B, Tq // tq), in_specs=in_specs,
        out_specs=pl.BlockSpec((1, tq, 256), lambda b, i: (b, i, 0)),
        out_shape=jax.ShapeDtypeStruct((B, Tq, 256), BF),
        compiler_params=_cparams("arbitrary", "arbitrary"),
        name="mla_attn",
    )(*args)


def _gla_superchunks(jobs, st_ref):
    n_chunk = SUPER // GLA_CHUNK
    masks = _lane_masks()
    rowblk = lax.broadcasted_iota(jnp.int32, (SUPER, LANES), 0) // GLA_CHUNK
    own = ((lax.broadcasted_iota(jnp.int32, (2 * GLA_DV, LANES), 0) < GLA_DV)
           == (lax.broadcasted_iota(jnp.int32, (2 * GLA_DV, LANES), 1) < GLA_DK))
    gs = [jax.nn.log_sigmoid(_dot(sm, w_a) + b_a) * (1.0 / GLA_TAU) for (_, _, _, sm, w_a, b_a, _, _, _) in jobs]
    pre = []
    for (q, k, v, sm, w_a, b_a, tri, d, reverse), g in zip(jobs, gs):
        g_hi = g.astype(BF)
        g_lo = (g - g_hi.astype(F32)).astype(BF)
        b = _dot(tri, g_hi) + _dot(tri, g_lo)
        ends = [i * GLA_CHUNK if reverse else (i + 1) * GLA_CHUNK - 1 for i in range(n_chunk)]
        tot = jnp.concatenate([jnp.broadcast_to(b[r:r + 1, :], (GLA_CHUNK, b.shape[1])) for r in ends], axis=0)
        pre.append((q.astype(F32) * (GLA_DK ** -0.5) * jnp.exp(b), k.astype(F32) * jnp.exp(-b),
                    k.astype(F32) * jnp.exp(tot - b), jnp.exp(tot), tri > 0))
    outs = [[] for _ in jobs]
    for hp in range(GLA_HEADS // 2):
        sl = slice(hp * LANES, (hp + 1) * LANES)
        stage = []
        for (q, k, v, sm, w_a, b_a, tri, d, reverse), (qd_all, kd_all, ke_all, dec_all, keep) in zip(jobs, pre):
            qd = qd_all[:, sl]
            kd = kd_all[:, sl].astype(BF)
            ke = ke_all[:, sl].astype(BF)
            zero = jnp.zeros_like(ke)
            ke_bd = jnp.concatenate([jnp.where(rowblk == i, ke, zero) for i in range(n_chunk)], axis=1)
            v_pair = v[:, hp * 2 * GLA_DV:(hp + 1) * 2 * GLA_DV]
            kv_t = _dot_tn(v_pair, ke_bd)
            a = [jnp.where(keep, _dot_nt(jnp.where(masks[hh], qd, 0.0).astype(BF), kd), 0.0).astype(BF)
                 for hh in range(2)]
            stage.append((qd, dec_all[:, sl], v_pair, kv_t, a))
        inters = []
        for (q, k, v, sm, w_a, b_a, tri, d, reverse), (qd, dec, v_pair, kv_t, a) in zip(jobs, stage):
            s = st_ref[d, hp]
            inter = [None] * n_chunk
            for i in (range(n_chunk - 1, -1, -1) if reverse else range(n_chunk)):
                rows = slice(i * GLA_CHUNK, (i + 1) * GLA_CHUNK)
                inter[i] = _dot_nt(qd[rows].astype(BF), s.astype(BF))
                s = s * dec[i * GLA_CHUNK:i * GLA_CHUNK + 1, :] + jnp.where(own, kv_t[:, i * LANES:(i + 1) * LANES], 0.0)
            st_ref[d, hp] = s
            inters.append(jnp.concatenate(inter, axis=0))
        for n, ((qd, dec, v_pair, kv_t, a), o_inter) in enumerate(zip(stage, inters)):
            for hh in range(2):
                hv = slice(hh * GLA_DV, (hh + 1) * GLA_DV)
                outs[n].append(_dot(a[hh], v_pair[:, hv]) + o_inter[:, hv])
    return [jnp.concatenate(o, axis=1) for o in outs]


def _gla_finish(o, og, onorm):
    ys = []
    for h in range(GLA_HEADS):
        ys.append(_rms(o[:, h * GLA_DV:(h + 1) * GLA_DV], onorm))
    y = jnp.concatenate(ys, axis=1)
    ogf = og.astype(F32)
    return (y * (ogf * jax.nn.sigmoid(ogf))).astype(BF)


def _gla_kernel(*refs, n_lat, ctx_out):
    (ql_ref, kl_ref, vl_ref, sml_ref, ogl_ref, qc_ref, kc_ref, vc_ref, smc_ref, ogc_ref,
     waf_ref, baf_ref, wab_ref, bab_ref, onorm_ref, trif_ref, trib_ref) = refs[:17]
    if ctx_out:
        ol_ref, oc_ref, of_ref, ob_ref, st_ref = refs[17:]
    else:
        ol_ref, of_ref, ob_ref, st_ref = refs[17:]
    onorm = onorm_ref[...]

    def both(fwd_in, bwd_in):
        return _gla_superchunks([fwd_in + (waf_ref[...], baf_ref[...], trif_ref[...], 0, False),
                                 bwd_in + (wab_ref[...], bab_ref[...], trib_ref[...], 1, True)], st_ref)

    st_ref[...] = jnp.zeros_like(st_ref)
    ctx_in = (qc_ref[0], kc_ref[0], vc_ref[0], smc_ref[0])
    o_cf, o_cb = both(ctx_in, ctx_in)
    if ctx_out:
        oc_ref[0] = _gla_finish(o_cf + o_cb, ogc_ref[0], onorm)

    def body(j, carry):
        rf = pl.ds(pl.multiple_of(j * SUPER, SUPER), SUPER)
        rb = pl.ds(pl.multiple_of((n_lat - 1 - j) * SUPER, SUPER), SUPER)
        o_f, o_b = both((ql_ref[0, rf, :], kl_ref[0, rf, :], vl_ref[0, rf, :], sml_ref[0, rf, :]),
                        (ql_ref[0, rb, :], kl_ref[0, rb, :], vl_ref[0, rb, :], sml_ref[0, rb, :]))
        of_ref[rf, :] = o_f
        ob_ref[rb, :] = o_b
        return carry

    lax.fori_loop(0, n_lat, body, 0)

    def finish(j, carry):
        r = pl.ds(pl.multiple_of(j * SUPER, SUPER), SUPER)
        ol_ref[0, r, :] = _gla_finish(of_ref[r, :] + ob_ref[r, :], ogl_ref[0, r, :], onorm)
        return carry

    lax.fori_loop(0, n_lat, finish, 0)


def _gla_consts():
    r = np.arange(SUPER)
    same = (r[:, None] // GLA_CHUNK) == (r[None, :] // GLA_CHUNK)
    return jnp.asarray(same & (r[None, :] <= r[:, None]), BF), jnp.asarray(same & (r[None, :] >= r[:, None]), BF)


def _gla(z_l, z_c, params, l, ctx_out):
    B, T, _ = z_l.shape
    L = z_c.shape[1]
    assert L == SUPER and T % SUPER == 0
    kw = GLA_HEADS * GLA_DK
    vw = GLA_HEADS * GLA_DV
    consts = _gla_consts()

    def zspec(n, width, col):
        return pl.BlockSpec((1, n, width), lambda b: (b, 0, col // width))

    in_specs = ([zspec(T, kw, Z_GQ), zspec(T, kw, Z_GK), zspec(T, vw, Z_GV), zspec(T, LANES, Z_SM), zspec(T, vw, Z_OG),
                 zspec(L, kw, Z_GQ), zspec(L, kw, Z_GK), zspec(L, vw, Z_GV), zspec(L, LANES, Z_SM), zspec(L, vw, Z_OG)]
                + [_layer(a, l) for a in params] + [_whole(a) for a in consts])
    ol_spec = pl.BlockSpec((1, T, vw), lambda b: (b, 0, 0))
    ol_shape = jax.ShapeDtypeStruct((B, T, vw), BF)
    if ctx_out:
        out_specs = [ol_spec, pl.BlockSpec((1, L, vw), lambda b: (b, 0, 0))]
        out_shape = [ol_shape, jax.ShapeDtypeStruct((B, L, vw), BF)]
    else:
        out_specs, out_shape = ol_spec, ol_shape
    scratch = [pltpu.VMEM((T, vw), F32), pltpu.VMEM((T, vw), F32),
               pltpu.VMEM((2, GLA_HEADS // 2, 2 * GLA_DV, LANES), F32)]
    res = pl.pallas_call(
        functools.partial(_gla_kernel, n_lat=T // SUPER, ctx_out=ctx_out),
        grid=(B,), in_specs=in_specs, out_specs=out_specs, out_shape=out_shape,
        scratch_shapes=scratch,
        compiler_params=_cparams("arbitrary"),
        name="gla",
    )(*([z_l] * 5 + [z_c] * 5 + list(params) + list(consts)))
    return (res[0], res[1]) if ctx_out else (res, None)


MERGE_SUB = 256


def _merge_kernel(oa_ref, ob_ref, oc_ref, ma_ref, mb_ref, mc_ref, x_ref, gate_ref, shift_ref, scale_ref, g_ref,
                  wa_ref, wb_ref, wc_ref, wo_ref, wr_ref, xo_ref, h_ref, aff_ref, y_ref):
    tm = x_ref.shape[1]
    gate = gate_ref[0]
    subs = [slice(r0, r0 + MERGE_SUB) for r0 in range(0, tm, MERGE_SUB)]

    def branch(o_ref, m_ref, w_ref, rows, n0):
        m = m_ref[0, rows, n0:n0 + 256].astype(F32)
        return jax.nn.sigmoid(m) * _dot(o_ref[0, rows, :], w_ref[:, n0:n0 + 256])

    for n0 in range(0, D_MODEL, 256):
        for rows in subs:
            y = (branch(oa_ref, ma_ref, wa_ref, rows, n0) + branch(ob_ref, mb_ref, wb_ref, rows, n0)
                 + branch(oc_ref, mc_ref, wc_ref, rows, n0))
            y_ref[rows, n0:n0 + 256] = y.astype(BF)
    for n0 in range(0, D_MODEL, 256):
        for rows in subs:
            xo_ref[0, rows, n0:n0 + 256] = (x_ref[0, rows, n0:n0 + 256]
                                            + gate[:, n0:n0 + 256] * _dot(y_ref[rows, :], wo_ref[:, n0:n0 + 256]))
    hs = [_rms(xo_ref[0, rows, :], g_ref[...]) * (1.0 + scale_ref[0]) + shift_ref[0] for rows in subs]
    wr = wr_ref[...]
    for rows, h in zip(subs, hs):
        h_hi = h.astype(BF)
        h_lo = (h - h_hi.astype(F32)).astype(BF)
        h_ref[0, rows, :] = h_hi
        part = _dot_nt(wr, h_hi)
        logits = part[:N_EXPERTS] + part[N_EXPERTS:] + _dot_nt(wr[:N_EXPERTS], h_lo)
        e = jnp.exp(logits - logits.max(axis=0, keepdims=True))
        aff_ref[0, :, rows] = e / e.sum(axis=0, keepdims=True)


def _merge(oa, ob, oc, z, x, mods3, mod_row, norm2_3, weights, l, tm):
    B, T, D = x.shape
    wa, wb, wc, wo, wr_t = weights

    def tok(width, col=0):
        return pl.BlockSpec((1, tm, width), lambda b, i: (b, i, col // width))

    def mod_spec(k):
        return pl.BlockSpec((1, 1, D), lambda b, i: (mod_row(b) * 6 + k, 0, 0))

    in_specs = [tok(256), tok(256), tok(512), tok(D, Z_MA), tok(D, Z_MB), tok(D, Z_MC), tok(D),
                mod_spec(2), mod_spec(3), mod_spec(4), _layer(norm2_3, l)] + [_layer(w, l) for w in weights]
    return pl.pallas_call(
        _merge_kernel,
        grid=(B, T // tm), in_specs=in_specs,
        out_specs=[tok(D), tok(D), pl.BlockSpec((1, N_EXPERTS, tm), lambda b, i: (b, 0, i))],
        out_shape=[jax.ShapeDtypeStruct((B, T, D), F32), jax.ShapeDtypeStruct((B, T, D), BF),
                   jax.ShapeDtypeStruct((B, N_EXPERTS, T), F32)],
        scratch_shapes=[pltpu.VMEM((tm, D), BF)],
        compiler_params=_cparams("arbitrary", "arbitrary"),
        name="merge",
    )(oa, ob, oc, z, z, z, x, mods3, mods3, mods3, norm2_3, wa, wb, wc, wo, wr_t)


def _route_kernel(aff_ref, tri_ref, pos_ref, post_ref, *, cap, slot_stride):
    aff = aff_ref[0]
    E, T = aff.shape

    def refine(thr, shift, patterns):
        best = thr
        for c in patterns:
            cand = thr | (jnp.int32(c) << shift)
            cnt = jnp.sum((aff >= pltpu.bitcast(cand, F32)).astype(F32), axis=1, keepdims=True)
            best = jnp.where(cnt >= cap, cand, best)
        return best

    thr = lax.fori_loop(0, 15, lambda i, t: refine(t, 29 - 2 * i, (1, 2, 3)), jnp.zeros((E, 1), jnp.int32))
    thr = refine(thr, 0, (1,))
    thr_f = pltpu.bitcast(thr, F32)
    gt = aff > thr_f
    eq = aff == thr_f
    need = cap - jnp.sum(gt.astype(F32), axis=1, keepdims=True)
    tri = tri_ref[...]

    def excl_prefix(flags):
        run = jnp.zeros((E, 1), F32)
        blocks = []
        for j in range(T // LANES):
            blk = flags[:, j * LANES:(j + 1) * LANES].astype(F32)
            blocks.append(_dot(blk.astype(BF), tri) + run)
            run = run + jnp.sum(blk, axis=1, keepdims=True)
        return jnp.concatenate(blocks, axis=1)

    sel = gt | (eq & (excl_prefix(eq) < need))
    base = (slot_stride * pl.program_id(0)).astype(F32)
    pos = jnp.where(sel, excl_prefix(sel) + base, -1.0)
    pos_ref[0] = pos.astype(jnp.int32)
    post_ref[0] = jnp.concatenate([pos, jnp.full((LANES - E, T), -1.0, F32)], axis=0).T


def _route(aff_t, cap, slot_stride):
    B, E, T = aff_t.shape
    r = np.arange(LANES)
    tri = jnp.asarray(r[:, None] < r[None, :], BF)
    return pl.pallas_call(
        functools.partial(_route_kernel, cap=cap, slot_stride=slot_stride),
        grid=(B,),
        in_specs=[pl.BlockSpec((1, E, T), lambda b: (b, 0, 0)), pl.BlockSpec((LANES, LANES), lambda b: (0, 0))],
        out_specs=[pl.BlockSpec((1, E, T), lambda b: (b, 0, 0)), pl.BlockSpec((1, T, LANES), lambda b: (b, 0, 0))],
        out_shape=[jax.ShapeDtypeStruct((B, E, T), jnp.int32), jax.ShapeDtypeStruct((B, T, LANES), F32)],
        compiler_params=_cparams("arbitrary"),
        name="route",
    )(aff_t, tri)


def _moe_kernel(*refs, n_slots, final):
    if final:
        h_ref, pos_ref, post_ref, aff_ref, wg_ref, wu_ref, wd_ref, x_ref, gate_ref, g_ref, o_ref = refs
    else:
        h_ref, pos_ref, post_ref, aff_ref, wg_ref, wu_ref, wd_ref, o_ref = refs
    e = pl.program_id(1)

    @pl.when(e == 0)
    def _():
        o_ref[...] = jnp.zeros_like(o_ref)

    pos = pos_ref[0, pl.ds(e, 1), :]
    aff = aff_ref[0, pl.ds(e, 1), :]
    T = pos.shape[1]
    hit = lax.broadcasted_iota(jnp.int32, (n_slots, T), 0) == pos
    w_slot = jnp.sum(jnp.where(hit, aff, 0.0), axis=1, keepdims=True)
    xg = _dot(hit.astype(BF), h_ref[0]).astype(BF)
    gt = _dot(xg, wg_ref[...].astype(BF))
    up = _dot(xg, wu_ref[...].astype(BF))
    hid = (gt * jax.nn.sigmoid(gt) * up * w_slot).astype(BF)
    y = _dot(hid, wd_ref[...]).astype(BF)

    lane = lax.broadcasted_iota(jnp.int32, (1, LANES), 1)
    slot_row = lax.broadcasted_iota(jnp.int32, (1, n_slots), 1).astype(F32)
    for t0 in range(0, T, 256):
        pos_col = jnp.sum(jnp.where(lane == e, post_ref[0, t0:t0 + 256, :], 0.0), axis=1, keepdims=True)
        o_ref[0, t0:t0 + 256, :] += _dot((pos_col == slot_row).astype(BF), y)

    if final:
        @pl.when(e == pl.num_programs(1) - 1)
        def _():
            for t0 in range(0, T, 256):
                rows = slice(t0, t0 + 256)
                o_ref[0, rows, :] = _rms(x_ref[0, rows, :] + gate_ref[0] * o_ref[0, rows, :], g_ref[...])


def _moe(h, pos, pos_t, aff_t, weights, l, n_slots, final=None):
    B, T, D = h.shape
    E = pos.shape[1]
    wg, wu, wd = weights

    def wspec(w):
        return pl.BlockSpec((None, None) + w.shape[2:], lambda b, e: (l, e, 0, 0))

    in_specs = [pl.BlockSpec((1, T, D), lambda b, e: (b, 0, 0)),
                pl.BlockSpec((1, E, T), lambda b, e: (b, 0, 0)),
                pl.BlockSpec((1, T, LANES), lambda b, e: (b, 0, 0)),
                pl.BlockSpec((1, E, T), lambda b, e: (b, 0, 0)),
                wspec(wg), wspec(wu), wspec(wd)]
    args = [h, pos, pos_t, aff_t, wg, wu, wd]
    if final is not None:
        x, mods3, g = final
        in_specs += [pl.BlockSpec((1, T, D), lambda b, e: (b, 0, 0), pipeline_mode=pl.Buffered(1)),
                     pl.BlockSpec((1, 1, D), lambda b, e: (b * 6 + 5, 0, 0)),
                     pl.BlockSpec((1, D), lambda b, e: (0, 0))]
        args += [x, mods3, g.reshape(1, D)]
    return pl.pallas_call(
        functools.partial(_moe_kernel, n_slots=n_slots, final=final is not None),
        grid=(B, E),
        in_specs=in_specs,
        out_specs=pl.BlockSpec((1, T, D), lambda b, e: (b, 0, 0)),
        out_shape=jax.ShapeDtypeStruct((B, T, D), F32),
        compiler_params=_cparams("arbitrary", "arbitrary"),
        name="moe",
    )(*args)


def _final_kernel(x_ref, moe_ref, gate_ref, g_ref, o_ref):
    o_ref[0] = _rms(x_ref[0] + gate_ref[0] * moe_ref[0], g_ref[...])


def _final(x, moe, mods3, g, tm=512):
    B, T, D = x.shape
    tok = pl.BlockSpec((1, tm, D), lambda b, i: (b, i, 0))
    return pl.pallas_call(
        _final_kernel,
        grid=(B, T // tm),
        in_specs=[tok, tok, pl.BlockSpec((1, 1, D), lambda b, i: (b * 6 + 5, 0, 0)),
                  pl.BlockSpec((1, D), lambda b, i: (0, 0))],
        out_specs=tok,
        out_shape=jax.ShapeDtypeStruct((B, T, D), F32),
        compiler_params=_cparams("arbitrary", "arbitrary"),
        name="final_norm",
    )(x, moe, mods3, g.reshape(1, D))


def kernel(x, c, ctx, c_ctx, ada_w, ada_b, norm1, norm2, w_in, na_rpb, mla_q_norm, mla_w_uq, mla_kv_norm, mla_w_ukv,
           gla_w_af, gla_b_af, gla_w_ab, gla_b_ab, gla_o_norm, w_br_a, w_br_b, w_br_c, w_out,
           moe_router, moe_w_gate, moe_w_up, moe_w_down, final_norm):
    B, T, D = x.shape
    L = ctx.shape[1]
    depth = ada_w.shape[0]
    assert B < MOD_ROWS
    cap_l = EC_CAPACITY * T // N_EXPERTS
    cap_c = EC_CAPACITY * L // N_EXPERTS
    ctx_row = B
    cc = jnp.concatenate([c, c_ctx[None], jnp.zeros((MOD_ROWS - B - 1, D), F32)], axis=0)

    def lat_row(b):
        return b

    def ctx_row_fn(b):
        return ctx_row

    def rows3(a):
        return a.reshape(depth, 1, -1)

    ada_b3, norm1_3, norm2_3 = rows3(ada_b), rows3(norm1), rows3(norm2)
    w_in_t = jnp.swapaxes(w_in, 1, 2).astype(BF)
    na_bias = _na_bias(na_rpb, T // GRID_W)
    mla_w = _mla_weights(mla_w_uq, mla_w_ukv)
    qn3, kvn3 = rows3(mla_q_norm), rows3(mla_kv_norm)
    rope_tabs = _rope_tables(T)
    gate_pad = lambda w, off: jnp.pad(w, ((0, 0), (off, LANES - off - GLA_GATE_RANK), (0, 0))).astype(BF)
    gla_params = (gate_pad(gla_w_af, SM_AF), rows3(gla_b_af), gate_pad(gla_w_ab, SM_AB), rows3(gla_b_ab),
                  rows3(gla_o_norm))
    wr_t = jnp.swapaxes(moe_router, 1, 2)
    wr_hi = wr_t.astype(BF)
    wr_lo = (wr_t - wr_hi.astype(F32)).astype(BF)
    merge_w = (w_br_a.astype(BF), w_br_b.astype(BF), w_br_c.astype(BF), w_out.astype(BF),
               jnp.concatenate([wr_hi, wr_lo], axis=1))
    moe_w = (moe_w_gate, moe_w_up, moe_w_down.astype(BF))

    moe_l = moe_c = mods3 = None
    for l in range(depth):
        ctx_out = l < depth - 1
        prev_mods3 = mods3
        mods3 = _ada(cc, ada_w, ada_b3, l).reshape(MOD_ROWS * 6, 1, D)
        z_l, x = _in_proj(x, moe_l, prev_mods3, mods3, lat_row, norm1_3, w_in_t, l)
        z_c, ctx = _in_proj(ctx, moe_c, prev_mods3, mods3, ctx_row_fn, norm1_3, w_in_t, l,
                            cols=None if ctx_out else CTX_KEY_COLS)

        oa_l, oa_c = _na_attention(z_l, z_c, na_bias, l, ctx_out)

        qb_l, kb_l, vb_l = _mla_proj(z_l, qn3, kvn3, mla_w, rope_tabs, l)
        qb_c, kb_c, vb_c = _mla_proj(z_c, qn3, kvn3, mla_w, None, l)
        ob_l = _mla_attn(qb_l, kb_c, vb_c, kb_l, vb_l)

        oc_l, oc_c = _gla(z_l, z_c, gla_params, l, ctx_out)

        x, h_l, aff_l = _merge(oa_l, ob_l, oc_l, z_l, x, mods3, lat_row, norm2_3, merge_w, l, tm=512)
        pos_l, post_l = _route(aff_l, cap_l, 0)
        moe_l = _moe(h_l, pos_l, post_l, aff_l, moe_w, l, cap_l)

        if ctx_out:
            ob_c = _mla_attn(qb_c, kb_c, vb_c)
            ctx, h_c, aff_c = _merge(oa_c, ob_c, oc_c, z_c, ctx, mods3, ctx_row_fn, norm2_3, merge_w, l, tm=256)
            pos_c, post_c = _route(aff_c, cap_c, cap_c)
            flat = lambda a: jnp.swapaxes(a, 0, 1).reshape(1, N_EXPERTS, B * L)
            moe_c = _moe(h_c.reshape(1, B * L, D), flat(pos_c), post_c.reshape(1, B * L, LANES), flat(aff_c),
                         moe_w, l, B * cap_c).reshape(B, L, D)
    return _final(x, moe_l, mods3, final_norm)
```

```python
import functools

import numpy as np
import jax
import jax.numpy as jnp
from jax import lax
from jax.experimental import pallas as pl
from jax.experimental.pallas import tpu as pltpu

BF = jnp.bfloat16
F32 = jnp.float32

D_MODEL = 1024
GRID_W = 64
EPS = 1e-6
NA_HEADS = 4
NA_HEAD_DIM = 64
NA_WIN_H = 8
NA_WIN_W = 16
MLA_HEADS = 4
MLA_Q_RANK = 256
MLA_KV_RANK = 128
MLA_NOPE_DIM = 64
MLA_ROPE_DIM = 32
MLA_V_DIM = 64
ROPE_THETA = 10000.0
GLA_HEADS = 4
GLA_DK = 64
GLA_DV = 128
GLA_GATE_RANK = 16
GLA_TAU = 16.0
GLA_CHUNK = 64
N_EXPERTS = 16
EXPERT_FF = 1024
EC_CAPACITY = 2
IN_SIZES = (256, 256, 256, 256, 128, 32, 256, 256, 512, 16, 16, 512, 1024, 1024, 1024)

LANES = 128
VMEM_LIMIT = 56 * 1024 * 1024
MOD_ROWS = 16

Z_MA, Z_MB, Z_MC = 0, 1024, 2048
Z_GV, Z_OG = 3072, 3584
Z_QA, Z_KA, Z_VA = 4096, 4352, 4608
Z_GQ, Z_GK = 4864, 5120
Z_MLA = 5376
Z_SM = 5760
Z_COLS = 5888
SM_KR, SM_AF, SM_AB = 0, 32, 48

CTX_KEY_COLS = (Z_GV, Z_GV + 256, Z_KA, Z_VA, Z_GK, Z_MLA, Z_MLA + 256)

IN_NAMES = ("qa", "ka", "va", "dq", "dkv", "kr", "gq", "gk", "gv", "a_f", "a_b", "og", "m_a", "m_b", "m_c")
Z_ORDER = ("m_a", "m_b", "m_c", "gv", "og", "qa", "ka", "va", "gq", "gk", "dq", "dkv", "kr", "a_f", "a_b")


def _z_chunk_sources(width=256):
    start, off = {}, 0
    for name, size in zip(IN_NAMES, IN_SIZES):
        start[name] = off
        off += size
    pieces = []
    z = 0
    for name in Z_ORDER:
        s0, n = start[name], IN_SIZES[IN_NAMES.index(name)]
        while n:
            take = min(n, width - z % width)
            pieces.append((z, s0, take))
            z, s0, n = z + take, s0 + take, n - take
    chunks = [[] for _ in range(Z_COLS // width)]
    for zc, s0, n in pieces:
        if chunks[zc // width] and sum(chunks[zc // width][-1]) == s0:
            chunks[zc // width][-1] = (chunks[zc // width][-1][0], chunks[zc // width][-1][1] + n)
        else:
            chunks[zc // width].append((s0, n))
    return chunks


SUPER = 256
NEG = -1e30


def _cparams(*sem):
    return pltpu.CompilerParams(dimension_semantics=sem, vmem_limit_bytes=VMEM_LIMIT)


def _layer(a, l):
    return pl.BlockSpec((None,) + a.shape[1:], lambda *_: (l,) + (0,) * (a.ndim - 1), pipeline_mode=pl.Buffered(1))


def _whole(a):
    return pl.BlockSpec(a.shape, lambda *_: (0,) * a.ndim, pipeline_mode=pl.Buffered(1))


def _dot(a, b):
    return jnp.dot(a, b, preferred_element_type=F32)


def _dot_nt(a, b):
    return lax.dot_general(a, b, (((1,), (1,)), ((), ())), preferred_element_type=F32)


def _dot_tn(a, b):
    return lax.dot_general(a, b, (((0,), (0,)), ((), ())), preferred_element_type=F32)


def _rms(x, g):
    return x * lax.rsqrt(jnp.mean(x * x, axis=-1, keepdims=True) + EPS) * g


def _ada_kernel(c_ref, w_ref, b_ref, o_ref):
    c = c_ref[...]
    a = (c * jax.nn.sigmoid(c)).astype(BF)
    o_ref[...] = _dot(a, w_ref[...].astype(BF)) + b_ref[...]


def _ada(cc, ada_w, ada_b3, l):
    n = ada_w.shape[2]
    tn = 1536
    return pl.pallas_call(
        _ada_kernel,
        grid=(n // tn,),
        in_specs=[pl.BlockSpec((MOD_ROWS, D_MODEL), lambda j: (0, 0)),
                  pl.BlockSpec((None, D_MODEL, tn), lambda j: (l, 0, j)),
                  pl.BlockSpec((None, 1, tn), lambda j: (l, 0, j))],
        out_specs=pl.BlockSpec((MOD_ROWS, tn), lambda j: (0, j)),
        out_shape=jax.ShapeDtypeStruct((MOD_ROWS, n), F32),
        compiler_params=_cparams("arbitrary"),
        name="ada_mod",
    )(cc, ada_w, ada_b3)


def _in_proj_kernel(*refs, with_moe, cols):
    if with_moe:
        x_ref, moe_ref, gate_ref, g_ref, shift_ref, scale_ref, w_ref, o_ref, xo_ref, h_ref = refs
        x = x_ref[0] + gate_ref[0] * moe_ref[0]
        xo_ref[0] = x
    else:
        x_ref, g_ref, shift_ref, scale_ref, w_ref, o_ref, h_ref = refs
        x = x_ref[0]
    h = _rms(x, g_ref[...]) * (1.0 + scale_ref[0]) + shift_ref[0]
    h_ref[...] = h.astype(BF)
    for n0, srcs in zip(range(0, Z_COLS, 256), _z_chunk_sources()):
        if cols is None or n0 in cols:
            rows = [w_ref[s0:s0 + n, :] for s0, n in srcs]
            pad = 256 - sum(n for _, n in srcs)
            if pad:
                rows.append(jnp.zeros((pad, w_ref.shape[1]), BF))
            wt = rows[0] if len(rows) == 1 else jnp.concatenate(rows, axis=0)
            o_ref[0, :, n0:n0 + 256] = _dot_nt(h_ref[...], wt).astype(BF)
        else:
            o_ref[0, :, n0:n0 + 256] = jnp.zeros((o_ref.shape[1], 256), BF)


def _in_proj(x, moe, prev_mods3, mods3, mod_row, norm1_3, w, l, tm=512, cols=None):
    B, T, D = x.shape
    tm = min(tm, T)
    with_moe = moe is not None

    def mod_spec(k):
        return pl.BlockSpec((1, 1, D), lambda b, i: (mod_row(b) * 6 + k, 0, 0))

    x_spec = pl.BlockSpec((1, tm, D), lambda b, i: (b, i, 0))
    in_specs = [x_spec]
    args = [x]
    if with_moe:
        in_specs += [x_spec, mod_spec(5)]
        args += [moe, prev_mods3]
    in_specs += [_layer(norm1_3, l), mod_spec(0), mod_spec(1), _layer(w, l)]
    args += [norm1_3, mods3, mods3, w]
    z_spec = pl.BlockSpec((1, tm, Z_COLS), lambda b, i: (b, i, 0))
    z_shape = jax.ShapeDtypeStruct((B, T, Z_COLS), BF)
    if with_moe:
        out_specs, out_shape = [z_spec, x_spec], [z_shape, jax.ShapeDtypeStruct((B, T, D), F32)]
    else:
        out_specs, out_shape = z_spec, z_shape
    res = pl.pallas_call(
        functools.partial(_in_proj_kernel, with_moe=with_moe, cols=cols),
        grid=(B, T // tm),
        in_specs=in_specs, out_specs=out_specs, out_shape=out_shape,
        scratch_shapes=[pltpu.VMEM((tm, D), BF)],
        compiler_params=_cparams("arbitrary", "arbitrary"),
        name="in_proj",
    )(*args)
    return (res[0], res[1]) if with_moe else (res, x)


def _softmax_pv_many(heads):
    maxes = []
    for s_list, _ in heads:
        m = s_list[0].max(axis=-1, keepdims=True)
        for s in s_list[1:]:
            m = jnp.maximum(m, s.max(axis=-1, keepdims=True))
        maxes.append(m)
    probs = [[jnp.exp((s - m).astype(BF)) for s in s_list] for (s_list, _), m in zip(heads, maxes)]
    outs = []
    for (_, v_list), p_list in zip(heads, probs):
        o = _dot(p_list[0], v_list[0])
        for p, v in zip(p_list[1:], v_list[1:]):
            o = o + _dot(p, v)
        outs.append(o / pltpu.roll(o, LANES // 2, axis=1))
    return outs


def _softmax_pv_sum(heads):
    maxes = []
    for s_list, _ in heads:
        m = s_list[0].max(axis=-1, keepdims=True)
        for s in s_list[1:]:
            m = jnp.maximum(m, s.max(axis=-1, keepdims=True))
        maxes.append(m)
    probs = [[jnp.exp(s - m) for s in s_list] for (s_list, _), m in zip(heads, maxes)]
    outs = []
    for (_, v_list), p_list in zip(heads, probs):
        l = p_list[0].sum(axis=-1, keepdims=True)
        o = _dot(p_list[0].astype(BF), v_list[0])
        for p, v in zip(p_list[1:], v_list[1:]):
            l = l + p.sum(axis=-1, keepdims=True)
            o = o + _dot(p.astype(BF), v)
        outs.append(o / l)
    return outs


def _lane_masks():
    lane = lax.broadcasted_iota(jnp.int32, (1, LANES), 1)
    return lane < LANES // 2, lane >= LANES // 2


NA_ROWS_PER_STEP = 2
NA_PAIRS_PER_TRIP = 4
NA_BAND = 10
NA_CASE_ROWS = (0, 2, 4, 28, 30)


def _na_bias(rpb, rows):
    depth = rpb.shape[0]
    qc = np.arange(GRID_W)
    kj = np.arange(GRID_W)
    cs = np.clip(qc - NA_WIN_W // 2, 0, GRID_W - NA_WIN_W)
    col_ok = (kj[None, :] >= cs[:, None]) & (kj[None, :] < cs[:, None] + NA_WIN_W)
    co = kj[None, :] - qc[:, None] + (NA_WIN_W - 1)
    onehot = np.zeros((2 * NA_WIN_W - 1, GRID_W, GRID_W), np.float32)
    qi, ki = np.nonzero(col_ok)
    onehot[co[qi, ki], qi, ki] = 1.0
    toep = jnp.einsum("lhrc,cp->lhrp", rpb.astype(F32), jnp.asarray(onehot.reshape(2 * NA_WIN_W - 1, -1)),
                      precision=lax.Precision.HIGHEST)
    toep = (toep.reshape(depth, NA_HEADS, 2 * NA_WIN_H - 1, GRID_W, GRID_W)
            + jnp.asarray(np.where(col_ok, 0.0, NEG), F32))
    neg = jnp.full((depth, NA_HEADS, GRID_W, GRID_W), NEG, F32)
    cases = []
    for r0 in NA_CASE_ROWS:
        bs = int(np.clip(r0 - NA_WIN_H // 2, 0, rows - NA_BAND))
        q_rows = []
        for qr in range(NA_ROWS_PER_STEP):
            r = r0 + qr
            rs = int(np.clip(r - NA_WIN_H // 2, 0, rows - NA_WIN_H))
            blocks = []
            for i in range(NA_BAND):
                krow = bs + i
                blocks.append(toep[:, :, krow - r + NA_WIN_H - 1] if rs <= krow < rs + NA_WIN_H else neg)
            q_rows.append(jnp.concatenate(blocks, axis=-1))
        cases.append(jnp.concatenate(q_rows, axis=-2))
    return jnp.stack(cases, axis=2)


def _na_kernel(*refs, rows, ctx_out):
    if ctx_out:
        ql_ref, kl_ref, vl_ref, qc_ref, kc_ref, vc_ref, bias_ref, ol_ref, oc_ref = refs
    else:
        ql_ref, kl_ref, vl_ref, kc_ref, vc_ref, bias_ref, ol_ref = refs
    scale = NA_HEAD_DIM ** -0.5
    masks = _lane_masks()
    nq = NA_ROWS_PER_STEP * GRID_W
    nk = NA_BAND * GRID_W

    def body(it, carry):
        heads, stores = [], []
        for sub in range(NA_PAIRS_PER_TRIP):
            r0 = (it * NA_PAIRS_PER_TRIP + sub) * NA_ROWS_PER_STEP
            bs = jnp.clip(r0 - NA_WIN_H // 2, 0, rows - NA_BAND)
            case = jnp.where(r0 < 4, r0 // 2, jnp.where(r0 > rows - 6, (r0 - (rows - 4)) // 2 + 3, 2))
            q0 = pl.multiple_of(r0 * GRID_W, nq)
            k0 = pl.multiple_of(bs * GRID_W, 2 * GRID_W)
            for hp in range(NA_HEADS // 2):
                sl = slice(hp * LANES, (hp + 1) * LANES)
                q = ql_ref[0, pl.ds(q0, nq), sl] * scale
                kb = kl_ref[0, pl.ds(k0, nk), sl]
                vb = vl_ref[0, pl.ds(k0, nk), sl]
                kc = kc_ref[0, :, sl]
                vc = vc_ref[0, :, sl]
                for hh in range(2):
                    qm = jnp.where(masks[hh], q, jnp.zeros_like(q))
                    s_loc = _dot_nt(qm, kb) + bias_ref[2 * hp + hh, case]
                    s_ctx = _dot_nt(qm, kc)
                    heads.append(([s_loc, s_ctx], [vb, vc]))
                stores.append((q0, sl))
        outs = _softmax_pv_sum(heads)
        for n, (q0, sl) in enumerate(stores):
            ol_ref[0, pl.ds(q0, nq), sl] = jnp.where(masks[0], outs[2 * n], outs[2 * n + 1]).astype(BF)
        return carry

    lax.fori_loop(0, rows // (NA_ROWS_PER_STEP * NA_PAIRS_PER_TRIP), body, 0)

    if ctx_out:
        heads = []
        for hp in range(NA_HEADS // 2):
            sl = slice(hp * LANES, (hp + 1) * LANES)
            q = qc_ref[0, :, sl] * scale
            kc = kc_ref[0, :, sl]
            for hh in range(2):
                qm = jnp.where(masks[hh], q, jnp.zeros_like(q))
                heads.append(([_dot_nt(qm, kc)], [vc_ref[0, :, sl]]))
        outs = _softmax_pv_sum(heads)
        for hp in range(NA_HEADS // 2):
            sl = slice(hp * LANES, (hp + 1) * LANES)
            oc_ref[0, :, sl] = jnp.where(masks[0], outs[2 * hp], outs[2 * hp + 1]).astype(BF)


def _na_attention(z_l, z_c, bias, l, ctx_out):
    B, T, _ = z_l.shape
    L = z_c.shape[1]
    rows = T // GRID_W

    def zl(col):
        return pl.BlockSpec((1, T, 256), lambda b: (b, 0, col // 256))

    def zc(col):
        return pl.BlockSpec((1, L, 256), lambda b: (b, 0, col // 256))

    ol_spec = pl.BlockSpec((1, T, 256), lambda b: (b, 0, 0))
    ol_shape = jax.ShapeDtypeStruct((B, T, 256), BF)
    if ctx_out:
        in_specs = [zl(Z_QA), zl(Z_KA), zl(Z_VA), zc(Z_QA), zc(Z_KA), zc(Z_VA), _layer(bias, l)]
        args = [z_l, z_l, z_l, z_c, z_c, z_c, bias]
        out_specs = [ol_spec, pl.BlockSpec((1, L, 256), lambda b: (b, 0, 0))]
        out_shape = [ol_shape, jax.ShapeDtypeStruct((B, L, 256), BF)]
    else:
        in_specs = [zl(Z_QA), zl(Z_KA), zl(Z_VA), zc(Z_KA), zc(Z_VA), _layer(bias, l)]
        args = [z_l, z_l, z_l, z_c, z_c, bias]
        out_specs, out_shape = ol_spec, ol_shape
    res = pl.pallas_call(
        functools.partial(_na_kernel, rows=rows, ctx_out=ctx_out),
        grid=(B,), in_specs=in_specs, out_specs=out_specs, out_shape=out_shape,
        compiler_params=_cparams("arbitrary"),
        name="na_attn",
    )(*args)
    return (res[0], res[1]) if ctx_out else (res, None)


MLA_QK = MLA_NOPE_DIM + MLA_ROPE_DIM
MLA_W = MLA_HEADS * LANES


def _mla_proj_kernel(*refs, rope):
    if rope:
        (zm_ref, sm_ref, qn_ref, kvn_ref, wq_ref, wqs_ref, wk_ref, wv_ref, e_ref, es_ref, vone_ref, cos_ref, sin_ref,
         q_ref, k_ref, v_ref) = refs
    else:
        zm_ref, sm_ref, qn_ref, kvn_ref, wq_ref, wk_ref, wv_ref, e_ref, vone_ref, q_ref, k_ref, v_ref = refs
    zm = zm_ref[0].astype(F32)
    nq = _rms(zm[:, :MLA_Q_RANK], qn_ref[...]).astype(BF)
    nkv = _rms(zm[:, MLA_Q_RANK:], kvn_ref[...]).astype(BF)
    sm = sm_ref[0]
    scale = MLA_QK ** -0.5
    q = _dot(nq, wq_ref[...])
    k = _dot(nkv, wk_ref[...]) + _dot(sm, e_ref[...])
    if rope:
        cos = cos_ref[...]
        sin = sin_ref[...]
        q = q * cos + _dot(nq, wqs_ref[...]) * sin
        k = k * cos + _dot(sm, es_ref[...]) * sin
    q_ref[0] = (q * scale).astype(BF)
    k_ref[0] = k.astype(BF)
    v_ref[0] = (_dot(nkv, wv_ref[...]) + vone_ref[...]).astype(BF)


def _rope_swap(e):
    return np.where(e % 16 < 8, e + 8, e - 8)


def _mla_weights(w_uq, w_ukv):
    depth = w_uq.shape[0]
    quarter = MLA_ROPE_DIM // 4
    q4 = w_uq.reshape(depth, MLA_Q_RANK, MLA_HEADS, MLA_QK)
    nope, rope = q4[..., :MLA_NOPE_DIM], q4[..., MLA_NOPE_DIM:]
    rope_sw = jnp.flip(rope.reshape(depth, MLA_Q_RANK, MLA_HEADS, 2, 2, quarter), axis=4).reshape(rope.shape)
    zpad = jnp.zeros(rope.shape, w_uq.dtype)
    wq = jnp.concatenate([nope, rope, zpad], axis=-1).reshape(depth, MLA_Q_RANK, MLA_W)
    wqs = jnp.concatenate([jnp.zeros(nope.shape, w_uq.dtype), rope_sw, zpad], axis=-1).reshape(depth, MLA_Q_RANK, MLA_W)
    kv4 = w_ukv.reshape(depth, MLA_KV_RANK, MLA_HEADS, MLA_NOPE_DIM + MLA_V_DIM)
    k_nope, v = kv4[..., :MLA_NOPE_DIM], kv4[..., MLA_NOPE_DIM:]
    wk = jnp.concatenate([k_nope, jnp.zeros(k_nope.shape, w_ukv.dtype)], axis=-1).reshape(depth, MLA_KV_RANK, MLA_W)
    vz = jnp.zeros(v.shape[:2] + (1, MLA_V_DIM), w_ukv.dtype)
    wv = jnp.concatenate([jnp.concatenate([v[:, :, h:h + 1], vz] if h % 2 == 0 else [vz, v[:, :, h:h + 1]], axis=-1)
                          for h in range(MLA_HEADS)], axis=2).reshape(depth, MLA_KV_RANK, MLA_W)
    e = np.arange(MLA_ROPE_DIM)
    em = np.zeros((LANES, MLA_W), np.float32)
    ems = np.zeros((LANES, MLA_W), np.float32)
    for h in range(MLA_HEADS):
        em[SM_KR + e, h * LANES + MLA_NOPE_DIM + e] = 1.0
        ems[SM_KR + _rope_swap(e), h * LANES + MLA_NOPE_DIM + e] = 1.0
    lane = np.arange(MLA_W) % LANES
    v_ones = ((lane >= MLA_V_DIM) == ((np.arange(MLA_W) // LANES) % 2 == 0)).astype(np.float32)[None, :]
    return (wq.astype(BF), wqs.astype(BF), wk.astype(BF), wv.astype(BF), jnp.asarray(em, BF), jnp.asarray(ems, BF),
            jnp.asarray(v_ones, F32))


def _rope_tables(T):
    t = jnp.arange(T)
    quarter = MLA_ROPE_DIM // 4
    inv_freq = ROPE_THETA ** (-jnp.arange(quarter, dtype=F32) / quarter)
    ang_r = (t // GRID_W).astype(F32)[:, None] * inv_freq
    ang_c = (t % GRID_W).astype(F32)[:, None] * inv_freq
    cr, sr, ccol, scol = jnp.cos(ang_r), jnp.sin(ang_r), jnp.cos(ang_c), jnp.sin(ang_c)
    ones = jnp.ones((T, MLA_NOPE_DIM), F32)
    pad1 = jnp.ones((T, LANES - MLA_QK), F32)
    cos_h = jnp.concatenate([ones, cr, cr, ccol, ccol, pad1], axis=1)
    sin_h = jnp.concatenate([0.0 * ones, -sr, sr, -scol, scol, 0.0 * pad1], axis=1)
    return jnp.tile(cos_h, (1, MLA_HEADS)), jnp.tile(sin_h, (1, MLA_HEADS))


def _mla_proj(z, qn3, kvn3, weights, rope_tabs, l, tm=512):
    B, T, _ = z.shape
    wq, wqs, wk, wv, em, ems, v_ones = weights
    rope = rope_tabs is not None
    tm = min(tm, T)
    zm_spec = pl.BlockSpec((1, tm, 384), lambda i, b: (b, i, Z_MLA // 384))
    sm_spec = pl.BlockSpec((1, tm, LANES), lambda i, b: (b, i, Z_SM // LANES))
    if rope:
        tab_spec = pl.BlockSpec((tm, MLA_W), lambda i, b: (i, 0))
        ins = [z, z, qn3, kvn3, wq, wqs, wk, wv, em, ems, v_ones, rope_tabs[0], rope_tabs[1]]
        in_specs = ([zm_spec, sm_spec] + [_layer(a, l) for a in ins[2:8]] + [_whole(em), _whole(ems), _whole(v_ones)]
                    + [tab_spec, tab_spec])
    else:
        ins = [z, z, qn3, kvn3, wq, wk, wv, em, v_ones]
        in_specs = [zm_spec, sm_spec] + [_layer(a, l) for a in ins[2:7]] + [_whole(em), _whole(v_ones)]
    return pl.pallas_call(
        functools.partial(_mla_proj_kernel, rope=rope),
        grid=(T // tm, B), in_specs=in_specs,
        out_specs=[pl.BlockSpec((1, tm, MLA_W), lambda i, b: (b, i, 0)),
                   pl.BlockSpec((1, tm, MLA_W), lambda i, b: (b, i, 0)),
                   pl.BlockSpec((1, tm, MLA_W), lambda i, b: (b, i, 0))],
        out_shape=[jax.ShapeDtypeStruct((B, T, MLA_W), BF)] * 3,
        compiler_params=_cparams("arbitrary", "arbitrary"),
        name="mla_proj",
    )(*ins)


def _mla_attn_kernel(*refs, with_latent):
    if with_latent:
        q_ref, kc_ref, vc_ref, kl_ref, vl_ref, o_ref = refs
    else:
        q_ref, kc_ref, vc_ref, o_ref = refs
    masks = _lane_masks()
    heads = []
    for h in range(MLA_HEADS):
        hsl = slice(h * LANES, (h + 1) * LANES)
        q = q_ref[0, :, hsl]
        s_list = [_dot_nt(q, kc_ref[0, :, hsl])]
        v_list = [vc_ref[0, :, hsl]]
        if with_latent:
            s_list.append(_dot_nt(q, kl_ref[0, :, hsl]))
            v_list.append(vl_ref[0, :, hsl])
        heads.append((s_list, v_list))
    outs = _softmax_pv_many(heads)
    for hp in range(MLA_HEADS // 2):
        vsl = slice(hp * LANES, (hp + 1) * LANES)
        o_ref[0, :, vsl] = jnp.where(masks[0], outs[2 * hp], outs[2 * hp + 1]).astype(BF)


def _mla_attn(q, k_c, v_c, k_l=None, v_l=None, tq=512):
    B, Tq, W = q.shape
    tq = min(tq, Tq)
    L = k_c.shape[1]
    with_latent = k_l is not None
    in_specs = [pl.BlockSpec((1, tq, W), lambda b, i: (b, i, 0)),
                pl.BlockSpec((1, L, W), lambda b, i: (b, 0, 0)),
                pl.BlockSpec((1, L, W), lambda b, i: (b, 0, 0))]
    args = [q, k_c, v_c]
    if with_latent:
        T = k_l.shape[1]
        in_specs += [pl.BlockSpec((1, T, W), lambda b, i: (b, 0, 0)),
                     pl.BlockSpec((1, T, W), lambda b, i: (b, 0, 0))]
        args += [k_l, v_l]
    return pl.pallas_call(
        functools.partial(_mla_attn_kernel, with_latent=with_latent),
        grid=(B, Tq // tq), in_specs=in_specs,
        out_specs=pl.BlockSpec((1, tq, 256), lambda b, i: (b, i, 0)),
        out_shape=jax.ShapeDtypeStruct((B, Tq, 256), BF),
        compiler_params=_cparams("arbitrary", "arbitrary"),
        name="mla_attn",
    )(*args)


def _gla_superchunks(jobs, st_ref):
    n_chunk = SUPER // GLA_CHUNK
    masks = _lane_masks()
    rowblk = lax.broadcasted_iota(jnp.int32, (SUPER, LANES), 0) // GLA_CHUNK
    own = ((lax.broadcasted_iota(jnp.int32, (2 * GLA_DV, LANES), 0) < GLA_DV)
           == (lax.broadcasted_iota(jnp.int32, (2 * GLA_DV, LANES), 1) < GLA_DK))
    gs = [jax.nn.log_sigmoid(_dot(sm, w_a) + b_a) * (1.0 / GLA_TAU) for (_, _, _, sm, w_a, b_a, _, _, _) in jobs]
    pre = []
    for (q, k, v, sm, w_a, b_a, tri, d, reverse), g in zip(jobs, gs):
        g_hi = g.astype(BF)
        g_lo = (g - g_hi.astype(F32)).astype(BF)
        b = _dot(tri, g_hi) + _dot(tri, g_lo)
        ends = [i * GLA_CHUNK if reverse else (i + 1) * GLA_CHUNK - 1 for i in range(n_chunk)]
        tot = jnp.concatenate([jnp.broadcast_to(b[r:r + 1, :], (GLA_CHUNK, b.shape[1])) for r in ends], axis=0)
        pre.append((q.astype(F32) * (GLA_DK ** -0.5) * jnp.exp(b), k.astype(F32) * jnp.exp(-b),
                    k.astype(F32) * jnp.exp(tot - b), jnp.exp(tot), tri > 0))
    outs = [[] for _ in jobs]
    for hp in range(GLA_HEADS // 2):
        sl = slice(hp * LANES, (hp + 1) * LANES)
        stage = []
        for (q, k, v, sm, w_a, b_a, tri, d, reverse), (qd_all, kd_all, ke_all, dec_all, keep) in zip(jobs, pre):
            qd = qd_all[:, sl]
            kd = kd_all[:, sl].astype(BF)
            ke = ke_all[:, sl].astype(BF)
            zero = jnp.zeros_like(ke)
            ke_bd = jnp.concatenate([jnp.where(rowblk == i, ke, zero) for i in range(n_chunk)], axis=1)
            v_pair = v[:, hp * 2 * GLA_DV:(hp + 1) * 2 * GLA_DV]
            kv_t = _dot_tn(v_pair, ke_bd)
            a = [jnp.where(keep, _dot_nt(jnp.where(masks[hh], qd, 0.0).astype(BF), kd), 0.0).astype(BF)
                 for hh in range(2)]
            stage.append((qd, dec_all[:, sl], v_pair, kv_t, a))
        inters = []
        for (q, k, v, sm, w_a, b_a, tri, d, reverse), (qd, dec, v_pair, kv_t, a) in zip(jobs, stage):
            s = st_ref[d, hp]
            inter = [None] * n_chunk
            for i in (range(n_chunk - 1, -1, -1) if reverse else range(n_chunk)):
                rows = slice(i * GLA_CHUNK, (i + 1) * GLA_CHUNK)
                inter[i] = _dot_nt(qd[rows].astype(BF), s.astype(BF))
                s = s * dec[i * GLA_CHUNK:i * GLA_CHUNK + 1, :] + jnp.where(own, kv_t[:, i * LANES:(i + 1) * LANES], 0.0)
            st_ref[d, hp] = s
            inters.append(jnp.concatenate(inter, axis=0))
        for n, ((qd, dec, v_pair, kv_t, a), o_inter) in enumerate(zip(stage, inters)):
            for hh in range(2):
                hv = slice(hh * GLA_DV, (hh + 1) * GLA_DV)
                outs[n].append(_dot(a[hh], v_pair[:, hv]) + o_inter[:, hv])
    return [jnp.concatenate(o, axis=1) for o in outs]


def _gla_finish(o, og, onorm):
    ys = []
    for h in range(GLA_HEADS):
        ys.append(_rms(o[:, h * GLA_DV:(h + 1) * GLA_DV], onorm))
    y = jnp.concatenate(ys, axis=1)
    ogf = og.astype(F32)
    return (y * (ogf * jax.nn.sigmoid(ogf))).astype(BF)


def _gla_kernel(*refs, n_lat, ctx_out):
    (ql_ref, kl_ref, vl_ref, sml_ref, ogl_ref, qc_ref, kc_ref, vc_ref, smc_ref, ogc_ref,
     waf_ref, baf_ref, wab_ref, bab_ref, onorm_ref, trif_ref, trib_ref) = refs[:17]
    if ctx_out:
        ol_ref, oc_ref, of_ref, ob_ref, st_ref = refs[17:]
    else:
        ol_ref, of_ref, ob_ref, st_ref = refs[17:]
    onorm = onorm_ref[...]

    def both(fwd_in, bwd_in):
        return _gla_superchunks([fwd_in + (waf_ref[...], baf_ref[...], trif_ref[...], 0, False),
                                 bwd_in + (wab_ref[...], bab_ref[...], trib_ref[...], 1, True)], st_ref)

    st_ref[...] = jnp.zeros_like(st_ref)
    ctx_in = (qc_ref[0], kc_ref[0], vc_ref[0], smc_ref[0])
    o_cf, o_cb = both(ctx_in, ctx_in)
    if ctx_out:
        oc_ref[0] = _gla_finish(o_cf + o_cb, ogc_ref[0], onorm)

    def body(j, carry):
        rf = pl.ds(pl.multiple_of(j * SUPER, SUPER), SUPER)
        rb = pl.ds(pl.multiple_of((n_lat - 1 - j) * SUPER, SUPER), SUPER)
        o_f, o_b = both((ql_ref[0, rf, :], kl_ref[0, rf, :], vl_ref[0, rf, :], sml_ref[0, rf, :]),
                        (ql_ref[0, rb, :], kl_ref[0, rb, :], vl_ref[0, rb, :], sml_ref[0, rb, :]))
        of_ref[rf, :] = o_f
        ob_ref[rb, :] = o_b
        return carry

    lax.fori_loop(0, n_lat, body, 0)

    def finish(j, carry):
        r = pl.ds(pl.multiple_of(j * SUPER, SUPER), SUPER)
        ol_ref[0, r, :] = _gla_finish(of_ref[r, :] + ob_ref[r, :], ogl_ref[0, r, :], onorm)
        return carry

    lax.fori_loop(0, n_lat, finish, 0)


def _gla_consts():
    r = np.arange(SUPER)
    same = (r[:, None] // GLA_CHUNK) == (r[None, :] // GLA_CHUNK)
    return jnp.asarray(same & (r[None, :] <= r[:, None]), BF), jnp.asarray(same & (r[None, :] >= r[:, None]), BF)


def _gla(z_l, z_c, params, l, ctx_out):
    B, T, _ = z_l.shape
    L = z_c.shape[1]
    assert L == SUPER and T % SUPER == 0
    kw = GLA_HEADS * GLA_DK
    vw = GLA_HEADS * GLA_DV
    consts = _gla_consts()

    def zspec(n, width, col):
        return pl.BlockSpec((1, n, width), lambda b: (b, 0, col // width))

    in_specs = ([zspec(T, kw, Z_GQ), zspec(T, kw, Z_GK), zspec(T, vw, Z_GV), zspec(T, LANES, Z_SM), zspec(T, vw, Z_OG),
                 zspec(L, kw, Z_GQ), zspec(L, kw, Z_GK), zspec(L, vw, Z_GV), zspec(L, LANES, Z_SM), zspec(L, vw, Z_OG)]
                + [_layer(a, l) for a in params] + [_whole(a) for a in consts])
    ol_spec = pl.BlockSpec((1, T, vw), lambda b: (b, 0, 0))
    ol_shape = jax.ShapeDtypeStruct((B, T, vw), BF)
    if ctx_out:
        out_specs = [ol_spec, pl.BlockSpec((1, L, vw), lambda b: (b, 0, 0))]
        out_shape = [ol_shape, jax.ShapeDtypeStruct((B, L, vw), BF)]
    else:
        out_specs, out_shape = ol_spec, ol_shape
    scratch = [pltpu.VMEM((T, vw), F32), pltpu.VMEM((T, vw), F32),
               pltpu.VMEM((2, GLA_HEADS // 2, 2 * GLA_DV, LANES), F32)]
    res = pl.pallas_call(
        functools.partial(_gla_kernel, n_lat=T // SUPER, ctx_out=ctx_out),
        grid=(B,), in_specs=in_specs, out_specs=out_specs, out_shape=out_shape,
        scratch_shapes=scratch,
        compiler_params=_cparams("arbitrary"),
        name="gla",
    )(*([z_l] * 5 + [z_c] * 5 + list(params) + list(consts)))
    return (res[0], res[1]) if ctx_out else (res, None)


MERGE_SUB = 256


def _merge_kernel(oa_ref, ob_ref, oc_ref, ma_ref, mb_ref, mc_ref, x_ref, gate_ref, shift_ref, scale_ref, g_ref,
                  wa_ref, wb_ref, wc_ref, wo_ref, wr_ref, xo_ref, h_ref, aff_ref, y_ref):
    tm = x_ref.shape[1]
    gate = gate_ref[0]
    subs = [slice(r0, r0 + MERGE_SUB) for r0 in range(0, tm, MERGE_SUB)]

    def branch(o_ref, m_ref, w_ref, rows, n0):
        m = m_ref[0, rows, n0:n0 + 256].astype(F32)
        return jax.nn.sigmoid(m) * _dot(o_ref[0, rows, :], w_ref[:, n0:n0 + 256])

    for n0 in range(0, D_MODEL, 256):
        for rows in subs:
            y = (branch(oa_ref, ma_ref, wa_ref, rows, n0) + branch(ob_ref, mb_ref, wb_ref, rows, n0)
                 + branch(oc_ref, mc_ref, wc_ref, rows, n0))
            y_ref[rows, n0:n0 + 256] = y.astype(BF)
    for n0 in range(0, D_MODEL, 256):
        for rows in subs:
            xo_ref[0, rows, n0:n0 + 256] = (x_ref[0, rows, n0:n0 + 256]
                                            + gate[:, n0:n0 + 256] * _dot(y_ref[rows, :], wo_ref[:, n0:n0 + 256]))
    hs = [_rms(xo_ref[0, rows, :], g_ref[...]) * (1.0 + scale_ref[0]) + shift_ref[0] for rows in subs]
    wr = wr_ref[...]
    for rows, h in zip(subs, hs):
        h_hi = h.astype(BF)
        h_lo = (h - h_hi.astype(F32)).astype(BF)
        h_ref[0, rows, :] = h_hi
        part = _dot_nt(wr, h_hi)
        logits = part[:N_EXPERTS] + part[N_EXPERTS:] + _dot_nt(wr[:N_EXPERTS], h_lo)
        e = jnp.exp(logits - logits.max(axis=0, keepdims=True))
        aff_ref[0, :, rows] = e / e.sum(axis=0, keepdims=True)


def _merge(oa, ob, oc, z, x, mods3, mod_row, norm2_3, weights, l, tm):
    B, T, D = x.shape
    wa, wb, wc, wo, wr_t = weights

    def tok(width, col=0):
        return pl.BlockSpec((1, tm, width), lambda b, i: (b, i, col // width))

    def mod_spec(k):
        return pl.BlockSpec((1, 1, D), lambda b, i: (mod_row(b) * 6 + k, 0, 0))

    in_specs = [tok(256), tok(256), tok(512), tok(D, Z_MA), tok(D, Z_MB), tok(D, Z_MC), tok(D),
                mod_spec(2), mod_spec(3), mod_spec(4), _layer(norm2_3, l)] + [_layer(w, l) for w in weights]
    return pl.pallas_call(
        _merge_kernel,
        grid=(B, T // tm), in_specs=in_specs,
        out_specs=[tok(D), tok(D), pl.BlockSpec((1, N_EXPERTS, tm), lambda b, i: (b, 0, i))],
        out_shape=[jax.ShapeDtypeStruct((B, T, D), F32), jax.ShapeDtypeStruct((B, T, D), BF),
                   jax.ShapeDtypeStruct((B, N_EXPERTS, T), F32)],
        scratch_shapes=[pltpu.VMEM((tm, D), BF)],
        compiler_params=_cparams("arbitrary", "arbitrary"),
        name="merge",
    )(oa, ob, oc, z, z, z, x, mods3, mods3, mods3, norm2_3, wa, wb, wc, wo, wr_t)


def _route_kernel(aff_ref, tri_ref, pos_ref, post_ref, *, cap, slot_stride):
    aff = aff_ref[0]
    E, T = aff.shape

    def refine(thr, shift, patterns):
        best = thr
        for c in patterns:
            cand = thr | (jnp.int32(c) << shift)
            cnt = jnp.sum((aff >= pltpu.bitcast(cand, F32)).astype(F32), axis=1, keepdims=True)
            best = jnp.where(cnt >= cap, cand, best)
        return best

    thr = lax.fori_loop(0, 15, lambda i, t: refine(t, 29 - 2 * i, (1, 2, 3)), jnp.zeros((E, 1), jnp.int32))
    thr = refine(thr, 0, (1,))
    thr_f = pltpu.bitcast(thr, F32)
    gt = aff > thr_f
    eq = aff == thr_f
    need = cap - jnp.sum(gt.astype(F32), axis=1, keepdims=True)
    tri = tri_ref[...]

    def excl_prefix(flags):
        run = jnp.zeros((E, 1), F32)
        blocks = []
        for j in range(T // LANES):
            blk = flags[:, j * LANES:(j + 1) * LANES].astype(F32)
            blocks.append(_dot(blk.astype(BF), tri) + run)
            run = run + jnp.sum(blk, axis=1, keepdims=True)
        return jnp.concatenate(blocks, axis=1)

    sel = gt | (eq & (excl_prefix(eq) < need))
    base = (slot_stride * pl.program_id(0)).astype(F32)
    pos = jnp.where(sel, excl_prefix(sel) + base, -1.0)
    pos_ref[0] = pos.astype(jnp.int32)
    post_ref[0] = jnp.concatenate([pos, jnp.full((LANES - E, T), -1.0, F32)], axis=0).T


def _route(aff_t, cap, slot_stride):
    B, E, T = aff_t.shape
    r = np.arange(LANES)
    tri = jnp.asarray(r[:, None] < r[None, :], BF)
    return pl.pallas_call(
        functools.partial(_route_kernel, cap=cap, slot_stride=slot_stride),
        grid=(B,),
        in_specs=[pl.BlockSpec((1, E, T), lambda b: (b, 0, 0)), pl.BlockSpec((LANES, LANES), lambda b: (0, 0))],
        out_specs=[pl.BlockSpec((1, E, T), lambda b: (b, 0, 0)), pl.BlockSpec((1, T, LANES), lambda b: (b, 0, 0))],
        out_shape=[jax.ShapeDtypeStruct((B, E, T), jnp.int32), jax.ShapeDtypeStruct((B, T, LANES), F32)],
        compiler_params=_cparams("arbitrary"),
        name="route",
    )(aff_t, tri)


def _moe_kernel(*refs, n_slots, final):
    if final:
        h_ref, pos_ref, post_ref, aff_ref, wg_ref, wu_ref, wd_ref, x_ref, gate_ref, g_ref, o_ref = refs
    else:
        h_ref, pos_ref, post_ref, aff_ref, wg_ref, wu_ref, wd_ref, o_ref = refs
    e = pl.program_id(1)

    @pl.when(e == 0)
    def _():
        o_ref[...] = jnp.zeros_like(o_ref)

    pos = pos_ref[0, pl.ds(e, 1), :]
    aff = aff_ref[0, pl.ds(e, 1), :]
    T = pos.shape[1]
    hit = lax.broadcasted_iota(jnp.int32, (n_slots, T), 0) == pos
    w_slot = jnp.sum(jnp.where(hit, aff, 0.0), axis=1, keepdims=True)
    xg = _dot(hit.astype(BF), h_ref[0]).astype(BF)
    gt = _dot(xg, wg_ref[...].astype(BF))
    up = _dot(xg, wu_ref[...].astype(BF))
    hid = (gt * jax.nn.sigmoid(gt) * up * w_slot).astype(BF)
    y = _dot(hid, wd_ref[...]).astype(BF)

    lane = lax.broadcasted_iota(jnp.int32, (1, LANES), 1)
    slot_row = lax.broadcasted_iota(jnp.int32, (1, n_slots), 1).astype(F32)
    for t0 in range(0, T, 256):
        pos_col = jnp.sum(jnp.where(lane == e, post_ref[0, t0:t0 + 256, :], 0.0), axis=1, keepdims=True)
        o_ref[0, t0:t0 + 256, :] += _dot((pos_col == slot_row).astype(BF), y)

    if final:
        @pl.when(e == pl.num_programs(1) - 1)
        def _():
            for t0 in range(0, T, 256):
                rows = slice(t0, t0 + 256)
                o_ref[0, rows, :] = _rms(x_ref[0, rows, :] + gate_ref[0] * o_ref[0, rows, :], g_ref[...])


def _moe(h, pos, pos_t, aff_t, weights, l, n_slots, final=None):
    B, T, D = h.shape
    E = pos.shape[1]
    wg, wu, wd = weights

    def wspec(w):
        return pl.BlockSpec((None, None) + w.shape[2:], lambda b, e: (l, e, 0, 0))

    in_specs = [pl.BlockSpec((1, T, D), lambda b, e: (b, 0, 0)),
                pl.BlockSpec((1, E, T), lambda b, e: (b, 0, 0)),
                pl.BlockSpec((1, T, LANES), lambda b, e: (b, 0, 0)),
                pl.BlockSpec((1, E, T), lambda b, e: (b, 0, 0)),
                wspec(wg), wspec(wu), wspec(wd)]
    args = [h, pos, pos_t, aff_t, wg, wu, wd]
    if final is not None:
        x, mods3, g = final
        in_specs += [pl.BlockSpec((1, T, D), lambda b, e: (b, 0, 0), pipeline_mode=pl.Buffered(1)),
                     pl.BlockSpec((1, 1, D), lambda b, e: (b * 6 + 5, 0, 0)),
                     pl.BlockSpec((1, D), lambda b, e: (0, 0))]
        args += [x, mods3, g.reshape(1, D)]
    return pl.pallas_call(
        functools.partial(_moe_kernel, n_slots=n_slots, final=final is not None),
        grid=(B, E),
        in_specs=in_specs,
        out_specs=pl.BlockSpec((1, T, D), lambda b, e: (b, 0, 0)),
        out_shape=jax.ShapeDtypeStruct((B, T, D), F32),
        compiler_params=_cparams("arbitrary", "arbitrary"),
        name="moe",
    )(*args)


def _final_kernel(x_ref, moe_ref, gate_ref, g_ref, o_ref):
    o_ref[0] = _rms(x_ref[0] + gate_ref[0] * moe_ref[0], g_ref[...])


def _final(x, moe, mods3, g, tm=512):
    B, T, D = x.shape
    tok = pl.BlockSpec((1, tm, D), lambda b, i: (b, i, 0))
    return pl.pallas_call(
        _final_kernel,
        grid=(B, T // tm),
        in_specs=[tok, tok, pl.BlockSpec((1, 1, D), lambda b, i: (b * 6 + 5, 0, 0)),
                  pl.BlockSpec((1, D), lambda b, i: (0, 0))],
        out_specs=tok,
        out_shape=jax.ShapeDtypeStruct((B, T, D), F32),
        compiler_params=_cparams("arbitrary", "arbitrary"),
        name="final_norm",
    )(x, moe, mods3, g.reshape(1, D))


def kernel(x, c, ctx, c_ctx, ada_w, ada_b, norm1, norm2, w_in, na_rpb, mla_q_norm, mla_w_uq, mla_kv_norm, mla_w_ukv,
           gla_w_af, gla_b_af, gla_w_ab, gla_b_ab, gla_o_norm, w_br_a, w_br_b, w_br_c, w_out,
           moe_router, moe_w_gate, moe_w_up, moe_w_down, final_norm):
    B, T, D = x.shape
    L = ctx.shape[1]
    depth = ada_w.shape[0]
    assert B < MOD_ROWS
    cap_l = EC_CAPACITY * T // N_EXPERTS
    cap_c = EC_CAPACITY * L // N_EXPERTS
    ctx_row = B
    cc = jnp.concatenate([c, c_ctx[None], jnp.zeros((MOD_ROWS - B - 1, D), F32)], axis=0)

    def lat_row(b):
        return b

    def ctx_row_fn(b):
        return ctx_row

    def rows3(a):
        return a.reshape(depth, 1, -1)

    ada_b3, norm1_3, norm2_3 = rows3(ada_b), rows3(norm1), rows3(norm2)
    w_in_t = jnp.swapaxes(w_in, 1, 2).astype(BF)
    na_bias = _na_bias(na_rpb, T // GRID_W)
    mla_w = _mla_weights(mla_w_uq, mla_w_ukv)
    qn3, kvn3 = rows3(mla_q_norm), rows3(mla_kv_norm)
    rope_tabs = _rope_tables(T)
    gate_pad = lambda w, off: jnp.pad(w, ((0, 0), (off, LANES - off - GLA_GATE_RANK), (0, 0))).astype(BF)
    gla_params = (gate_pad(gla_w_af, SM_AF), rows3(gla_b_af), gate_pad(gla_w_ab, SM_AB), rows3(gla_b_ab),
                  rows3(gla_o_norm))
    wr_t = jnp.swapaxes(moe_router, 1, 2)
    wr_hi = wr_t.astype(BF)
    wr_lo = (wr_t - wr_hi.astype(F32)).astype(BF)
    merge_w = (w_br_a.astype(BF), w_br_b.astype(BF), w_br_c.astype(BF), w_out.astype(BF),
               jnp.concatenate([wr_hi, wr_lo], axis=1))
    moe_w = (moe_w_gate, moe_w_up, moe_w_down.astype(BF))

    moe_l = moe_c = mods3 = None
    for l in range(depth):
        ctx_out = l < depth - 1
        prev_mods3 = mods3
        mods3 = _ada(cc, ada_w, ada_b3, l).reshape(MOD_ROWS * 6, 1, D)
        z_l, x = _in_proj(x, moe_l, prev_mods3, mods3, lat_row, norm1_3, w_in_t, l)
        z_c, ctx = _in_proj(ctx, moe_c, prev_mods3, mods3, ctx_row_fn, norm1_3, w_in_t, l,
                            cols=None if ctx_out else CTX_KEY_COLS)

        oa_l, oa_c = _na_attention(z_l, z_c, na_bias, l, ctx_out)

        qb_l, kb_l, vb_l = _mla_proj(z_l, qn3, kvn3, mla_w, rope_tabs, l)
        qb_c, kb_c, vb_c = _mla_proj(z_c, qn3, kvn3, mla_w, None, l)
        ob_l = _mla_attn(qb_l, kb_c, vb_c, kb_l, vb_l)

        oc_l, oc_c = _gla(z_l, z_c, gla_params, l, ctx_out)

        x, h_l, aff_l = _merge(oa_l, ob_l, oc_l, z_l, x, mods3, lat_row, norm2_3, merge_w, l, tm=512)
        pos_l, post_l = _route(aff_l, cap_l, 0)
        moe_l = _moe(h_l, pos_l, post_l, aff_l, moe_w, l, cap_l)

        if ctx_out:
            ob_c = _mla_attn(qb_c, kb_c, vb_c)
            ctx, h_c, aff_c = _merge(oa_c, ob_c, oc_c, z_c, ctx, mods3, ctx_row_fn, norm2_3, merge_w, l, tm=256)
            pos_c, post_c = _route(aff_c, cap_c, cap_c)
            flat = lambda a: jnp.swapaxes(a, 0, 1).reshape(1, N_EXPERTS, B * L)
            moe_c = _moe(h_c.reshape(1, B * L, D), flat(pos_c), post_c.reshape(1, B * L, LANES), flat(aff_c),
                         moe_w, l, B * cap_c).reshape(B, L, D)
    return _final(x, moe_l, mods3, final_norm)
```

```python
import functools

import numpy as np
import jax
import jax.numpy as jnp
from jax import lax
from jax.experimental import pallas as pl
from jax.experimental.pallas import tpu as pltpu

BF = jnp.bfloat16
F32 = jnp.float32

D_MODEL = 1024
GRID_W = 64
EPS = 1e-6
NA_HEADS = 4
NA_HEAD_DIM = 64
NA_WIN_H = 8
NA_WIN_W = 16
MLA_HEADS = 4
MLA_Q_RANK = 256
MLA_KV_RANK = 128
MLA_NOPE_DIM = 64
MLA_ROPE_DIM = 32
MLA_V_DIM = 64
ROPE_THETA = 10000.0
GLA_HEADS = 4
GLA_DK = 64
GLA_DV = 128
GLA_GATE_RANK = 16
GLA_TAU = 16.0
GLA_CHUNK = 64
N_EXPERTS = 16
EXPERT_FF = 1024
EC_CAPACITY = 2
IN_SIZES = (256, 256, 256, 256, 128, 32, 256, 256, 512, 16, 16, 512, 1024, 1024, 1024)

LANES = 128
VMEM_LIMIT = 56 * 1024 * 1024
MOD_ROWS = 16

Z_MA, Z_MB, Z_MC = 0, 1024, 2048
Z_GV, Z_OG = 3072, 3584
Z_QA, Z_KA, Z_VA = 4096, 4352, 4608
Z_GQ, Z_GK = 4864, 5120
Z_MLA = 5376
Z_SM = 5760
Z_COLS = 5888
SM_KR, SM_AF, SM_AB = 0, 32, 48

CTX_KEY_COLS = (Z_GV, Z_GV + 256, Z_KA, Z_VA, Z_GK, Z_MLA, Z_MLA + 256)

IN_NAMES = ("qa", "ka", "va", "dq", "dkv", "kr", "gq", "gk", "gv", "a_f", "a_b", "og", "m_a", "m_b", "m_c")
Z_ORDER = ("m_a", "m_b", "m_c", "gv", "og", "qa", "ka", "va", "gq", "gk", "dq", "dkv", "kr", "a_f", "a_b")


def _z_chunk_sources(width=256):
    start, off = {}, 0
    for name, size in zip(IN_NAMES, IN_SIZES):
        start[name] = off
        off += size
    pieces = []
    z = 0
    for name in Z_ORDER:
        s0, n = start[name], IN_SIZES[IN_NAMES.index(name)]
        while n:
            take = min(n, width - z % width)
            pieces.append((z, s0, take))
            z, s0, n = z + take, s0 + take, n - take
    chunks = [[] for _ in range(Z_COLS // width)]
    for zc, s0, n in pieces:
        if chunks[zc // width] and sum(chunks[zc // width][-1]) == s0:
            chunks[zc // width][-1] = (chunks[zc // width][-1][0], chunks[zc // width][-1][1] + n)
        else:
            chunks[zc // width].append((s0, n))
    return chunks


SUPER = 256
GLA_BLOCKS_PER_TRIP = 2
NEG = -1e30


def _cparams(*sem):
    return pltpu.CompilerParams(dimension_semantics=sem, vmem_limit_bytes=VMEM_LIMIT)


def _layer(a, l):
    return pl.BlockSpec((None,) + a.shape[1:], lambda *_: (l,) + (0,) * (a.ndim - 1), pipeline_mode=pl.Buffered(1))


def _whole(a):
    return pl.BlockSpec(a.shape, lambda *_: (0,) * a.ndim, pipeline_mode=pl.Buffered(1))


def _dot(a, b):
    return jnp.dot(a, b, preferred_element_type=F32)


def _dot_nt(a, b):
    return lax.dot_general(a, b, (((1,), (1,)), ((), ())), preferred_element_type=F32)


def _dot_tn(a, b):
    return lax.dot_general(a, b, (((0,), (0,)), ((), ())), preferred_element_type=F32)


def _rms(x, g):
    return x * lax.rsqrt(jnp.mean(x * x, axis=-1, keepdims=True) + EPS) * g


def _ada_kernel(c_ref, w_ref, b_ref, o_ref):
    c = c_ref[...]
    a = (c * jax.nn.sigmoid(c)).astype(BF)
    o_ref[...] = _dot(a, w_ref[...].astype(BF)) + b_ref[...]


def _ada(cc, ada_w, ada_b3, l):
    n = ada_w.shape[2]
    tn = 1536
    return pl.pallas_call(
        _ada_kernel,
        grid=(n // tn,),
        in_specs=[pl.BlockSpec((MOD_ROWS, D_MODEL), lambda j: (0, 0)),
                  pl.BlockSpec((None, D_MODEL, tn), lambda j: (l, 0, j)),
                  pl.BlockSpec((None, 1, tn), lambda j: (l, 0, j))],
        out_specs=pl.BlockSpec((MOD_ROWS, tn), lambda j: (0, j)),
        out_shape=jax.ShapeDtypeStruct((MOD_ROWS, n), F32),
        compiler_params=_cparams("arbitrary"),
        name="ada_mod",
    )(cc, ada_w, ada_b3)


def _in_proj_kernel(*refs, with_moe, cols):
    if with_moe:
        x_ref, moe_ref, gate_ref, g_ref, shift_ref, scale_ref, w_ref, o_ref, xo_ref, h_ref = refs
        x = x_ref[0] + gate_ref[0] * moe_ref[0]
        xo_ref[0] = x
    else:
        x_ref, g_ref, shift_ref, scale_ref, w_ref, o_ref, h_ref = refs
        x = x_ref[0]
    h = _rms(x, g_ref[...]) * (1.0 + scale_ref[0]) + shift_ref[0]
    h_ref[...] = h.astype(BF)
    for n0, srcs in zip(range(0, Z_COLS, 256), _z_chunk_sources()):
        if cols is None or n0 in cols:
            rows = [w_ref[s0:s0 + n, :] for s0, n in srcs]
            pad = 256 - sum(n for _, n in srcs)
            if pad:
                rows.append(jnp.zeros((pad, w_ref.shape[1]), BF))
            wt = rows[0] if len(rows) == 1 else jnp.concatenate(rows, axis=0)
            o_ref[0, :, n0:n0 + 256] = _dot_nt(h_ref[...], wt).astype(BF)
        else:
            o_ref[0, :, n0:n0 + 256] = jnp.zeros((o_ref.shape[1], 256), BF)


def _in_proj(x, moe, prev_mods3, mods3, mod_row, norm1_3, w, l, tm=512, cols=None):
    B, T, D = x.shape
    tm = min(tm, T)
    with_moe = moe is not None

    def mod_spec(k):
        return pl.BlockSpec((1, 1, D), lambda b, i: (mod_row(b) * 6 + k, 0, 0))

    x_spec = pl.BlockSpec((1, tm, D), lambda b, i: (b, i, 0))
    in_specs = [x_spec]
    args = [x]
    if with_moe:
        in_specs += [x_spec, mod_spec(5)]
        args += [moe, prev_mods3]
    in_specs += [_layer(norm1_3, l), mod_spec(0), mod_spec(1), _layer(w, l)]
    args += [norm1_3, mods3, mods3, w]
    z_spec = pl.BlockSpec((1, tm, Z_COLS), lambda b, i: (b, i, 0))
    z_shape = jax.ShapeDtypeStruct((B, T, Z_COLS), BF)
    if with_moe:
        out_specs, out_shape = [z_spec, x_spec], [z_shape, jax.ShapeDtypeStruct((B, T, D), F32)]
    else:
        out_specs, out_shape = z_spec, z_shape
    res = pl.pallas_call(
        functools.partial(_in_proj_kernel, with_moe=with_moe, cols=cols),
        grid=(B, T // tm),
        in_specs=in_specs, out_specs=out_specs, out_shape=out_shape,
        scratch_shapes=[pltpu.VMEM((tm, D), BF)],
        compiler_params=_cparams("arbitrary", "arbitrary"),
        name="in_proj",
    )(*args)
    return (res[0], res[1]) if with_moe else (res, x)


def _softmax_pv_many(heads):
    maxes = []
    for s_list, _ in heads:
        m = s_list[0].max(axis=-1, keepdims=True)
        for s in s_list[1:]:
            m = jnp.maximum(m, s.max(axis=-1, keepdims=True))
        maxes.append(m)
    probs = [[jnp.exp((s - m).astype(BF)) for s in s_list] for (s_list, _), m in zip(heads, maxes)]
    outs = []
    for (_, v_list), p_list in zip(heads, probs):
        o = _dot(p_list[0], v_list[0])
        for p, v in zip(p_list[1:], v_list[1:]):
            o = o + _dot(p, v)
        outs.append(o / pltpu.roll(o, LANES // 2, axis=1))
    return outs


def _softmax_pv_sum(heads):
    maxes = []
    for s_list, _ in heads:
        m = s_list[0].max(axis=-1, keepdims=True)
        for s in s_list[1:]:
            m = jnp.maximum(m, s.max(axis=-1, keepdims=True))
        maxes.append(m)
    probs = [[jnp.exp(s - m) for s in s_list] for (s_list, _), m in zip(heads, maxes)]
    outs = []
    for (_, v_list), p_list in zip(heads, probs):
        l = p_list[0].sum(axis=-1, keepdims=True)
        o = _dot(p_list[0].astype(BF), v_list[0])
        for p, v in zip(p_list[1:], v_list[1:]):
            l = l + p.sum(axis=-1, keepdims=True)
            o = o + _dot(p.astype(BF), v)
        outs.append(o / l)
    return outs


def _lane_masks():
    lane = lax.broadcasted_iota(jnp.int32, (1, LANES), 1)
    return lane < LANES // 2, lane >= LANES // 2


NA_ROWS_PER_STEP = 2
NA_PAIRS_PER_TRIP = 4
NA_BAND = 10
NA_CASE_ROWS = (0, 2, 4, 28, 30)


def _na_bias(rpb, rows):
    depth = rpb.shape[0]
    qc = np.arange(GRID_W)
    kj = np.arange(GRID_W)
    cs = np.clip(qc - NA_WIN_W // 2, 0, GRID_W - NA_WIN_W)
    col_ok = (kj[None, :] >= cs[:, None]) & (kj[None, :] < cs[:, None] + NA_WIN_W)
    co = kj[None, :] - qc[:, None] + (NA_WIN_W - 1)
    onehot = np.zeros((2 * NA_WIN_W - 1, GRID_W, GRID_W), np.float32)
    qi, ki = np.nonzero(col_ok)
    onehot[co[qi, ki], qi, ki] = 1.0
    toep = jnp.einsum("lhrc,cp->lhrp", rpb.astype(F32), jnp.asarray(onehot.reshape(2 * NA_WIN_W - 1, -1)),
                      precision=lax.Precision.HIGHEST)
    toep = (toep.reshape(depth, NA_HEADS, 2 * NA_WIN_H - 1, GRID_W, GRID_W)
            + jnp.asarray(np.where(col_ok, 0.0, NEG), F32))
    neg = jnp.full((depth, NA_HEADS, GRID_W, GRID_W), NEG, F32)
    cases = []
    for r0 in NA_CASE_ROWS:
        bs = int(np.clip(r0 - NA_WIN_H // 2, 0, rows - NA_BAND))
        q_rows = []
        for qr in range(NA_ROWS_PER_STEP):
            r = r0 + qr
            rs = int(np.clip(r - NA_WIN_H // 2, 0, rows - NA_WIN_H))
            blocks = []
            for i in range(NA_BAND):
                krow = bs + i
                blocks.append(toep[:, :, krow - r + NA_WIN_H - 1] if rs <= krow < rs + NA_WIN_H else neg)
            q_rows.append(jnp.concatenate(blocks, axis=-1))
        cases.append(jnp.concatenate(q_rows, axis=-2))
    return jnp.stack(cases, axis=2)


def _na_kernel(*refs, rows, ctx_out):
    if ctx_out:
        ql_ref, kl_ref, vl_ref, qc_ref, kc_ref, vc_ref, bias_ref, ol_ref, oc_ref = refs
    else:
        ql_ref, kl_ref, vl_ref, kc_ref, vc_ref, bias_ref, ol_ref = refs
    scale = NA_HEAD_DIM ** -0.5
    masks = _lane_masks()
    nq = NA_ROWS_PER_STEP * GRID_W
    nk = NA_BAND * GRID_W

    def body(it, carry):
        heads, stores = [], []
        for sub in range(NA_PAIRS_PER_TRIP):
            r0 = (it * NA_PAIRS_PER_TRIP + sub) * NA_ROWS_PER_STEP
            bs = jnp.clip(r0 - NA_WIN_H // 2, 0, rows - NA_BAND)
            case = jnp.where(r0 < 4, r0 // 2, jnp.where(r0 > rows - 6, (r0 - (rows - 4)) // 2 + 3, 2))
            q0 = pl.multiple_of(r0 * GRID_W, nq)
            k0 = pl.multiple_of(bs * GRID_W, 2 * GRID_W)
            for hp in range(NA_HEADS // 2):
                sl = slice(hp * LANES, (hp + 1) * LANES)
                q = ql_ref[0, pl.ds(q0, nq), sl] * scale
                kb = kl_ref[0, pl.ds(k0, nk), sl]
                vb = vl_ref[0, pl.ds(k0, nk), sl]
                kc = kc_ref[0, :, sl]
                vc = vc_ref[0, :, sl]
                for hh in range(2):
                    qm = jnp.where(masks[hh], q, jnp.zeros_like(q))
                    s_loc = _dot_nt(qm, kb) + bias_ref[2 * hp + hh, case]
                    s_ctx = _dot_nt(qm, kc)
                    heads.append(([s_loc, s_ctx], [vb, vc]))
                stores.append((q0, sl))
        outs = _softmax_pv_sum(heads)
        for n, (q0, sl) in enumerate(stores):
            ol_ref[0, pl.ds(q0, nq), sl] = jnp.where(masks[0], outs[2 * n], outs[2 * n + 1]).astype(BF)
        return carry

    lax.fori_loop(0, rows // (NA_ROWS_PER_STEP * NA_PAIRS_PER_TRIP), body, 0)

    if ctx_out:
        heads = []
        for hp in range(NA_HEADS // 2):
            sl = slice(hp * LANES, (hp + 1) * LANES)
            q = qc_ref[0, :, sl] * scale
            kc = kc_ref[0, :, sl]
            for hh in range(2):
                qm = jnp.where(masks[hh], q, jnp.zeros_like(q))
                heads.append(([_dot_nt(qm, kc)], [vc_ref[0, :, sl]]))
        outs = _softmax_pv_sum(heads)
        for hp in range(NA_HEADS // 2):
            sl = slice(hp * LANES, (hp + 1) * LANES)
            oc_ref[0, :, sl] = jnp.where(masks[0], outs[2 * hp], outs[2 * hp + 1]).astype(BF)


def _na_attention(z_l, z_c, bias, l, ctx_out):
    B, T, _ = z_l.shape
    L = z_c.shape[1]
    rows = T // GRID_W

    def zl(col):
        return pl.BlockSpec((1, T, 256), lambda b: (b, 0, col // 256))

    def zc(col):
        return pl.BlockSpec((1, L, 256), lambda b: (b, 0, col // 256))

    ol_spec = pl.BlockSpec((1, T, 256), lambda b: (b, 0, 0))
    ol_shape = jax.ShapeDtypeStruct((B, T, 256), BF)
    if ctx_out:
        in_specs = [zl(Z_QA), zl(Z_KA), zl(Z_VA), zc(Z_QA), zc(Z_KA), zc(Z_VA), _layer(bias, l)]
        args = [z_l, z_l, z_l, z_c, z_c, z_c, bias]
        out_specs = [ol_spec, pl.BlockSpec((1, L, 256), lambda b: (b, 0, 0))]
        out_shape = [ol_shape, jax.ShapeDtypeStruct((B, L, 256), BF)]
    else:
        in_specs = [zl(Z_QA), zl(Z_KA), zl(Z_VA), zc(Z_KA), zc(Z_VA), _layer(bias, l)]
        args = [z_l, z_l, z_l, z_c, z_c, bias]
        out_specs, out_shape = ol_spec, ol_shape
    res = pl.pallas_call(
        functools.partial(_na_kernel, rows=rows, ctx_out=ctx_out),
        grid=(B,), in_specs=in_specs, out_specs=out_specs, out_shape=out_shape,
        compiler_params=_cparams("arbitrary"),
        name="na_attn",
    )(*args)
    return (res[0], res[1]) if ctx_out else (res, None)


MLA_QK = MLA_NOPE_DIM + MLA_ROPE_DIM
MLA_W = MLA_HEADS * LANES


def _mla_proj_kernel(*refs, rope):
    if rope:
        (zm_ref, sm_ref, qn_ref, kvn_ref, wq_ref, wqs_ref, wk_ref, wv_ref, e_ref, es_ref, vone_ref, cos_ref, sin_ref,
         q_ref, k_ref, v_ref) = refs
    else:
        zm_ref, sm_ref, qn_ref, kvn_ref, wq_ref, wk_ref, wv_ref, e_ref, vone_ref, q_ref, k_ref, v_ref = refs
    zm = zm_ref[0].astype(F32)
    nq = _rms(zm[:, :MLA_Q_RANK], qn_ref[...]).astype(BF)
    nkv = _rms(zm[:, MLA_Q_RANK:], kvn_ref[...]).astype(BF)
    sm = sm_ref[0]
    scale = MLA_QK ** -0.5
    q = _dot(nq, wq_ref[...])
    k = _dot(nkv, wk_ref[...]) + _dot(sm, e_ref[...])
    if rope:
        cos = cos_ref[...]
        sin = sin_ref[...]
        q = q * cos + _dot(nq, wqs_ref[...]) * sin
        k = k * cos + _dot(sm, es_ref[...]) * sin
    q_ref[0] = (q * scale).astype(BF)
    k_ref[0] = k.astype(BF)
    v_ref[0] = (_dot(nkv, wv_ref[...]) + vone_ref[...]).astype(BF)


def _rope_swap(e):
    return np.where(e % 16 < 8, e + 8, e - 8)


def _mla_weights(w_uq, w_ukv):
    depth = w_uq.shape[0]
    quarter = MLA_ROPE_DIM // 4
    q4 = w_uq.reshape(depth, MLA_Q_RANK, MLA_HEADS, MLA_QK)
    nope, rope = q4[..., :MLA_NOPE_DIM], q4[..., MLA_NOPE_DIM:]
    rope_sw = jnp.flip(rope.reshape(depth, MLA_Q_RANK, MLA_HEADS, 2, 2, quarter), axis=4).reshape(rope.shape)
    zpad = jnp.zeros(rope.shape, w_uq.dtype)
    wq = jnp.concatenate([nope, rope, zpad], axis=-1).reshape(depth, MLA_Q_RANK, MLA_W)
    wqs = jnp.concatenate([jnp.zeros(nope.shape, w_uq.dtype), rope_sw, zpad], axis=-1).reshape(depth, MLA_Q_RANK, MLA_W)
    kv4 = w_ukv.reshape(depth, MLA_KV_RANK, MLA_HEADS, MLA_NOPE_DIM + MLA_V_DIM)
    k_nope, v = kv4[..., :MLA_NOPE_DIM], kv4[..., MLA_NOPE_DIM:]
    wk = jnp.concatenate([k_nope, jnp.zeros(k_nope.shape, w_ukv.dtype)], axis=-1).reshape(depth, MLA_KV_RANK, MLA_W)
    vz = jnp.zeros(v.shape[:2] + (1, MLA_V_DIM), w_ukv.dtype)
    wv = jnp.concatenate([jnp.concatenate([v[:, :, h:h + 1], vz] if h % 2 == 0 else [vz, v[:, :, h:h + 1]], axis=-1)
                          for h in range(MLA_HEADS)], axis=2).reshape(depth, MLA_KV_RANK, MLA_W)
    e = np.arange(MLA_ROPE_DIM)
    em = np.zeros((LANES, MLA_W), np.float32)
    ems = np.zeros((LANES, MLA_W), np.float32)
    for h in range(MLA_HEADS):
        em[SM_KR + e, h * LANES + MLA_NOPE_DIM + e] = 1.0
        ems[SM_KR + _rope_swap(e), h * LANES + MLA_NOPE_DIM + e] = 1.0
    lane = np.arange(MLA_W) % LANES
    v_ones = ((lane >= MLA_V_DIM) == ((np.arange(MLA_W) // LANES) % 2 == 0)).astype(np.float32)[None, :]
    return (wq.astype(BF), wqs.astype(BF), wk.astype(BF), wv.astype(BF), jnp.asarray(em, BF), jnp.asarray(ems, BF),
            jnp.asarray(v_ones, F32))


def _rope_tables(T):
    t = jnp.arange(T)
    quarter = MLA_ROPE_DIM // 4
    inv_freq = ROPE_THETA ** (-jnp.arange(quarter, dtype=F32) / quarter)
    ang_r = (t // GRID_W).astype(F32)[:, None] * inv_freq
    ang_c = (t % GRID_W).astype(F32)[:, None] * inv_freq
    cr, sr, ccol, scol = jnp.cos(ang_r), jnp.sin(ang_r), jnp.cos(ang_c), jnp.sin(ang_c)
    ones = jnp.ones((T, MLA_NOPE_DIM), F32)
    pad1 = jnp.ones((T, LANES - MLA_QK), F32)
    cos_h = jnp.concatenate([ones, cr, cr, ccol, ccol, pad1], axis=1)
    sin_h = jnp.concatenate([0.0 * ones, -sr, sr, -scol, scol, 0.0 * pad1], axis=1)
    return jnp.tile(cos_h, (1, MLA_HEADS)), jnp.tile(sin_h, (1, MLA_HEADS))


def _mla_proj(z, qn3, kvn3, weights, rope_tabs, l, tm=512):
    B, T, _ = z.shape
    wq, wqs, wk, wv, em, ems, v_ones = weights
    rope = rope_tabs is not None
    tm = min(tm, T)
    zm_spec = pl.BlockSpec((1, tm, 384), lambda i, b: (b, i, Z_MLA // 384))
    sm_spec = pl.BlockSpec((1, tm, LANES), lambda i, b: (b, i, Z_SM // LANES))
    if rope:
        tab_spec = pl.BlockSpec((tm, MLA_W), lambda i, b: (i, 0))
        ins = [z, z, qn3, kvn3, wq, wqs, wk, wv, em, ems, v_ones, rope_tabs[0], rope_tabs[1]]
        in_specs = ([zm_spec, sm_spec] + [_layer(a, l) for a in ins[2:8]] + [_whole(em), _whole(ems), _whole(v_ones)]
                    + [tab_spec, tab_spec])
    else:
        ins = [z, z, qn3, kvn3, wq, wk, wv, em, v_ones]
        in_specs = [zm_spec, sm_spec] + [_layer(a, l) for a in ins[2:7]] + [_whole(em), _whole(v_ones)]
    return pl.pallas_call(
        functools.partial(_mla_proj_kernel, rope=rope),
        grid=(T // tm, B), in_specs=in_specs,
        out_specs=[pl.BlockSpec((1, tm, MLA_W), lambda i, b: (b, i, 0)),
                   pl.BlockSpec((1, tm, MLA_W), lambda i, b: (b, i, 0)),
                   pl.BlockSpec((1, tm, MLA_W), lambda i, b: (b, i, 0))],
        out_shape=[jax.ShapeDtypeStruct((B, T, MLA_W), BF)] * 3,
        compiler_params=_cparams("arbitrary", "arbitrary"),
        name="mla_proj",
    )(*ins)


def _mla_attn_kernel(*refs, with_latent):
    if with_latent:
        q_ref, kc_ref, vc_ref, kl_ref, vl_ref, o_ref = refs
    else:
        q_ref, kc_ref, vc_ref, o_ref = refs
    masks = _lane_masks()
    heads = []
    for h in range(MLA_HEADS):
        hsl = slice(h * LANES, (h + 1) * LANES)
        q = q_ref[0, :, hsl]
        s_list = [_dot_nt(q, kc_ref[0, :, hsl])]
        v_list = [vc_ref[0, :, hsl]]
        if with_latent:
            s_list.append(_dot_nt(q, kl_ref[0, :, hsl]))
            v_list.append(vl_ref[0, :, hsl])
        heads.append((s_list, v_list))
    outs = _softmax_pv_many(heads)
    for hp in range(MLA_HEADS // 2):
        vsl = slice(hp * LANES, (hp + 1) * LANES)
        o_ref[0, :, vsl] = jnp.where(masks[0], outs[2 * hp], outs[2 * hp + 1]).astype(BF)


def _mla_attn(q, k_c, v_c, k_l=None, v_l=None, tq=512):
    B, Tq, W = q.shape
    tq = min(tq, Tq)
    L = k_c.shape[1]
    with_latent = k_l is not None
    in_specs = [pl.BlockSpec((1, tq, W), lambda b, i: (b, i, 0)),
                pl.BlockSpec((1, L, W), lambda b, i: (b, 0, 0)),
                pl.BlockSpec((1, L, W), lambda b, i: (b, 0, 0))]
    args = [q, k_c, v_c]
    if with_latent:
        T = k_l.shape[1]
        in_specs += [pl.BlockSpec((1, T, W), lambda b, i: (b, 0, 0)),
                     pl.BlockSpec((1, T, W), lambda b, i: (b, 0, 0))]
        args += [k_l, v_l]
    return pl.pallas_call(
        functools.partial(_mla_attn_kernel, with_latent=with_latent),
        grid=(B, Tq // tq), in_specs=in_specs,
        out_specs=pl.BlockSpec((1, tq, 256), lambda b, i: (b, i, 0)),
        out_shape=jax.ShapeDtypeStruct((B, Tq, 256), BF),
        compiler_params=_cparams("arbitrary", "arbitrary"),
        name="mla_attn",
    )(*args)


def _gla_superchunks(jobs, st_ref):
    n_chunk = SUPER // GLA_CHUNK
    masks = _lane_masks()
    rowblk = lax.broadcasted_iota(jnp.int32, (SUPER, LANES), 0) // GLA_CHUNK
    own = ((lax.broadcasted_iota(jnp.int32, (2 * GLA_DV, LANES), 0) < GLA_DV)
           == (lax.broadcasted_iota(jnp.int32, (2 * GLA_DV, LANES), 1) < GLA_DK))
    gs = [jax.nn.log_sigmoid(_dot(sm, w_a) + b_a) * (1.0 / GLA_TAU) for (_, _, _, sm, w_a, b_a, _, _, _) in jobs]
    pre = []
    for (q, k, v, sm, w_a, b_a, tri, d, reverse), g in zip(jobs, gs):
        g_hi = g.astype(BF)
        g_lo = (g - g_hi.astype(F32)).astype(BF)
        b = _dot(tri, g_hi) + _dot(tri, g_lo)
        ends = [i * GLA_CHUNK if reverse else (i + 1) * GLA_CHUNK - 1 for i in range(n_chunk)]
        tot = jnp.concatenate([jnp.broadcast_to(b[r:r + 1, :], (GLA_CHUNK, b.shape[1])) for r in ends], axis=0)
        pre.append((q.astype(F32) * (GLA_DK ** -0.5) * jnp.exp(b), k.astype(F32) * jnp.exp(-b),
                    k.astype(F32) * jnp.exp(tot - b), jnp.exp(tot), tri > 0))
    outs = [[] for _ in jobs]
    for hp in range(GLA_HEADS // 2):
        sl = slice(hp * LANES, (hp + 1) * LANES)
        stage = []
        for (q, k, v, sm, w_a, b_a, tri, d, reverse), (qd_all, kd_all, ke_all, dec_all, keep) in zip(jobs, pre):
            qd = qd_all[:, sl]
            kd = kd_all[:, sl].astype(BF)
            ke = ke_all[:, sl].astype(BF)
            zero = jnp.zeros_like(ke)
            ke_bd = jnp.concatenate([jnp.where(rowblk == i, ke, zero) for i in range(n_chunk)], axis=1)
            v_pair = v[:, hp * 2 * GLA_DV:(hp + 1) * 2 * GLA_DV]
            kv_t = _dot_tn(v_pair, ke_bd)
            a = [jnp.where(keep, _dot_nt(jnp.where(masks[hh], qd, 0.0).astype(BF), kd), 0.0).astype(BF)
                 for hh in range(2)]
            stage.append((qd, dec_all[:, sl], v_pair, kv_t, a))
        inters = []
        for (q, k, v, sm, w_a, b_a, tri, d, reverse), (qd, dec, v_pair, kv_t, a) in zip(jobs, stage):
            s = st_ref[d, hp]
            inter = [None] * n_chunk
            for i in (range(n_chunk - 1, -1, -1) if reverse else range(n_chunk)):
                rows = slice(i * GLA_CHUNK, (i + 1) * GLA_CHUNK)
                inter[i] = _dot_nt(qd[rows].astype(BF), s.astype(BF))
                s = s * dec[i * GLA_CHUNK:i * GLA_CHUNK + 1, :] + jnp.where(own, kv_t[:, i * LANES:(i + 1) * LANES], 0.0)
            st_ref[d, hp] = s
            inters.append(jnp.concatenate(inter, axis=0))
        for n, ((qd, dec, v_pair, kv_t, a), o_inter) in enumerate(zip(stage, inters)):
            for hh in range(2):
                hv = slice(hh * GLA_DV, (hh + 1) * GLA_DV)
                outs[n].append(_dot(a[hh], v_pair[:, hv]) + o_inter[:, hv])
    return [jnp.concatenate(o, axis=1) for o in outs]


def _gla_finish(o, og, onorm):
    ys = []
    for h in range(GLA_HEADS):
        ys.append(_rms(o[:, h * GLA_DV:(h + 1) * GLA_DV], onorm))
    y = jnp.concatenate(ys, axis=1)
    ogf = og.astype(F32)
    return (y * (ogf * jax.nn.sigmoid(ogf))).astype(BF)


def _gla_kernel(*refs, n_lat, ctx_out):
    (ql_ref, kl_ref, vl_ref, sml_ref, ogl_ref, qc_ref, kc_ref, vc_ref, smc_ref, ogc_ref,
     waf_ref, baf_ref, wab_ref, bab_ref, onorm_ref, trif_ref, trib_ref) = refs[:17]
    if ctx_out:
        ol_ref, oc_ref, of_ref, ob_ref, st_ref = refs[17:]
    else:
        ol_ref, of_ref, ob_ref, st_ref = refs[17:]
    onorm = onorm_ref[...]

    def both(fwd_in, bwd_in):
        return _gla_superchunks([fwd_in + (waf_ref[...], baf_ref[...], trif_ref[...], 0, False),
                                 bwd_in + (wab_ref[...], bab_ref[...], trib_ref[...], 1, True)], st_ref)

    st_ref[...] = jnp.zeros_like(st_ref)
    ctx_in = (qc_ref[0], kc_ref[0], vc_ref[0], smc_ref[0])
    o_cf, o_cb = both(ctx_in, ctx_in)
    if ctx_out:
        oc_ref[0] = _gla_finish(o_cf + o_cb, ogc_ref[0], onorm)

    def rows_of(blk):
        return pl.ds(pl.multiple_of(blk * SUPER, SUPER), SUPER)

    def scan_in(r):
        return ql_ref[0, r, :], kl_ref[0, r, :], vl_ref[0, r, :], sml_ref[0, r, :]

    def body(j, carry):
        fwd = [(0, rows_of(j * GLA_BLOCKS_PER_TRIP + u)) for u in range(GLA_BLOCKS_PER_TRIP)]
        bwd = [(1, rows_of(n_lat - 1 - (j * GLA_BLOCKS_PER_TRIP + u))) for u in range(GLA_BLOCKS_PER_TRIP)]
        jobs = [job for pair in zip(fwd, bwd) for job in pair]
        outs = _gla_superchunks(
            [scan_in(r) + ((waf_ref[...], baf_ref[...], trif_ref[...], 0, False) if d == 0 else
                           (wab_ref[...], bab_ref[...], trib_ref[...], 1, True)) for d, r in jobs], st_ref)
        for (d, r), o in zip(jobs, outs):
            (of_ref if d == 0 else ob_ref)[r, :] = o
        return carry

    lax.fori_loop(0, n_lat // GLA_BLOCKS_PER_TRIP, body, 0)

    def finish(j, carry):
        r = pl.ds(pl.multiple_of(j * SUPER, SUPER), SUPER)
        ol_ref[0, r, :] = _gla_finish(of_ref[r, :] + ob_ref[r, :], ogl_ref[0, r, :], onorm)
        return carry

    lax.fori_loop(0, n_lat, finish, 0)


def _gla_consts():
    r = np.arange(SUPER)
    same = (r[:, None] // GLA_CHUNK) == (r[None, :] // GLA_CHUNK)
    return jnp.asarray(same & (r[None, :] <= r[:, None]), BF), jnp.asarray(same & (r[None, :] >= r[:, None]), BF)


def _gla(z_l, z_c, params, l, ctx_out):
    B, T, _ = z_l.shape
    L = z_c.shape[1]
    assert L == SUPER and T % SUPER == 0
    kw = GLA_HEADS * GLA_DK
    vw = GLA_HEADS * GLA_DV
    consts = _gla_consts()

    def zspec(n, width, col):
        return pl.BlockSpec((1, n, width), lambda b: (b, 0, col // width))

    in_specs = ([zspec(T, kw, Z_GQ), zspec(T, kw, Z_GK), zspec(T, vw, Z_GV), zspec(T, LANES, Z_SM), zspec(T, vw, Z_OG),
                 zspec(L, kw, Z_GQ), zspec(L, kw, Z_GK), zspec(L, vw, Z_GV), zspec(L, LANES, Z_SM), zspec(L, vw, Z_OG)]
                + [_layer(a, l) for a in params] + [_whole(a) for a in consts])
    ol_spec = pl.BlockSpec((1, T, vw), lambda b: (b, 0, 0))
    ol_shape = jax.ShapeDtypeStruct((B, T, vw), BF)
    if ctx_out:
        out_specs = [ol_spec, pl.BlockSpec((1, L, vw), lambda b: (b, 0, 0))]
        out_shape = [ol_shape, jax.ShapeDtypeStruct((B, L, vw), BF)]
    else:
        out_specs, out_shape = ol_spec, ol_shape
    scratch = [pltpu.VMEM((T, vw), F32), pltpu.VMEM((T, vw), F32),
               pltpu.VMEM((2, GLA_HEADS // 2, 2 * GLA_DV, LANES), F32)]
    res = pl.pallas_call(
        functools.partial(_gla_kernel, n_lat=T // SUPER, ctx_out=ctx_out),
        grid=(B,), in_specs=in_specs, out_specs=out_specs, out_shape=out_shape,
        scratch_shapes=scratch,
        compiler_params=_cparams("arbitrary"),
        name="gla",
    )(*([z_l] * 5 + [z_c] * 5 + list(params) + list(consts)))
    return (res[0], res[1]) if ctx_out else (res, None)


MERGE_SUB = 256


def _merge_kernel(oa_ref, ob_ref, oc_ref, ma_ref, mb_ref, mc_ref, x_ref, gate_ref, shift_ref, scale_ref, g_ref,
                  wa_ref, wb_ref, wc_ref, wo_ref, wr_ref, xo_ref, h_ref, aff_ref, y_ref):
    tm = x_ref.shape[1]
    gate = gate_ref[0]
    subs = [slice(r0, r0 + MERGE_SUB) for r0 in range(0, tm, MERGE_SUB)]

    def branch(o_ref, m_ref, w_ref, rows, n0):
        m = m_ref[0, rows, n0:n0 + 256].astype(F32)
        return jax.nn.sigmoid(m) * _dot(o_ref[0, rows, :], w_ref[:, n0:n0 + 256])

    for n0 in range(0, D_MODEL, 256):
        for rows in subs:
            y = (branch(oa_ref, ma_ref, wa_ref, rows, n0) + branch(ob_ref, mb_ref, wb_ref, rows, n0)
                 + branch(oc_ref, mc_ref, wc_ref, rows, n0))
            y_ref[rows, n0:n0 + 256] = y.astype(BF)
    for n0 in range(0, D_MODEL, 256):
        for rows in subs:
            xo_ref[0, rows, n0:n0 + 256] = (x_ref[0, rows, n0:n0 + 256]
                                            + gate[:, n0:n0 + 256] * _dot(y_ref[rows, :], wo_ref[:, n0:n0 + 256]))
    hs = [_rms(xo_ref[0, rows, :], g_ref[...]) * (1.0 + scale_ref[0]) + shift_ref[0] for rows in subs]
    wr = wr_ref[...]
    for rows, h in zip(subs, hs):
        h_hi = h.astype(BF)
        h_lo = (h - h_hi.astype(F32)).astype(BF)
        h_ref[0, rows, :] = h_hi
        part = _dot_nt(wr, h_hi)
        logits = part[:N_EXPERTS] + part[N_EXPERTS:] + _dot_nt(wr[:N_EXPERTS], h_lo)
        e = jnp.exp(logits - logits.max(axis=0, keepdims=True))
        aff_ref[0, :, rows] = e / e.sum(axis=0, keepdims=True)


def _merge(oa, ob, oc, z, x, mods3, mod_row, norm2_3, weights, l, tm):
    B, T, D = x.shape
    wa, wb, wc, wo, wr_t = weights

    def tok(width, col=0):
        return pl.BlockSpec((1, tm, width), lambda b, i: (b, i, col // width))

    def mod_spec(k):
        return pl.BlockSpec((1, 1, D), lambda b, i: (mod_row(b) * 6 + k, 0, 0))

    in_specs = [tok(256), tok(256), tok(512), tok(D, Z_MA), tok(D, Z_MB), tok(D, Z_MC), tok(D),
                mod_spec(2), mod_spec(3), mod_spec(4), _layer(norm2_3, l)] + [_layer(w, l) for w in weights]
    return pl.pallas_call(
        _merge_kernel,
        grid=(B, T // tm), in_specs=in_specs,
        out_specs=[tok(D), tok(D), pl.BlockSpec((1, N_EXPERTS, tm), lambda b, i: (b, 0, i))],
        out_shape=[jax.ShapeDtypeStruct((B, T, D), F32), jax.ShapeDtypeStruct((B, T, D), BF),
                   jax.ShapeDtypeStruct((B, N_EXPERTS, T), F32)],
        scratch_shapes=[pltpu.VMEM((tm, D), BF)],
        compiler_params=_cparams("arbitrary", "arbitrary"),
        name="merge",
    )(oa, ob, oc, z, z, z, x, mods3, mods3, mods3, norm2_3, wa, wb, wc, wo, wr_t)


def _route_kernel(aff_ref, tri_ref, pos_ref, post_ref, *, cap, slot_stride):
    aff = aff_ref[0]
    E, T = aff.shape

    def refine(thr, shift, patterns):
        best = thr
        for c in patterns:
            cand = thr | (jnp.int32(c) << shift)
            cnt = jnp.sum((aff >= pltpu.bitcast(cand, F32)).astype(F32), axis=1, keepdims=True)
            best = jnp.where(cnt >= cap, cand, best)
        return best

    thr = lax.fori_loop(0, 15, lambda i, t: refine(t, 29 - 2 * i, (1, 2, 3)), jnp.zeros((E, 1), jnp.int32))
    thr = refine(thr, 0, (1,))
    thr_f = pltpu.bitcast(thr, F32)
    gt = aff > thr_f
    eq = aff == thr_f
    need = cap - jnp.sum(gt.astype(F32), axis=1, keepdims=True)
    tri = tri_ref[...]

    def excl_prefix(flags):
        run = jnp.zeros((E, 1), F32)
        blocks = []
        for j in range(T // LANES):
            blk = flags[:, j * LANES:(j + 1) * LANES].astype(F32)
            blocks.append(_dot(blk.astype(BF), tri) + run)
            run = run + jnp.sum(blk, axis=1, keepdims=True)
        return jnp.concatenate(blocks, axis=1)

    sel = gt | (eq & (excl_prefix(eq) < need))
    base = (slot_stride * pl.program_id(0)).astype(F32)
    pos = jnp.where(sel, excl_prefix(sel) + base, -1.0)
    pos_ref[0] = pos.astype(jnp.int32)
    post_ref[0] = jnp.concatenate([pos, jnp.full((LANES - E, T), -1.0, F32)], axis=0).T


def _route(aff_t, cap, slot_stride):
    B, E, T = aff_t.shape
    r = np.arange(LANES)
    tri = jnp.asarray(r[:, None] < r[None, :], BF)
    return pl.pallas_call(
        functools.partial(_route_kernel, cap=cap, slot_stride=slot_stride),
        grid=(B,),
        in_specs=[pl.BlockSpec((1, E, T), lambda b: (b, 0, 0)), pl.BlockSpec((LANES, LANES), lambda b: (0, 0))],
        out_specs=[pl.BlockSpec((1, E, T), lambda b: (b, 0, 0)), pl.BlockSpec((1, T, LANES), lambda b: (b, 0, 0))],
        out_shape=[jax.ShapeDtypeStruct((B, E, T), jnp.int32), jax.ShapeDtypeStruct((B, T, LANES), F32)],
        compiler_params=_cparams("arbitrary"),
        name="route",
    )(aff_t, tri)


def _moe_kernel(*refs, n_slots, final):
    if final:
        h_ref, pos_ref, post_ref, aff_ref, wg_ref, wu_ref, wd_ref, x_ref, gate_ref, g_ref, o_ref = refs
    else:
        h_ref, pos_ref, post_ref, aff_ref, wg_ref, wu_ref, wd_ref, o_ref = refs
    e = pl.program_id(1)

    @pl.when(e == 0)
    def _():
        o_ref[...] = jnp.zeros_like(o_ref)

    pos = pos_ref[0, pl.ds(e, 1), :]
    aff = aff_ref[0, pl.ds(e, 1), :]
    T = pos.shape[1]
    hit = lax.broadcasted_iota(jnp.int32, (n_slots, T), 0) == pos
    w_slot = jnp.sum(jnp.where(hit, aff, 0.0), axis=1, keepdims=True)
    xg = _dot(hit.astype(BF), h_ref[0]).astype(BF)
    gt = _dot(xg, wg_ref[...].astype(BF))
    up = _dot(xg, wu_ref[...].astype(BF))
    hid = (gt * jax.nn.sigmoid(gt) * up * w_slot).astype(BF)
    y = _dot(hid, wd_ref[...]).astype(BF)

    lane = lax.broadcasted_iota(jnp.int32, (1, LANES), 1)
    slot_row = lax.broadcasted_iota(jnp.int32, (1, n_slots), 1).astype(F32)
    for t0 in range(0, T, 256):
        pos_col = jnp.sum(jnp.where(lane == e, post_ref[0, t0:t0 + 256, :], 0.0), axis=1, keepdims=True)
        o_ref[0, t0:t0 + 256, :] += _dot((pos_col == slot_row).astype(BF), y)

    if final:
        @pl.when(e == pl.num_programs(1) - 1)
        def _():
            for t0 in range(0, T, 256):
                rows = slice(t0, t0 + 256)
                o_ref[0, rows, :] = _rms(x_ref[0, rows, :] + gate_ref[0] * o_ref[0, rows, :], g_ref[...])


def _moe(h, pos, pos_t, aff_t, weights, l, n_slots, final=None):
    B, T, D = h.shape
    E = pos.shape[1]
    wg, wu, wd = weights

    def wspec(w):
        return pl.BlockSpec((None, None) + w.shape[2:], lambda b, e: (l, e, 0, 0))

    in_specs = [pl.BlockSpec((1, T, D), lambda b, e: (b, 0, 0)),
                pl.BlockSpec((1, E, T), lambda b, e: (b, 0, 0)),
                pl.BlockSpec((1, T, LANES), lambda b, e: (b, 0, 0)),
                pl.BlockSpec((1, E, T), lambda b, e: (b, 0, 0)),
                wspec(wg), wspec(wu), wspec(wd)]
    args = [h, pos, pos_t, aff_t, wg, wu, wd]
    if final is not None:
        x, mods3, g = final
        in_specs += [pl.BlockSpec((1, T, D), lambda b, e: (b, 0, 0), pipeline_mode=pl.Buffered(1)),
                     pl.BlockSpec((1, 1, D), lambda b, e: (b * 6 + 5, 0, 0)),
                     pl.BlockSpec((1, D), lambda b, e: (0, 0))]
        args += [x, mods3, g.reshape(1, D)]
    return pl.pallas_call(
        functools.partial(_moe_kernel, n_slots=n_slots, final=final is not None),
        grid=(B, E),
        in_specs=in_specs,
        out_specs=pl.BlockSpec((1, T, D), lambda b, e: (b, 0, 0)),
        out_shape=jax.ShapeDtypeStruct((B, T, D), F32),
        compiler_params=_cparams("arbitrary", "arbitrary"),
        name="moe",
    )(*args)


def _final_kernel(x_ref, moe_ref, gate_ref, g_ref, o_ref):
    o_ref[0] = _rms(x_ref[0] + gate_ref[0] * moe_ref[0], g_ref[...])


def _final(x, moe, mods3, g, tm=512):
    B, T, D = x.shape
    tok = pl.BlockSpec((1, tm, D), lambda b, i: (b, i, 0))
    return pl.pallas_call(
        _final_kernel,
        grid=(B, T // tm),
        in_specs=[tok, tok, pl.BlockSpec((1, 1, D), lambda b, i: (b * 6 + 5, 0, 0)),
                  pl.BlockSpec((1, D), lambda b, i: (0, 0))],
        out_specs=tok,
        out_shape=jax.ShapeDtypeStruct((B, T, D), F32),
        compiler_params=_cparams("arbitrary", "arbitrary"),
        name="final_norm",
    )(x, moe, mods3, g.reshape(1, D))


def kernel(x, c, ctx, c_ctx, ada_w, ada_b, norm1, norm2, w_in, na_rpb, mla_q_norm, mla_w_uq, mla_kv_norm, mla_w_ukv,
           gla_w_af, gla_b_af, gla_w_ab, gla_b_ab, gla_o_norm, w_br_a, w_br_b, w_br_c, w_out,
           moe_router, moe_w_gate, moe_w_up, moe_w_down, final_norm):
    B, T, D = x.shape
    L = ctx.shape[1]
    depth = ada_w.shape[0]
    assert B < MOD_ROWS
    cap_l = EC_CAPACITY * T // N_EXPERTS
    cap_c = EC_CAPACITY * L // N_EXPERTS
    ctx_row = B
    cc = jnp.concatenate([c, c_ctx[None], jnp.zeros((MOD_ROWS - B - 1, D), F32)], axis=0)

    def lat_row(b):
        return b

    def ctx_row_fn(b):
        return ctx_row

    def rows3(a):
        return a.reshape(depth, 1, -1)

    ada_b3, norm1_3, norm2_3 = rows3(ada_b), rows3(norm1), rows3(norm2)
    w_in_t = jnp.swapaxes(w_in, 1, 2).astype(BF)
    na_bias = _na_bias(na_rpb, T // GRID_W)
    mla_w = _mla_weights(mla_w_uq, mla_w_ukv)
    qn3, kvn3 = rows3(mla_q_norm), rows3(mla_kv_norm)
    rope_tabs = _rope_tables(T)
    gate_pad = lambda w, off: jnp.pad(w, ((0, 0), (off, LANES - off - GLA_GATE_RANK), (0, 0))).astype(BF)
    gla_params = (gate_pad(gla_w_af, SM_AF), rows3(gla_b_af), gate_pad(gla_w_ab, SM_AB), rows3(gla_b_ab),
                  rows3(gla_o_norm))
    wr_t = jnp.swapaxes(moe_router, 1, 2)
    wr_hi = wr_t.astype(BF)
    wr_lo = (wr_t - wr_hi.astype(F32)).astype(BF)
    merge_w = (w_br_a.astype(BF), w_br_b.astype(BF), w_br_c.astype(BF), w_out.astype(BF),
               jnp.concatenate([wr_hi, wr_lo], axis=1))
    moe_w = (moe_w_gate, moe_w_up, moe_w_down.astype(BF))

    moe_l = moe_c = mods3 = None
    for l in range(depth):
        ctx_out = l < depth - 1
        prev_mods3 = mods3
        mods3 = _ada(cc, ada_w, ada_b3, l).reshape(MOD_ROWS * 6, 1, D)
        z_l, x = _in_proj(x, moe_l, prev_mods3, mods3, lat_row, norm1_3, w_in_t, l)
        z_c, ctx = _in_proj(ctx, moe_c, prev_mods3, mods3, ctx_row_fn, norm1_3, w_in_t, l,
                            cols=None if ctx_out else CTX_KEY_COLS)

        oa_l, oa_c = _na_attention(z_l, z_c, na_bias, l, ctx_out)

        qb_l, kb_l, vb_l = _mla_proj(z_l, qn3, kvn3, mla_w, rope_tabs, l)
        qb_c, kb_c, vb_c = _mla_proj(z_c, qn3, kvn3, mla_w, None, l)
        ob_l = _mla_attn(qb_l, kb_c, vb_c, kb_l, vb_l)

        oc_l, oc_c = _gla(z_l, z_c, gla_params, l, ctx_out)

        x, h_l, aff_l = _merge(oa_l, ob_l, oc_l, z_l, x, mods3, lat_row, norm2_3, merge_w, l, tm=1024)
        pos_l, post_l = _route(aff_l, cap_l, 0)
        moe_l = _moe(h_l, pos_l, post_l, aff_l, moe_w, l, cap_l)

        if ctx_out:
            ob_c = _mla_attn(qb_c, kb_c, vb_c)
            ctx, h_c, aff_c = _merge(oa_c, ob_c, oc_c, z_c, ctx, mods3, ctx_row_fn, norm2_3, merge_w, l, tm=256)
            pos_c, post_c = _route(aff_c, cap_c, cap_c)
            flat = lambda a: jnp.swapaxes(a, 0, 1).reshape(1, N_EXPERTS, B * L)
            moe_c = _moe(h_c.reshape(1, B * L, D), flat(pos_c), post_c.reshape(1, B * L, LANES), flat(aff_c),
                         moe_w, l, B * cap_c).reshape(B, L, D)
    return _final(x, moe_l, mods3, final_norm)
```

```python
import functools

import numpy as np
import jax
import jax.numpy as jnp
from jax import lax
from jax.experimental import pallas as pl
from jax.experimental.pallas import tpu as pltpu

BF = jnp.bfloat16
F32 = jnp.float32

D_MODEL = 1024
GRID_W = 64
EPS = 1e-6
NA_HEADS = 4
NA_HEAD_DIM = 64
NA_WIN_H = 8
NA_WIN_W = 16
MLA_HEADS = 4
MLA_Q_RANK = 256
MLA_KV_RANK = 128
MLA_NOPE_DIM = 64
MLA_ROPE_DIM = 32
MLA_V_DIM = 64
ROPE_THETA = 10000.0
GLA_HEADS = 4
GLA_DK = 64
GLA_DV = 128
GLA_GATE_RANK = 16
GLA_TAU = 16.0
GLA_CHUNK = 64
N_EXPERTS = 16
EXPERT_FF = 1024
EC_CAPACITY = 2
IN_SIZES = (256, 256, 256, 256, 128, 32, 256, 256, 512, 16, 16, 512, 1024, 1024, 1024)

LANES = 128
VMEM_LIMIT = 56 * 1024 * 1024
MOD_ROWS = 16

Z_MA, Z_MB, Z_MC = 0, 1024, 2048
Z_GV, Z_OG = 3072, 3584
Z_QA, Z_KA, Z_VA = 4096, 4352, 4608
Z_GQ, Z_GK = 4864, 5120
Z_MLA = 5376
Z_SM = 5760
Z_COLS = 5888
SM_KR, SM_AF, SM_AB = 0, 32, 48

CTX_KEY_COLS = (Z_GV, Z_GV + 256, Z_KA, Z_VA, Z_GK, Z_MLA, Z_MLA + 256)

IN_NAMES = ("qa", "ka", "va", "dq", "dkv", "kr", "gq", "gk", "gv", "a_f", "a_b", "og", "m_a", "m_b", "m_c")
Z_ORDER = ("m_a", "m_b", "m_c", "gv", "og", "qa", "ka", "va", "gq", "gk", "dq", "dkv", "kr", "a_f", "a_b")


def _z_chunk_sources(width=256):
    start, off = {}, 0
    for name, size in zip(IN_NAMES, IN_SIZES):
        start[name] = off
        off += size
    pieces = []
    z = 0
    for name in Z_ORDER:
        s0, n = start[name], IN_SIZES[IN_NAMES.index(name)]
        while n:
            take = min(n, width - z % width)
            pieces.append((z, s0, take))
            z, s0, n = z + take, s0 + take, n - take
    chunks = [[] for _ in range(Z_COLS // width)]
    for zc, s0, n in pieces:
        if chunks[zc // width] and sum(chunks[zc // width][-1]) == s0:
            chunks[zc // width][-1] = (chunks[zc // width][-1][0], chunks[zc // width][-1][1] + n)
        else:
            chunks[zc // width].append((s0, n))
    return chunks


SUPER = 256
GLA_BLOCKS_PER_TRIP = 2
NEG = -1e30


def _cparams(*sem):
    return pltpu.CompilerParams(dimension_semantics=sem, vmem_limit_bytes=VMEM_LIMIT)


def _layer(a, l):
    return pl.BlockSpec((None,) + a.shape[1:], lambda *_: (l,) + (0,) * (a.ndim - 1), pipeline_mode=pl.Buffered(1))


def _whole(a):
    return pl.BlockSpec(a.shape, lambda *_: (0,) * a.ndim, pipeline_mode=pl.Buffered(1))


def _dot(a, b):
    return jnp.dot(a, b, preferred_element_type=F32)


def _dot_nt(a, b):
    return lax.dot_general(a, b, (((1,), (1,)), ((), ())), preferred_element_type=F32)


def _dot_tn(a, b):
    return lax.dot_general(a, b, (((0,), (0,)), ((), ())), preferred_element_type=F32)


def _rms(x, g):
    return x * lax.rsqrt(jnp.mean(x * x, axis=-1, keepdims=True) + EPS) * g


def _ada_kernel(c_ref, w_ref, b_ref, o_ref):
    c = c_ref[...]
    a = (c * jax.nn.sigmoid(c)).astype(BF)
    o_ref[...] = _dot(a, w_ref[...].astype(BF)) + b_ref[...]


def _ada(cc, ada_w, ada_b3, l):
    n = ada_w.shape[2]
    tn = 1536
    return pl.pallas_call(
        _ada_kernel,
        grid=(n // tn,),
        in_specs=[pl.BlockSpec((MOD_ROWS, D_MODEL), lambda j: (0, 0)),
                  pl.BlockSpec((None, D_MODEL, tn), lambda j: (l, 0, j)),
                  pl.BlockSpec((None, 1, tn), lambda j: (l, 0, j))],
        out_specs=pl.BlockSpec((MOD_ROWS, tn), lambda j: (0, j)),
        out_shape=jax.ShapeDtypeStruct((MOD_ROWS, n), F32),
        compiler_params=_cparams("arbitrary"),
        name="ada_mod",
    )(cc, ada_w, ada_b3)


def _in_proj_kernel(*refs, with_moe, cols):
    if with_moe:
        x_ref, moe_ref, gate_ref, g_ref, shift_ref, scale_ref, w_ref, o_ref, xo_ref, h_ref = refs
        x = x_ref[0] + gate_ref[0] * moe_ref[0]
        xo_ref[0] = x
    else:
        x_ref, g_ref, shift_ref, scale_ref, w_ref, o_ref, h_ref = refs
        x = x_ref[0]
    h = _rms(x, g_ref[...]) * (1.0 + scale_ref[0]) + shift_ref[0]
    h_ref[...] = h.astype(BF)
    for n0, srcs in zip(range(0, Z_COLS, 256), _z_chunk_sources()):
        if cols is None or n0 in cols:
            rows = [w_ref[s0:s0 + n, :] for s0, n in srcs]
            pad = 256 - sum(n for _, n in srcs)
            if pad:
                rows.append(jnp.zeros((pad, w_ref.shape[1]), BF))
            wt = rows[0] if len(rows) == 1 else jnp.concatenate(rows, axis=0)
            o_ref[0, :, n0:n0 + 256] = _dot_nt(h_ref[...], wt).astype(BF)
        else:
            o_ref[0, :, n0:n0 + 256] = jnp.zeros((o_ref.shape[1], 256), BF)


def _in_proj(x, moe, prev_mods3, mods3, mod_row, norm1_3, w, l, tm=512, cols=None):
    B, T, D = x.shape
    tm = min(tm, T)
    with_moe = moe is not None

    def mod_spec(k):
        return pl.BlockSpec((1, 1, D), lambda b, i: (mod_row(b) * 6 + k, 0, 0))

    x_spec = pl.BlockSpec((1, tm, D), lambda b, i: (b, i, 0))
    in_specs = [x_spec]
    args = [x]
    if with_moe:
        in_specs += [x_spec, mod_spec(5)]
        args += [moe, prev_mods3]
    in_specs += [_layer(norm1_3, l), mod_spec(0), mod_spec(1), _layer(w, l)]
    args += [norm1_3, mods3, mods3, w]
    z_spec = pl.BlockSpec((1, tm, Z_COLS), lambda b, i: (b, i, 0))
    z_shape = jax.ShapeDtypeStruct((B, T, Z_COLS), BF)
    if with_moe:
        out_specs, out_shape = [z_spec, x_spec], [z_shape, jax.ShapeDtypeStruct((B, T, D), F32)]
    else:
        out_specs, out_shape = z_spec, z_shape
    res = pl.pallas_call(
        functools.partial(_in_proj_kernel, with_moe=with_moe, cols=cols),
        grid=(B, T // tm),
        in_specs=in_specs, out_specs=out_specs, out_shape=out_shape,
        scratch_shapes=[pltpu.VMEM((tm, D), BF)],
        compiler_params=_cparams("arbitrary", "arbitrary"),
        name="in_proj",
    )(*args)
    return (res[0], res[1]) if with_moe else (res, x)


def _softmax_pv_many(heads):
    maxes = []
    for s_list, _ in heads:
        m = s_list[0].max(axis=-1, keepdims=True)
        for s in s_list[1:]:
            m = jnp.maximum(m, s.max(axis=-1, keepdims=True))
        maxes.append(m)
    probs = [[jnp.exp((s - m).astype(BF)) for s in s_list] for (s_list, _), m in zip(heads, maxes)]
    outs = []
    for (_, v_list), p_list in zip(heads, probs):
        o = _dot(p_list[0], v_list[0])
        for p, v in zip(p_list[1:], v_list[1:]):
            o = o + _dot(p, v)
        outs.append(o / pltpu.roll(o, LANES // 2, axis=1))
    return outs


def _softmax_pv_sum(heads):
    maxes = []
    for s_list, _ in heads:
        m = s_list[0].max(axis=-1, keepdims=True)
        for s in s_list[1:]:
            m = jnp.maximum(m, s.max(axis=-1, keepdims=True))
        maxes.append(m)
    probs = [[jnp.exp(s - m) for s in s_list] for (s_list, _), m in zip(heads, maxes)]
    outs = []
    for (_, v_list), p_list in zip(heads, probs):
        l = p_list[0].sum(axis=-1, keepdims=True)
        o = _dot(p_list[0].astype(BF), v_list[0])
        for p, v in zip(p_list[1:], v_list[1:]):
            l = l + p.sum(axis=-1, keepdims=True)
            o = o + _dot(p.astype(BF), v)
        outs.append(o / l)
    return outs


def _lane_masks():
    lane = lax.broadcasted_iota(jnp.int32, (1, LANES), 1)
    return lane < LANES // 2, lane >= LANES // 2


NA_ROWS_PER_STEP = 2
NA_PAIRS_PER_TRIP = 4
NA_BAND = 10
NA_CASE_ROWS = (0, 2, 4, 28, 30)


def _na_bias(rpb, rows):
    depth = rpb.shape[0]
    qc = np.arange(GRID_W)
    kj = np.arange(GRID_W)
    cs = np.clip(qc - NA_WIN_W // 2, 0, GRID_W - NA_WIN_W)
    col_ok = (kj[None, :] >= cs[:, None]) & (kj[None, :] < cs[:, None] + NA_WIN_W)
    co = kj[None, :] - qc[:, None] + (NA_WIN_W - 1)
    onehot = np.zeros((2 * NA_WIN_W - 1, GRID_W, GRID_W), np.float32)
    qi, ki = np.nonzero(col_ok)
    onehot[co[qi, ki], qi, ki] = 1.0
    toep = jnp.einsum("lhrc,cp->lhrp", rpb.astype(F32), jnp.asarray(onehot.reshape(2 * NA_WIN_W - 1, -1)),
                      precision=lax.Precision.HIGHEST)
    toep = (toep.reshape(depth, NA_HEADS, 2 * NA_WIN_H - 1, GRID_W, GRID_W)
            + jnp.asarray(np.where(col_ok, 0.0, NEG), F32))
    neg = jnp.full((depth, NA_HEADS, GRID_W, GRID_W), NEG, F32)
    cases = []
    for r0 in NA_CASE_ROWS:
        bs = int(np.clip(r0 - NA_WIN_H // 2, 0, rows - NA_BAND))
        q_rows = []
        for qr in range(NA_ROWS_PER_STEP):
            r = r0 + qr
            rs = int(np.clip(r - NA_WIN_H // 2, 0, rows - NA_WIN_H))
            blocks = []
            for i in range(NA_BAND):
                krow = bs + i
                blocks.append(toep[:, :, krow - r + NA_WIN_H - 1] if rs <= krow < rs + NA_WIN_H else neg)
            q_rows.append(jnp.concatenate(blocks, axis=-1))
        cases.append(jnp.concatenate(q_rows, axis=-2))
    return jnp.stack(cases, axis=2)


def _na_kernel(*refs, rows, ctx_out):
    if ctx_out:
        ql_ref, kl_ref, vl_ref, qc_ref, kc_ref, vc_ref, bias_ref, ol_ref, oc_ref = refs
    else:
        ql_ref, kl_ref, vl_ref, kc_ref, vc_ref, bias_ref, ol_ref = refs
    scale = NA_HEAD_DIM ** -0.5
    masks = _lane_masks()
    nq = NA_ROWS_PER_STEP * GRID_W
    nk = NA_BAND * GRID_W

    def body(it, carry):
        heads, stores = [], []
        for sub in range(NA_PAIRS_PER_TRIP):
            r0 = (it * NA_PAIRS_PER_TRIP + sub) * NA_ROWS_PER_STEP
            bs = jnp.clip(r0 - NA_WIN_H // 2, 0, rows - NA_BAND)
            case = jnp.where(r0 < 4, r0 // 2, jnp.where(r0 > rows - 6, (r0 - (rows - 4)) // 2 + 3, 2))
            q0 = pl.multiple_of(r0 * GRID_W, nq)
            k0 = pl.multiple_of(bs * GRID_W, 2 * GRID_W)
            for hp in range(NA_HEADS // 2):
                sl = slice(hp * LANES, (hp + 1) * LANES)
                q = ql_ref[0, pl.ds(q0, nq), sl] * scale
                kb = kl_ref[0, pl.ds(k0, nk), sl]
                vb = vl_ref[0, pl.ds(k0, nk), sl]
                kc = kc_ref[0, :, sl]
                vc = vc_ref[0, :, sl]
                for hh in range(2):
                    qm = jnp.where(masks[hh], q, jnp.zeros_like(q))
                    s_loc = _dot_nt(qm, kb) + bias_ref[2 * hp + hh, case]
                    s_ctx = _dot_nt(qm, kc)
                    heads.append(([s_loc, s_ctx], [vb, vc]))
                stores.append((q0, sl))
        outs = _softmax_pv_sum(heads)
        for n, (q0, sl) in enumerate(stores):
            ol_ref[0, pl.ds(q0, nq), sl] = jnp.where(masks[0], outs[2 * n], outs[2 * n + 1]).astype(BF)
        return carry

    lax.fori_loop(0, rows // (NA_ROWS_PER_STEP * NA_PAIRS_PER_TRIP), body, 0)

    if ctx_out:
        heads = []
        for hp in range(NA_HEADS // 2):
            sl = slice(hp * LANES, (hp + 1) * LANES)
            q = qc_ref[0, :, sl] * scale
            kc = kc_ref[0, :, sl]
            for hh in range(2):
                qm = jnp.where(masks[hh], q, jnp.zeros_like(q))
                heads.append(([_dot_nt(qm, kc)], [vc_ref[0, :, sl]]))
        outs = _softmax_pv_sum(heads)
        for hp in range(NA_HEADS // 2):
            sl = slice(hp * LANES, (hp + 1) * LANES)
            oc_ref[0, :, sl] = jnp.where(masks[0], outs[2 * hp], outs[2 * hp + 1]).astype(BF)


def _na_attention(z_l, z_c, bias, l, ctx_out):
    B, T, _ = z_l.shape
    L = z_c.shape[1]
    rows = T // GRID_W

    def zl(col):
        return pl.BlockSpec((1, T, 256), lambda b: (b, 0, col // 256))

    def zc(col):
        return pl.BlockSpec((1, L, 256), lambda b: (b, 0, col // 256))

    ol_spec = pl.BlockSpec((1, T, 256), lambda b: (b, 0, 0))
    ol_shape = jax.ShapeDtypeStruct((B, T, 256), BF)
    if ctx_out:
        in_specs = [zl(Z_QA), zl(Z_KA), zl(Z_VA), zc(Z_QA), zc(Z_KA), zc(Z_VA), _layer(bias, l)]
        args = [z_l, z_l, z_l, z_c, z_c, z_c, bias]
        out_specs = [ol_spec, pl.BlockSpec((1, L, 256), lambda b: (b, 0, 0))]
        out_shape = [ol_shape, jax.ShapeDtypeStruct((B, L, 256), BF)]
    else:
        in_specs = [zl(Z_QA), zl(Z_KA), zl(Z_VA), zc(Z_KA), zc(Z_VA), _layer(bias, l)]
        args = [z_l, z_l, z_l, z_c, z_c, bias]
        out_specs, out_shape = ol_spec, ol_shape
    res = pl.pallas_call(
        functools.partial(_na_kernel, rows=rows, ctx_out=ctx_out),
        grid=(B,), in_specs=in_specs, out_specs=out_specs, out_shape=out_shape,
        compiler_params=_cparams("arbitrary"),
        name="na_attn",
    )(*args)
    return (res[0], res[1]) if ctx_out else (res, None)


MLA_QK = MLA_NOPE_DIM + MLA_ROPE_DIM
MLA_W = MLA_HEADS * LANES


def _mla_proj_kernel(*refs, rope):
    if rope:
        (zm_ref, sm_ref, qn_ref, kvn_ref, wq_ref, wqs_ref, wk_ref, wv_ref, e_ref, es_ref, vone_ref, cos_ref, sin_ref,
         q_ref, k_ref, v_ref) = refs
    else:
        zm_ref, sm_ref, qn_ref, kvn_ref, wq_ref, wk_ref, wv_ref, e_ref, vone_ref, q_ref, k_ref, v_ref = refs
    zm = zm_ref[0].astype(F32)
    nq = _rms(zm[:, :MLA_Q_RANK], qn_ref[...]).astype(BF)
    nkv = _rms(zm[:, MLA_Q_RANK:], kvn_ref[...]).astype(BF)
    sm = sm_ref[0]
    scale = MLA_QK ** -0.5
    q = _dot(nq, wq_ref[...])
    k = _dot(nkv, wk_ref[...]) + _dot(sm, e_ref[...])
    if rope:
        cos = cos_ref[...]
        sin = sin_ref[...]
        q = q * cos + _dot(nq, wqs_ref[...]) * sin
        k = k * cos + _dot(sm, es_ref[...]) * sin
    q_ref[0] = (q * scale).astype(BF)
    k_ref[0] = k.astype(BF)
    v_ref[0] = (_dot(nkv, wv_ref[...]) + vone_ref[...]).astype(BF)


def _rope_swap(e):
    return np.where(e % 16 < 8, e + 8, e - 8)


def _mla_weights(w_uq, w_ukv):
    depth = w_uq.shape[0]
    quarter = MLA_ROPE_DIM // 4
    q4 = w_uq.reshape(depth, MLA_Q_RANK, MLA_HEADS, MLA_QK)
    nope, rope = q4[..., :MLA_NOPE_DIM], q4[..., MLA_NOPE_DIM:]
    rope_sw = jnp.flip(rope.reshape(depth, MLA_Q_RANK, MLA_HEADS, 2, 2, quarter), axis=4).reshape(rope.shape)
    zpad = jnp.zeros(rope.shape, w_uq.dtype)
    wq = jnp.concatenate([nope, rope, zpad], axis=-1).reshape(depth, MLA_Q_RANK, MLA_W)
    wqs = jnp.concatenate([jnp.zeros(nope.shape, w_uq.dtype), rope_sw, zpad], axis=-1).reshape(depth, MLA_Q_RANK, MLA_W)
    kv4 = w_ukv.reshape(depth, MLA_KV_RANK, MLA_HEADS, MLA_NOPE_DIM + MLA_V_DIM)
    k_nope, v = kv4[..., :MLA_NOPE_DIM], kv4[..., MLA_NOPE_DIM:]
    wk = jnp.concatenate([k_nope, jnp.zeros(k_nope.shape, w_ukv.dtype)], axis=-1).reshape(depth, MLA_KV_RANK, MLA_W)
    vz = jnp.zeros(v.shape[:2] + (1, MLA_V_DIM), w_ukv.dtype)
    wv = jnp.concatenate([jnp.concatenate([v[:, :, h:h + 1], vz] if h % 2 == 0 else [vz, v[:, :, h:h + 1]], axis=-1)
                          for h in range(MLA_HEADS)], axis=2).reshape(depth, MLA_KV_RANK, MLA_W)
    e = np.arange(MLA_ROPE_DIM)
    em = np.zeros((LANES, MLA_W), np.float32)
    ems = np.zeros((LANES, MLA_W), np.float32)
    for h in range(MLA_HEADS):
        em[SM_KR + e, h * LANES + MLA_NOPE_DIM + e] = 1.0
        ems[SM_KR + _rope_swap(e), h * LANES + MLA_NOPE_DIM + e] = 1.0
    lane = np.arange(MLA_W) % LANES
    v_ones = ((lane >= MLA_V_DIM) == ((np.arange(MLA_W) // LANES) % 2 == 0)).astype(np.float32)[None, :]
    return (wq.astype(BF), wqs.astype(BF), wk.astype(BF), wv.astype(BF), jnp.asarray(em, BF), jnp.asarray(ems, BF),
            jnp.asarray(v_ones, F32))


def _rope_tables(T):
    t = jnp.arange(T)
    quarter = MLA_ROPE_DIM // 4
    inv_freq = ROPE_THETA ** (-jnp.arange(quarter, dtype=F32) / quarter)
    ang_r = (t // GRID_W).astype(F32)[:, None] * inv_freq
    ang_c = (t % GRID_W).astype(F32)[:, None] * inv_freq
    cr, sr, ccol, scol = jnp.cos(ang_r), jnp.sin(ang_r), jnp.cos(ang_c), jnp.sin(ang_c)
    ones = jnp.ones((T, MLA_NOPE_DIM), F32)
    pad1 = jnp.ones((T, LANES - MLA_QK), F32)
    cos_h = jnp.concatenate([ones, cr, cr, ccol, ccol, pad1], axis=1)
    sin_h = jnp.concatenate([0.0 * ones, -sr, sr, -scol, scol, 0.0 * pad1], axis=1)
    return jnp.tile(cos_h, (1, MLA_HEADS)), jnp.tile(sin_h, (1, MLA_HEADS))


def _mla_proj(z, qn3, kvn3, weights, rope_tabs, l, tm=512):
    B, T, _ = z.shape
    wq, wqs, wk, wv, em, ems, v_ones = weights
    rope = rope_tabs is not None
    tm = min(tm, T)
    zm_spec = pl.BlockSpec((1, tm, 384), lambda i, b: (b, i, Z_MLA // 384))
    sm_spec = pl.BlockSpec((1, tm, LANES), lambda i, b: (b, i, Z_SM // LANES))
    if rope:
        tab_spec = pl.BlockSpec((tm, MLA_W), lambda i, b: (i, 0))
        ins = [z, z, qn3, kvn3, wq, wqs, wk, wv, em, ems, v_ones, rope_tabs[0], rope_tabs[1]]
        in_specs = ([zm_spec, sm_spec] + [_layer(a, l) for a in ins[2:8]] + [_whole(em), _whole(ems), _whole(v_ones)]
                    + [tab_spec, tab_spec])
    else:
        ins = [z, z, qn3, kvn3, wq, wk, wv, em, v_ones]
        in_specs = [zm_spec, sm_spec] + [_layer(a, l) for a in ins[2:7]] + [_whole(em), _whole(v_ones)]
    return pl.pallas_call(
        functools.partial(_mla_proj_kernel, rope=rope),
        grid=(T // tm, B), in_specs=in_specs,
        out_specs=[pl.BlockSpec((1, tm, MLA_W), lambda i, b: (b, i, 0)),
                   pl.BlockSpec((1, tm, MLA_W), lambda i, b: (b, i, 0)),
                   pl.BlockSpec((1, tm, MLA_W), lambda i, b: (b, i, 0))],
        out_shape=[jax.ShapeDtypeStruct((B, T, MLA_W), BF)] * 3,
        compiler_params=_cparams("arbitrary", "arbitrary"),
        name="mla_proj",
    )(*ins)


def _mla_attn_kernel(*refs, with_latent):
    if with_latent:
        q_ref, kc_ref, vc_ref, kl_ref, vl_ref, o_ref = refs
    else:
        q_ref, kc_ref, vc_ref, o_ref = refs
    masks = _lane_masks()
    heads = []
    for h in range(MLA_HEADS):
        hsl = slice(h * LANES, (h + 1) * LANES)
        q = q_ref[0, :, hsl]
        s_list = [_dot_nt(q, kc_ref[0, :, hsl])]
        v_list = [vc_ref[0, :, hsl]]
        if with_latent:
            s_list.append(_dot_nt(q, kl_ref[0, :, hsl]))
            v_list.append(vl_ref[0, :, hsl])
        heads.append((s_list, v_list))
    outs = _softmax_pv_many(heads)
    for hp in range(MLA_HEADS // 2):
        vsl = slice(hp * LANES, (hp + 1) * LANES)
        o_ref[0, :, vsl] = jnp.where(masks[0], outs[2 * hp], outs[2 * hp + 1]).astype(BF)


def _mla_attn(q, k_c, v_c, k_l=None, v_l=None, tq=512):
    B, Tq, W = q.shape
    tq = min(tq, Tq)
    L = k_c.shape[1]
    with_latent = k_l is not None
    in_specs = [pl.BlockSpec((1, tq, W), lambda b, i: (b, i, 0)),
                pl.BlockSpec((1, L, W), lambda b, i: (b, 0, 0)),
                pl.BlockSpec((1, L, W), lambda b, i: (b, 0, 0))]
    args = [q, k_c, v_c]
    if with_latent:
        T = k_l.shape[1]
        in_specs += [pl.BlockSpec((1, T, W), lambda b, i: (b, 0, 0)),
                     pl.BlockSpec((1, T, W), lambda b, i: (b, 0, 0))]
        args += [k_l, v_l]
    return pl.pallas_call(
        functools.partial(_mla_attn_kernel, with_latent=with_latent),
        grid=(B, Tq // tq), in_specs=in_specs,
        out_specs=pl.BlockSpec((1, tq, 256), lambda b, i: (b, i, 0)),
        out_shape=jax.ShapeDtypeStruct((B, Tq, 256), BF),
        compiler_params=_cparams("arbitrary", "arbitrary"),
        name="mla_attn",
    )(*args)


def _gla_superchunks(jobs, st_ref):
    n_chunk = SUPER // GLA_CHUNK
    masks = _lane_masks()
    rowblk = lax.broadcasted_iota(jnp.int32, (SUPER, LANES), 0) // GLA_CHUNK
    own = ((lax.broadcasted_iota(jnp.int32, (2 * GLA_DV, LANES), 0) < GLA_DV)
           == (lax.broadcasted_iota(jnp.int32, (2 * GLA_DV, LANES), 1) < GLA_DK))
    gs = [jax.nn.log_sigmoid(_dot(sm, w_a) + b_a) * (1.0 / GLA_TAU) for (_, _, _, sm, w_a, b_a, _, _, _) in jobs]
    pre = []
    for (q, k, v, sm, w_a, b_a, tri, d, reverse), g in zip(jobs, gs):
        g_hi = g.astype(BF)
        g_lo = (g - g_hi.astype(F32)).astype(BF)
        b = _dot(tri, g_hi) + _dot(tri, g_lo)
        ends = [i * GLA_CHUNK if reverse else (i + 1) * GLA_CHUNK - 1 for i in range(n_chunk)]
        tot = jnp.concatenate([jnp.broadcast_to(b[r:r + 1, :], (GLA_CHUNK, b.shape[1])) for r in ends], axis=0)
        pre.append((q.astype(F32) * (GLA_DK ** -0.5) * jnp.exp(b), k.astype(F32) * jnp.exp(-b),
                    k.astype(F32) * jnp.exp(tot - b), jnp.exp(tot), tri > 0))
    outs = [[] for _ in jobs]
    for hp in range(GLA_HEADS // 2):
        sl = slice(hp * LANES, (hp + 1) * LANES)
        stage = []
        for (q, k, v, sm, w_a, b_a, tri, d, reverse), (qd_all, kd_all, ke_all, dec_all, keep) in zip(jobs, pre):
            qd = qd_all[:, sl]
            kd = kd_all[:, sl].astype(BF)
            ke = ke_all[:, sl].astype(BF)
            zero = jnp.zeros_like(ke)
            ke_bd = jnp.concatenate([jnp.where(rowblk == i, ke, zero) for i in range(n_chunk)], axis=1)
            v_pair = v[:, hp * 2 * GLA_DV:(hp + 1) * 2 * GLA_DV]
            kv_t = _dot_tn(v_pair, ke_bd)
            a = [jnp.where(keep, _dot_nt(jnp.where(masks[hh], qd, 0.0).astype(BF), kd), 0.0).astype(BF)
                 for hh in range(2)]
            stage.append((qd, dec_all[:, sl], v_pair, kv_t, a))
        inters = []
        for (q, k, v, sm, w_a, b_a, tri, d, reverse), (qd, dec, v_pair, kv_t, a) in zip(jobs, stage):
            s = st_ref[d, hp]
            inter = [None] * n_chunk
            for i in (range(n_chunk - 1, -1, -1) if reverse else range(n_chunk)):
                rows = slice(i * GLA_CHUNK, (i + 1) * GLA_CHUNK)
                inter[i] = _dot_nt(qd[rows].astype(BF), s.astype(BF))
                s = s * dec[i * GLA_CHUNK:i * GLA_CHUNK + 1, :] + jnp.where(own, kv_t[:, i * LANES:(i + 1) * LANES], 0.0)
            st_ref[d, hp] = s
            inters.append(jnp.concatenate(inter, axis=0))
        for n, ((qd, dec, v_pair, kv_t, a), o_inter) in enumerate(zip(stage, inters)):
            for hh in range(2):
                hv = slice(hh * GLA_DV, (hh + 1) * GLA_DV)
                outs[n].append(_dot(a[hh], v_pair[:, hv]) + o_inter[:, hv])
    return [jnp.concatenate(o, axis=1) for o in outs]


def _gla_finish(o, og, onorm):
    ys = []
    for h in range(GLA_HEADS):
        ys.append(_rms(o[:, h * GLA_DV:(h + 1) * GLA_DV], onorm))
    y = jnp.concatenate(ys, axis=1)
    ogf = og.astype(F32)
    return (y * (ogf * jax.nn.sigmoid(ogf))).astype(BF)


def _gla_kernel(*refs, n_lat, ctx_out):
    (ql_ref, kl_ref, vl_ref, sml_ref, ogl_ref, qc_ref, kc_ref, vc_ref, smc_ref, ogc_ref,
     waf_ref, baf_ref, wab_ref, bab_ref, onorm_ref, trif_ref, trib_ref) = refs[:17]
    if ctx_out:
        ol_ref, oc_ref, of_ref, ob_ref, st_ref = refs[17:]
    else:
        ol_ref, of_ref, ob_ref, st_ref = refs[17:]
    onorm = onorm_ref[...]

    def both(fwd_in, bwd_in):
        return _gla_superchunks([fwd_in + (waf_ref[...], baf_ref[...], trif_ref[...], 0, False),
                                 bwd_in + (wab_ref[...], bab_ref[...], trib_ref[...], 1, True)], st_ref)

    st_ref[...] = jnp.zeros_like(st_ref)
    ctx_in = (qc_ref[0], kc_ref[0], vc_ref[0], smc_ref[0])
    o_cf, o_cb = both(ctx_in, ctx_in)
    if ctx_out:
        oc_ref[0] = _gla_finish(o_cf + o_cb, ogc_ref[0], onorm)

    def rows_of(blk):
        return pl.ds(pl.multiple_of(blk * SUPER, SUPER), SUPER)

    def scan_in(r):
        return ql_ref[0, r, :], kl_ref[0, r, :], vl_ref[0, r, :], sml_ref[0, r, :]

    def body(j, carry):
        fwd = [(0, rows_of(j * GLA_BLOCKS_PER_TRIP + u)) for u in range(GLA_BLOCKS_PER_TRIP)]
        bwd = [(1, rows_of(n_lat - 1 - (j * GLA_BLOCKS_PER_TRIP + u))) for u in range(GLA_BLOCKS_PER_TRIP)]
        jobs = [job for pair in zip(fwd, bwd) for job in pair]
        outs = _gla_superchunks(
            [scan_in(r) + ((waf_ref[...], baf_ref[...], trif_ref[...], 0, False) if d == 0 else
                           (wab_ref[...], bab_ref[...], trib_ref[...], 1, True)) for d, r in jobs], st_ref)
        for (d, r), o in zip(jobs, outs):
            (of_ref if d == 0 else ob_ref)[r, :] = o
        return carry

    lax.fori_loop(0, n_lat // GLA_BLOCKS_PER_TRIP, body, 0)

    def finish(j, carry):
        r = pl.ds(pl.multiple_of(j * SUPER, SUPER), SUPER)
        ol_ref[0, r, :] = _gla_finish(of_ref[r, :] + ob_ref[r, :], ogl_ref[0, r, :], onorm)
        return carry

    lax.fori_loop(0, n_lat, finish, 0)


def _gla_consts():
    r = np.arange(SUPER)
    same = (r[:, None] // GLA_CHUNK) == (r[None, :] // GLA_CHUNK)
    return jnp.asarray(same & (r[None, :] <= r[:, None]), BF), jnp.asarray(same & (r[None, :] >= r[:, None]), BF)


def _gla(z_l, z_c, params, l, ctx_out):
    B, T, _ = z_l.shape
    L = z_c.shape[1]
    assert L == SUPER and T % SUPER == 0
    kw = GLA_HEADS * GLA_DK
    vw = GLA_HEADS * GLA_DV
    consts = _gla_consts()

    def zspec(n, width, col):
        return pl.BlockSpec((1, n, width), lambda b: (b, 0, col // width))

    in_specs = ([zspec(T, kw, Z_GQ), zspec(T, kw, Z_GK), zspec(T, vw, Z_GV), zspec(T, LANES, Z_SM), zspec(T, vw, Z_OG),
                 zspec(L, kw, Z_GQ), zspec(L, kw, Z_GK), zspec(L, vw, Z_GV), zspec(L, LANES, Z_SM), zspec(L, vw, Z_OG)]
                + [_layer(a, l) for a in params] + [_whole(a) for a in consts])
    ol_spec = pl.BlockSpec((1, T, vw), lambda b: (b, 0, 0))
    ol_shape = jax.ShapeDtypeStruct((B, T, vw), BF)
    if ctx_out:
        out_specs = [ol_spec, pl.BlockSpec((1, L, vw), lambda b: (b, 0, 0))]
        out_shape = [ol_shape, jax.ShapeDtypeStruct((B, L, vw), BF)]
    else:
        out_specs, out_shape = ol_spec, ol_shape
    scratch = [pltpu.VMEM((T, vw), F32), pltpu.VMEM((T, vw), F32),
               pltpu.VMEM((2, GLA_HEADS // 2, 2 * GLA_DV, LANES), F32)]
    res = pl.pallas_call(
        functools.partial(_gla_kernel, n_lat=T // SUPER, ctx_out=ctx_out),
        grid=(B,), in_specs=in_specs, out_specs=out_specs, out_shape=out_shape,
        scratch_shapes=scratch,
        compiler_params=_cparams("arbitrary"),
        name="gla",
    )(*([z_l] * 5 + [z_c] * 5 + list(params) + list(consts)))
    return (res[0], res[1]) if ctx_out else (res, None)


MERGE_SUB = 256


def _merge_kernel(oa_ref, ob_ref, oc_ref, ma_ref, mb_ref, mc_ref, x_ref, gate_ref, shift_ref, scale_ref, g_ref,
                  wa_ref, wb_ref, wc_ref, wo_ref, wr_ref, xo_ref, h_ref, aff_ref, y_ref):
    tm = x_ref.shape[1]
    gate = gate_ref[0]
    subs = [slice(r0, r0 + MERGE_SUB) for r0 in range(0, tm, MERGE_SUB)]

    def branch(o_ref, m_ref, w_ref, rows, n0):
        m = m_ref[0, rows, n0:n0 + 256].astype(F32)
        return jax.nn.sigmoid(m) * _dot(o_ref[0, rows, :], w_ref[:, n0:n0 + 256])

    for n0 in range(0, D_MODEL, 256):
        for rows in subs:
            y = (branch(oa_ref, ma_ref, wa_ref, rows, n0) + branch(ob_ref, mb_ref, wb_ref, rows, n0)
                 + branch(oc_ref, mc_ref, wc_ref, rows, n0))
            y_ref[rows, n0:n0 + 256] = y.astype(BF)
    for n0 in range(0, D_MODEL, 256):
        for rows in subs:
            xo_ref[0, rows, n0:n0 + 256] = (x_ref[0, rows, n0:n0 + 256]
                                            + gate[:, n0:n0 + 256] * _dot(y_ref[rows, :], wo_ref[:, n0:n0 + 256]))
    hs = [_rms(xo_ref[0, rows, :], g_ref[...]) * (1.0 + scale_ref[0]) + shift_ref[0] for rows in subs]
    wr = wr_ref[...]
    for rows, h in zip(subs, hs):
        h_hi = h.astype(BF)
        h_lo = (h - h_hi.astype(F32)).astype(BF)
        h_ref[0, rows, :] = h_hi
        part = _dot_nt(wr, h_hi)
        logits = part[:N_EXPERTS] + part[N_EXPERTS:] + _dot_nt(wr[:N_EXPERTS], h_lo)
        e = jnp.exp(logits - logits.max(axis=0, keepdims=True))
        aff_ref[0, :, rows] = e / e.sum(axis=0, keepdims=True)


def _merge(oa, ob, oc, z, x, mods3, mod_row, norm2_3, weights, l, tm):
    B, T, D = x.shape
    wa, wb, wc, wo, wr_t = weights

    def tok(width, col=0):
        return pl.BlockSpec((1, tm, width), lambda b, i: (b, i, col // width))

    def mod_spec(k):
        return pl.BlockSpec((1, 1, D), lambda b, i: (mod_row(b) * 6 + k, 0, 0))

    in_specs = [tok(256), tok(256), tok(512), tok(D, Z_MA), tok(D, Z_MB), tok(D, Z_MC), tok(D),
                mod_spec(2), mod_spec(3), mod_spec(4), _layer(norm2_3, l)] + [_layer(w, l) for w in weights]
    return pl.pallas_call(
        _merge_kernel,
        grid=(B, T // tm), in_specs=in_specs,
        out_specs=[tok(D), tok(D), pl.BlockSpec((1, N_EXPERTS, tm), lambda b, i: (b, 0, i))],
        out_shape=[jax.ShapeDtypeStruct((B, T, D), F32), jax.ShapeDtypeStruct((B, T, D), BF),
                   jax.ShapeDtypeStruct((B, N_EXPERTS, T), F32)],
        scratch_shapes=[pltpu.VMEM((tm, D), BF)],
        compiler_params=_cparams("arbitrary", "arbitrary"),
        name="merge",
    )(oa, ob, oc, z, z, z, x, mods3, mods3, mods3, norm2_3, wa, wb, wc, wo, wr_t)


def _route_kernel(aff_ref, tri_ref, pos_ref, post_ref, *, cap, slot_stride):
    aff = aff_ref[0]
    E, T = aff.shape

    def refine(thr, shift, patterns):
        best = thr
        for c in patterns:
            cand = thr | (jnp.int32(c) << shift)
            cnt = jnp.sum((aff >= pltpu.bitcast(cand, F32)).astype(F32), axis=1, keepdims=True)
            best = jnp.where(cnt >= cap, cand, best)
        return best

    thr = lax.fori_loop(0, 15, lambda i, t: refine(t, 29 - 2 * i, (1, 2, 3)), jnp.zeros((E, 1), jnp.int32))
    thr = refine(thr, 0, (1,))
    thr_f = pltpu.bitcast(thr, F32)
    gt = aff > thr_f
    eq = aff == thr_f
    need = cap - jnp.sum(gt.astype(F32), axis=1, keepdims=True)
    tri = tri_ref[...]

    def excl_prefix(flags):
        run = jnp.zeros((E, 1), F32)
        blocks = []
        for j in range(T // LANES):
            blk = flags[:, j * LANES:(j + 1) * LANES].astype(F32)
            blocks.append(_dot(blk.astype(BF), tri) + run)
            run = run + jnp.sum(blk, axis=1, keepdims=True)
        return jnp.concatenate(blocks, axis=1)

    sel = gt | (eq & (excl_prefix(eq) < need))
    base = (slot_stride * pl.program_id(0)).astype(F32)
    pos = jnp.where(sel, excl_prefix(sel) + base, -1.0)
    pos_ref[0] = pos.astype(jnp.int32)
    post_ref[0] = jnp.concatenate([pos, jnp.full((LANES - E, T), -1.0, F32)], axis=0).T


def _route(aff_t, cap, slot_stride):
    B, E, T = aff_t.shape
    r = np.arange(LANES)
    tri = jnp.asarray(r[:, None] < r[None, :], BF)
    return pl.pallas_call(
        functools.partial(_route_kernel, cap=cap, slot_stride=slot_stride),
        grid=(B,),
        in_specs=[pl.BlockSpec((1, E, T), lambda b: (b, 0, 0)), pl.BlockSpec((LANES, LANES), lambda b: (0, 0))],
        out_specs=[pl.BlockSpec((1, E, T), lambda b: (b, 0, 0)), pl.BlockSpec((1, T, LANES), lambda b: (b, 0, 0))],
        out_shape=[jax.ShapeDtypeStruct((B, E, T), jnp.int32), jax.ShapeDtypeStruct((B, T, LANES), F32)],
        compiler_params=_cparams("arbitrary"),
        name="route",
    )(aff_t, tri)


def _moe_kernel(*refs, n_slots, final):
    if final:
        h_ref, pos_ref, post_ref, aff_ref, wg_ref, wu_ref, wd_ref, x_ref, gate_ref, g_ref, o_ref = refs
    else:
        h_ref, pos_ref, post_ref, aff_ref, wg_ref, wu_ref, wd_ref, o_ref = refs
    e = pl.program_id(1)

    @pl.when(e == 0)
    def _():
        o_ref[...] = jnp.zeros_like(o_ref)

    pos = pos_ref[0, pl.ds(e, 1), :]
    aff = aff_ref[0, pl.ds(e, 1), :]
    T = pos.shape[1]
    hit = lax.broadcasted_iota(jnp.int32, (n_slots, T), 0) == pos
    w_slot = jnp.sum(jnp.where(hit, aff, 0.0), axis=1, keepdims=True)
    xg = _dot(hit.astype(BF), h_ref[0]).astype(BF)
    gt = _dot(xg, wg_ref[...].astype(BF))
    up = _dot(xg, wu_ref[...].astype(BF))
    hid = (gt * jax.nn.sigmoid(gt) * up * w_slot).astype(BF)
    y = _dot(hid, wd_ref[...].astype(BF)).astype(BF)

    lane = lax.broadcasted_iota(jnp.int32, (1, LANES), 1)
    slot_row = lax.broadcasted_iota(jnp.int32, (1, n_slots), 1).astype(F32)
    for t0 in range(0, T, 256):
        pos_col = jnp.sum(jnp.where(lane == e, post_ref[0, t0:t0 + 256, :], 0.0), axis=1, keepdims=True)
        o_ref[0, t0:t0 + 256, :] += _dot((pos_col == slot_row).astype(BF), y)

    if final:
        @pl.when(e == pl.num_programs(1) - 1)
        def _():
            for t0 in range(0, T, 256):
                rows = slice(t0, t0 + 256)
                o_ref[0, rows, :] = _rms(x_ref[0, rows, :] + gate_ref[0] * o_ref[0, rows, :], g_ref[...])


def _moe(h, pos, pos_t, aff_t, weights, l, n_slots, final=None):
    B, T, D = h.shape
    E = pos.shape[1]
    wg, wu, wd = weights

    def wspec(w):
        return pl.BlockSpec((None, None) + w.shape[2:], lambda b, e: (l, e, 0, 0))

    in_specs = [pl.BlockSpec((1, T, D), lambda b, e: (b, 0, 0)),
                pl.BlockSpec((1, E, T), lambda b, e: (b, 0, 0)),
                pl.BlockSpec((1, T, LANES), lambda b, e: (b, 0, 0)),
                pl.BlockSpec((1, E, T), lambda b, e: (b, 0, 0)),
                wspec(wg), wspec(wu), wspec(wd)]
    args = [h, pos, pos_t, aff_t, wg, wu, wd]
    if final is not None:
        x, mods3, g = final
        in_specs += [pl.BlockSpec((1, T, D), lambda b, e: (b, 0, 0), pipeline_mode=pl.Buffered(1)),
                     pl.BlockSpec((1, 1, D), lambda b, e: (b * 6 + 5, 0, 0)),
                     pl.BlockSpec((1, D), lambda b, e: (0, 0))]
        args += [x, mods3, g.reshape(1, D)]
    return pl.pallas_call(
        functools.partial(_moe_kernel, n_slots=n_slots, final=final is not None),
        grid=(B, E),
        in_specs=in_specs,
        out_specs=pl.BlockSpec((1, T, D), lambda b, e: (b, 0, 0)),
        out_shape=jax.ShapeDtypeStruct((B, T, D), F32),
        compiler_params=_cparams("arbitrary", "arbitrary"),
        name="moe",
    )(*args)


def _final_kernel(x_ref, moe_ref, gate_ref, g_ref, o_ref):
    o_ref[0] = _rms(x_ref[0] + gate_ref[0] * moe_ref[0], g_ref[...])


def _final(x, moe, mods3, g, tm=512):
    B, T, D = x.shape
    tok = pl.BlockSpec((1, tm, D), lambda b, i: (b, i, 0))
    return pl.pallas_call(
        _final_kernel,
        grid=(B, T // tm),
        in_specs=[tok, tok, pl.BlockSpec((1, 1, D), lambda b, i: (b * 6 + 5, 0, 0)),
                  pl.BlockSpec((1, D), lambda b, i: (0, 0))],
        out_specs=tok,
        out_shape=jax.ShapeDtypeStruct((B, T, D), F32),
        compiler_params=_cparams("arbitrary", "arbitrary"),
        name="final_norm",
    )(x, moe, mods3, g.reshape(1, D))


def kernel(x, c, ctx, c_ctx, ada_w, ada_b, norm1, norm2, w_in, na_rpb, mla_q_norm, mla_w_uq, mla_kv_norm, mla_w_ukv,
           gla_w_af, gla_b_af, gla_w_ab, gla_b_ab, gla_o_norm, w_br_a, w_br_b, w_br_c, w_out,
           moe_router, moe_w_gate, moe_w_up, moe_w_down, final_norm):
    B, T, D = x.shape
    L = ctx.shape[1]
    depth = ada_w.shape[0]
    assert B < MOD_ROWS
    cap_l = EC_CAPACITY * T // N_EXPERTS
    cap_c = EC_CAPACITY * L // N_EXPERTS
    ctx_row = B
    cc = jnp.concatenate([c, c_ctx[None], jnp.zeros((MOD_ROWS - B - 1, D), F32)], axis=0)

    def lat_row(b):
        return b

    def ctx_row_fn(b):
        return ctx_row

    def rows3(a):
        return a.reshape(depth, 1, -1)

    ada_b3, norm1_3, norm2_3 = rows3(ada_b), rows3(norm1), rows3(norm2)
    w_in_t = jnp.swapaxes(w_in, 1, 2).astype(BF)
    na_bias = _na_bias(na_rpb, T // GRID_W)
    mla_w = _mla_weights(mla_w_uq, mla_w_ukv)
    qn3, kvn3 = rows3(mla_q_norm), rows3(mla_kv_norm)
    rope_tabs = _rope_tables(T)
    gate_pad = lambda w, off: jnp.pad(w, ((0, 0), (off, LANES - off - GLA_GATE_RANK), (0, 0))).astype(BF)
    gla_params = (gate_pad(gla_w_af, SM_AF), rows3(gla_b_af), gate_pad(gla_w_ab, SM_AB), rows3(gla_b_ab),
                  rows3(gla_o_norm))
    wr_t = jnp.swapaxes(moe_router, 1, 2)
    wr_hi = wr_t.astype(BF)
    wr_lo = (wr_t - wr_hi.astype(F32)).astype(BF)
    merge_w = (w_br_a.astype(BF), w_br_b.astype(BF), w_br_c.astype(BF), w_out.astype(BF),
               jnp.concatenate([wr_hi, wr_lo], axis=1))
    moe_w = (moe_w_gate, moe_w_up, moe_w_down)

    moe_l = moe_c = mods3 = None
    for l in range(depth):
        ctx_out = l < depth - 1
        prev_mods3 = mods3
        mods3 = _ada(cc, ada_w, ada_b3, l).reshape(MOD_ROWS * 6, 1, D)
        z_l, x = _in_proj(x, moe_l, prev_mods3, mods3, lat_row, norm1_3, w_in_t, l)
        z_c, ctx = _in_proj(ctx, moe_c, prev_mods3, mods3, ctx_row_fn, norm1_3, w_in_t, l,
                            cols=None if ctx_out else CTX_KEY_COLS)

        oa_l, oa_c = _na_attention(z_l, z_c, na_bias, l, ctx_out)

        qb_l, kb_l, vb_l = _mla_proj(z_l, qn3, kvn3, mla_w, rope_tabs, l)
        qb_c, kb_c, vb_c = _mla_proj(z_c, qn3, kvn3, mla_w, None, l)
        ob_l = _mla_attn(qb_l, kb_c, vb_c, kb_l, vb_l)

        oc_l, oc_c = _gla(z_l, z_c, gla_params, l, ctx_out)

        x, h_l, aff_l = _merge(oa_l, ob_l, oc_l, z_l, x, mods3, lat_row, norm2_3, merge_w, l, tm=1024)
        pos_l, post_l = _route(aff_l, cap_l, 0)
        moe_l = _moe(h_l, pos_l, post_l, aff_l, moe_w, l, cap_l)

        if ctx_out:
            ob_c = _mla_attn(qb_c, kb_c, vb_c)
            ctx, h_c, aff_c = _merge(oa_c, ob_c, oc_c, z_c, ctx, mods3, ctx_row_fn, norm2_3, merge_w, l, tm=256)
            pos_c, post_c = _route(aff_c, cap_c, cap_c)
            flat = lambda a: jnp.swapaxes(a, 0, 1).reshape(1, N_EXPERTS, B * L)
            moe_c = _moe(h_c.reshape(1, B * L, D), flat(pos_c), post_c.reshape(1, B * L, LANES), flat(aff_c),
                         moe_w, l, B * cap_c).reshape(B, L, D)
    return _final(x, moe_l, mods3, final_norm)
```

```python
import functools

import numpy as np
import jax
import jax.numpy as jnp
from jax import lax
from jax.experimental import pallas as pl
from jax.experimental.pallas import tpu as pltpu

BF = jnp.bfloat16
F32 = jnp.float32

D_MODEL = 1024
GRID_W = 64
EPS = 1e-6
NA_HEADS = 4
NA_HEAD_DIM = 64
NA_WIN_H = 8
NA_WIN_W = 16
MLA_HEADS = 4
MLA_Q_RANK = 256
MLA_KV_RANK = 128
MLA_NOPE_DIM = 64
MLA_ROPE_DIM = 32
MLA_V_DIM = 64
ROPE_THETA = 10000.0
GLA_HEADS = 4
GLA_DK = 64
GLA_DV = 128
GLA_GATE_RANK = 16
GLA_TAU = 16.0
GLA_CHUNK = 64
N_EXPERTS = 16
EXPERT_FF = 1024
EC_CAPACITY = 2
IN_SIZES = (256, 256, 256, 256, 128, 32, 256, 256, 512, 16, 16, 512, 1024, 1024, 1024)

LANES = 128
VMEM_LIMIT = 56 * 1024 * 1024
MOD_ROWS = 16

Z_MA, Z_MB, Z_MC = 0, 1024, 2048
Z_GV, Z_OG = 3072, 3584
Z_QA, Z_KA, Z_VA = 4096, 4352, 4608
Z_GQ, Z_GK = 4864, 5120
Z_MLA = 5376
Z_SM = 5760
Z_COLS = 5888
SM_KR, SM_AF, SM_AB = 0, 32, 48

CTX_KEY_COLS = (Z_GV, Z_GV + 256, Z_KA, Z_VA, Z_GK, Z_MLA, Z_MLA + 256)

IN_NAMES = ("qa", "ka", "va", "dq", "dkv", "kr", "gq", "gk", "gv", "a_f", "a_b", "og", "m_a", "m_b", "m_c")
Z_ORDER = ("m_a", "m_b", "m_c", "gv", "og", "qa", "ka", "va", "gq", "gk", "dq", "dkv", "kr", "a_f", "a_b")


def _z_chunk_sources(width=256):
    start, off = {}, 0
    for name, size in zip(IN_NAMES, IN_SIZES):
        start[name] = off
        off += size
    pieces = []
    z = 0
    for name in Z_ORDER:
        s0, n = start[name], IN_SIZES[IN_NAMES.index(name)]
        while n:
            take = min(n, width - z % width)
            pieces.append((z, s0, take))
            z, s0, n = z + take, s0 + take, n - take
    chunks = [[] for _ in range(Z_COLS // width)]
    for zc, s0, n in pieces:
        if chunks[zc // width] and sum(chunks[zc // width][-1]) == s0:
            chunks[zc // width][-1] = (chunks[zc // width][-1][0], chunks[zc // width][-1][1] + n)
        else:
            chunks[zc // width].append((s0, n))
    return chunks


SUPER = 256
GLA_BLOCKS_PER_TRIP = 2
NEG = -1e30


def _cparams(*sem):
    return pltpu.CompilerParams(dimension_semantics=sem, vmem_limit_bytes=VMEM_LIMIT)


def _layer(a, l):
    return pl.BlockSpec((None,) + a.shape[1:], lambda *_: (l,) + (0,) * (a.ndim - 1), pipeline_mode=pl.Buffered(1))


def _whole(a):
    return pl.BlockSpec(a.shape, lambda *_: (0,) * a.ndim, pipeline_mode=pl.Buffered(1))


def _dot(a, b):
    return jnp.dot(a, b, preferred_element_type=F32)


def _dot_nt(a, b):
    return lax.dot_general(a, b, (((1,), (1,)), ((), ())), preferred_element_type=F32)


def _dot_tn(a, b):
    return lax.dot_general(a, b, (((0,), (0,)), ((), ())), preferred_element_type=F32)


def _rms(x, g):
    return x * lax.rsqrt(jnp.mean(x * x, axis=-1, keepdims=True) + EPS) * g


def _ada_kernel(c_ref, w_ref, b_ref, o_ref):
    c = c_ref[...]
    a = (c * jax.nn.sigmoid(c)).astype(BF)
    o_ref[...] = _dot(a, w_ref[...].astype(BF)) + b_ref[...]


def _ada(cc, ada_w, ada_b3, l):
    n = ada_w.shape[2]
    tn = 1536
    return pl.pallas_call(
        _ada_kernel,
        grid=(n // tn,),
        in_specs=[pl.BlockSpec((MOD_ROWS, D_MODEL), lambda j: (0, 0)),
                  pl.BlockSpec((None, D_MODEL, tn), lambda j: (l, 0, j)),
                  pl.BlockSpec((None, 1, tn), lambda j: (l, 0, j))],
        out_specs=pl.BlockSpec((MOD_ROWS, tn), lambda j: (0, j)),
        out_shape=jax.ShapeDtypeStruct((MOD_ROWS, n), F32),
        compiler_params=_cparams("arbitrary"),
        name="ada_mod",
    )(cc, ada_w, ada_b3)


def _in_proj_kernel(*refs, with_moe, cols):
    if with_moe:
        x_ref, moe_ref, gate_ref, g_ref, shift_ref, scale_ref, w_ref, o_ref, xo_ref, h_ref = refs
        x = x_ref[0] + gate_ref[0] * moe_ref[0]
        xo_ref[0] = x
    else:
        x_ref, g_ref, shift_ref, scale_ref, w_ref, o_ref, h_ref = refs
        x = x_ref[0]
    h = _rms(x, g_ref[...]) * (1.0 + scale_ref[0]) + shift_ref[0]
    h_ref[...] = h.astype(BF)
    for n0, srcs in zip(range(0, Z_COLS, 256), _z_chunk_sources()):
        if cols is None or n0 in cols:
            rows = [w_ref[s0:s0 + n, :] for s0, n in srcs]
            pad = 256 - sum(n for _, n in srcs)
            if pad:
                rows.append(jnp.zeros((pad, w_ref.shape[1]), BF))
            wt = rows[0] if len(rows) == 1 else jnp.concatenate(rows, axis=0)
            o_ref[0, :, n0:n0 + 256] = _dot_nt(h_ref[...], wt).astype(BF)
        else:
            o_ref[0, :, n0:n0 + 256] = jnp.zeros((o_ref.shape[1], 256), BF)


def _in_proj(x, moe, prev_mods3, mods3, mod_row, norm1_3, w, l, tm=512, cols=None):
    B, T, D = x.shape
    tm = min(tm, T)
    with_moe = moe is not None

    def mod_spec(k):
        return pl.BlockSpec((1, 1, D), lambda b, i: (mod_row(b) * 6 + k, 0, 0))

    x_spec = pl.BlockSpec((1, tm, D), lambda b, i: (b, i, 0))
    in_specs = [x_spec]
    args = [x]
    if with_moe:
        in_specs += [x_spec, mod_spec(5)]
        args += [moe, prev_mods3]
    in_specs += [_layer(norm1_3, l), mod_spec(0), mod_spec(1), _layer(w, l)]
    args += [norm1_3, mods3, mods3, w]
    z_spec = pl.BlockSpec((1, tm, Z_COLS), lambda b, i: (b, i, 0))
    z_shape = jax.ShapeDtypeStruct((B, T, Z_COLS), BF)
    if with_moe:
        out_specs, out_shape = [z_spec, x_spec], [z_shape, jax.ShapeDtypeStruct((B, T, D), F32)]
    else:
        out_specs, out_shape = z_spec, z_shape
    res = pl.pallas_call(
        functools.partial(_in_proj_kernel, with_moe=with_moe, cols=cols),
        grid=(B, T // tm),
        in_specs=in_specs, out_specs=out_specs, out_shape=out_shape,
        scratch_shapes=[pltpu.VMEM((tm, D), BF)],
        compiler_params=_cparams("arbitrary", "arbitrary"),
        name="in_proj",
    )(*args)
    return (res[0], res[1]) if with_moe else (res, x)


def _softmax_pv_many(heads):
    maxes = []
    for s_list, _ in heads:
        m = s_list[0].max(axis=-1, keepdims=True)
        for s in s_list[1:]:
            m = jnp.maximum(m, s.max(axis=-1, keepdims=True))
        maxes.append(m)
    probs = [[jnp.exp((s - m).astype(BF)) for s in s_list] for (s_list, _), m in zip(heads, maxes)]
    outs = []
    for (_, v_list), p_list in zip(heads, probs):
        o = _dot(p_list[0], v_list[0])
        for p, v in zip(p_list[1:], v_list[1:]):
            o = o + _dot(p, v)
        outs.append(o / pltpu.roll(o, LANES // 2, axis=1))
    return outs


def _softmax_pv_sum(heads):
    maxes = []
    for s_list, _ in heads:
        m = s_list[0].max(axis=-1, keepdims=True)
        for s in s_list[1:]:
            m = jnp.maximum(m, s.max(axis=-1, keepdims=True))
        maxes.append(m)
    probs = [[jnp.exp(s - m) for s in s_list] for (s_list, _), m in zip(heads, maxes)]
    outs = []
    for (_, v_list), p_list in zip(heads, probs):
        l = p_list[0].sum(axis=-1, keepdims=True)
        o = _dot(p_list[0].astype(BF), v_list[0])
        for p, v in zip(p_list[1:], v_list[1:]):
            l = l + p.sum(axis=-1, keepdims=True)
            o = o + _dot(p.astype(BF), v)
        outs.append(o / l)
    return outs


def _lane_masks():
    lane = lax.broadcasted_iota(jnp.int32, (1, LANES), 1)
    return lane < LANES // 2, lane >= LANES // 2


NA_ROWS_PER_STEP = 2
NA_PAIRS_PER_TRIP = 8
NA_BAND = 10
NA_CASE_ROWS = (0, 2, 4, 28, 30)


def _na_bias(rpb, rows):
    depth = rpb.shape[0]
    qc = np.arange(GRID_W)
    kj = np.arange(GRID_W)
    cs = np.clip(qc - NA_WIN_W // 2, 0, GRID_W - NA_WIN_W)
    col_ok = (kj[None, :] >= cs[:, None]) & (kj[None, :] < cs[:, None] + NA_WIN_W)
    co = kj[None, :] - qc[:, None] + (NA_WIN_W - 1)
    onehot = np.zeros((2 * NA_WIN_W - 1, GRID_W, GRID_W), np.float32)
    qi, ki = np.nonzero(col_ok)
    onehot[co[qi, ki], qi, ki] = 1.0
    toep = jnp.einsum("lhrc,cp->lhrp", rpb.astype(F32), jnp.asarray(onehot.reshape(2 * NA_WIN_W - 1, -1)),
                      precision=lax.Precision.HIGHEST)
    toep = (toep.reshape(depth, NA_HEADS, 2 * NA_WIN_H - 1, GRID_W, GRID_W)
            + jnp.asarray(np.where(col_ok, 0.0, NEG), F32))
    neg = jnp.full((depth, NA_HEADS, GRID_W, GRID_W), NEG, F32)
    cases = []
    for r0 in NA_CASE_ROWS:
        bs = int(np.clip(r0 - NA_WIN_H // 2, 0, rows - NA_BAND))
        q_rows = []
        for qr in range(NA_ROWS_PER_STEP):
            r = r0 + qr
            rs = int(np.clip(r - NA_WIN_H // 2, 0, rows - NA_WIN_H))
            blocks = []
            for i in range(NA_BAND):
                krow = bs + i
                blocks.append(toep[:, :, krow - r + NA_WIN_H - 1] if rs <= krow < rs + NA_WIN_H else neg)
            q_rows.append(jnp.concatenate(blocks, axis=-1))
        cases.append(jnp.concatenate(q_rows, axis=-2))
    return jnp.stack(cases, axis=2)


def _na_kernel(*refs, rows, ctx_out):
    if ctx_out:
        ql_ref, kl_ref, vl_ref, qc_ref, kc_ref, vc_ref, bias_ref, ol_ref, oc_ref = refs
    else:
        ql_ref, kl_ref, vl_ref, kc_ref, vc_ref, bias_ref, ol_ref = refs
    scale = NA_HEAD_DIM ** -0.5
    masks = _lane_masks()
    nq = NA_ROWS_PER_STEP * GRID_W
    nk = NA_BAND * GRID_W

    def body(it, carry):
        heads, stores = [], []
        for sub in range(NA_PAIRS_PER_TRIP):
            r0 = (it * NA_PAIRS_PER_TRIP + sub) * NA_ROWS_PER_STEP
            bs = jnp.clip(r0 - NA_WIN_H // 2, 0, rows - NA_BAND)
            case = jnp.where(r0 < 4, r0 // 2, jnp.where(r0 > rows - 6, (r0 - (rows - 4)) // 2 + 3, 2))
            q0 = pl.multiple_of(r0 * GRID_W, nq)
            k0 = pl.multiple_of(bs * GRID_W, 2 * GRID_W)
            for hp in range(NA_HEADS // 2):
                sl = slice(hp * LANES, (hp + 1) * LANES)
                q = ql_ref[0, pl.ds(q0, nq), sl] * scale
                kb = kl_ref[0, pl.ds(k0, nk), sl]
                vb = vl_ref[0, pl.ds(k0, nk), sl]
                kc = kc_ref[0, :, sl]
                vc = vc_ref[0, :, sl]
                for hh in range(2):
                    qm = jnp.where(masks[hh], q, jnp.zeros_like(q))
                    s_loc = _dot_nt(qm, kb) + bias_ref[2 * hp + hh, case]
                    s_ctx = _dot_nt(qm, kc)
                    heads.append(([s_loc, s_ctx], [vb, vc]))
                stores.append((q0, sl))
        outs = _softmax_pv_sum(heads)
        for n, (q0, sl) in enumerate(stores):
            ol_ref[0, pl.ds(q0, nq), sl] = jnp.where(masks[0], outs[2 * n], outs[2 * n + 1]).astype(BF)
        return carry

    lax.fori_loop(0, rows // (NA_ROWS_PER_STEP * NA_PAIRS_PER_TRIP), body, 0)

    if ctx_out:
        heads = []
        for hp in range(NA_HEADS // 2):
            sl = slice(hp * LANES, (hp + 1) * LANES)
            q = qc_ref[0, :, sl] * scale
            kc = kc_ref[0, :, sl]
            for hh in range(2):
                qm = jnp.where(masks[hh], q, jnp.zeros_like(q))
                heads.append(([_dot_nt(qm, kc)], [vc_ref[0, :, sl]]))
        outs = _softmax_pv_sum(heads)
        for hp in range(NA_HEADS // 2):
            sl = slice(hp * LANES, (hp + 1) * LANES)
            oc_ref[0, :, sl] = jnp.where(masks[0], outs[2 * hp], outs[2 * hp + 1]).astype(BF)


def _na_attention(z_l, z_c, bias, l, ctx_out):
    B, T, _ = z_l.shape
    L = z_c.shape[1]
    rows = T // GRID_W

    def zl(col):
        return pl.BlockSpec((1, T, 256), lambda b: (b, 0, col // 256))

    def zc(col):
        return pl.BlockSpec((1, L, 256), lambda b: (b, 0, col // 256))

    ol_spec = pl.BlockSpec((1, T, 256), lambda b: (b, 0, 0))
    ol_shape = jax.ShapeDtypeStruct((B, T, 256), BF)
    if ctx_out:
        in_specs = [zl(Z_QA), zl(Z_KA), zl(Z_VA), zc(Z_QA), zc(Z_KA), zc(Z_VA), _layer(bias, l)]
        args = [z_l, z_l, z_l, z_c, z_c, z_c, bias]
        out_specs = [ol_spec, pl.BlockSpec((1, L, 256), lambda b: (b, 0, 0))]
        out_shape = [ol_shape, jax.ShapeDtypeStruct((B, L, 256), BF)]
    else:
        in_specs = [zl(Z_QA), zl(Z_KA), zl(Z_VA), zc(Z_KA), zc(Z_VA), _layer(bias, l)]
        args = [z_l, z_l, z_l, z_c, z_c, bias]
        out_specs, out_shape = ol_spec, ol_shape
    res = pl.pallas_call(
        functools.partial(_na_kernel, rows=rows, ctx_out=ctx_out),
        grid=(B,), in_specs=in_specs, out_specs=out_specs, out_shape=out_shape,
        compiler_params=_cparams("arbitrary"),
        name="na_attn",
    )(*args)
    return (res[0], res[1]) if ctx_out else (res, None)


MLA_QK = MLA_NOPE_DIM + MLA_ROPE_DIM
MLA_W = MLA_HEADS * LANES


def _mla_proj_kernel(*refs, rope):
    if rope:
        (zm_ref, sm_ref, qn_ref, kvn_ref, wq_ref, wqs_ref, wk_ref, wv_ref, e_ref, es_ref, vone_ref, cos_ref, sin_ref,
         q_ref, k_ref, v_ref) = refs
    else:
        zm_ref, sm_ref, qn_ref, kvn_ref, wq_ref, wk_ref, wv_ref, e_ref, vone_ref, q_ref, k_ref, v_ref = refs
    zm = zm_ref[0].astype(F32)
    nq = _rms(zm[:, :MLA_Q_RANK], qn_ref[...]).astype(BF)
    nkv = _rms(zm[:, MLA_Q_RANK:], kvn_ref[...]).astype(BF)
    sm = sm_ref[0]
    scale = MLA_QK ** -0.5
    q = _dot(nq, wq_ref[...])
    k = _dot(nkv, wk_ref[...]) + _dot(sm, e_ref[...])
    if rope:
        cos = cos_ref[...]
        sin = sin_ref[...]
        q = q * cos + _dot(nq, wqs_ref[...]) * sin
        k = k * cos + _dot(sm, es_ref[...]) * sin
    q_ref[0] = (q * scale).astype(BF)
    k_ref[0] = k.astype(BF)
    v_ref[0] = (_dot(nkv, wv_ref[...]) + vone_ref[...]).astype(BF)


def _rope_swap(e):
    return np.where(e % 16 < 8, e + 8, e - 8)


def _mla_weights(w_uq, w_ukv):
    depth = w_uq.shape[0]
    quarter = MLA_ROPE_DIM // 4
    q4 = w_uq.reshape(depth, MLA_Q_RANK, MLA_HEADS, MLA_QK)
    nope, rope = q4[..., :MLA_NOPE_DIM], q4[..., MLA_NOPE_DIM:]
    rope_sw = jnp.flip(rope.reshape(depth, MLA_Q_RANK, MLA_HEADS, 2, 2, quarter), axis=4).reshape(rope.shape)
    zpad = jnp.zeros(rope.shape, w_uq.dtype)
    wq = jnp.concatenate([nope, rope, zpad], axis=-1).reshape(depth, MLA_Q_RANK, MLA_W)
    wqs = jnp.concatenate([jnp.zeros(nope.shape, w_uq.dtype), rope_sw, zpad], axis=-1).reshape(depth, MLA_Q_RANK, MLA_W)
    kv4 = w_ukv.reshape(depth, MLA_KV_RANK, MLA_HEADS, MLA_NOPE_DIM + MLA_V_DIM)
    k_nope, v = kv4[..., :MLA_NOPE_DIM], kv4[..., MLA_NOPE_DIM:]
    wk = jnp.concatenate([k_nope, jnp.zeros(k_nope.shape, w_ukv.dtype)], axis=-1).reshape(depth, MLA_KV_RANK, MLA_W)
    vz = jnp.zeros(v.shape[:2] + (1, MLA_V_DIM), w_ukv.dtype)
    wv = jnp.concatenate([jnp.concatenate([v[:, :, h:h + 1], vz] if h % 2 == 0 else [vz, v[:, :, h:h + 1]], axis=-1)
                          for h in range(MLA_HEADS)], axis=2).reshape(depth, MLA_KV_RANK, MLA_W)
    e = np.arange(MLA_ROPE_DIM)
    em = np.zeros((LANES, MLA_W), np.float32)
    ems = np.zeros((LANES, MLA_W), np.float32)
    for h in range(MLA_HEADS):
        em[SM_KR + e, h * LANES + MLA_NOPE_DIM + e] = 1.0
        ems[SM_KR + _rope_swap(e), h * LANES + MLA_NOPE_DIM + e] = 1.0
    lane = np.arange(MLA_W) % LANES
    v_ones = ((lane >= MLA_V_DIM) == ((np.arange(MLA_W) // LANES) % 2 == 0)).astype(np.float32)[None, :]
    return (wq.astype(BF), wqs.astype(BF), wk.astype(BF), wv.astype(BF), jnp.asarray(em, BF), jnp.asarray(ems, BF),
            jnp.asarray(v_ones, F32))


def _rope_tables(T):
    t = jnp.arange(T)
    quarter = MLA_ROPE_DIM // 4
    inv_freq = ROPE_THETA ** (-jnp.arange(quarter, dtype=F32) / quarter)
    ang_r = (t // GRID_W).astype(F32)[:, None] * inv_freq
    ang_c = (t % GRID_W).astype(F32)[:, None] * inv_freq
    cr, sr, ccol, scol = jnp.cos(ang_r), jnp.sin(ang_r), jnp.cos(ang_c), jnp.sin(ang_c)
    ones = jnp.ones((T, MLA_NOPE_DIM), F32)
    pad1 = jnp.ones((T, LANES - MLA_QK), F32)
    cos_h = jnp.concatenate([ones, cr, cr, ccol, ccol, pad1], axis=1)
    sin_h = jnp.concatenate([0.0 * ones, -sr, sr, -scol, scol, 0.0 * pad1], axis=1)
    return jnp.tile(cos_h, (1, MLA_HEADS)), jnp.tile(sin_h, (1, MLA_HEADS))


def _mla_proj(z, qn3, kvn3, weights, rope_tabs, l, tm=512):
    B, T, _ = z.shape
    wq, wqs, wk, wv, em, ems, v_ones = weights
    rope = rope_tabs is not None
    tm = min(tm, T)
    zm_spec = pl.BlockSpec((1, tm, 384), lambda i, b: (b, i, Z_MLA // 384))
    sm_spec = pl.BlockSpec((1, tm, LANES), lambda i, b: (b, i, Z_SM // LANES))
    if rope:
        tab_spec = pl.BlockSpec((tm, MLA_W), lambda i, b: (i, 0))
        ins = [z, z, qn3, kvn3, wq, wqs, wk, wv, em, ems, v_ones, rope_tabs[0], rope_tabs[1]]
        in_specs = ([zm_spec, sm_spec] + [_layer(a, l) for a in ins[2:8]] + [_whole(em), _whole(ems), _whole(v_ones)]
                    + [tab_spec, tab_spec])
    else:
        ins = [z, z, qn3, kvn3, wq, wk, wv, em, v_ones]
        in_specs = [zm_spec, sm_spec] + [_layer(a, l) for a in ins[2:7]] + [_whole(em), _whole(v_ones)]
    return pl.pallas_call(
        functools.partial(_mla_proj_kernel, rope=rope),
        grid=(T // tm, B), in_specs=in_specs,
        out_specs=[pl.BlockSpec((1, tm, MLA_W), lambda i, b: (b, i, 0)),
                   pl.BlockSpec((1, tm, MLA_W), lambda i, b: (b, i, 0)),
                   pl.BlockSpec((1, tm, MLA_W), lambda i, b: (b, i, 0))],
        out_shape=[jax.ShapeDtypeStruct((B, T, MLA_W), BF)] * 3,
        compiler_params=_cparams("arbitrary", "arbitrary"),
        name="mla_proj",
    )(*ins)


def _mla_attn_kernel(*refs, with_latent):
    if with_latent:
        q_ref, kc_ref, vc_ref, kl_ref, vl_ref, o_ref = refs
    else:
        q_ref, kc_ref, vc_ref, o_ref = refs
    masks = _lane_masks()
    heads = []
    for h in range(MLA_HEADS):
        hsl = slice(h * LANES, (h + 1) * LANES)
        q = q_ref[0, :, hsl]
        s_list = [_dot_nt(q, kc_ref[0, :, hsl])]
        v_list = [vc_ref[0, :, hsl]]
        if with_latent:
            s_list.append(_dot_nt(q, kl_ref[0, :, hsl]))
            v_list.append(vl_ref[0, :, hsl])
        heads.append((s_list, v_list))
    outs = _softmax_pv_many(heads)
    for hp in range(MLA_HEADS // 2):
        vsl = slice(hp * LANES, (hp + 1) * LANES)
        o_ref[0, :, vsl] = jnp.where(masks[0], outs[2 * hp], outs[2 * hp + 1]).astype(BF)


def _mla_attn(q, k_c, v_c, k_l=None, v_l=None, tq=1024):
    B, Tq, W = q.shape
    tq = min(tq, Tq)
    L = k_c.shape[1]
    with_latent = k_l is not None
    in_specs = [pl.BlockSpec((1, tq, W), lambda b, i: (b, i, 0)),
                pl.BlockSpec((1, L, W), lambda b, i: (b, 0, 0)),
                pl.BlockSpec((1, L, W), lambda b, i: (b, 0, 0))]
    args = [q, k_c, v_c]
    if with_latent:
        T = k_l.shape[1]
        in_specs += [pl.BlockSpec((1, T, W), lambda b, i: (b, 0, 0)),
                     pl.BlockSpec((1, T, W), lambda b, i: (b, 0, 0))]
        args += [k_l, v_l]
    return pl.pallas_call(
        functools.partial(_mla_attn_kernel, with_latent=with_latent),
        grid=(B, Tq // tq), in_specs=in_specs,
        out_specs=pl.BlockSpec((1, tq, 256), lambda b, i: (b, i, 0)),
        out_shape=jax.ShapeDtypeStruct((B, Tq, 256), BF),
        compiler_params=_cparams("arbitrary", "arbitrary"),
        name="mla_attn",
    )(*args)


def _gla_superchunks(jobs, st_ref):
    n_chunk = SUPER // GLA_CHUNK
    masks = _lane_masks()
    rowblk = lax.broadcasted_iota(jnp.int32, (SUPER, LANES), 0) // GLA_CHUNK
    own = ((lax.broadcasted_iota(jnp.int32, (2 * GLA_DV, LANES), 0) < GLA_DV)
           == (lax.broadcasted_iota(jnp.int32, (2 * GLA_DV, LANES), 1) < GLA_DK))
    gs = [jax.nn.log_sigmoid(_dot(sm, w_a) + b_a) * (1.0 / GLA_TAU) for (_, _, _, sm, w_a, b_a, _, _, _) in jobs]
    pre = []
    for (q, k, v, sm, w_a, b_a, tri, d, reverse), g in zip(jobs, gs):
        g_hi = g.astype(BF)
        g_lo = (g - g_hi.astype(F32)).astype(BF)
        b = _dot(tri, g_hi) + _dot(tri, g_lo)
        ends = [i * GLA_CHUNK if reverse else (i + 1) * GLA_CHUNK - 1 for i in range(n_chunk)]
        tot = jnp.concatenate([jnp.broadcast_to(b[r:r + 1, :], (GLA_CHUNK, b.shape[1])) for r in ends], axis=0)
        pre.append((q.astype(F32) * (GLA_DK ** -0.5) * jnp.exp(b), k.astype(F32) * jnp.exp(-b),
                    k.astype(F32) * jnp.exp(tot - b), jnp.exp(tot), tri > 0))
    outs = [[] for _ in jobs]
    for hp in range(GLA_HEADS // 2):
        sl = slice(hp * LANES, (hp + 1) * LANES)
        stage = []
        for (q, k, v, sm, w_a, b_a, tri, d, reverse), (qd_all, kd_all, ke_all, dec_all, keep) in zip(jobs, pre):
            qd = qd_all[:, sl]
            kd = kd_all[:, sl].astype(BF)
            ke = ke_all[:, sl].astype(BF)
            zero = jnp.zeros_like(ke)
            ke_bd = jnp.concatenate([jnp.where(rowblk == i, ke, zero) for i in range(n_chunk)], axis=1)
            v_pair = v[:, hp * 2 * GLA_DV:(hp + 1) * 2 * GLA_DV]
            kv_t = _dot_tn(v_pair, ke_bd)
            a = [jnp.where(keep, _dot_nt(jnp.where(masks[hh], qd, 0.0).astype(BF), kd), 0.0).astype(BF)
                 for hh in range(2)]
            stage.append((qd, dec_all[:, sl], v_pair, kv_t, a))
        inters = []
        for (q, k, v, sm, w_a, b_a, tri, d, reverse), (qd, dec, v_pair, kv_t, a) in zip(jobs, stage):
            s = st_ref[d, hp]
            inter = [None] * n_chunk
            for i in (range(n_chunk - 1, -1, -1) if reverse else range(n_chunk)):
                rows = slice(i * GLA_CHUNK, (i + 1) * GLA_CHUNK)
                inter[i] = _dot_nt(qd[rows].astype(BF), s.astype(BF))
                s = s * dec[i * GLA_CHUNK:i * GLA_CHUNK + 1, :] + jnp.where(own, kv_t[:, i * LANES:(i + 1) * LANES], 0.0)
            st_ref[d, hp] = s
            inters.append(jnp.concatenate(inter, axis=0))
        for n, ((qd, dec, v_pair, kv_t, a), o_inter) in enumerate(zip(stage, inters)):
            for hh in range(2):
                hv = slice(hh * GLA_DV, (hh + 1) * GLA_DV)
                outs[n].append(_dot(a[hh], v_pair[:, hv]) + o_inter[:, hv])
    return [jnp.concatenate(o, axis=1) for o in outs]


def _gla_finish(o, og, onorm):
    ys = []
    for h in range(GLA_HEADS):
        ys.append(_rms(o[:, h * GLA_DV:(h + 1) * GLA_DV], onorm))
    y = jnp.concatenate(ys, axis=1)
    ogf = og.astype(F32)
    return (y * (ogf * jax.nn.sigmoid(ogf))).astype(BF)


def _gla_kernel(*refs, n_lat, ctx_out):
    (ql_ref, kl_ref, vl_ref, sml_ref, ogl_ref, qc_ref, kc_ref, vc_ref, smc_ref, ogc_ref,
     waf_ref, baf_ref, wab_ref, bab_ref, onorm_ref, trif_ref, trib_ref) = refs[:17]
    if ctx_out:
        ol_ref, oc_ref, of_ref, ob_ref, st_ref = refs[17:]
    else:
        ol_ref, of_ref, ob_ref, st_ref = refs[17:]
    onorm = onorm_ref[...]

    def both(fwd_in, bwd_in):
        return _gla_superchunks([fwd_in + (waf_ref[...], baf_ref[...], trif_ref[...], 0, False),
                                 bwd_in + (wab_ref[...], bab_ref[...], trib_ref[...], 1, True)], st_ref)

    st_ref[...] = jnp.zeros_like(st_ref)
    ctx_in = (qc_ref[0], kc_ref[0], vc_ref[0], smc_ref[0])
    o_cf, o_cb = both(ctx_in, ctx_in)
    if ctx_out:
        oc_ref[0] = _gla_finish(o_cf + o_cb, ogc_ref[0], onorm)

    def rows_of(blk):
        return pl.ds(pl.multiple_of(blk * SUPER, SUPER), SUPER)

    def scan_in(r):
        return ql_ref[0, r, :], kl_ref[0, r, :], vl_ref[0, r, :], sml_ref[0, r, :]

    def body(j, carry):
        fwd = [(0, rows_of(j * GLA_BLOCKS_PER_TRIP + u)) for u in range(GLA_BLOCKS_PER_TRIP)]
        bwd = [(1, rows_of(n_lat - 1 - (j * GLA_BLOCKS_PER_TRIP + u))) for u in range(GLA_BLOCKS_PER_TRIP)]
        jobs = [job for pair in zip(fwd, bwd) for job in pair]
        outs = _gla_superchunks(
            [scan_in(r) + ((waf_ref[...], baf_ref[...], trif_ref[...], 0, False) if d == 0 else
                           (wab_ref[...], bab_ref[...], trib_ref[...], 1, True)) for d, r in jobs], st_ref)
        for (d, r), o in zip(jobs, outs):
            (of_ref if d == 0 else ob_ref)[r, :] = o
        return carry

    lax.fori_loop(0, n_lat // GLA_BLOCKS_PER_TRIP, body, 0)

    def finish(j, carry):
        r = pl.ds(pl.multiple_of(j * SUPER, SUPER), SUPER)
        ol_ref[0, r, :] = _gla_finish(of_ref[r, :] + ob_ref[r, :], ogl_ref[0, r, :], onorm)
        return carry

    lax.fori_loop(0, n_lat, finish, 0)


def _gla_consts():
    r = np.arange(SUPER)
    same = (r[:, None] // GLA_CHUNK) == (r[None, :] // GLA_CHUNK)
    return jnp.asarray(same & (r[None, :] <= r[:, None]), BF), jnp.asarray(same & (r[None, :] >= r[:, None]), BF)


def _gla(z_l, z_c, params, l, ctx_out):
    B, T, _ = z_l.shape
    L = z_c.shape[1]
    assert L == SUPER and T % SUPER == 0
    kw = GLA_HEADS * GLA_DK
    vw = GLA_HEADS * GLA_DV
    consts = _gla_consts()

    def zspec(n, width, col):
        return pl.BlockSpec((1, n, width), lambda b: (b, 0, col // width))

    in_specs = ([zspec(T, kw, Z_GQ), zspec(T, kw, Z_GK), zspec(T, vw, Z_GV), zspec(T, LANES, Z_SM), zspec(T, vw, Z_OG),
                 zspec(L, kw, Z_GQ), zspec(L, kw, Z_GK), zspec(L, vw, Z_GV), zspec(L, LANES, Z_SM), zspec(L, vw, Z_OG)]
                + [_layer(a, l) for a in params] + [_whole(a) for a in consts])
    ol_spec = pl.BlockSpec((1, T, vw), lambda b: (b, 0, 0))
    ol_shape = jax.ShapeDtypeStruct((B, T, vw), BF)
    if ctx_out:
        out_specs = [ol_spec, pl.BlockSpec((1, L, vw), lambda b: (b, 0, 0))]
        out_shape = [ol_shape, jax.ShapeDtypeStruct((B, L, vw), BF)]
    else:
        out_specs, out_shape = ol_spec, ol_shape
    scratch = [pltpu.VMEM((T, vw), F32), pltpu.VMEM((T, vw), F32),
               pltpu.VMEM((2, GLA_HEADS // 2, 2 * GLA_DV, LANES), F32)]
    res = pl.pallas_call(
        functools.partial(_gla_kernel, n_lat=T // SUPER, ctx_out=ctx_out),
        grid=(B,), in_specs=in_specs, out_specs=out_specs, out_shape=out_shape,
        scratch_shapes=scratch,
        compiler_params=_cparams("arbitrary"),
        name="gla",
    )(*([z_l] * 5 + [z_c] * 5 + list(params) + list(consts)))
    return (res[0], res[1]) if ctx_out else (res, None)


MERGE_SUB = 256


def _merge_kernel(oa_ref, ob_ref, oc_ref, ma_ref, mb_ref, mc_ref, x_ref, gate_ref, shift_ref, scale_ref, g_ref,
                  wa_ref, wb_ref, wc_ref, wo_ref, wr_ref, xo_ref, h_ref, aff_ref, y_ref):
    tm = x_ref.shape[1]
    gate = gate_ref[0]
    subs = [slice(r0, r0 + MERGE_SUB) for r0 in range(0, tm, MERGE_SUB)]

    def branch(o_ref, m_ref, w_ref, rows, n0):
        m = m_ref[0, rows, n0:n0 + 256].astype(F32)
        return jax.nn.sigmoid(m) * _dot(o_ref[0, rows, :], w_ref[:, n0:n0 + 256])

    for n0 in range(0, D_MODEL, 256):
        for rows in subs:
            y = (branch(oa_ref, ma_ref, wa_ref, rows, n0) + branch(ob_ref, mb_ref, wb_ref, rows, n0)
                 + branch(oc_ref, mc_ref, wc_ref, rows, n0))
            y_ref[rows, n0:n0 + 256] = y.astype(BF)
    for n0 in range(0, D_MODEL, 256):
        for rows in subs:
            xo_ref[0, rows, n0:n0 + 256] = (x_ref[0, rows, n0:n0 + 256]
                                            + gate[:, n0:n0 + 256] * _dot(y_ref[rows, :], wo_ref[:, n0:n0 + 256]))
    hs = [_rms(xo_ref[0, rows, :], g_ref[...]) * (1.0 + scale_ref[0]) + shift_ref[0] for rows in subs]
    wr = wr_ref[...]
    for rows, h in zip(subs, hs):
        h_hi = h.astype(BF)
        h_lo = (h - h_hi.astype(F32)).astype(BF)
        h_ref[0, rows, :] = h_hi
        part = _dot_nt(wr, h_hi)
        logits = part[:N_EXPERTS] + part[N_EXPERTS:] + _dot_nt(wr[:N_EXPERTS], h_lo)
        e = jnp.exp(logits - logits.max(axis=0, keepdims=True))
        aff_ref[0, :, rows] = e / e.sum(axis=0, keepdims=True)


def _merge(oa, ob, oc, z, x, mods3, mod_row, norm2_3, weights, l, tm):
    B, T, D = x.shape
    wa, wb, wc, wo, wr_t = weights

    def tok(width, col=0):
        return pl.BlockSpec((1, tm, width), lambda b, i: (b, i, col // width))

    def mod_spec(k):
        return pl.BlockSpec((1, 1, D), lambda b, i: (mod_row(b) * 6 + k, 0, 0))

    in_specs = [tok(256), tok(256), tok(512), tok(D, Z_MA), tok(D, Z_MB), tok(D, Z_MC), tok(D),
                mod_spec(2), mod_spec(3), mod_spec(4), _layer(norm2_3, l)] + [_layer(w, l) for w in weights]
    return pl.pallas_call(
        _merge_kernel,
        grid=(B, T // tm), in_specs=in_specs,
        out_specs=[tok(D), tok(D), pl.BlockSpec((1, N_EXPERTS, tm), lambda b, i: (b, 0, i))],
        out_shape=[jax.ShapeDtypeStruct((B, T, D), F32), jax.ShapeDtypeStruct((B, T, D), BF),
                   jax.ShapeDtypeStruct((B, N_EXPERTS, T), F32)],
        scratch_shapes=[pltpu.VMEM((tm, D), BF)],
        compiler_params=_cparams("arbitrary", "arbitrary"),
        name="merge",
    )(oa, ob, oc, z, z, z, x, mods3, mods3, mods3, norm2_3, wa, wb, wc, wo, wr_t)


def _route_kernel(aff_ref, tri_ref, pos_ref, post_ref, *, cap, slot_stride):
    aff = aff_ref[0]
    E, T = aff.shape

    def refine(thr, shift, patterns):
        best = thr
        for c in patterns:
            cand = thr | (jnp.int32(c) << shift)
            cnt = jnp.sum((aff >= pltpu.bitcast(cand, F32)).astype(F32), axis=1, keepdims=True)
            best = jnp.where(cnt >= cap, cand, best)
        return best

    thr = lax.fori_loop(0, 15, lambda i, t: refine(t, 29 - 2 * i, (1, 2, 3)), jnp.zeros((E, 1), jnp.int32))
    thr = refine(thr, 0, (1,))
    thr_f = pltpu.bitcast(thr, F32)
    gt = aff > thr_f
    eq = aff == thr_f
    need = cap - jnp.sum(gt.astype(F32), axis=1, keepdims=True)
    tri = tri_ref[...]

    def excl_prefix(flags):
        run = jnp.zeros((E, 1), F32)
        blocks = []
        for j in range(T // LANES):
            blk = flags[:, j * LANES:(j + 1) * LANES].astype(F32)
            blocks.append(_dot(blk.astype(BF), tri) + run)
            run = run + jnp.sum(blk, axis=1, keepdims=True)
        return jnp.concatenate(blocks, axis=1)

    sel = gt | (eq & (excl_prefix(eq) < need))
    base = (slot_stride * pl.program_id(0)).astype(F32)
    pos = jnp.where(sel, excl_prefix(sel) + base, -1.0)
    pos_ref[0] = pos.astype(jnp.int32)
    post_ref[0] = jnp.concatenate([pos, jnp.full((LANES - E, T), -1.0, F32)], axis=0).T


def _route(aff_t, cap, slot_stride):
    B, E, T = aff_t.shape
    r = np.arange(LANES)
    tri = jnp.asarray(r[:, None] < r[None, :], BF)
    return pl.pallas_call(
        functools.partial(_route_kernel, cap=cap, slot_stride=slot_stride),
        grid=(B,),
        in_specs=[pl.BlockSpec((1, E, T), lambda b: (b, 0, 0)), pl.BlockSpec((LANES, LANES), lambda b: (0, 0))],
        out_specs=[pl.BlockSpec((1, E, T), lambda b: (b, 0, 0)), pl.BlockSpec((1, T, LANES), lambda b: (b, 0, 0))],
        out_shape=[jax.ShapeDtypeStruct((B, E, T), jnp.int32), jax.ShapeDtypeStruct((B, T, LANES), F32)],
        compiler_params=_cparams("arbitrary"),
        name="route",
    )(aff_t, tri)


def _moe_kernel(*refs, n_slots, final):
    if final:
        h_ref, pos_ref, post_ref, aff_ref, wg_ref, wu_ref, wd_ref, x_ref, gate_ref, g_ref, o_ref = refs
    else:
        h_ref, pos_ref, post_ref, aff_ref, wg_ref, wu_ref, wd_ref, o_ref = refs
    e = pl.program_id(1)

    @pl.when(e == 0)
    def _():
        o_ref[...] = jnp.zeros_like(o_ref)

    pos = pos_ref[0, pl.ds(e, 1), :]
    aff = aff_ref[0, pl.ds(e, 1), :]
    T = pos.shape[1]
    hit = lax.broadcasted_iota(jnp.int32, (n_slots, T), 0) == pos
    w_slot = jnp.sum(jnp.where(hit, aff, 0.0), axis=1, keepdims=True)
    xg = _dot(hit.astype(BF), h_ref[0]).astype(BF)
    gt = _dot(xg, wg_ref[...].astype(BF))
    up = _dot(xg, wu_ref[...].astype(BF))
    hid = (gt * jax.nn.sigmoid(gt) * up * w_slot).astype(BF)
    y = _dot(hid, wd_ref[...]).astype(BF)

    lane = lax.broadcasted_iota(jnp.int32, (1, LANES), 1)
    slot_row = lax.broadcasted_iota(jnp.int32, (1, n_slots), 1).astype(F32)
    for t0 in range(0, T, 256):
        pos_col = jnp.sum(jnp.where(lane == e, post_ref[0, t0:t0 + 256, :], 0.0), axis=1, keepdims=True)
        o_ref[0, t0:t0 + 256, :] += _dot((pos_col == slot_row).astype(BF), y)

    if final:
        @pl.when(e == pl.num_programs(1) - 1)
        def _():
            for t0 in range(0, T, 256):
                rows = slice(t0, t0 + 256)
                o_ref[0, rows, :] = _rms(x_ref[0, rows, :] + gate_ref[0] * o_ref[0, rows, :], g_ref[...])


def _moe(h, pos, pos_t, aff_t, weights, l, n_slots, final=None):
    B, T, D = h.shape
    E = pos.shape[1]
    wg, wu, wd = weights

    def wspec(w):
        return pl.BlockSpec((None, None) + w.shape[2:], lambda b, e: (l, e, 0, 0))

    in_specs = [pl.BlockSpec((1, T, D), lambda b, e: (b, 0, 0)),
                pl.BlockSpec((1, E, T), lambda b, e: (b, 0, 0)),
                pl.BlockSpec((1, T, LANES), lambda b, e: (b, 0, 0)),
                pl.BlockSpec((1, E, T), lambda b, e: (b, 0, 0)),
                wspec(wg), wspec(wu), wspec(wd)]
    args = [h, pos, pos_t, aff_t, wg, wu, wd]
    if final is not None:
        x, mods3, g = final
        in_specs += [pl.BlockSpec((1, T, D), lambda b, e: (b, 0, 0), pipeline_mode=pl.Buffered(1)),
                     pl.BlockSpec((1, 1, D), lambda b, e: (b * 6 + 5, 0, 0)),
                     pl.BlockSpec((1, D), lambda b, e: (0, 0))]
        args += [x, mods3, g.reshape(1, D)]
    return pl.pallas_call(
        functools.partial(_moe_kernel, n_slots=n_slots, final=final is not None),
        grid=(B, E),
        in_specs=in_specs,
        out_specs=pl.BlockSpec((1, T, D), lambda b, e: (b, 0, 0)),
        out_shape=jax.ShapeDtypeStruct((B, T, D), F32),
        compiler_params=_cparams("arbitrary", "arbitrary"),
        name="moe",
    )(*args)


def _final_kernel(x_ref, moe_ref, gate_ref, g_ref, o_ref):
    o_ref[0] = _rms(x_ref[0] + gate_ref[0] * moe_ref[0], g_ref[...])


def _final(x, moe, mods3, g, tm=512):
    B, T, D = x.shape
    tok = pl.BlockSpec((1, tm, D), lambda b, i: (b, i, 0))
    return pl.pallas_call(
        _final_kernel,
        grid=(B, T // tm),
        in_specs=[tok, tok, pl.BlockSpec((1, 1, D), lambda b, i: (b * 6 + 5, 0, 0)),
                  pl.BlockSpec((1, D), lambda b, i: (0, 0))],
        out_specs=tok,
        out_shape=jax.ShapeDtypeStruct((B, T, D), F32),
        compiler_params=_cparams("arbitrary", "arbitrary"),
        name="final_norm",
    )(x, moe, mods3, g.reshape(1, D))


def kernel(x, c, ctx, c_ctx, ada_w, ada_b, norm1, norm2, w_in, na_rpb, mla_q_norm, mla_w_uq, mla_kv_norm, mla_w_ukv,
           gla_w_af, gla_b_af, gla_w_ab, gla_b_ab, gla_o_norm, w_br_a, w_br_b, w_br_c, w_out,
           moe_router, moe_w_gate, moe_w_up, moe_w_down, final_norm):
    B, T, D = x.shape
    L = ctx.shape[1]
    depth = ada_w.shape[0]
    assert B < MOD_ROWS
    cap_l = EC_CAPACITY * T // N_EXPERTS
    cap_c = EC_CAPACITY * L // N_EXPERTS
    ctx_row = B
    cc = jnp.concatenate([c, c_ctx[None], jnp.zeros((MOD_ROWS - B - 1, D), F32)], axis=0)

    def lat_row(b):
        return b

    def ctx_row_fn(b):
        return ctx_row

    def rows3(a):
        return a.reshape(depth, 1, -1)

    ada_b3, norm1_3, norm2_3 = rows3(ada_b), rows3(norm1), rows3(norm2)
    w_in_t = jnp.swapaxes(w_in, 1, 2).astype(BF)
    na_bias = _na_bias(na_rpb, T // GRID_W)
    mla_w = _mla_weights(mla_w_uq, mla_w_ukv)
    qn3, kvn3 = rows3(mla_q_norm), rows3(mla_kv_norm)
    rope_tabs = _rope_tables(T)
    gate_pad = lambda w, off: jnp.pad(w, ((0, 0), (off, LANES - off - GLA_GATE_RANK), (0, 0))).astype(BF)
    gla_params = (gate_pad(gla_w_af, SM_AF), rows3(gla_b_af), gate_pad(gla_w_ab, SM_AB), rows3(gla_b_ab),
                  rows3(gla_o_norm))
    wr_t = jnp.swapaxes(moe_router, 1, 2)
    wr_hi = wr_t.astype(BF)
    wr_lo = (wr_t - wr_hi.astype(F32)).astype(BF)
    merge_w = (w_br_a.astype(BF), w_br_b.astype(BF), w_br_c.astype(BF), w_out.astype(BF),
               jnp.concatenate([wr_hi, wr_lo], axis=1))
    moe_w = (moe_w_gate, moe_w_up, moe_w_down.astype(BF))

    moe_l = moe_c = mods3 = None
    for l in range(depth):
        ctx_out = l < depth - 1
        prev_mods3 = mods3
        mods3 = _ada(cc, ada_w, ada_b3, l).reshape(MOD_ROWS * 6, 1, D)
        z_l, x = _in_proj(x, moe_l, prev_mods3, mods3, lat_row, norm1_3, w_in_t, l)
        z_c, ctx = _in_proj(ctx, moe_c, prev_mods3, mods3, ctx_row_fn, norm1_3, w_in_t, l,
                            cols=None if ctx_out else CTX_KEY_COLS)

        oa_l, oa_c = _na_attention(z_l, z_c, na_bias, l, ctx_out)

        qb_l, kb_l, vb_l = _mla_proj(z_l, qn3, kvn3, mla_w, rope_tabs, l)
        qb_c, kb_c, vb_c = _mla_proj(z_c, qn3, kvn3, mla_w, None, l)
        ob_l = _mla_attn(qb_l, kb_c, vb_c, kb_l, vb_l)

        oc_l, oc_c = _gla(z_l, z_c, gla_params, l, ctx_out)

        x, h_l, aff_l = _merge(oa_l, ob_l, oc_l, z_l, x, mods3, lat_row, norm2_3, merge_w, l, tm=1024)
        pos_l, post_l = _route(aff_l, cap_l, 0)
        moe_l = _moe(h_l, pos_l, post_l, aff_l, moe_w, l, cap_l)

        if ctx_out:
            ob_c = _mla_attn(qb_c, kb_c, vb_c)
            ctx, h_c, aff_c = _merge(oa_c, ob_c, oc_c, z_c, ctx, mods3, ctx_row_fn, norm2_3, merge_w, l, tm=256)
            pos_c, post_c = _route(aff_c, cap_c, cap_c)
            flat = lambda a: jnp.swapaxes(a, 0, 1).reshape(1, N_EXPERTS, B * L)
            moe_c = _moe(h_c.reshape(1, B * L, D), flat(pos_c), post_c.reshape(1, B * L, LANES), flat(aff_c),
                         moe_w, l, B * cap_c).reshape(B, L, D)
    return _final(x, moe_l, mods3, final_norm)
```
